```python
import math, functools
import jax, jax.numpy as jnp
from jax import lax
import numpy as np

D_MODEL = 2048
BATCH = 2
SEQ = 4096
DEPTH = 2
DEC_BATCH = 32
DEC_SEQ = 4
PAST_LEN = 16384
PAGE_SIZE = 128

N_META = 16
EPS = 1e-6
NEG_INF = -1e30
LB_FLOOR = 1e-30
WINDOW = 128
ATTN_BLOCK = 128
HEAD_DIM_A = 64
N_HEADS_A = (D_MODEL // 2) // HEAD_DIM_A
N_KV_A = 4
GROUP_A = N_HEADS_A // N_KV_A
N_ROT = HEAD_DIM_A // 4
ROPE_THETA = 500000.0
H_B = 4
DK_B = 128
DV_B = (D_MODEL // 4) // H_B
H_C = 4
DK_C = (D_MODEL // 4) // H_C
DV_C = DK_C
RET_THETA = 10000.0
CHUNK = 64
Q_A_W = N_HEADS_A * HEAD_DIM_A
KV_A_W = N_KV_A * HEAD_DIM_A
W_B = H_B * DK_B
W_C = H_C * DK_C
IN_SIZES = (Q_A_W, KV_A_W, KV_A_W, W_B, W_B, H_B * DV_B, H_B * DV_B, W_C, W_C, H_C * DV_C, H_C * DV_C)
D_IN = sum(IN_SIZES)
D_MIX = Q_A_W + H_B * DV_B + H_C * DV_C
D_FF = -(-8 * D_MODEL // (3 * 256)) * 256

kernel_name = 'hymba_swa_hgrn2_retention_step'


def _rmsnorm(x, g):
    xf = x.astype(jnp.float32)
    y = xf * lax.rsqrt(jnp.mean(xf * xf, axis=-1, keepdims=True) + EPS)
    return (y * g.astype(jnp.float32)).astype(x.dtype)


def _rope(x, pos, n_rot, theta):
    half = n_rot // 2
    inv = theta ** (-jnp.arange(half, dtype=jnp.float32) * (2.0 / n_rot))
    ang = pos.astype(jnp.float32)[:, None] * inv[None, :]
    cos = jnp.cos(ang)[:, None, :]
    sin = jnp.sin(ang)[:, None, :]
    xf = x.astype(jnp.float32)
    x1 = xf[..., :half]
    x2 = xf[..., half:n_rot]
    parts = [x1 * cos - x2 * sin, x2 * cos + x1 * sin]
    if n_rot < x.shape[-1]:
        parts.append(xf[..., n_rot:])
    return jnp.concatenate(parts, axis=-1).astype(x.dtype)


def _lower_bounds(lb_param):
    p = jax.nn.softmax(lb_param.astype(jnp.float32), axis=0)
    return jnp.cumsum(p, axis=0) - p[0]


def _mixer_inputs(h, pos, w_in_l, qn, kn, lb):
    B, T, _ = h.shape
    idx = np.cumsum(IN_SIZES)[:-1].tolist()
    qa, ka, va, qb, fb, ib, gb, qc, kc, vc, gc = jnp.split(h @ w_in_l, idx, axis=-1)
    qa = _rope(_rmsnorm(qa.reshape(B, T, N_HEADS_A, HEAD_DIM_A), qn), pos, N_ROT, ROPE_THETA)
    ka = _rope(_rmsnorm(ka.reshape(B, T, N_KV_A, HEAD_DIM_A), kn), pos, N_ROT, ROPE_THETA)
    va = va.reshape(B, T, N_KV_A, HEAD_DIM_A)
    lbf = lb.astype(jnp.float32)
    log_lb = jnp.log(jnp.maximum(lbf, LB_FLOOR))
    logf = jnp.logaddexp(log_lb, jnp.log1p(-lbf) + jax.nn.log_sigmoid(fb.astype(jnp.float32)))
    kb = (-jnp.expm1(logf)).reshape(B, T, H_B, DK_B)
    logf = logf.reshape(B, T, H_B, DK_B)
    qb = jax.nn.silu(qb).reshape(B, T, H_B, DK_B)
    vb = ib.reshape(B, T, H_B, DV_B)
    qc = _rope(qc.reshape(B, T, H_C, DK_C), pos, DK_C, RET_THETA)
    kc = _rope(kc.reshape(B, T, H_C, DK_C), pos, DK_C, RET_THETA) * (DK_C ** -0.5)
    vc = vc.reshape(B, T, H_C, DV_C)
    return (qa, ka, va), (qb, kb, vb, logf), (qc, kc, vc), gb, gc


def _sink_attend(q, k, v, mask, sinks):
    scale = q.shape[-1] ** -0.5
    s = jnp.einsum('...qhgd,...khd->...hgqk', q.astype(jnp.float32), k.astype(jnp.float32)) * scale
    s = jnp.where(mask, s, NEG_INF)
    sk = sinks.astype(jnp.float32)[:, :, None, None]
    m = jnp.maximum(jnp.max(s, axis=-1, keepdims=True), sk)
    p = jnp.exp(s - m)
    w = p / (jnp.sum(p, axis=-1, keepdims=True) + jnp.exp(sk - m))
    return jnp.einsum('...hgqk,...khd->...qhgd', w, v.astype(jnp.float32))


def _swa_prompt(q, k, v, sinks):
    B, L, H, D = q.shape
    pad = ATTN_BLOCK - N_META
    nb = (L + pad) // ATTN_BLOCK

    def blocks(a):
        a = jnp.pad(a, ((0, 0), (pad, 0), (0, 0), (0, 0)))
        return a.reshape((B, nb, ATTN_BLOCK) + a.shape[2:])

    qb = blocks(q).reshape(B, nb, ATTN_BLOCK, N_KV_A, GROUP_A, D)
    kb = blocks(k)
    vb = blocks(v)
    kk = jnp.concatenate([jnp.concatenate([jnp.zeros_like(kb[:, :1]), kb[:, :-1]], axis=1), kb], axis=2)
    vv = jnp.concatenate([jnp.concatenate([jnp.zeros_like(vb[:, :1]), vb[:, :-1]], axis=1), vb], axis=2)
    qpos = (jnp.arange(nb * ATTN_BLOCK) - pad).reshape(nb, ATTN_BLOCK)
    kpos = jnp.concatenate([qpos - ATTN_BLOCK, qpos], axis=1)
    d = qpos[:, :, None] - kpos[:, None, :]
    mask = (kpos[:, None, :] >= 0) & (d >= 0) & (d < WINDOW)
    o = _sink_attend(qb, kk, vv, mask[:, None, None], sinks.reshape(N_KV_A, GROUP_A))
    return o.reshape(B, nb * ATTN_BLOCK, H * D)[:, pad:]


def _swa_sample(q, k, v, ck, cv, sinks):
    B, T, H, D = q.shape
    W = ck.shape[1]
    kk = jnp.concatenate([ck, k], axis=1)
    vv = jnp.concatenate([cv, v], axis=1)
    qpos = PAST_LEN + jnp.arange(T)
    kpos = jnp.concatenate([PAST_LEN - W + jnp.arange(W), qpos])
    d = qpos[:, None] - kpos[None, :]
    mask = (d >= 0) & (d < WINDOW)
    o = _sink_attend(q.reshape(B, T, N_KV_A, GROUP_A, D), kk, vv, mask, sinks.reshape(N_KV_A, GROUP_A))
    return o.reshape(B, T, H * D), kk[:, -W:], vv[:, -W:]


def _hgrn_chunk(S, q, k, v, logf):
    q, k, v, logf = (a.astype(jnp.float32) for a in (q, k, v, logf))
    L = q.shape[1]
    G = jnp.cumsum(logf, axis=1)
    inter = jnp.einsum('blhk,bhkv->blhv', q * jnp.exp(G), S)
    causal = jnp.tril(jnp.ones((L, L), bool))[None, :, :, None, None]
    diff = G[:, :, None] - G[:, None, :]
    dec = jnp.where(causal, jnp.exp(jnp.where(causal, diff, 0.0)), 0.0)
    a = jnp.sum(q[:, :, None] * k[:, None, :] * dec, axis=-1)
    intra = jnp.einsum('bqsh,bshv->bqhv', a, v)
    g_last = G[:, -1]
    S_new = jnp.exp(g_last)[..., None] * S + jnp.einsum('bshk,bshv->bhkv', k * jnp.exp(g_last[:, None] - G), v)
    return S_new, inter + intra


def _ret_chunk(lg, S, q, k, v):
    q, k, v = (a.astype(jnp.float32) for a in (q, k, v))
    L = q.shape[1]
    j = jnp.arange(L, dtype=jnp.float32)
    inter = jnp.einsum('blhk,bhkv->blhv', q, S) * jnp.exp((j + 1.0)[:, None] * lg)[None, :, :, None]
    rel = j[:, None] - j[None, :]
    causal = rel >= 0
    dec = jnp.where(causal[..., None], jnp.exp(jnp.where(causal, rel, 0.0)[..., None] * lg), 0.0)
    a = jnp.einsum('bqhk,bshk->bqsh', q, k) * dec
    intra = jnp.einsum('bqsh,bshv->bqhv', a, v)
    tail = jnp.exp((L - 1.0 - j)[:, None] * lg)
    S_new = jnp.exp(L * lg)[None, :, None, None] * S + jnp.einsum('bshk,bshv->bhkv', k * tail[None, :, :, None], v)
    return S_new, inter + intra


def _chunked(fn, S0, seqs):
    B = seqs[0].shape[0]
    S, o_head = fn(S0, *[a[:, :N_META] for a in seqs])
    rest = [a[:, N_META:] for a in seqs]
    n = rest[0].shape[1] // CHUNK
    xs = tuple(jnp.moveaxis(a.reshape((B, n, CHUNK) + a.shape[2:]), 1, 0) for a in rest)
    S, o = lax.scan(lambda s, c: fn(s, *c), S, xs)
    o = jnp.moveaxis(o, 0, 1).reshape((B, n * CHUNK) + o.shape[3:])
    return S, jnp.concatenate([o_head, o], axis=1)


def _finish(x, attn, ob, gb, oc, gc, hn, rn, w_o, nf, wg, wu, wd):
    B, T, _ = x.shape
    yb = (_rmsnorm(ob, hn) * jax.nn.silu(gb.reshape(B, T, H_B, DV_B).astype(jnp.float32))).reshape(B, T, H_B * DV_B)
    yc = (_rmsnorm(oc, rn) * jax.nn.silu(gc.reshape(B, T, H_C, DV_C).astype(jnp.float32))).reshape(B, T, H_C * DV_C)
    mix = jnp.concatenate([attn.astype(jnp.float32), yb, yc], axis=-1)
    x = x + (mix @ w_o).astype(x.dtype)
    h2 = _rmsnorm(x, nf)
    ff = (jax.nn.silu(h2 @ wg) * (h2 @ wu)) @ wd
    return x + ff.astype(x.dtype)


def setup_inputs(seed: int = 0) -> dict:
    key = jax.random.key(seed)
    ks = jax.random.split(key, 20)
    W = min(WINDOW, PAST_LEN)

    def nrm(k, shape, scale):
        return jax.random.normal(k, shape, jnp.float32) * scale

    return {
        'x_prompt': nrm(ks[0], (BATCH, SEQ, D_MODEL), 1.0),
        'x_sample': nrm(ks[1], (DEC_BATCH, DEC_SEQ, D_MODEL), 1.0),
        'cache_k': nrm(ks[2], (DEPTH, DEC_BATCH, W, N_KV_A, HEAD_DIM_A), 1.0),
        'cache_v': nrm(ks[3], (DEPTH, DEC_BATCH, W, N_KV_A, HEAD_DIM_A), 1.0),
        'state_hgrn': nrm(ks[4], (DEPTH, DEC_BATCH, H_B, DK_B, DV_B), 0.5),
        'state_ret': nrm(ks[5], (DEPTH, DEC_BATCH, H_C, DK_C, DV_C), 0.5),
        'meta_tokens': nrm(ks[6], (N_META, D_MODEL), 1.0),
        'norm_mix': 1.0 + nrm(ks[7], (DEPTH, D_MODEL), 0.02),
        'norm_ffn': 1.0 + nrm(ks[8], (DEPTH, D_MODEL), 0.02),
        'w_in': nrm(ks[9], (DEPTH, D_MODEL, D_IN), D_MODEL ** -0.5),
        'q_norm': 1.0 + nrm(ks[10], (DEPTH, HEAD_DIM_A), 0.02),
        'k_norm': 1.0 + nrm(ks[11], (DEPTH, HEAD_DIM_A), 0.02),
        'attn_sinks': nrm(ks[12], (DEPTH, N_HEADS_A), 0.5),
        'hgrn_lb': nrm(ks[13], (DEPTH, W_B), 0.5),
        'hgrn_norm': 1.0 + nrm(ks[14], (DEPTH, DV_B), 0.02),
        'ret_norm': 1.0 + nrm(ks[15], (DEPTH, DV_C), 0.02),
        'w_out': nrm(ks[16], (DEPTH, D_MIX, D_MODEL), D_MIX ** -0.5),
        'w_gate': nrm(ks[17], (DEPTH, D_MODEL, D_FF), D_MODEL ** -0.5),
        'w_up': nrm(ks[18], (DEPTH, D_MODEL, D_FF), D_MODEL ** -0.5),
        'w_down': nrm(ks[19], (DEPTH, D_FF, D_MODEL), D_FF ** -0.5),
    }


def reference(x_prompt, x_sample, cache_k, cache_v, state_hgrn, state_ret, meta_tokens, norm_mix, norm_ffn,
              w_in, q_norm, k_norm, attn_sinks, hgrn_lb, hgrn_norm, ret_norm, w_out, w_gate, w_up, w_down):
    lbs = _lower_bounds(hgrn_lb)
    lg = jnp.log1p(-jnp.exp2(-5.0 - jnp.arange(H_C, dtype=jnp.float32)))
    ret_fn = functools.partial(_ret_chunk, lg)

    B = x_prompt.shape[0]
    meta = jnp.broadcast_to(meta_tokens.astype(x_prompt.dtype)[None], (B, N_META, meta_tokens.shape[-1]))
    xp = jnp.concatenate([meta, x_prompt], axis=1)
    L = xp.shape[1]
    W_p = min(WINDOW, L)
    pos_p = jnp.arange(L)
    xs = x_sample
    pos_s = PAST_LEN + jnp.arange(xs.shape[1])

    ck_p, cv_p, sh_p, sr_p = [], [], [], []
    ck_s, cv_s, sh_s, sr_s = [], [], [], []
    for l in range(DEPTH):
        h = _rmsnorm(xp, norm_mix[l])
        (qa, ka, va), (qb, kb, vb, lf), (qc, kc, vc), gb, gc = _mixer_inputs(h, pos_p, w_in[l], q_norm[l], k_norm[l], lbs[l])
        attn = _swa_prompt(qa, ka, va, attn_sinks[l])
        sb, ob = _chunked(_hgrn_chunk, jnp.zeros((B, H_B, DK_B, DV_B), jnp.float32), (qb, kb, vb, lf))
        sc, oc = _chunked(ret_fn, jnp.zeros((B, H_C, DK_C, DV_C), jnp.float32), (qc, kc, vc))
        xp = _finish(xp, attn, ob, gb, oc, gc, hgrn_norm[l], ret_norm[l], w_out[l], norm_ffn[l], w_gate[l], w_up[l], w_down[l])
        ck_p.append(ka[:, -W_p:])
        cv_p.append(va[:, -W_p:])
        sh_p.append(sb)
        sr_p.append(sc)

        h = _rmsnorm(xs, norm_mix[l])
        (qa, ka, va), (qb, kb, vb, lf), (qc, kc, vc), gb, gc = _mixer_inputs(h, pos_s, w_in[l], q_norm[l], k_norm[l], lbs[l])
        attn, nk, nv = _swa_sample(qa, ka, va, cache_k[l], cache_v[l], attn_sinks[l])
        sb, ob = _hgrn_chunk(state_hgrn[l].astype(jnp.float32), qb, kb, vb, lf)
        sc, oc = ret_fn(state_ret[l].astype(jnp.float32), qc, kc, vc)
        xs = _finish(xs, attn, ob, gb, oc, gc, hgrn_norm[l], ret_norm[l], w_out[l], norm_ffn[l], w_gate[l], w_up[l], w_down[l])
        ck_s.append(nk)
        cv_s.append(nv)
        sh_s.append(sb)
        sr_s.append(sc)

    y_prompt = xp[:, N_META:]
    y_sample = xs
    return (y_prompt, y_sample,
            jnp.stack(ck_p), jnp.stack(cv_p), jnp.stack(sh_p), jnp.stack(sr_p),
            jnp.stack(ck_s), jnp.stack(cv_s), jnp.stack(sh_s), jnp.stack(sr_s))
```

```python
import functools

import jax
import jax.numpy as jnp
from jax import lax
from jax.experimental import pallas as pl
from jax.experimental.pallas import tpu as pltpu

f32 = jnp.float32
bf16 = jnp.bfloat16

N_META = 16
EPS = 1e-6
NEG_INF = -1e30
LB_FLOOR = 1e-30
WINDOW = 128
HEAD_DIM_A = 64
N_HEADS_A = 16
N_KV_A = 4
GROUP_A = N_HEADS_A // N_KV_A
N_ROT = 16
ROPE_THETA = 500000.0
H_B = 4
DK_B = 128
H_C = 4
DK_C = 128
RET_THETA = 10000.0
PAST_LEN = 16384

BLK = 128
PAD = BLK - N_META
SUB = 16
SROWS = 16
VMEM_LIMIT = 56 * 1024 * 1024

Q_A, K_A, V_A = 0, 1024, 1280
Q_B, F_B, I_B, G_B = 1536, 2048, 2560, 3072
Q_C, K_C, V_C, G_C = 3584, 4096, 4608, 5120
D_IN = 5632
W_A = 1024
NT = (((1,), (1,)), ((), ()))
TN = (((0,), (0,)), ((), ()))


def _dot(a, b, dims=None):
    if dims is None:
        return jnp.dot(a, b, preferred_element_type=f32)
    return lax.dot_general(a, b, dims, preferred_element_type=f32)


def _row_tile(rows, cap):
    best = 8
    for t in range(8, cap + 1, 8):
        if rows % t == 0:
            best = t
    return best


def _silu(x):
    return x * jax.nn.sigmoid(x)


def _rms(x, g):
    return x * lax.rsqrt(jnp.mean(x * x, axis=-1, keepdims=True) + EPS) * g


def _inproj_kernel(x_ref, g_ref, w_ref, o_ref, h_ref):
    @pl.when(pl.program_id(1) == 0)
    def _():
        h_ref[...] = _rms(x_ref[...], g_ref[...]).astype(bf16)

    o_ref[...] = _dot(h_ref[...], w_ref[...])


def _inproj(x, g, w):
    rows, d = x.shape
    n = w.shape[1]
    tm = _row_tile(rows, 1072)
    tn = 1408 if n % 1408 == 0 else 512
    return pl.pallas_call(
        _inproj_kernel,
        grid=(rows // tm, n // tn),
        in_specs=[pl.BlockSpec((tm, d), lambda i, j: (i, 0)),
                  pl.BlockSpec((1, d), lambda i, j: (0, 0)),
                  pl.BlockSpec((d, tn), lambda i, j: (0, j))],
        out_specs=pl.BlockSpec((tm, tn), lambda i, j: (i, j)),
        out_shape=jax.ShapeDtypeStruct((rows, n), f32),
        scratch_shapes=[pltpu.VMEM((tm, d), bf16)],
        compiler_params=pltpu.CompilerParams(dimension_semantics=("arbitrary", "arbitrary"),
                                             vmem_limit_bytes=VMEM_LIMIT),
        name="inproj",
    )(x, g, w)


def _ffn_kernel(x_ref, mix_ref, wo_ref, nf_ref, wg_ref, wu_ref, wd_ref, o_ref, h_ref):
    @pl.when(pl.program_id(1) == 0)
    def _():
        x1 = x_ref[...] + _dot(mix_ref[...], wo_ref[...])
        h_ref[...] = _rms(x1, nf_ref[...]).astype(bf16)
        o_ref[...] = x1

    h = h_ref[...]
    a = _silu(_dot(h, wg_ref[...])) * _dot(h, wu_ref[...])
    o_ref[...] += _dot(a.astype(bf16), wd_ref[...])


def _ffn(x, mix, wo, nf, wg, wu, wd):
    rows, d = x.shape
    dff = wg.shape[1]
    tm = _row_tile(rows, 536)
    tf = 512
    return pl.pallas_call(
        _ffn_kernel,
        grid=(rows // tm, dff // tf),
        in_specs=[pl.BlockSpec((tm, d), lambda i, j: (i, 0)),
                  pl.BlockSpec((tm, d), lambda i, j: (i, 0)),
                  pl.BlockSpec((d, d), lambda i, j: (0, 0), pipeline_mode=pl.Buffered(1)),
                  pl.BlockSpec((1, d), lambda i, j: (0, 0)),
                  pl.BlockSpec((d, tf), lambda i, j: (0, j)),
                  pl.BlockSpec((d, tf), lambda i, j: (0, j)),
                  pl.BlockSpec((tf, d), lambda i, j: (j, 0))],
        out_specs=pl.BlockSpec((tm, d), lambda i, j: (i, 0)),
        out_shape=jax.ShapeDtypeStruct((rows, d), f32),
        scratch_shapes=[pltpu.VMEM((tm, d), bf16)],
        compiler_params=pltpu.CompilerParams(dimension_semantics=("arbitrary", "arbitrary"),
                                             vmem_limit_bytes=VMEM_LIMIT),
        name="outproj_ffn",
    )(x, mix, wo, nf, wg, wu, wd)


def _head_norm(x, bd, g):
    x2 = x * x
    hi = x2.astype(bf16)
    lo = (x2 - hi.astype(f32)).astype(bf16)
    ssq = _dot(hi, bd) + _dot(lo, bd)
    return x * lax.rsqrt(ssq * (1.0 / HEAD_DIM_A) + EPS) * g


def _rope_a(x, tab_ref):
    out = []
    for t in range(x.shape[1] // 128):
        xt = x[:, 128 * t:128 * (t + 1)]
        out.append(xt * tab_ref[0] + pltpu.roll(xt, 128 - N_ROT // 2, 1) * tab_ref[1]
                   + pltpu.roll(xt, N_ROT // 2, 1) * tab_ref[2])
    return jnp.concatenate(out, axis=1)


def _rope_r(x, tab_ref):
    return x * tab_ref[0] + pltpu.roll(x, DK_C // 2, 1) * tab_ref[1]


def _log_forget(fb, lb):
    log_lb = jnp.log(jnp.maximum(lb, LB_FLOOR))
    logsig = jnp.minimum(fb, 0.0) - jnp.log1p(jnp.exp(-jnp.abs(fb)))
    b = jnp.log1p(-lb) + logsig
    return jnp.maximum(log_lb, b) + jnp.log1p(jnp.exp(-jnp.abs(log_lb - b)))


def _split3(x):
    h1 = x.astype(bf16)
    r1 = x - h1.astype(f32)
    h2 = r1.astype(bf16)
    h3 = (r1 - h2.astype(f32)).astype(bf16)
    return h1, h2, h3


def _mixer_prompt_kernel(sink_ref, gpow_ref, proj_ref, ropea_ref, roper_ref, qn_ref, kn_ref, lb_ref, hn_ref, rn_ref,
                         bd_ref, tri_ref, dec_ref, rsc_ref, tail_ref,
                         mix_ref, ck_ref, cv_ref, sh_ref, sr_ref,
                         kk, vv, sht, srs, qb_s, kb_s, g_s, a_s):
    i = pl.program_id(1)
    nb = pl.num_programs(1)

    @pl.when(i == 0)
    def _():
        kk[...] = jnp.zeros_like(kk)
        vv[...] = jnp.zeros_like(vv)
        sht[...] = jnp.zeros_like(sht)
        srs[...] = jnp.zeros_like(srs)

    qa = _rope_a(_head_norm(proj_ref[:, Q_A:Q_A + 1024], bd_ref[...], qn_ref[...]), ropea_ref)
    qs = (qa * (HEAD_DIM_A ** -0.5)).astype(bf16)
    k_cur = _rope_a(_head_norm(proj_ref[:, K_A:K_A + 256], bd_ref[0:256, 0:256], kn_ref[...]), ropea_ref)
    v_cur = proj_ref[:, V_A:V_A + 256]
    ck_ref[0] = k_cur
    cv_ref[0] = v_cur
    kk[BLK:2 * BLK, :] = k_cur.astype(bf16)
    vv[BLK:2 * BLK, :] = v_cur.astype(bf16)

    r = lax.broadcasted_iota(jnp.int32, (BLK, 2 * BLK), 0)
    c = lax.broadcasted_iota(jnp.int32, (BLK, 2 * BLK), 1)
    valid = (c > r) & (c <= r + WINDOW) & ((i - 1) * BLK + c >= PAD)
    outs = []
    for h in range(N_HEADS_A):
        g = h // GROUP_A
        s = _dot(qs[:, 64 * h:64 * (h + 1)], kk[:, 64 * g:64 * (g + 1)], NT)
        s = jnp.where(valid, s, NEG_INF)
        sk = sink_ref[h]
        m = jnp.maximum(jnp.max(s, axis=-1, keepdims=True), sk)
        p = jnp.exp(s - m)
        den = jnp.sum(p, axis=-1, keepdims=True) + jnp.exp(sk - m)
        outs.append(_dot(p.astype(bf16), vv[:, 64 * g:64 * (g + 1)]) / den)
        if h % 2 == 1:
            mix_ref[:, 64 * (h - 1):64 * (h + 1)] = jnp.concatenate(outs, axis=1).astype(bf16)
            outs = []
    kk[0:BLK, :] = kk[BLK:2 * BLK, :]
    vv[0:BLK, :] = vv[BLK:2 * BLK, :]

    rowabs = i * BLK + lax.broadcasted_iota(jnp.int32, (BLK, 1), 0)
    logf = jnp.where(rowabs >= PAD, _log_forget(proj_ref[:, F_B:F_B + 512], lb_ref[...]), 0.0)
    kb = 1.0 - jnp.exp(logf)
    qb = _silu(proj_ref[:, Q_B:Q_B + 512])
    vb = proj_ref[:, I_B:I_B + 512].astype(bf16)
    l1, l2, l3 = _split3(logf)
    tri = tri_ref[...]
    gcum = _dot(tri, l1) + _dot(tri, l2) + _dot(tri, l3)
    qb_s[...] = qb
    kb_s[...] = kb
    g_s[...] = gcum

    qg = (qb * jnp.exp(gcum)).astype(bf16)
    inter = [_dot(qg[:, 128 * h:128 * (h + 1)], sht[h].astype(bf16), NT) for h in range(H_B)]

    rr = lax.broadcasted_iota(jnp.int32, (BLK, 1), 0)
    r2 = lax.broadcasted_iota(jnp.int32, (BLK, BLK), 0)
    c2 = lax.broadcasted_iota(jnp.int32, (BLK, BLK), 1)
    amat = [None] * H_B
    bz = BLK
    while bz > SUB:
        half = bz // 2
        gref = jnp.concatenate(
            [jnp.broadcast_to(g_s[pl.ds(st + half - 1, 1), :], (bz, 512)) for st in range(0, BLK, bz)], axis=0)
        upper = (rr % bz) >= half
        e = jnp.exp(jnp.where(upper, gcum - gref, gref - gcum))
        qt = jnp.where(upper, qb * e, 0.0).astype(bf16)
        kt = jnp.where(upper, 0.0, kb * e).astype(bf16)
        same = (r2 // bz) == (c2 // bz)
        for h in range(H_B):
            pm = _dot(qt[:, 128 * h:128 * (h + 1)], kt[:, 128 * h:128 * (h + 1)], NT)
            if bz < BLK:
                pm = jnp.where(same, pm, 0.0)
            amat[h] = pm if amat[h] is None else amat[h] + pm
        bz = half
    for h in range(H_B):
        a_s[h] = amat[h]

    sub_i = lax.broadcasted_iota(jnp.int32, (SUB, 1), 0)
    lane = lax.broadcasted_iota(jnp.int32, (SUB, BLK), 1)

    def diag_body(j, carry):
        r0 = pl.multiple_of(j * SUB, SUB)
        qblk = qb_s[pl.ds(r0, SUB), :]
        gblk = g_s[pl.ds(r0, SUB), :]
        accs = [jnp.zeros((SUB, BLK), f32) for _ in range(H_B)]
        for s in range(SUB):
            krow = kb_s[pl.ds(r0 + s, 1), :]
            grow = g_s[pl.ds(r0 + s, 1), :]
            y = qblk * krow * jnp.exp(jnp.where(sub_i >= s, gblk - grow, NEG_INF))
            for h in range(H_B):
                col = jnp.sum(y[:, 128 * h:128 * (h + 1)], axis=-1, keepdims=True)
                accs[h] = jnp.where(lane == r0 + s, col, accs[h])
        for h in range(H_B):
            a_s[h, pl.ds(r0, SUB), :] += accs[h]
        return carry

    lax.fori_loop(0, BLK // SUB, diag_body, 0)

    glast = g_s[pl.ds(BLK - 1, 1), :]
    kt2 = (kb * jnp.exp(glast - gcum)).astype(bf16)
    dlast = jnp.exp(glast)
    for h in range(H_B):
        sl = slice(128 * h, 128 * (h + 1))
        ob = inter[h] + _dot(a_s[h].astype(bf16), vb[:, sl])
        sht[h] = sht[h] * dlast[:, sl] + _dot(vb[:, sl], kt2[:, sl], TN)
        yb = _rms(ob, hn_ref[...]) * _silu(proj_ref[:, G_B + 128 * h:G_B + 128 * (h + 1)])
        mix_ref[:, W_A + 128 * h:W_A + 128 * (h + 1)] = yb.astype(bf16)

    for h in range(H_C):
        sl = slice(128 * h, 128 * (h + 1))
        qc = _rope_r(proj_ref[:, Q_C + 128 * h:Q_C + 128 * (h + 1)], roper_ref)
        kc = _rope_r(proj_ref[:, K_C + 128 * h:K_C + 128 * (h + 1)], roper_ref) * (DK_C ** -0.5)
        vc = proj_ref[:, V_C + 128 * h:V_C + 128 * (h + 1)].astype(bf16)
        qcb = qc.astype(bf16)
        sc = _dot(qcb, kc.astype(bf16), NT) * dec_ref[h]
        oc = _dot(qcb, srs[h].astype(bf16)) * rsc_ref[h] + _dot(sc.astype(bf16), vc)
        srs[h] = gpow_ref[h] * srs[h] + _dot((kc * tail_ref[h]).astype(bf16), vc, TN)
        yc = _rms(oc, rn_ref[...]) * _silu(proj_ref[:, G_C + 128 * h:G_C + 128 * (h + 1)])
        mix_ref[:, W_A + 512 + 128 * h:W_A + 512 + 128 * (h + 1)] = yc.astype(bf16)

    @pl.when(i == nb - 1)
    def _():
        for h in range(H_B):
            sh_ref[0, h] = sht[h].T
            sr_ref[0, h] = srs[h]


def _const_spec(shape):
    nd = len(shape)
    return pl.BlockSpec(shape, lambda *_: (0,) * nd)


def _mixer_prompt(proj, rows_total, batch, nb, sinks, gpow, ropea, roper, qn, kn, lb, hn, rn, bd, tri, dec, rsc, tail):
    smem = pl.BlockSpec(memory_space=pltpu.SMEM)
    return pl.pallas_call(
        _mixer_prompt_kernel,
        grid=(batch, nb),
        in_specs=[smem, smem,
                  pl.BlockSpec((BLK, D_IN), lambda b, i: (b * nb + i, 0)),
                  pl.BlockSpec((3, BLK, 128), lambda b, i: (0, i, 0)),
                  pl.BlockSpec((2, BLK, 128), lambda b, i: (0, i, 0)),
                  _const_spec((1, 1024)), _const_spec((1, 256)), _const_spec((1, 512)),
                  _const_spec((1, 128)), _const_spec((1, 128)),
                  _const_spec((1024, 1024)), _const_spec((BLK, BLK)),
                  _const_spec((H_C, BLK, BLK)), _const_spec((H_C, BLK, 128)), _const_spec((H_C, BLK, 128))],
        out_specs=[pl.BlockSpec((BLK, 2048), lambda b, i: (b * nb + i, 0)),
                   pl.BlockSpec((1, BLK, 256), lambda b, i: (b, 0, 0)),
                   pl.BlockSpec((1, BLK, 256), lambda b, i: (b, 0, 0)),
                   pl.BlockSpec((1, H_B, 128, 128), lambda b, i: (b, 0, 0, 0)),
                   pl.BlockSpec((1, H_C, 128, 128), lambda b, i: (b, 0, 0, 0))],
        out_shape=[jax.ShapeDtypeStruct((rows_total, 2048), bf16),
                   jax.ShapeDtypeStruct((batch, BLK, 256), f32),
                   jax.ShapeDtypeStruct((batch, BLK, 256), f32),
                   jax.ShapeDtypeStruct((batch, H_B, 128, 128), f32),
                   jax.ShapeDtypeStruct((batch, H_C, 128, 128), f32)],
        scratch_shapes=[pltpu.VMEM((2 * BLK, 256), bf16), pltpu.VMEM((2 * BLK, 256), bf16),
                        pltpu.VMEM((H_B, 128, 128), f32), pltpu.VMEM((H_C, 128, 128), f32),
                        pltpu.VMEM((BLK, 512), f32), pltpu.VMEM((BLK, 512), f32), pltpu.VMEM((BLK, 512), f32),
                        pltpu.VMEM((H_B, BLK, BLK), f32)],
        compiler_params=pltpu.CompilerParams(dimension_semantics=("arbitrary", "arbitrary"),
                                             vmem_limit_bytes=VMEM_LIMIT),
        name="mixer_prompt",
    )(sinks, gpow, proj, ropea, roper, qn, kn, lb, hn, rn, bd, tri, dec, rsc, tail)


def _mixer_sample_kernel(sink_ref, gpow_ref, mixin_ref, proj_ref, ropea_ref, roper_ref, qn_ref, kn_ref, lb_ref,
                         hn_ref, rn_ref, bd_ref, dec_ref, rsc_ref, tail_ref, ck_ref, cv_ref, sh_ref, sr_ref,
                         mix_ref, nk_ref, nv_ref, nsh_ref, nsr_ref, *, dec_seq):
    del mixin_ref
    nbat = SROWS // dec_seq
    w = ck_ref.shape[1]
    rb = lax.broadcasted_iota(jnp.int32, (SROWS, 1), 0) // dec_seq
    rt = lax.broadcasted_iota(jnp.int32, (SROWS, 1), 0) % dec_seq

    def pick(parts):
        out = parts[nbat - 1]
        for b in range(nbat - 2, -1, -1):
            out = jnp.where(rb == b, parts[b], out)
        return out

    qa = _rope_a(_head_norm(proj_ref[:, Q_A:Q_A + 1024], bd_ref[...], qn_ref[...]), ropea_ref)
    qs = (qa * (HEAD_DIM_A ** -0.5)).astype(bf16)
    k_new = _rope_a(_head_norm(proj_ref[:, K_A:K_A + 256], bd_ref[0:256, 0:256], kn_ref[...]), ropea_ref)
    v_new = proj_ref[:, V_A:V_A + 256]
    for b in range(nbat):
        nk_ref[b, 0:w - dec_seq, :] = ck_ref[b, dec_seq:w, :]
        nk_ref[b, w - dec_seq:w, :] = k_new[dec_seq * b:dec_seq * (b + 1), :]
        nv_ref[b, 0:w - dec_seq, :] = cv_ref[b, dec_seq:w, :]
        nv_ref[b, w - dec_seq:w, :] = v_new[dec_seq * b:dec_seq * (b + 1), :]
    knb = k_new.astype(bf16)
    vnb = v_new.astype(bf16)

    grows = GROUP_A * SROWS
    rt_g = jnp.concatenate([rt] * GROUP_A, axis=0)
    rb_g = jnp.concatenate([rb] * GROUP_A, axis=0)
    jc = lax.broadcasted_iota(jnp.int32, (grows, w), 1)
    valid_c = jc > rt_g + (w - WINDOW)
    cn = lax.broadcasted_iota(jnp.int32, (grows, SROWS), 1)
    valid_n = (cn // dec_seq == rb_g) & (cn % dec_seq <= rt_g)
    heads = [None] * N_HEADS_A
    for g in range(N_KV_A):
        ksl = slice(64 * g, 64 * (g + 1))
        qg = jnp.concatenate([qs[:, 64 * h:64 * (h + 1)] for h in range(GROUP_A * g, GROUP_A * (g + 1))], axis=0)
        sparts = [_dot(qg, ck_ref[b, :, ksl].astype(bf16), NT) for b in range(nbat)]
        s_c = sparts[nbat - 1]
        for b in range(nbat - 2, -1, -1):
            s_c = jnp.where(rb_g == b, sparts[b], s_c)
        s_c = jnp.where(valid_c, s_c, NEG_INF)
        s_n = jnp.where(valid_n, _dot(qg, knb[:, ksl], NT), NEG_INF)
        sk = jnp.concatenate([jnp.full((SROWS, 1), sink_ref[h], f32)
                              for h in range(GROUP_A * g, GROUP_A * (g + 1))], axis=0)
        m = jnp.maximum(jnp.maximum(jnp.max(s_c, axis=-1, keepdims=True), jnp.max(s_n, axis=-1, keepdims=True)), sk)
        p_c = jnp.exp(s_c - m)
        p_n = jnp.exp(s_n - m)
        den = jnp.sum(p_c, axis=-1, keepdims=True) + jnp.sum(p_n, axis=-1, keepdims=True) + jnp.exp(sk - m)
        pcb = p_c.astype(bf16)
        oparts = [_dot(pcb, cv_ref[b, :, ksl].astype(bf16)) for b in range(nbat)]
        o = oparts[nbat - 1]
        for b in range(nbat - 2, -1, -1):
            o = jnp.where(rb_g == b, oparts[b], o)
        o = (o + _dot(p_n.astype(bf16), vnb[:, ksl])) / den
        for hh in range(GROUP_A):
            heads[GROUP_A * g + hh] = o[SROWS * hh:SROWS * (hh + 1), :]
    mix_ref[:, 0:W_A] = jnp.concatenate(heads, axis=1).astype(bf16)

    logf = _log_forget(proj_ref[:, F_B:F_B + 512], lb_ref[...])
    kb = 1.0 - jnp.exp(logf)
    qb = _silu(proj_ref[:, Q_B:Q_B + 512])
    vb = proj_ref[:, I_B:I_B + 512]
    gcum = logf
    for d in range(1, dec_seq):
        gcum = gcum + jnp.where(rt >= d, pltpu.roll(logf, d, 0), 0.0)
    cb = lax.broadcasted_iota(jnp.int32, (1, SROWS), 1) // dec_seq
    ct = lax.broadcasted_iota(jnp.int32, (1, SROWS), 1) % dec_seq
    intra = [jnp.zeros((SROWS, 128), f32) for _ in range(H_B)]
    for s in range(SROWS):
        ok = (rb == s // dec_seq) & (rt >= s % dec_seq)
        y = qb * kb[s:s + 1, :] * jnp.exp(jnp.where(ok, gcum - gcum[s:s + 1, :], NEG_INF))
        for h in range(H_B):
            sl = slice(128 * h, 128 * (h + 1))
            intra[h] = intra[h] + jnp.sum(y[:, sl], axis=-1, keepdims=True) * vb[s:s + 1, sl]
    qg = (qb * jnp.exp(gcum)).astype(bf16)
    vbb = vb.astype(bf16)
    glast = [gcum[dec_seq * (b + 1) - 1:dec_seq * (b + 1), :] for b in range(nbat)]
    for h in range(H_B):
        sl = slice(128 * h, 128 * (h + 1))
        ob = pick([_dot(qg[:, sl], sh_ref[b, h].astype(bf16)) for b in range(nbat)]) + intra[h]
        for b in range(nbat):
            kt2 = jnp.where(rb == b, kb[:, sl] * jnp.exp(glast[b][:, sl] - gcum[:, sl]), 0.0).astype(bf16)
            dcol = jnp.broadcast_to(jnp.exp(glast[b][:, sl]), (128, 128)).T
            nsh_ref[b, h] = dcol * sh_ref[b, h] + _dot(kt2, vbb[:, sl], TN)
        yb = _rms(ob, hn_ref[...]) * _silu(proj_ref[:, G_B + 128 * h:G_B + 128 * (h + 1)])
        mix_ref[:, W_A + 128 * h:W_A + 128 * (h + 1)] = yb.astype(bf16)

    for h in range(H_C):
        qc = _rope_r(proj_ref[:, Q_C + 128 * h:Q_C + 128 * (h + 1)], roper_ref)
        kc = _rope_r(proj_ref[:, K_C + 128 * h:K_C + 128 * (h + 1)], roper_ref) * (DK_C ** -0.5)
        vc = proj_ref[:, V_C + 128 * h:V_C + 128 * (h + 1)]
        qcb = qc.astype(bf16)
        vcb = vc.astype(bf16)
        sc = _dot(qcb, kc.astype(bf16), NT) * dec_ref[h]
        oc = pick([_dot(qcb, sr_ref[b, h].astype(bf16)) for b in range(nbat)]) * rsc_ref[h] \
            + _dot(sc.astype(bf16), vcb)
        kct = kc * tail_ref[h]
        for b in range(nbat):
            nsr_ref[b, h] = gpow_ref[h] * sr_ref[b, h] + _dot(jnp.where(rb == b, kct, 0.0).astype(bf16), vcb, TN)
        yc = _rms(oc, rn_ref[...]) * _silu(proj_ref[:, G_C + 128 * h:G_C + 128 * (h + 1)])
        mix_ref[:, W_A + 512 + 128 * h:W_A + 512 + 128 * (h + 1)] = yc.astype(bf16)


def _mixer_sample(mix, proj, row0, layer, dec_batch, dec_seq, sinks, gpow, ropea, roper, qn, kn, lb, hn, rn, bd,
                  dec, rsc, tail, cache_k, cache_v, state_h, state_r):
    smem = pl.BlockSpec(memory_space=pltpu.SMEM)
    nbat = SROWS // dec_seq
    steps = dec_batch // nbat
    blk0 = row0 // SROWS
    w = cache_k.shape[1]
    return pl.pallas_call(
        functools.partial(_mixer_sample_kernel, dec_seq=dec_seq),
        grid=(steps,),
        in_specs=[smem, smem,
                  pl.BlockSpec(memory_space=pl.ANY),
                  pl.BlockSpec((SROWS, D_IN), lambda c: (blk0 + c, 0)),
                  _const_spec((3, SROWS, 128)), _const_spec((2, SROWS, 128)),
                  _const_spec((1, 1024)), _const_spec((1, 256)), _const_spec((1, 512)),
                  _const_spec((1, 128)), _const_spec((1, 128)),
                  _const_spec((1024, 1024)),
                  _const_spec((H_C, SROWS, SROWS)), _const_spec((H_C, SROWS, 128)), _const_spec((H_C, SROWS, 128)),
                  pl.BlockSpec((nbat, w, 256), lambda c: (layer * steps + c, 0, 0)),
                  pl.BlockSpec((nbat, w, 256), lambda c: (layer * steps + c, 0, 0)),
                  pl.BlockSpec((nbat, H_B, 128, 128), lambda c: (layer * steps + c, 0, 0, 0)),
                  pl.BlockSpec((nbat, H_C, 128, 128), lambda c: (layer * steps + c, 0, 0, 0))],
        out_specs=[pl.BlockSpec((SROWS, 2048), lambda c: (blk0 + c, 0)),
                   pl.BlockSpec((nbat, w, 256), lambda c: (c, 0, 0)),
                   pl.BlockSpec((nbat, w, 256), lambda c: (c, 0, 0)),
                   pl.BlockSpec((nbat, H_B, 128, 128), lambda c: (c, 0, 0, 0)),
                   pl.BlockSpec((nbat, H_C, 128, 128), lambda c: (c, 0, 0, 0))],
        out_shape=[jax.ShapeDtypeStruct(mix.shape, bf16),
                   jax.ShapeDtypeStruct((dec_batch, w, 256), f32),
                   jax.ShapeDtypeStruct((dec_batch, w, 256), f32),
                   jax.ShapeDtypeStruct((dec_batch, H_B, 128, 128), f32),
                   jax.ShapeDtypeStruct((dec_batch, H_C, 128, 128), f32)],
        input_output_aliases={2: 0},
        compiler_params=pltpu.CompilerParams(dimension_semantics=("arbitrary",), vmem_limit_bytes=VMEM_LIMIT),
        name="mixer_sample",
    )(sinks, gpow, mix, proj, ropea, roper, qn, kn, lb, hn, rn, bd, dec, rsc, tail, cache_k, cache_v, state_h, state_r)


def _rope_tables(pos):
    posf = pos.astype(f32)[:, None]
    t = pos.shape[0]
    half = N_ROT // 2
    inv = ROPE_THETA ** (-jnp.arange(half, dtype=f32) * (2.0 / N_ROT))
    ang = posf * inv[None, :]
    cos, sin = jnp.cos(ang), jnp.sin(ang)
    z8 = jnp.zeros((t, half), f32)
    rest0 = jnp.zeros((t, HEAD_DIM_A - N_ROT), f32)
    c64 = jnp.concatenate([cos, cos, rest0 + 1.0], axis=1)
    sm64 = jnp.concatenate([-sin, z8, rest0], axis=1)
    sp64 = jnp.concatenate([z8, sin, rest0], axis=1)
    ropea = jnp.stack([jnp.tile(c64, (1, 2)), jnp.tile(sm64, (1, 2)), jnp.tile(sp64, (1, 2))])
    invr = RET_THETA ** (-jnp.arange(DK_C // 2, dtype=f32) * (2.0 / DK_C))
    angr = posf * invr[None, :]
    cr, sr = jnp.cos(angr), jnp.sin(angr)
    roper = jnp.stack([jnp.concatenate([cr, cr], axis=1), jnp.concatenate([-sr, sr], axis=1)])
    return ropea, roper


def _ret_tables(lg, seq_of_row, tok_of_row, length):
    tq = tok_of_row.astype(f32)
    rel = tq[:, None] - tq[None, :]
    ok = (seq_of_row[:, None] == seq_of_row[None, :]) & (rel >= 0)
    dec = jnp.where(ok[None], jnp.exp(jnp.where(ok, rel, 0.0)[None] * lg[:, None, None]), 0.0)
    n = tq.shape[0]
    rsc = jnp.broadcast_to(jnp.exp((tq + 1.0)[None, :, None] * lg[:, None, None]), (H_C, n, 128))
    tail = jnp.broadcast_to(jnp.exp((length - 1.0 - tq)[None, :, None] * lg[:, None, None]), (H_C, n, 128))
    gpow = jnp.exp(length * lg)
    return dec, rsc, tail, gpow


def kernel(x_prompt, x_sample, cache_k, cache_v, state_hgrn, state_ret, meta_tokens, norm_mix, norm_ffn, w_in, q_norm, k_norm, attn_sinks, hgrn_lb, hgrn_norm, ret_norm, w_out, w_gate, w_up, w_down):
    batch, seq, d = x_prompt.shape
    dec_batch, dec_seq, _ = x_sample.shape
    depth = w_in.shape[0]
    w = cache_k.shape[2]
    assert d == 2048 and w_in.shape[2] == D_IN and seq % BLK == 0
    assert SROWS % dec_seq == 0 and dec_batch % (SROWS // dec_seq) == 0 and w == WINDOW
    nb = seq // BLK + 1
    lp = nb * BLK
    rows_p = batch * lp
    rows = rows_p + dec_batch * dec_seq

    meta = jnp.broadcast_to(meta_tokens.astype(f32)[None], (batch, N_META, d))
    xp = jnp.concatenate([jnp.zeros((batch, PAD, d), f32), meta, x_prompt], axis=1)
    x = jnp.concatenate([xp.reshape(rows_p, d), x_sample.reshape(dec_batch * dec_seq, d)], axis=0)

    p = jax.nn.softmax(hgrn_lb.astype(f32), axis=0)
    lbs = jnp.cumsum(p, axis=0) - p[0]
    lg = jnp.log1p(-jnp.exp2(-5.0 - jnp.arange(H_C, dtype=f32)))
    ropea_p, roper_p = _rope_tables(jnp.arange(lp) - PAD)
    srow = jnp.arange(SROWS)
    ropea_s, roper_s = _rope_tables(PAST_LEN + srow % dec_seq)
    dec_p, rsc_p, tail_p, gpow_p = _ret_tables(lg, jnp.zeros((BLK,), jnp.int32), jnp.arange(BLK), float(BLK))
    dec_s, rsc_s, tail_s, gpow_s = _ret_tables(lg, srow // dec_seq, srow % dec_seq, float(dec_seq))
    hid = jnp.arange(1024) // HEAD_DIM_A
    bd = (hid[:, None] == hid[None, :]).astype(bf16)
    tri = (jnp.arange(BLK)[:, None] >= jnp.arange(BLK)[None, :]).astype(bf16)

    ck_flat = cache_k.reshape(depth * dec_batch, w, N_KV_A * HEAD_DIM_A)
    cv_flat = cache_v.reshape(depth * dec_batch, w, N_KV_A * HEAD_DIM_A)
    sh_flat = state_hgrn.reshape(depth * dec_batch, H_B, 128, 128)
    sr_flat = state_ret.reshape(depth * dec_batch, H_C, 128, 128)

    outs_p, outs_s = [], []
    for l in range(depth):
        proj = _inproj(x, norm_mix[l][None], w_in[l].astype(bf16))
        qn = jnp.tile(q_norm[l], N_HEADS_A)[None]
        kn = jnp.tile(k_norm[l], N_KV_A)[None]
        common = (qn, kn, lbs[l][None], hgrn_norm[l][None], ret_norm[l][None], bd)
        mix, ck, cv, sh, sr = _mixer_prompt(proj, rows, batch, nb, attn_sinks[l], gpow_p, ropea_p, roper_p,
                                            *common, tri, dec_p, rsc_p, tail_p)
        mix, nk, nv, nsh, nsr = _mixer_sample(mix, proj, rows_p, l, dec_batch, dec_seq, attn_sinks[l], gpow_s,
                                              ropea_s, roper_s, *common, dec_s, rsc_s, tail_s,
                                              ck_flat, cv_flat, sh_flat, sr_flat)
        x = _ffn(x, mix, w_out[l].astype(bf16), norm_ffn[l][None], w_gate[l].astype(bf16), w_up[l].astype(bf16),
                 w_down[l].astype(bf16))
        outs_p.append((ck, cv, sh, sr))
        outs_s.append((nk, nv, nsh, nsr))

    y_prompt = x[:rows_p].reshape(batch, lp, d)[:, BLK:]
    y_sample = x[rows_p:].reshape(dec_batch, dec_seq, d)
    kv_shape = (depth, -1, w, N_KV_A, HEAD_DIM_A)
    stack = lambda outs, k: jnp.stack([o[k] for o in outs])
    return (y_prompt, y_sample,
            stack(outs_p, 0).reshape(kv_shape), stack(outs_p, 1).reshape(kv_shape), stack(outs_p, 2), stack(outs_p, 3),
            stack(outs_s, 0).reshape(kv_shape), stack(outs_s, 1).reshape(kv_shape), stack(outs_s, 2), stack(outs_s, 3))
```

```python
import functools

import jax
import jax.numpy as jnp
from jax import lax
from jax.experimental import pallas as pl
from jax.experimental.pallas import tpu as pltpu

f32 = jnp.float32
bf16 = jnp.bfloat16

N_META = 16
EPS = 1e-6
NEG_INF = -1e30
LB_FLOOR = 1e-30
WINDOW = 128
HEAD_DIM_A = 64
N_HEADS_A = 16
N_KV_A = 4
GROUP_A = N_HEADS_A // N_KV_A
N_ROT = 16
ROPE_THETA = 500000.0
H_B = 4
DK_B = 128
H_C = 4
DK_C = 128
RET_THETA = 10000.0
PAST_LEN = 16384

BLK = 128
PAD = BLK - N_META
SUB = 16
SROWS = 16
TM_F = 512
VMEM_LIMIT = 56 * 1024 * 1024

Q_A, K_A, V_A = 0, 1024, 1280
Q_B, F_B, I_B, G_B = 1536, 2048, 2560, 3072
Q_C, K_C, V_C, G_C = 3584, 4096, 4608, 5120
D_IN = 5632
W_A = 1024
NT = (((1,), (1,)), ((), ()))
TN = (((0,), (0,)), ((), ()))


def _dot(a, b, dims=None):
    if dims is None:
        return jnp.dot(a, b, preferred_element_type=f32)
    return lax.dot_general(a, b, dims, preferred_element_type=f32)


def _row_tile(rows, cap):
    best = 8
    for t in range(8, cap + 1, 8):
        if rows % t == 0:
            best = t
    return best


def _silu(x):
    return x * jax.nn.sigmoid(x)


def _rms(x, g):
    return x * lax.rsqrt(jnp.mean(x * x, axis=-1, keepdims=True) + EPS) * g


def _inproj_kernel(h_ref, w_ref, o_ref, wb_ref):
    @pl.when(pl.program_id(1) == 0)
    def _():
        wb_ref[...] = w_ref[...].astype(bf16)

    o_ref[...] = _dot(h_ref[...], wb_ref[...])


def _inproj(h, w_all, layer):
    rows, d = h.shape
    n = w_all.shape[2]
    tm = _row_tile(rows, 1088)
    tn = 1408
    return pl.pallas_call(
        _inproj_kernel,
        grid=(n // tn, rows // tm),
        in_specs=[pl.BlockSpec((tm, d), lambda j, i: (i, 0)),
                  pl.BlockSpec((None, d, tn), lambda j, i: (layer, 0, j))],
        out_specs=pl.BlockSpec((tm, tn), lambda j, i: (i, j)),
        out_shape=jax.ShapeDtypeStruct((rows, n), f32),
        scratch_shapes=[pltpu.VMEM((d, tn), bf16)],
        compiler_params=pltpu.CompilerParams(dimension_semantics=("arbitrary", "arbitrary"),
                                             vmem_limit_bytes=VMEM_LIMIT),
        name="inproj",
    )(h, w_all)


def _embed_kernel(xp_ref, tail_ref, g_ref, x_ref, h_ref):
    i = pl.program_id(0)
    last = pl.num_programs(0) - 1

    def emit(src_ref):
        x = src_ref[...]
        x_ref[...] = x
        h_ref[...] = _rms(x, g_ref[...]).astype(bf16)

    pl.when(i < last)(lambda: emit(xp_ref))
    pl.when(i == last)(lambda: emit(tail_ref))


def _embed(xp, tail, g):
    rows_main, d = xp.shape
    n_main = rows_main // TM_F
    rows = rows_main + TM_F
    return pl.pallas_call(
        _embed_kernel,
        grid=(n_main + 1,),
        in_specs=[pl.BlockSpec((TM_F, d), lambda i: (jnp.minimum(i, n_main - 1), 0)),
                  pl.BlockSpec((TM_F, d), lambda i: (0, 0)),
                  pl.BlockSpec((1, d), lambda i: (0, 0))],
        out_specs=[pl.BlockSpec((TM_F, d), lambda i: (i, 0)),
                   pl.BlockSpec((TM_F, d), lambda i: (i, 0))],
        out_shape=[jax.ShapeDtypeStruct((rows, d), f32), jax.ShapeDtypeStruct((rows, d), bf16)],
        compiler_params=pltpu.CompilerParams(dimension_semantics=("arbitrary",), vmem_limit_bytes=VMEM_LIMIT),
        name="embed_norm",
    )(xp, tail, g)


def _ffn_kernel(*refs, with_next):
    if with_next:
        x_ref, mix_ref, wo_ref, nf_ref, wg_ref, wu_ref, wd_ref, gn_ref, o_ref, hn_ref, h_ref = refs
    else:
        x_ref, mix_ref, wo_ref, nf_ref, wg_ref, wu_ref, wd_ref, o_ref, h_ref = refs
    j = pl.program_id(1)

    @pl.when(j == 0)
    def _():
        x1 = x_ref[...] + _dot(mix_ref[...], wo_ref[...])
        h_ref[...] = _rms(x1, nf_ref[...]).astype(bf16)
        o_ref[...] = x1

    h = h_ref[...]
    a = _silu(_dot(h, wg_ref[...])) * _dot(h, wu_ref[...])
    o_ref[...] += _dot(a.astype(bf16), wd_ref[...])

    if with_next:
        @pl.when(j == pl.num_programs(1) - 1)
        def _():
            hn_ref[...] = _rms(o_ref[...], gn_ref[...]).astype(bf16)


def _ffn(x, mix, wo, nf, wg, wu, wd, g_next, tile0, n_tiles):
    d = x.shape[1]
    dff = wg.shape[1]
    tf = 512
    with_next = g_next is not None
    row_spec = pl.BlockSpec((TM_F, d), lambda i, j: (tile0 + i, 0))
    out_spec = pl.BlockSpec((TM_F, d), lambda i, j: (i, 0))
    in_specs = [row_spec, row_spec,
                pl.BlockSpec((d, d), lambda i, j: (0, 0), pipeline_mode=pl.Buffered(1)),
                pl.BlockSpec((1, d), lambda i, j: (0, 0)),
                pl.BlockSpec((d, tf), lambda i, j: (0, j)),
                pl.BlockSpec((d, tf), lambda i, j: (0, j)),
                pl.BlockSpec((tf, d), lambda i, j: (j, 0))]
    args = [x, mix, wo, nf, wg, wu, wd]
    out_specs = [out_spec]
    out_shape = [jax.ShapeDtypeStruct((n_tiles * TM_F, d), f32)]
    if with_next:
        in_specs.append(pl.BlockSpec((1, d), lambda i, j: (0, 0)))
        args.append(g_next)
        out_specs.append(out_spec)
        out_shape.append(jax.ShapeDtypeStruct((n_tiles * TM_F, d), bf16))
    return pl.pallas_call(
        functools.partial(_ffn_kernel, with_next=with_next),
        grid=(n_tiles, dff // tf),
        in_specs=in_specs,
        out_specs=out_specs,
        out_shape=out_shape,
        scratch_shapes=[pltpu.VMEM((TM_F, d), bf16)],
        compiler_params=pltpu.CompilerParams(dimension_semantics=("arbitrary", "arbitrary"),
                                             vmem_limit_bytes=VMEM_LIMIT),
        name="outproj_ffn",
    )(*args)


def _head_norm(x, bd, g):
    x2 = x * x
    hi = x2.astype(bf16)
    lo = (x2 - hi.astype(f32)).astype(bf16)
    ssq = _dot(hi, bd) + _dot(lo, bd)
    return x * lax.rsqrt(ssq * (1.0 / HEAD_DIM_A) + EPS) * g


def _rope_a(x, tab_ref):
    out = []
    for t in range(x.shape[1] // 128):
        xt = x[:, 128 * t:128 * (t + 1)]
        out.append(xt * tab_ref[0] + pltpu.roll(xt, 128 - N_ROT // 2, 1) * tab_ref[1]
                   + pltpu.roll(xt, N_ROT // 2, 1) * tab_ref[2])
    return jnp.concatenate(out, axis=1)


def _rope_r(x, tab_ref):
    return x * tab_ref[0] + pltpu.roll(x, DK_C // 2, 1) * tab_ref[1]


def _log_forget(fb, lb):
    log_lb = jnp.log(jnp.maximum(lb, LB_FLOOR))
    logsig = jnp.minimum(fb, 0.0) - jnp.log1p(jnp.exp(-jnp.abs(fb)))
    b = jnp.log1p(-lb) + logsig
    return jnp.maximum(log_lb, b) + jnp.log1p(jnp.exp(-jnp.abs(log_lb - b)))


def _split3(x):
    h1 = x.astype(bf16)
    r1 = x - h1.astype(f32)
    h2 = r1.astype(bf16)
    h3 = (r1 - h2.astype(f32)).astype(bf16)
    return h1, h2, h3


def _mixer_prompt_kernel(sink_ref, gpow_ref, proj_ref, ropea_ref, roper_ref, qn_ref, kn_ref, lb_ref, hn_ref, rn_ref,
                         bd_ref, tri_ref, dec_ref, rsc_ref, tail_ref,
                         mix_ref, ck_ref, cv_ref, sh_ref, sr_ref,
                         kk, vv, sht, srs, qb_s, kb_s, g_s, a_s):
    i = pl.program_id(1)
    nb = pl.num_programs(1)

    @pl.when(i == 0)
    def _():
        kk[...] = jnp.zeros_like(kk)
        vv[...] = jnp.zeros_like(vv)
        sht[...] = jnp.zeros_like(sht)
        srs[...] = jnp.zeros_like(srs)

    qa = _rope_a(_head_norm(proj_ref[:, Q_A:Q_A + 1024], bd_ref[...], qn_ref[...]), ropea_ref)
    qs = (qa * (HEAD_DIM_A ** -0.5)).astype(bf16)
    k_cur = _rope_a(_head_norm(proj_ref[:, K_A:K_A + 256], bd_ref[0:256, 0:256], kn_ref[...]), ropea_ref)
    v_cur = proj_ref[:, V_A:V_A + 256]
    ck_ref[0] = k_cur
    cv_ref[0] = v_cur
    kk[BLK:2 * BLK, :] = k_cur.astype(bf16)
    vv[BLK:2 * BLK, :] = v_cur.astype(bf16)

    r = lax.broadcasted_iota(jnp.int32, (BLK, 2 * BLK), 0)
    c = lax.broadcasted_iota(jnp.int32, (BLK, 2 * BLK), 1)
    valid = (c > r) & (c <= r + WINDOW) & ((i - 1) * BLK + c >= PAD)
    outs = []
    for h in range(N_HEADS_A):
        g = h // GROUP_A
        s = _dot(qs[:, 64 * h:64 * (h + 1)], kk[:, 64 * g:64 * (g + 1)], NT)
        s = jnp.where(valid, s, NEG_INF)
        sk = sink_ref[h]
        m = jnp.maximum(jnp.max(s, axis=-1, keepdims=True), sk)
        p = jnp.exp(s - m)
        den = jnp.sum(p, axis=-1, keepdims=True) + jnp.exp(sk - m)
        outs.append(_dot(p.astype(bf16), vv[:, 64 * g:64 * (g + 1)]) / den)
        if h % 2 == 1:
            mix_ref[:, 64 * (h - 1):64 * (h + 1)] = jnp.concatenate(outs, axis=1).astype(bf16)
            outs = []
    kk[0:BLK, :] = kk[BLK:2 * BLK, :]
    vv[0:BLK, :] = vv[BLK:2 * BLK, :]

    rowabs = i * BLK + lax.broadcasted_iota(jnp.int32, (BLK, 1), 0)
    logf = jnp.where(rowabs >= PAD, _log_forget(proj_ref[:, F_B:F_B + 512], lb_ref[...]), 0.0)
    kb = 1.0 - jnp.exp(logf)
    qb = _silu(proj_ref[:, Q_B:Q_B + 512])
    vb = proj_ref[:, I_B:I_B + 512].astype(bf16)
    l1, l2, l3 = _split3(logf)
    tri = tri_ref[...]
    gcum = _dot(tri, l1) + _dot(tri, l2) + _dot(tri, l3)
    qb_s[...] = qb
    kb_s[...] = kb
    g_s[...] = gcum

    qg = (qb * jnp.exp(gcum)).astype(bf16)
    inter = [_dot(qg[:, 128 * h:128 * (h + 1)], sht[h].astype(bf16), NT) for h in range(H_B)]

    rr = lax.broadcasted_iota(jnp.int32, (BLK, 1), 0)
    r2 = lax.broadcasted_iota(jnp.int32, (BLK, BLK), 0)
    c2 = lax.broadcasted_iota(jnp.int32, (BLK, BLK), 1)
    amat = [None] * H_B
    bz = BLK
    while bz > SUB:
        half = bz // 2
        gref = jnp.concatenate(
            [jnp.broadcast_to(g_s[pl.ds(st + half - 1, 1), :], (bz, 512)) for st in range(0, BLK, bz)], axis=0)
        upper = (rr % bz) >= half
        e = jnp.exp(jnp.where(upper, gcum - gref, gref - gcum))
        qt = jnp.where(upper, qb * e, 0.0).astype(bf16)
        kt = jnp.where(upper, 0.0, kb * e).astype(bf16)
        same = (r2 // bz) == (c2 // bz)
        for h in range(H_B):
            pm = _dot(qt[:, 128 * h:128 * (h + 1)], kt[:, 128 * h:128 * (h + 1)], NT)
            if bz < BLK:
                pm = jnp.where(same, pm, 0.0)
            amat[h] = pm if amat[h] is None else amat[h] + pm
        bz = half
    for h in range(H_B):
        a_s[h] = amat[h]

    sub_i = lax.broadcasted_iota(jnp.int32, (SUB, 1), 0)
    lane = lax.broadcasted_iota(jnp.int32, (SUB, BLK), 1)

    def diag_body(j, carry):
        r0 = pl.multiple_of(j * SUB, SUB)
        qblk = qb_s[pl.ds(r0, SUB), :]
        gblk = g_s[pl.ds(r0, SUB), :]
        accs = [jnp.zeros((SUB, BLK), f32) for _ in range(H_B)]
        for s in range(SUB):
            krow = kb_s[pl.ds(r0 + s, 1), :]
            grow = g_s[pl.ds(r0 + s, 1), :]
            y = qblk * krow * jnp.exp(jnp.where(sub_i >= s, gblk - grow, NEG_INF))
            for h in range(H_B):
                col = jnp.sum(y[:, 128 * h:128 * (h + 1)], axis=-1, keepdims=True)
                accs[h] = jnp.where(lane == r0 + s, col, accs[h])
        for h in range(H_B):
            a_s[h, pl.ds(r0, SUB), :] += accs[h]
        return carry

    lax.fori_loop(0, BLK // SUB, diag_body, 0)

    glast = g_s[pl.ds(BLK - 1, 1), :]
    kt2 = (kb * jnp.exp(glast - gcum)).astype(bf16)
    dlast = jnp.exp(glast)
    for h in range(H_B):
        sl = slice(128 * h, 128 * (h + 1))
        ob = inter[h] + _dot(a_s[h].astype(bf16), vb[:, sl])
        sht[h] = sht[h] * dlast[:, sl] + _dot(vb[:, sl], kt2[:, sl], TN)
        yb = _rms(ob, hn_ref[...]) * _silu(proj_ref[:, G_B + 128 * h:G_B + 128 * (h + 1)])
        mix_ref[:, W_A + 128 * h:W_A + 128 * (h + 1)] = yb.astype(bf16)

    for h in range(H_C):
        sl = slice(128 * h, 128 * (h + 1))
        qc = _rope_r(proj_ref[:, Q_C + 128 * h:Q_C + 128 * (h + 1)], roper_ref)
        kc = _rope_r(proj_ref[:, K_C + 128 * h:K_C + 128 * (h + 1)], roper_ref) * (DK_C ** -0.5)
        vc = proj_ref[:, V_C + 128 * h:V_C + 128 * (h + 1)].astype(bf16)
        qcb = qc.astype(bf16)
        sc = _dot(qcb, kc.astype(bf16), NT) * dec_ref[h]
        oc = _dot(qcb, srs[h].astype(bf16)) * rsc_ref[h] + _dot(sc.astype(bf16), vc)
        srs[h] = gpow_ref[h] * srs[h] + _dot((kc * tail_ref[h]).astype(bf16), vc, TN)
        yc = _rms(oc, rn_ref[...]) * _silu(proj_ref[:, G_C + 128 * h:G_C + 128 * (h + 1)])
        mix_ref[:, W_A + 512 + 128 * h:W_A + 512 + 128 * (h + 1)] = yc.astype(bf16)

    @pl.when(i == nb - 1)
    def _():
        for h in range(H_B):
            sh_ref[0, h] = sht[h].T
            sr_ref[0, h] = srs[h]


def _const_spec(shape):
    nd = len(shape)
    return pl.BlockSpec(shape, lambda *_: (0,) * nd)


def _mixer_prompt(proj, rows_total, batch, nb, sinks, gpow, ropea, roper, qn, kn, lb, hn, rn, bd, tri, dec, rsc, tail):
    smem = pl.BlockSpec(memory_space=pltpu.SMEM)

    def row_blk(b, i):
        return jnp.where(i == 0, batch * (nb - 1) + b, b * (nb - 1) + i - 1)

    return pl.pallas_call(
        _mixer_prompt_kernel,
        grid=(batch, nb),
        in_specs=[smem, smem,
                  pl.BlockSpec((BLK, D_IN), lambda b, i: (row_blk(b, i), 0)),
                  pl.BlockSpec((3, BLK, 128), lambda b, i: (0, i, 0)),
                  pl.BlockSpec((2, BLK, 128), lambda b, i: (0, i, 0)),
                  _const_spec((1, 1024)), _const_spec((1, 256)), _const_spec((1, 512)),
                  _const_spec((1, 128)), _const_spec((1, 128)),
                  _const_spec((1024, 1024)), _const_spec((BLK, BLK)),
                  _const_spec((H_C, BLK, BLK)), _const_spec((H_C, BLK, 128)), _const_spec((H_C, BLK, 128))],
        out_specs=[pl.BlockSpec((BLK, 2048), lambda b, i: (row_blk(b, i), 0)),
                   pl.BlockSpec((1, BLK, 256), lambda b, i: (b, 0, 0)),
                   pl.BlockSpec((1, BLK, 256), lambda b, i: (b, 0, 0)),
                   pl.BlockSpec((1, H_B, 128, 128), lambda b, i: (b, 0, 0, 0)),
                   pl.BlockSpec((1, H_C, 128, 128), lambda b, i: (b, 0, 0, 0))],
        out_shape=[jax.ShapeDtypeStruct((rows_total, 2048), bf16),
                   jax.ShapeDtypeStruct((batch, BLK, 256), f32),
                   jax.ShapeDtypeStruct((batch, BLK, 256), f32),
                   jax.ShapeDtypeStruct((batch, H_B, 128, 128), f32),
                   jax.ShapeDtypeStruct((batch, H_C, 128, 128), f32)],
        scratch_shapes=[pltpu.VMEM((2 * BLK, 256), bf16), pltpu.VMEM((2 * BLK, 256), bf16),
                        pltpu.VMEM((H_B, 128, 128), f32), pltpu.VMEM((H_C, 128, 128), f32),
                        pltpu.VMEM((BLK, 512), f32), pltpu.VMEM((BLK, 512), f32), pltpu.VMEM((BLK, 512), f32),
                        pltpu.VMEM((H_B, BLK, BLK), f32)],
        compiler_params=pltpu.CompilerParams(dimension_semantics=("arbitrary", "arbitrary"),
                                             vmem_limit_bytes=VMEM_LIMIT),
        name="mixer_prompt",
    )(sinks, gpow, proj, ropea, roper, qn, kn, lb, hn, rn, bd, tri, dec, rsc, tail)


def _mixer_sample_kernel(sink_ref, gpow_ref, mixin_ref, *refs, dec_seq, steps):
    del mixin_ref
    c = pl.program_id(0)
    mix_ref = refs[16]
    pl.when(c < steps)(lambda: _mixer_sample_body(sink_ref, gpow_ref, *refs, dec_seq=dec_seq))

    @pl.when(c >= steps)
    def _():
        mix_ref[...] = jnp.zeros_like(mix_ref)


def _mixer_sample_body(sink_ref, gpow_ref, proj_ref, ropea_ref, roper_ref, qn_ref, kn_ref, lb_ref,
                       hn_ref, rn_ref, bd_ref, dec_ref, rsc_ref, tail_ref, ck_ref, cv_ref, sh_ref, sr_ref,
                       mix_ref, nk_ref, nv_ref, nsh_ref, nsr_ref, *, dec_seq):
    nbat = SROWS // dec_seq
    w = ck_ref.shape[1]
    rb = lax.broadcasted_iota(jnp.int32, (SROWS, 1), 0) // dec_seq
    rt = lax.broadcasted_iota(jnp.int32, (SROWS, 1), 0) % dec_seq

    def pick(parts):
        out = parts[nbat - 1]
        for b in range(nbat - 2, -1, -1):
            out = jnp.where(rb == b, parts[b], out)
        return out

    qa = _rope_a(_head_norm(proj_ref[:, Q_A:Q_A + 1024], bd_ref[...], qn_ref[...]), ropea_ref)
    qs = (qa * (HEAD_DIM_A ** -0.5)).astype(bf16)
    k_new = _rope_a(_head_norm(proj_ref[:, K_A:K_A + 256], bd_ref[0:256, 0:256], kn_ref[...]), ropea_ref)
    v_new = proj_ref[:, V_A:V_A + 256]
    for b in range(nbat):
        nk_ref[b, 0:w - dec_seq, :] = ck_ref[b, dec_seq:w, :]
        nk_ref[b, w - dec_seq:w, :] = k_new[dec_seq * b:dec_seq * (b + 1), :]
        nv_ref[b, 0:w - dec_seq, :] = cv_ref[b, dec_seq:w, :]
        nv_ref[b, w - dec_seq:w, :] = v_new[dec_seq * b:dec_seq * (b + 1), :]
    knb = k_new.astype(bf16)
    vnb = v_new.astype(bf16)

    grows = GROUP_A * SROWS
    rt_g = jnp.concatenate([rt] * GROUP_A, axis=0)
    rb_g = jnp.concatenate([rb] * GROUP_A, axis=0)
    jc = lax.broadcasted_iota(jnp.int32, (grows, w), 1)
    valid_c = jc > rt_g + (w - WINDOW)
    cn = lax.broadcasted_iota(jnp.int32, (grows, SROWS), 1)
    valid_n = (cn // dec_seq == rb_g) & (cn % dec_seq <= rt_g)
    heads = [None] * N_HEADS_A
    for g in range(N_KV_A):
        ksl = slice(64 * g, 64 * (g + 1))
        qg = jnp.concatenate([qs[:, 64 * h:64 * (h + 1)] for h in range(GROUP_A * g, GROUP_A * (g + 1))], axis=0)
        sparts = [_dot(qg, ck_ref[b, :, ksl].astype(bf16), NT) for b in range(nbat)]
        s_c = sparts[nbat - 1]
        for b in range(nbat - 2, -1, -1):
            s_c = jnp.where(rb_g == b, sparts[b], s_c)
        s_c = jnp.where(valid_c, s_c, NEG_INF)
        s_n = jnp.where(valid_n, _dot(qg, knb[:, ksl], NT), NEG_INF)
        sk = jnp.concatenate([jnp.full((SROWS, 1), sink_ref[h], f32)
                              for h in range(GROUP_A * g, GROUP_A * (g + 1))], axis=0)
        m = jnp.maximum(jnp.maximum(jnp.max(s_c, axis=-1, keepdims=True), jnp.max(s_n, axis=-1, keepdims=True)), sk)
        p_c = jnp.exp(s_c - m)
        p_n = jnp.exp(s_n - m)
        den = jnp.sum(p_c, axis=-1, keepdims=True) + jnp.sum(p_n, axis=-1, keepdims=True) + jnp.exp(sk - m)
        pcb = p_c.astype(bf16)
        oparts = [_dot(pcb, cv_ref[b, :, ksl].astype(bf16)) for b in range(nbat)]
        o = oparts[nbat - 1]
        for b in range(nbat - 2, -1, -1):
            o = jnp.where(rb_g == b, oparts[b], o)
        o = (o + _dot(p_n.astype(bf16), vnb[:, ksl])) / den
        for hh in range(GROUP_A):
            heads[GROUP_A * g + hh] = o[SROWS * hh:SROWS * (hh + 1), :]
    mix_ref[:, 0:W_A] = jnp.concatenate(heads, axis=1).astype(bf16)

    logf = _log_forget(proj_ref[:, F_B:F_B + 512], lb_ref[...])
    kb = 1.0 - jnp.exp(logf)
    qb = _silu(proj_ref[:, Q_B:Q_B + 512])
    vb = proj_ref[:, I_B:I_B + 512]
    gcum = logf
    for d in range(1, dec_seq):
        gcum = gcum + jnp.where(rt >= d, pltpu.roll(logf, d, 0), 0.0)
    cb = lax.broadcasted_iota(jnp.int32, (1, SROWS), 1) // dec_seq
    ct = lax.broadcasted_iota(jnp.int32, (1, SROWS), 1) % dec_seq
    intra = [jnp.zeros((SROWS, 128), f32) for _ in range(H_B)]
    for s in range(SROWS):
        ok = (rb == s // dec_seq) & (rt >= s % dec_seq)
        y = qb * kb[s:s + 1, :] * jnp.exp(jnp.where(ok, gcum - gcum[s:s + 1, :], NEG_INF))
        for h in range(H_B):
            sl = slice(128 * h, 128 * (h + 1))
            intra[h] = intra[h] + jnp.sum(y[:, sl], axis=-1, keepdims=True) * vb[s:s + 1, sl]
    qg = (qb * jnp.exp(gcum)).astype(bf16)
    vbb = vb.astype(bf16)
    glast = [gcum[dec_seq * (b + 1) - 1:dec_seq * (b + 1), :] for b in range(nbat)]
    for h in range(H_B):
        sl = slice(128 * h, 128 * (h + 1))
        ob = pick([_dot(qg[:, sl], sh_ref[b, h].astype(bf16)) for b in range(nbat)]) + intra[h]
        for b in range(nbat):
            kt2 = jnp.where(rb == b, kb[:, sl] * jnp.exp(glast[b][:, sl] - gcum[:, sl]), 0.0).astype(bf16)
            dcol = jnp.broadcast_to(jnp.exp(glast[b][:, sl]), (128, 128)).T
            nsh_ref[b, h] = dcol * sh_ref[b, h] + _dot(kt2, vbb[:, sl], TN)
        yb = _rms(ob, hn_ref[...]) * _silu(proj_ref[:, G_B + 128 * h:G_B + 128 * (h + 1)])
        mix_ref[:, W_A + 128 * h:W_A + 128 * (h + 1)] = yb.astype(bf16)

    for h in range(H_C):
        qc = _rope_r(proj_ref[:, Q_C + 128 * h:Q_C + 128 * (h + 1)], roper_ref)
        kc = _rope_r(proj_ref[:, K_C + 128 * h:K_C + 128 * (h + 1)], roper_ref) * (DK_C ** -0.5)
        vc = proj_ref[:, V_C + 128 * h:V_C + 128 * (h + 1)]
        qcb = qc.astype(bf16)
        vcb = vc.astype(bf16)
        sc = _dot(qcb, kc.astype(bf16), NT) * dec_ref[h]
        oc = pick([_dot(qcb, sr_ref[b, h].astype(bf16)) for b in range(nbat)]) * rsc_ref[h] \
            + _dot(sc.astype(bf16), vcb)
        kct = kc * tail_ref[h]
        for b in range(nbat):
            nsr_ref[b, h] = gpow_ref[h] * sr_ref[b, h] + _dot(jnp.where(rb == b, kct, 0.0).astype(bf16), vcb, TN)
        yc = _rms(oc, rn_ref[...]) * _silu(proj_ref[:, G_C + 128 * h:G_C + 128 * (h + 1)])
        mix_ref[:, W_A + 512 + 128 * h:W_A + 512 + 128 * (h + 1)] = yc.astype(bf16)


def _mixer_sample(mix, proj, row0, layer, dec_batch, dec_seq, sinks, gpow, ropea, roper, qn, kn, lb, hn, rn, bd,
                  dec, rsc, tail, cache_k, cache_v, state_h, state_r):
    smem = pl.BlockSpec(memory_space=pltpu.SMEM)
    nbat = SROWS // dec_seq
    steps = dec_batch // nbat
    blk0 = row0 // SROWS
    w = cache_k.shape[1]
    fill_steps = (mix.shape[0] - row0) // SROWS - steps
    cl = lambda c: jnp.minimum(c, steps - 1)
    return pl.pallas_call(
        functools.partial(_mixer_sample_kernel, dec_seq=dec_seq, steps=steps),
        grid=(steps + fill_steps,),
        in_specs=[smem, smem,
                  pl.BlockSpec(memory_space=pl.ANY),
                  pl.BlockSpec((SROWS, D_IN), lambda c: (blk0 + cl(c), 0)),
                  _const_spec((3, SROWS, 128)), _const_spec((2, SROWS, 128)),
                  _const_spec((1, 1024)), _const_spec((1, 256)), _const_spec((1, 512)),
                  _const_spec((1, 128)), _const_spec((1, 128)),
                  _const_spec((1024, 1024)),
                  _const_spec((H_C, SROWS, SROWS)), _const_spec((H_C, SROWS, 128)), _const_spec((H_C, SROWS, 128)),
                  pl.BlockSpec((nbat, w, 256), lambda c: (layer * steps + cl(c), 0, 0)),
                  pl.BlockSpec((nbat, w, 256), lambda c: (layer * steps + cl(c), 0, 0)),
                  pl.BlockSpec((nbat, H_B, 128, 128), lambda c: (layer * steps + cl(c), 0, 0, 0)),
                  pl.BlockSpec((nbat, H_C, 128, 128), lambda c: (layer * steps + cl(c), 0, 0, 0))],
        out_specs=[pl.BlockSpec((SROWS, 2048), lambda c: (blk0 + c, 0)),
                   pl.BlockSpec((nbat, w, 256), lambda c: (cl(c), 0, 0)),
                   pl.BlockSpec((nbat, w, 256), lambda c: (cl(c), 0, 0)),
                   pl.BlockSpec((nbat, H_B, 128, 128), lambda c: (cl(c), 0, 0, 0)),
                   pl.BlockSpec((nbat, H_C, 128, 128), lambda c: (cl(c), 0, 0, 0))],
        out_shape=[jax.ShapeDtypeStruct(mix.shape, bf16),
                   jax.ShapeDtypeStruct((dec_batch, w, 256), f32),
                   jax.ShapeDtypeStruct((dec_batch, w, 256), f32),
                   jax.ShapeDtypeStruct((dec_batch, H_B, 128, 128), f32),
                   jax.ShapeDtypeStruct((dec_batch, H_C, 128, 128), f32)],
        input_output_aliases={2: 0},
        compiler_params=pltpu.CompilerParams(dimension_semantics=("arbitrary",), vmem_limit_bytes=VMEM_LIMIT),
        name="mixer_sample",
    )(sinks, gpow, mix, proj, ropea, roper, qn, kn, lb, hn, rn, bd, dec, rsc, tail, cache_k, cache_v, state_h, state_r)


def _rope_tables(pos):
    posf = pos.astype(f32)[:, None]
    t = pos.shape[0]
    half = N_ROT // 2
    inv = ROPE_THETA ** (-jnp.arange(half, dtype=f32) * (2.0 / N_ROT))
    ang = posf * inv[None, :]
    cos, sin = jnp.cos(ang), jnp.sin(ang)
    z8 = jnp.zeros((t, half), f32)
    rest0 = jnp.zeros((t, HEAD_DIM_A - N_ROT), f32)
    c64 = jnp.concatenate([cos, cos, rest0 + 1.0], axis=1)
    sm64 = jnp.concatenate([-sin, z8, rest0], axis=1)
    sp64 = jnp.concatenate([z8, sin, rest0], axis=1)
    ropea = jnp.stack([jnp.tile(c64, (1, 2)), jnp.tile(sm64, (1, 2)), jnp.tile(sp64, (1, 2))])
    invr = RET_THETA ** (-jnp.arange(DK_C // 2, dtype=f32) * (2.0 / DK_C))
    angr = posf * invr[None, :]
    cr, sr = jnp.cos(angr), jnp.sin(angr)
    roper = jnp.stack([jnp.concatenate([cr, cr], axis=1), jnp.concatenate([-sr, sr], axis=1)])
    return ropea, roper


def _ret_tables(lg, seq_of_row, tok_of_row, length):
    tq = tok_of_row.astype(f32)
    rel = tq[:, None] - tq[None, :]
    ok = (seq_of_row[:, None] == seq_of_row[None, :]) & (rel >= 0)
    dec = jnp.where(ok[None], jnp.exp(jnp.where(ok, rel, 0.0)[None] * lg[:, None, None]), 0.0)
    n = tq.shape[0]
    rsc = jnp.broadcast_to(jnp.exp((tq + 1.0)[None, :, None] * lg[:, None, None]), (H_C, n, 128))
    tail = jnp.broadcast_to(jnp.exp((length - 1.0 - tq)[None, :, None] * lg[:, None, None]), (H_C, n, 128))
    gpow = jnp.exp(length * lg)
    return dec, rsc, tail, gpow


def kernel(x_prompt, x_sample, cache_k, cache_v, state_hgrn, state_ret, meta_tokens, norm_mix, norm_ffn, w_in, q_norm, k_norm, attn_sinks, hgrn_lb, hgrn_norm, ret_norm, w_out, w_gate, w_up, w_down):
    batch, seq, d = x_prompt.shape
    dec_batch, dec_seq, _ = x_sample.shape
    depth = w_in.shape[0]
    w = cache_k.shape[2]
    assert d == 2048 and w_in.shape[2] == D_IN and seq % BLK == 0
    assert SROWS % dec_seq == 0 and dec_batch % (SROWS // dec_seq) == 0 and w == WINDOW
    nb = seq // BLK + 1
    lp = nb * BLK
    rows_main = batch * seq
    rows_s = dec_batch * dec_seq
    row_s0 = rows_main + batch * BLK
    rows = rows_main + TM_F
    assert rows_main % TM_F == 0 and batch * BLK + rows_s <= TM_F and (rows - row_s0) % SROWS == 0

    meta_blk = jnp.concatenate([jnp.zeros((PAD, d), f32), meta_tokens.astype(f32)], axis=0)
    tail = jnp.concatenate([jnp.tile(meta_blk, (batch, 1)), x_sample.reshape(rows_s, d),
                            jnp.zeros((rows - row_s0 - rows_s, d), f32)], axis=0)
    x, h = _embed(x_prompt.reshape(rows_main, d), tail, norm_mix[0][None])

    p = jax.nn.softmax(hgrn_lb.astype(f32), axis=0)
    lbs = jnp.cumsum(p, axis=0) - p[0]
    lg = jnp.log1p(-jnp.exp2(-5.0 - jnp.arange(H_C, dtype=f32)))
    ropea_p, roper_p = _rope_tables(jnp.arange(lp) - PAD)
    srow = jnp.arange(SROWS)
    ropea_s, roper_s = _rope_tables(PAST_LEN + srow % dec_seq)
    dec_p, rsc_p, tail_p, gpow_p = _ret_tables(lg, jnp.zeros((BLK,), jnp.int32), jnp.arange(BLK), float(BLK))
    dec_s, rsc_s, tail_s, gpow_s = _ret_tables(lg, srow // dec_seq, srow % dec_seq, float(dec_seq))
    hid = jnp.arange(1024) // HEAD_DIM_A
    bd = (hid[:, None] == hid[None, :]).astype(bf16)
    tri = (jnp.arange(BLK)[:, None] >= jnp.arange(BLK)[None, :]).astype(bf16)

    ck_flat = cache_k.reshape(depth * dec_batch, w, N_KV_A * HEAD_DIM_A)
    cv_flat = cache_v.reshape(depth * dec_batch, w, N_KV_A * HEAD_DIM_A)
    sh_flat = state_hgrn.reshape(depth * dec_batch, H_B, 128, 128)
    sr_flat = state_ret.reshape(depth * dec_batch, H_C, 128, 128)

    outs_p, outs_s = [], []
    for l in range(depth):
        proj = _inproj(h, w_in, l)
        qn = jnp.tile(q_norm[l], N_HEADS_A)[None]
        kn = jnp.tile(k_norm[l], N_KV_A)[None]
        common = (qn, kn, lbs[l][None], hgrn_norm[l][None], ret_norm[l][None], bd)
        mix, ck, cv, sh, sr = _mixer_prompt(proj, rows, batch, nb, attn_sinks[l], gpow_p, ropea_p, roper_p,
                                            *common, tri, dec_p, rsc_p, tail_p)
        mix, nk, nv, nsh, nsr = _mixer_sample(mix, proj, row_s0, l, dec_batch, dec_seq, attn_sinks[l], gpow_s,
                                              ropea_s, roper_s, *common, dec_s, rsc_s, tail_s,
                                              ck_flat, cv_flat, sh_flat, sr_flat)
        ffn_w = (w_out[l].astype(bf16), norm_ffn[l][None], w_gate[l].astype(bf16), w_up[l].astype(bf16),
                 w_down[l].astype(bf16))
        outs_p.append((ck, cv, sh, sr))
        outs_s.append((nk, nv, nsh, nsr))
        if l + 1 < depth:
            x, h = _ffn(x, mix, *ffn_w, norm_mix[l + 1][None], 0, rows // TM_F)
        else:
            (y_main,) = _ffn(x, mix, *ffn_w, None, 0, rows_main // TM_F)
            (y_tail,) = _ffn(x, mix, *ffn_w, None, rows_main // TM_F, 1)

    y_prompt = y_main.reshape(batch, seq, d)
    y_sample = y_tail[row_s0 - rows_main:row_s0 - rows_main + rows_s].reshape(dec_batch, dec_seq, d)
    kv_shape = (depth, -1, w, N_KV_A, HEAD_DIM_A)
    stack = lambda outs, k: jnp.stack([o[k] for o in outs])
    return (y_prompt, y_sample,
            stack(outs_p, 0).reshape(kv_shape), stack(outs_p, 1).reshape(kv_shape), stack(outs_p, 2), stack(outs_p, 3),
            stack(outs_s, 0).reshape(kv_shape), stack(outs_s, 1).reshape(kv_shape), stack(outs_s, 2), stack(outs_s, 3))
```

```python
import functools

import jax
import jax.numpy as jnp
from jax import lax
from jax.experimental import pallas as pl
from jax.experimental.pallas import tpu as pltpu

f32 = jnp.float32
bf16 = jnp.bfloat16

N_META = 16
EPS = 1e-6
NEG_INF = -1e30
LB_FLOOR = 1e-30
WINDOW = 128
HEAD_DIM_A = 64
N_HEADS_A = 16
N_KV_A = 4
GROUP_A = N_HEADS_A // N_KV_A
N_ROT = 16
ROPE_THETA = 500000.0
H_B = 4
DK_B = 128
H_C = 4
DK_C = 128
RET_THETA = 10000.0
PAST_LEN = 16384

BLK = 128
PAD = BLK - N_META
SUB = 8
SROWS = 16
TM_F = 512
VMEM_LIMIT = 56 * 1024 * 1024

Q_A, K_A, V_A = 0, 1024, 1280
Q_B, F_B, I_B, G_B = 1536, 2048, 2560, 3072
Q_C, K_C, V_C, G_C = 3584, 4096, 4608, 5120
D_IN = 5632
W_A = 1024
NT = (((1,), (1,)), ((), ()))
TN = (((0,), (0,)), ((), ()))


def _dot(a, b, dims=None):
    if dims is None:
        return jnp.dot(a, b, preferred_element_type=f32)
    return lax.dot_general(a, b, dims, preferred_element_type=f32)


def _row_tile(rows, cap):
    best = 8
    for t in range(8, cap + 1, 8):
        if rows % t == 0:
            best = t
    return best


def _silu(x):
    return x * jax.nn.sigmoid(x)


def _rms(x, g):
    return x * lax.rsqrt(jnp.mean(x * x, axis=-1, keepdims=True) + EPS) * g


def _inproj_kernel(h_ref, w_ref, o_ref, wb_ref):
    @pl.when(pl.program_id(1) == 0)
    def _():
        wb_ref[...] = w_ref[...].astype(bf16)

    o_ref[...] = _dot(h_ref[...], wb_ref[...])


def _inproj(h, w_all, layer):
    rows, d = h.shape
    n = w_all.shape[2]
    tm = _row_tile(rows, 1088)
    tn = 1408
    return pl.pallas_call(
        _inproj_kernel,
        grid=(n // tn, rows // tm),
        in_specs=[pl.BlockSpec((tm, d), lambda j, i: (i, 0)),
                  pl.BlockSpec((None, d, tn), lambda j, i: (layer, 0, j))],
        out_specs=pl.BlockSpec((tm, tn), lambda j, i: (i, j)),
        out_shape=jax.ShapeDtypeStruct((rows, n), f32),
        scratch_shapes=[pltpu.VMEM((d, tn), bf16)],
        compiler_params=pltpu.CompilerParams(dimension_semantics=("arbitrary", "arbitrary"),
                                             vmem_limit_bytes=VMEM_LIMIT),
        name="inproj",
    )(h, w_all)


def _embed_kernel(xp_ref, tail_ref, g_ref, x_ref, h_ref):
    i = pl.program_id(0)
    last = pl.num_programs(0) - 1

    def emit(src_ref):
        x = src_ref[...]
        x_ref[...] = x
        h_ref[...] = _rms(x, g_ref[...]).astype(bf16)

    pl.when(i < last)(lambda: emit(xp_ref))
    pl.when(i == last)(lambda: emit(tail_ref))


def _embed(xp, tail, g):
    rows_main, d = xp.shape
    n_main = rows_main // TM_F
    rows = rows_main + TM_F
    return pl.pallas_call(
        _embed_kernel,
        grid=(n_main + 1,),
        in_specs=[pl.BlockSpec((TM_F, d), lambda i: (jnp.minimum(i, n_main - 1), 0)),
                  pl.BlockSpec((TM_F, d), lambda i: (0, 0)),
                  pl.BlockSpec((1, d), lambda i: (0, 0))],
        out_specs=[pl.BlockSpec((TM_F, d), lambda i: (i, 0)),
                   pl.BlockSpec((TM_F, d), lambda i: (i, 0))],
        out_shape=[jax.ShapeDtypeStruct((rows, d), f32), jax.ShapeDtypeStruct((rows, d), bf16)],
        compiler_params=pltpu.CompilerParams(dimension_semantics=("arbitrary",), vmem_limit_bytes=VMEM_LIMIT),
        name="embed_norm",
    )(xp, tail, g)


def _ffn_kernel(*refs, with_next):
    if with_next:
        x_ref, mix_ref, wo_ref, nf_ref, wg_ref, wu_ref, wd_ref, gn_ref, o_ref, hn_ref, h_ref = refs
    else:
        x_ref, mix_ref, wo_ref, nf_ref, wg_ref, wu_ref, wd_ref, o_ref, h_ref = refs
    j = pl.program_id(1)

    @pl.when(j == 0)
    def _():
        x1 = x_ref[...] + _dot(mix_ref[...], wo_ref[...])
        h_ref[...] = _rms(x1, nf_ref[...]).astype(bf16)
        o_ref[...] = x1

    h = h_ref[...]
    a = _silu(_dot(h, wg_ref[...])) * _dot(h, wu_ref[...])
    o_ref[...] += _dot(a.astype(bf16), wd_ref[...])

    if with_next:
        @pl.when(j == pl.num_programs(1) - 1)
        def _():
            hn_ref[...] = _rms(o_ref[...], gn_ref[...]).astype(bf16)


def _ffn(x, mix, wo, nf, wg, wu, wd, g_next, tile0, n_tiles):
    d = x.shape[1]
    dff = wg.shape[1]
    tf = 512
    with_next = g_next is not None
    row_spec = pl.BlockSpec((TM_F, d), lambda i, j: (tile0 + i, 0))
    out_spec = pl.BlockSpec((TM_F, d), lambda i, j: (i, 0))
    in_specs = [row_spec, row_spec,
                pl.BlockSpec((d, d), lambda i, j: (0, 0), pipeline_mode=pl.Buffered(1)),
                pl.BlockSpec((1, d), lambda i, j: (0, 0)),
                pl.BlockSpec((d, tf), lambda i, j: (0, j)),
                pl.BlockSpec((d, tf), lambda i, j: (0, j)),
                pl.BlockSpec((tf, d), lambda i, j: (j, 0))]
    args = [x, mix, wo, nf, wg, wu, wd]
    out_specs = [out_spec]
    out_shape = [jax.ShapeDtypeStruct((n_tiles * TM_F, d), f32)]
    if with_next:
        in_specs.append(pl.BlockSpec((1, d), lambda i, j: (0, 0)))
        args.append(g_next)
        out_specs.append(out_spec)
        out_shape.append(jax.ShapeDtypeStruct((n_tiles * TM_F, d), bf16))
    return pl.pallas_call(
        functools.partial(_ffn_kernel, with_next=with_next),
        grid=(n_tiles, dff // tf),
        in_specs=in_specs,
        out_specs=out_specs,
        out_shape=out_shape,
        scratch_shapes=[pltpu.VMEM((TM_F, d), bf16)],
        compiler_params=pltpu.CompilerParams(dimension_semantics=("arbitrary", "arbitrary"),
                                             vmem_limit_bytes=VMEM_LIMIT),
        name="outproj_ffn",
    )(*args)


def _head_norm(x, sel_ref, exp_ref, g):
    n, w = x.shape
    x2 = x * x
    hi = x2.astype(bf16)
    lo = (x2 - hi.astype(f32)).astype(bf16)
    s = _dot(jnp.concatenate([hi, lo], axis=0), sel_ref[0:w, :])
    r1, r2, r3 = _split3(lax.rsqrt((s[0:n] + s[n:2 * n]) * (1.0 / HEAD_DIM_A) + EPS))
    e = exp_ref[:, 0:w]
    return x * (_dot(r1, e) + _dot(r2, e) + _dot(r3, e)) * g


def _rotate_half(x, perm, exact):
    if not exact:
        return _dot(x.astype(bf16), perm)
    x1, x2, x3 = _split3(x)
    return _dot(x1, perm) + _dot(x2, perm) + _dot(x3, perm)


def _rope_a(x, tab_ref, perm_ref, exact):
    out = []
    for t in range(x.shape[1] // 128):
        xt = x[:, 128 * t:128 * (t + 1)]
        out.append(xt * tab_ref[0] + _rotate_half(xt, perm_ref[0], exact) * tab_ref[1])
    return jnp.concatenate(out, axis=1)


def _rope_r(x, tab_ref, perm_ref):
    return x * tab_ref[0] + _rotate_half(x, perm_ref[1], False) * tab_ref[1]


def _log_forget(fb, lb):
    log_lb = jnp.log(jnp.maximum(lb, LB_FLOOR))
    logsig = jnp.minimum(fb, 0.0) - jnp.log(1.0 + jnp.exp(-jnp.abs(fb)))
    b = jnp.log1p(-lb) + logsig
    return jnp.maximum(log_lb, b) + jnp.log(1.0 + jnp.exp(-jnp.abs(log_lb - b)))


def _rms_mxu(x, g):
    n = x.shape[0]
    x2 = x * x
    hi = x2.astype(bf16)
    lo = (x2 - hi.astype(f32)).astype(bf16)
    s = _dot(jnp.concatenate([hi, lo], axis=0), jnp.ones((128, 128), bf16))
    return x * lax.rsqrt((s[0:n] + s[n:2 * n]) * (1.0 / 128) + EPS) * g


def _split3(x):
    h1 = x.astype(bf16)
    r1 = x - h1.astype(f32)
    h2 = r1.astype(bf16)
    h3 = (r1 - h2.astype(f32)).astype(bf16)
    return h1, h2, h3


def _mixer_prompt_kernel(sink_ref, gpow_ref, proj_ref, ropea_ref, roper_ref, qn_ref, kn_ref, lb_ref, hn_ref, rn_ref,
                         sel_ref, exp_ref, perm_ref, tri_ref, dsel_ref, dec_ref, rsc_ref, tail_ref,
                         wg_ref, wu_ref, wd_ref, wo_ref,
                         mix_ref, ck_ref, cv_ref, sh_ref, sr_ref, wgb_ref, wub_ref, wdb_ref, wob_ref,
                         kk, vv, sht, srs, qs_s, qb_s, kb_s, g_s):
    i = pl.program_id(1)
    nb = pl.num_programs(1)

    @pl.when(i == 0)
    def _():
        kk[...] = jnp.zeros_like(kk)
        vv[...] = jnp.zeros_like(vv)
        for g in range(N_KV_A):
            vv[:, 256 * g + 128:256 * (g + 1)] = jnp.ones((2 * BLK, 128), bf16)
        sht[...] = jnp.zeros_like(sht)
        srs[...] = jnp.zeros_like(srs)

    r2 = lax.broadcasted_iota(jnp.int32, (BLK, BLK), 0)
    c2 = lax.broadcasted_iota(jnp.int32, (BLK, BLK), 1)
    lane_lo = c2 < HEAD_DIM_A
    qa = _rope_a(_head_norm(proj_ref[:, Q_A:Q_A + 1024], sel_ref, exp_ref, qn_ref[...]), ropea_ref, perm_ref, False)
    for t in range(N_HEADS_A // 2):
        xt = qa[:, 128 * t:128 * (t + 1)] * (HEAD_DIM_A ** -0.5)
        qs_s[256 * t:256 * t + 128, :] = jnp.where(lane_lo, xt, 0.0).astype(bf16)
        qs_s[256 * t + 128:256 * (t + 1), :] = jnp.where(lane_lo, 0.0, xt).astype(bf16)
    k_cur = _rope_a(_head_norm(proj_ref[:, K_A:K_A + 256], sel_ref, exp_ref, kn_ref[...]), ropea_ref, perm_ref, True)
    v_cur = proj_ref[:, V_A:V_A + 256]
    ck_ref[0] = k_cur
    cv_ref[0] = v_cur
    for t in range(N_KV_A // 2):
        for src, dst, width in ((k_cur, kk, 128), (v_cur, vv, 256)):
            xt = src[:, 128 * t:128 * (t + 1)]
            xs = pltpu.roll(xt, HEAD_DIM_A, 1)
            dst[BLK:2 * BLK, width * 2 * t:width * 2 * t + 128] = jnp.where(lane_lo, xt, xs).astype(bf16)
            dst[BLK:2 * BLK, width * (2 * t + 1):width * (2 * t + 1) + 128] = jnp.where(lane_lo, xs, xt).astype(bf16)

    grows = GROUP_A * BLK
    r4 = lax.broadcasted_iota(jnp.int32, (grows, BLK), 0) % BLK
    c4 = lax.broadcasted_iota(jnp.int32, (grows, BLK), 1)
    up = c4 > r4
    ok = c4 >= jnp.where(up, PAD - (i - 1) * BLK, PAD - i * BLK)
    s_all, m_all, sk_all = [], [], []
    for g in range(N_KV_A):
        s2 = _dot(qs_s[grows * g:grows * (g + 1), :], kk[:, 128 * g:128 * (g + 1)], NT)
        s_all.append(jnp.where(ok, jnp.where(up, s2[:, 0:BLK], s2[:, BLK:2 * BLK]), NEG_INF))
        sk_all.append(jnp.concatenate([jnp.full((BLK, BLK), sink_ref[GROUP_A * g + j], f32) for j in range(GROUP_A)],
                                      axis=0))
    for g in range(N_KV_A):
        m_all.append(jnp.maximum(jnp.max(s_all[g], axis=-1, keepdims=True), sk_all[g]))
    for g in range(N_KV_A):
        p = jnp.exp(s_all[g] - m_all[g])
        p2 = jnp.concatenate([jnp.where(up, p, 0.0), jnp.where(up, 0.0, p)], axis=1).astype(bf16)
        res = _dot(p2, vv[:, 256 * g:256 * (g + 1)])
        y = res[:, 0:128] / (res[:, 128:256] + jnp.exp(sk_all[g] - m_all[g]))
        for j in range(GROUP_A // 2):
            tile = jnp.where(lane_lo, y[256 * j:256 * j + 128], y[256 * j + 128:256 * (j + 1)])
            col = 128 * (2 * g + j)
            mix_ref[:, col:col + 128] = tile.astype(bf16)
    kk[0:BLK, :] = kk[BLK:2 * BLK, :]
    vv[0:BLK, :] = vv[BLK:2 * BLK, :]

    rowabs = i * BLK + lax.broadcasted_iota(jnp.int32, (BLK, 1), 0)
    logf = jnp.where(rowabs >= PAD, _log_forget(proj_ref[:, F_B:F_B + 512], lb_ref[...]), 0.0)
    kb_s[...] = 1.0 - jnp.exp(logf)
    qb_s[...] = _silu(proj_ref[:, Q_B:Q_B + 512])
    l1, l2, l3 = _split3(logf)
    tri = tri_ref[...]
    g_s[...] = _dot(tri, l1) + _dot(tri, l2) + _dot(tri, l3)

    levels = []
    bz = BLK
    while bz > SUB:
        levels.append((bz, (r2 % bz) >= bz // 2, None if bz == BLK else (r2 // bz) == (c2 // bz)))
        bz //= 2
    same_sub = (r2 // SUB) == (c2 // SUB)
    sub_ge = [(r2 % SUB) >= s for s in range(SUB)]

    def retention_head(h):
        qc = _rope_r(proj_ref[:, Q_C + 128 * h:Q_C + 128 * (h + 1)], roper_ref, perm_ref)
        kc = _rope_r(proj_ref[:, K_C + 128 * h:K_C + 128 * (h + 1)], roper_ref, perm_ref) * (DK_C ** -0.5)
        vc = proj_ref[:, V_C + 128 * h:V_C + 128 * (h + 1)].astype(bf16)
        qcb = qc.astype(bf16)
        sc = _dot(qcb, kc.astype(bf16), NT) * dec_ref[h]
        oc = _dot(qcb, srs[h].astype(bf16)) * rsc_ref[h] + _dot(sc.astype(bf16), vc)
        srs[h] = gpow_ref[h] * srs[h] + _dot((kc * tail_ref[h]).astype(bf16), vc, TN)
        yc = _rms_mxu(oc, rn_ref[...]) * _silu(proj_ref[:, G_C + 128 * h:G_C + 128 * (h + 1)])
        mix_ref[:, W_A + 512 + 128 * h:W_A + 512 + 128 * (h + 1)] = yc.astype(bf16)

    for h in range(H_B):
        retention_head(h)
        sl = slice(128 * h, 128 * (h + 1))
        gc = g_s[:, sl]
        qb = qb_s[:, sl]
        kb = kb_s[:, sl]
        vb = proj_ref[:, I_B + 128 * h:I_B + 128 * (h + 1)].astype(bf16)
        inter = _dot((qb * jnp.exp(gc)).astype(bf16), sht[h].astype(bf16), NT)
        amat = None
        for bz, upper, same in levels:
            gref = jnp.concatenate([jnp.broadcast_to(g_s[pl.ds(st + bz // 2 - 1, 1), sl], (bz, 128))
                                    for st in range(0, BLK, bz)], axis=0)
            dg = gc - gref
            e = jnp.exp(jnp.where(upper, dg, -dg))
            qt = jnp.where(upper, qb * e, 0.0).astype(bf16)
            kt = jnp.where(upper, 0.0, kb * e).astype(bf16)
            pm = _dot(qt, kt, NT)
            if same is not None:
                pm = jnp.where(same, pm, 0.0)
            amat = pm if amat is None else amat + pm
        ys = []
        for s in range(SUB):
            ks = jnp.concatenate([jnp.broadcast_to(kb_s[pl.ds(SUB * j + s, 1), sl], (SUB, 128))
                                  for j in range(BLK // SUB)], axis=0)
            gs = jnp.concatenate([jnp.broadcast_to(g_s[pl.ds(SUB * j + s, 1), sl], (SUB, 128))
                                  for j in range(BLK // SUB)], axis=0)
            ys.append((qb * ks * jnp.exp(jnp.where(sub_ge[s], gc - gs, NEG_INF))).astype(bf16))
        amat = amat + jnp.where(same_sub, _dot(jnp.concatenate(ys, axis=1), dsel_ref[...]), 0.0)

        ob = inter + _dot(amat.astype(bf16), vb)
        glast = g_s[pl.ds(BLK - 1, 1), sl]
        kt2 = (kb * jnp.exp(glast - gc)).astype(bf16)
        sht[h] = sht[h] * jnp.exp(glast) + _dot(vb, kt2, TN)
        yb = _rms_mxu(ob, hn_ref[...]) * _silu(proj_ref[:, G_B + 128 * h:G_B + 128 * (h + 1)])
        mix_ref[:, W_A + 128 * h:W_A + 128 * (h + 1)] = yb.astype(bf16)

    wgb_ref[...] = wg_ref[...].astype(bf16)
    wub_ref[...] = wu_ref[...].astype(bf16)
    wdb_ref[...] = wd_ref[...].astype(bf16)
    wob_ref[...] = wo_ref[...].astype(bf16)

    @pl.when(i == nb - 1)
    def _():
        for h in range(H_B):
            sh_ref[0, h] = sht[h].T
            sr_ref[0, h] = srs[h]


def _const_spec(shape):
    nd = len(shape)
    return pl.BlockSpec(shape, lambda *_: (0,) * nd)


def _slab_count(nrows, limit):
    n = 1
    while 2 * n <= limit and nrows % (2 * n) == 0 and (nrows // (2 * n)) % 16 == 0:
        n *= 2
    return n


def _mixer_prompt(proj, rows_total, batch, nb, layer, sinks, gpow, ropea, roper, qn, kn, lb, hn, rn, sel, expand, perm,
                  tri, dsel, dec, rsc, tail, w_gate, w_up, w_down, w_out):
    smem = pl.BlockSpec(memory_space=pltpu.SMEM)
    d, dff = w_gate.shape[1], w_gate.shape[2]
    n_cast = _slab_count(d // 2, batch * nb)

    def row_blk(b, i):
        return jnp.where(i == 0, batch * (nb - 1) + b, b * (nb - 1) + i - 1)

    def slab_in(nrows, ncols):
        return pl.BlockSpec((None, nrows // n_cast, ncols),
                            lambda b, i: (layer, jnp.minimum(b * nb + i, n_cast - 1), 0))

    def slab_out(nrows, ncols):
        return pl.BlockSpec((nrows // n_cast, ncols), lambda b, i: (jnp.minimum(b * nb + i, n_cast - 1), 0))

    return pl.pallas_call(
        _mixer_prompt_kernel,
        grid=(batch, nb),
        in_specs=[smem, smem,
                  pl.BlockSpec((BLK, D_IN), lambda b, i: (row_blk(b, i), 0)),
                  pl.BlockSpec((2, BLK, 128), lambda b, i: (0, i, 0)),
                  pl.BlockSpec((2, BLK, 128), lambda b, i: (0, i, 0)),
                  _const_spec((1, 1024)), _const_spec((1, 256)), _const_spec((1, 512)),
                  _const_spec((1, 128)), _const_spec((1, 128)),
                  _const_spec((1024, 128)), _const_spec((128, 1024)), _const_spec((2, 128, 128)),
                  _const_spec((BLK, BLK)), _const_spec((SUB * 128, 128)),
                  _const_spec((H_C, BLK, BLK)), _const_spec((H_C, BLK, 128)), _const_spec((H_C, BLK, 128)),
                  slab_in(d, dff), slab_in(d, dff), slab_in(d // 2, 2 * dff), slab_in(d, d)],
        out_specs=[pl.BlockSpec((BLK, 2048), lambda b, i: (row_blk(b, i), 0)),
                   pl.BlockSpec((1, BLK, 256), lambda b, i: (b, 0, 0)),
                   pl.BlockSpec((1, BLK, 256), lambda b, i: (b, 0, 0)),
                   pl.BlockSpec((1, H_B, 128, 128), lambda b, i: (b, 0, 0, 0)),
                   pl.BlockSpec((1, H_C, 128, 128), lambda b, i: (b, 0, 0, 0)),
                   slab_out(d, dff), slab_out(d, dff), slab_out(d // 2, 2 * dff), slab_out(d, d)],
        out_shape=[jax.ShapeDtypeStruct((rows_total, 2048), bf16),
                   jax.ShapeDtypeStruct((batch, BLK, 256), f32),
                   jax.ShapeDtypeStruct((batch, BLK, 256), f32),
                   jax.ShapeDtypeStruct((batch, H_B, 128, 128), f32),
                   jax.ShapeDtypeStruct((batch, H_C, 128, 128), f32),
                   jax.ShapeDtypeStruct((d, dff), bf16), jax.ShapeDtypeStruct((d, dff), bf16),
                   jax.ShapeDtypeStruct((d // 2, 2 * dff), bf16), jax.ShapeDtypeStruct((d, d), bf16)],
        scratch_shapes=[pltpu.VMEM((2 * BLK, N_KV_A * 128), bf16), pltpu.VMEM((2 * BLK, N_KV_A * 256), bf16),
                        pltpu.VMEM((H_B, 128, 128), f32), pltpu.VMEM((H_C, 128, 128), f32),
                        pltpu.VMEM((N_HEADS_A * BLK, 128), bf16),
                        pltpu.VMEM((BLK, 512), f32), pltpu.VMEM((BLK, 512), f32), pltpu.VMEM((BLK, 512), f32)],
        compiler_params=pltpu.CompilerParams(dimension_semantics=("arbitrary", "arbitrary"),
                                             vmem_limit_bytes=VMEM_LIMIT),
        name="mixer_prompt",
    )(sinks, gpow, proj, ropea, roper, qn, kn, lb, hn, rn, sel, expand, perm, tri, dsel, dec, rsc, tail,
      w_gate, w_up, w_down, w_out)


def _mixer_sample_kernel(sink_ref, gpow_ref, mixin_ref, *refs, dec_seq, steps):
    del mixin_ref
    c = pl.program_id(0)
    mix_ref = refs[18]
    pl.when(c < steps)(lambda: _mixer_sample_body(sink_ref, gpow_ref, *refs, dec_seq=dec_seq))

    @pl.when(c >= steps)
    def _():
        mix_ref[...] = jnp.zeros_like(mix_ref)


def _mixer_sample_body(sink_ref, gpow_ref, proj_ref, ropea_ref, roper_ref, qn_ref, kn_ref, lb_ref,
                       hn_ref, rn_ref, sel_ref, exp_ref, perm_ref, dec_ref, rsc_ref, tail_ref,
                       ck_ref, cv_ref, sh_ref, sr_ref,
                       mix_ref, nk_ref, nv_ref, nsh_ref, nsr_ref, *, dec_seq):
    nbat = SROWS // dec_seq
    w = ck_ref.shape[1]
    rb = lax.broadcasted_iota(jnp.int32, (SROWS, 1), 0) // dec_seq
    rt = lax.broadcasted_iota(jnp.int32, (SROWS, 1), 0) % dec_seq

    def pick(parts):
        out = parts[nbat - 1]
        for b in range(nbat - 2, -1, -1):
            out = jnp.where(rb == b, parts[b], out)
        return out

    qa = _rope_a(_head_norm(proj_ref[:, Q_A:Q_A + 1024], sel_ref, exp_ref, qn_ref[...]), ropea_ref, perm_ref, False)
    qs = (qa * (HEAD_DIM_A ** -0.5)).astype(bf16)
    k_new = _rope_a(_head_norm(proj_ref[:, K_A:K_A + 256], sel_ref, exp_ref, kn_ref[...]), ropea_ref, perm_ref, True)
    v_new = proj_ref[:, V_A:V_A + 256]
    for b in range(nbat):
        nk_ref[b, 0:w - dec_seq, :] = ck_ref[b, dec_seq:w, :]
        nk_ref[b, w - dec_seq:w, :] = k_new[dec_seq * b:dec_seq * (b + 1), :]
        nv_ref[b, 0:w - dec_seq, :] = cv_ref[b, dec_seq:w, :]
        nv_ref[b, w - dec_seq:w, :] = v_new[dec_seq * b:dec_seq * (b + 1), :]
    knb = k_new.astype(bf16)
    vnb = v_new.astype(bf16)

    grows = GROUP_A * SROWS
    rt_g = jnp.concatenate([rt] * GROUP_A, axis=0)
    rb_g = jnp.concatenate([rb] * GROUP_A, axis=0)
    jc = lax.broadcasted_iota(jnp.int32, (grows, w), 1)
    valid_c = jc > rt_g + (w - WINDOW)
    cn = lax.broadcasted_iota(jnp.int32, (grows, SROWS), 1)
    valid_n = (cn // dec_seq == rb_g) & (cn % dec_seq <= rt_g)
    heads = [None] * N_HEADS_A
    for g in range(N_KV_A):
        ksl = slice(64 * g, 64 * (g + 1))
        qg = jnp.concatenate([qs[:, 64 * h:64 * (h + 1)] for h in range(GROUP_A * g, GROUP_A * (g + 1))], axis=0)
        sparts = [_dot(qg, ck_ref[b, :, ksl].astype(bf16), NT) for b in range(nbat)]
        s_c = sparts[nbat - 1]
        for b in range(nbat - 2, -1, -1):
            s_c = jnp.where(rb_g == b, sparts[b], s_c)
        s_c = jnp.where(valid_c, s_c, NEG_INF)
        s_n = jnp.where(valid_n, _dot(qg, knb[:, ksl], NT), NEG_INF)
        sk = jnp.concatenate([jnp.full((SROWS, 1), sink_ref[h], f32)
                              for h in range(GROUP_A * g, GROUP_A * (g + 1))], axis=0)
        m = jnp.maximum(jnp.maximum(jnp.max(s_c, axis=-1, keepdims=True), jnp.max(s_n, axis=-1, keepdims=True)), sk)
        p_c = jnp.exp(s_c - m)
        p_n = jnp.exp(s_n - m)
        den = jnp.sum(p_c, axis=-1, keepdims=True) + jnp.sum(p_n, axis=-1, keepdims=True) + jnp.exp(sk - m)
        pcb = p_c.astype(bf16)
        oparts = [_dot(pcb, cv_ref[b, :, ksl].astype(bf16)) for b in range(nbat)]
        o = oparts[nbat - 1]
        for b in range(nbat - 2, -1, -1):
            o = jnp.where(rb_g == b, oparts[b], o)
        o = (o + _dot(p_n.astype(bf16), vnb[:, ksl])) / den
        for hh in range(GROUP_A):
            heads[GROUP_A * g + hh] = o[SROWS * hh:SROWS * (hh + 1), :]
    mix_ref[:, 0:W_A] = jnp.concatenate(heads, axis=1).astype(bf16)

    logf = _log_forget(proj_ref[:, F_B:F_B + 512], lb_ref[...])
    kb = 1.0 - jnp.exp(logf)
    qb = _silu(proj_ref[:, Q_B:Q_B + 512])
    vb = proj_ref[:, I_B:I_B + 512]
    gcum = logf
    for d in range(1, dec_seq):
        gcum = gcum + jnp.where(rt >= d, pltpu.roll(logf, d, 0), 0.0)
    cb = lax.broadcasted_iota(jnp.int32, (1, SROWS), 1) // dec_seq
    ct = lax.broadcasted_iota(jnp.int32, (1, SROWS), 1) % dec_seq
    intra = [jnp.zeros((SROWS, 128), f32) for _ in range(H_B)]
    for s in range(SROWS):
        ok = (rb == s // dec_seq) & (rt >= s % dec_seq)
        y = qb * kb[s:s + 1, :] * jnp.exp(jnp.where(ok, gcum - gcum[s:s + 1, :], NEG_INF))
        for h in range(H_B):
            sl = slice(128 * h, 128 * (h + 1))
            intra[h] = intra[h] + jnp.sum(y[:, sl], axis=-1, keepdims=True) * vb[s:s + 1, sl]
    qg = (qb * jnp.exp(gcum)).astype(bf16)
    vbb = vb.astype(bf16)
    glast = [gcum[dec_seq * (b + 1) - 1:dec_seq * (b + 1), :] for b in range(nbat)]
    for h in range(H_B):
        sl = slice(128 * h, 128 * (h + 1))
        ob = pick([_dot(qg[:, sl], sh_ref[b, h].astype(bf16)) for b in range(nbat)]) + intra[h]
        for b in range(nbat):
            kt2 = jnp.where(rb == b, kb[:, sl] * jnp.exp(glast[b][:, sl] - gcum[:, sl]), 0.0).astype(bf16)
            dcol = jnp.broadcast_to(jnp.exp(glast[b][:, sl]), (128, 128)).T
            nsh_ref[b, h] = dcol * sh_ref[b, h] + _dot(kt2, vbb[:, sl], TN)
        yb = _rms(ob, hn_ref[...]) * _silu(proj_ref[:, G_B + 128 * h:G_B + 128 * (h + 1)])
        mix_ref[:, W_A + 128 * h:W_A + 128 * (h + 1)] = yb.astype(bf16)

    for h in range(H_C):
        qc = _rope_r(proj_ref[:, Q_C + 128 * h:Q_C + 128 * (h + 1)], roper_ref, perm_ref)
        kc = _rope_r(proj_ref[:, K_C + 128 * h:K_C + 128 * (h + 1)], roper_ref, perm_ref) * (DK_C ** -0.5)
        vc = proj_ref[:, V_C + 128 * h:V_C + 128 * (h + 1)]
        qcb = qc.astype(bf16)
        vcb = vc.astype(bf16)
        sc = _dot(qcb, kc.astype(bf16), NT) * dec_ref[h]
        oc = pick([_dot(qcb, sr_ref[b, h].astype(bf16)) for b in range(nbat)]) * rsc_ref[h] \
            + _dot(sc.astype(bf16), vcb)
        kct = kc * tail_ref[h]
        for b in range(nbat):
            nsr_ref[b, h] = gpow_ref[h] * sr_ref[b, h] + _dot(jnp.where(rb == b, kct, 0.0).astype(bf16), vcb, TN)
        yc = _rms(oc, rn_ref[...]) * _silu(proj_ref[:, G_C + 128 * h:G_C + 128 * (h + 1)])
        mix_ref[:, W_A + 512 + 128 * h:W_A + 512 + 128 * (h + 1)] = yc.astype(bf16)


def _mixer_sample(mix, proj, row0, layer, dec_batch, dec_seq, sinks, gpow, ropea, roper, qn, kn, lb, hn, rn, sel, expand, perm,
                  dec, rsc, tail, cache_k, cache_v, state_h, state_r):
    smem = pl.BlockSpec(memory_space=pltpu.SMEM)
    nbat = SROWS // dec_seq
    steps = dec_batch // nbat
    blk0 = row0 // SROWS
    w = cache_k.shape[1]
    fill_steps = (mix.shape[0] - row0) // SROWS - steps
    cl = lambda c: jnp.minimum(c, steps - 1)
    return pl.pallas_call(
        functools.partial(_mixer_sample_kernel, dec_seq=dec_seq, steps=steps),
        grid=(steps + fill_steps,),
        in_specs=[smem, smem,
                  pl.BlockSpec(memory_space=pl.ANY),
                  pl.BlockSpec((SROWS, D_IN), lambda c: (blk0 + cl(c), 0)),
                  _const_spec((2, SROWS, 128)), _const_spec((2, SROWS, 128)),
                  _const_spec((1, 1024)), _const_spec((1, 256)), _const_spec((1, 512)),
                  _const_spec((1, 128)), _const_spec((1, 128)),
                  _const_spec((1024, 128)), _const_spec((128, 1024)), _const_spec((2, 128, 128)),
                  _const_spec((H_C, SROWS, SROWS)), _const_spec((H_C, SROWS, 128)), _const_spec((H_C, SROWS, 128)),
                  pl.BlockSpec((nbat, w, 256), lambda c: (layer * steps + cl(c), 0, 0)),
                  pl.BlockSpec((nbat, w, 256), lambda c: (layer * steps + cl(c), 0, 0)),
                  pl.BlockSpec((nbat, H_B, 128, 128), lambda c: (layer * steps + cl(c), 0, 0, 0)),
                  pl.BlockSpec((nbat, H_C, 128, 128), lambda c: (layer * steps + cl(c), 0, 0, 0))],
        out_specs=[pl.BlockSpec((SROWS, 2048), lambda c: (blk0 + c, 0)),
                   pl.BlockSpec((nbat, w, 256), lambda c: (cl(c), 0, 0)),
                   pl.BlockSpec((nbat, w, 256), lambda c: (cl(c), 0, 0)),
                   pl.BlockSpec((nbat, H_B, 128, 128), lambda c: (cl(c), 0, 0, 0)),
                   pl.BlockSpec((nbat, H_C, 128, 128), lambda c: (cl(c), 0, 0, 0))],
        out_shape=[jax.ShapeDtypeStruct(mix.shape, bf16),
                   jax.ShapeDtypeStruct((dec_batch, w, 256), f32),
                   jax.ShapeDtypeStruct((dec_batch, w, 256), f32),
                   jax.ShapeDtypeStruct((dec_batch, H_B, 128, 128), f32),
                   jax.ShapeDtypeStruct((dec_batch, H_C, 128, 128), f32)],
        input_output_aliases={2: 0},
        compiler_params=pltpu.CompilerParams(dimension_semantics=("arbitrary",), vmem_limit_bytes=VMEM_LIMIT),
        name="mixer_sample",
    )(sinks, gpow, mix, proj, ropea, roper, qn, kn, lb, hn, rn, sel, expand, perm, dec, rsc, tail,
      cache_k, cache_v, state_h, state_r)


def _rope_tables(pos):
    posf = pos.astype(f32)[:, None]
    t = pos.shape[0]
    half = N_ROT // 2
    inv = ROPE_THETA ** (-jnp.arange(half, dtype=f32) * (2.0 / N_ROT))
    ang = posf * inv[None, :]
    cos, sin = jnp.cos(ang), jnp.sin(ang)
    rest0 = jnp.zeros((t, HEAD_DIM_A - N_ROT), f32)
    c64 = jnp.concatenate([cos, cos, rest0 + 1.0], axis=1)
    s64 = jnp.concatenate([sin, sin, rest0], axis=1)
    ropea = jnp.stack([jnp.tile(c64, (1, 2)), jnp.tile(s64, (1, 2))])
    invr = RET_THETA ** (-jnp.arange(DK_C // 2, dtype=f32) * (2.0 / DK_C))
    angr = posf * invr[None, :]
    cr, sr = jnp.cos(angr), jnp.sin(angr)
    roper = jnp.stack([jnp.concatenate([cr, cr], axis=1), jnp.concatenate([sr, sr], axis=1)])
    return ropea, roper


def _rotate_half_perms():
    k = jnp.arange(128)[:, None]
    l = jnp.arange(128)[None, :]
    half = N_ROT // 2
    da = l % HEAD_DIM_A
    pa = jnp.where((da < half) & (k == l + half), -1.0, 0.0) + jnp.where((da >= half) & (da < N_ROT) & (k == l - half),
                                                                          1.0, 0.0)
    hr = DK_C // 2
    pr = jnp.where((l < hr) & (k == l + hr), -1.0, 0.0) + jnp.where((l >= hr) & (k == l - hr), 1.0, 0.0)
    return jnp.stack([pa, pr]).astype(bf16)


def _ret_tables(lg, seq_of_row, tok_of_row, length):
    tq = tok_of_row.astype(f32)
    rel = tq[:, None] - tq[None, :]
    ok = (seq_of_row[:, None] == seq_of_row[None, :]) & (rel >= 0)
    dec = jnp.where(ok[None], jnp.exp(jnp.where(ok, rel, 0.0)[None] * lg[:, None, None]), 0.0)
    n = tq.shape[0]
    rsc = jnp.broadcast_to(jnp.exp((tq + 1.0)[None, :, None] * lg[:, None, None]), (H_C, n, 128))
    tail = jnp.broadcast_to(jnp.exp((length - 1.0 - tq)[None, :, None] * lg[:, None, None]), (H_C, n, 128))
    gpow = jnp.exp(length * lg)
    return dec, rsc, tail, gpow


def kernel(x_prompt, x_sample, cache_k, cache_v, state_hgrn, state_ret, meta_tokens, norm_mix, norm_ffn, w_in, q_norm, k_norm, attn_sinks, hgrn_lb, hgrn_norm, ret_norm, w_out, w_gate, w_up, w_down):
    batch, seq, d = x_prompt.shape
    dec_batch, dec_seq, _ = x_sample.shape
    depth = w_in.shape[0]
    w = cache_k.shape[2]
    assert d == 2048 and w_in.shape[2] == D_IN and seq % BLK == 0
    assert SROWS % dec_seq == 0 and dec_batch % (SROWS // dec_seq) == 0 and w == WINDOW
    nb = seq // BLK + 1
    lp = nb * BLK
    rows_main = batch * seq
    rows_s = dec_batch * dec_seq
    row_s0 = rows_main + batch * BLK
    rows = rows_main + TM_F
    assert rows_main % TM_F == 0 and batch * BLK + rows_s <= TM_F and (rows - row_s0) % SROWS == 0

    meta_blk = jnp.concatenate([jnp.zeros((PAD, d), f32), meta_tokens.astype(f32)], axis=0)
    tail = jnp.concatenate([jnp.tile(meta_blk, (batch, 1)), x_sample.reshape(rows_s, d),
                            jnp.zeros((rows - row_s0 - rows_s, d), f32)], axis=0)
    x, h = _embed(x_prompt.reshape(rows_main, d), tail, norm_mix[0][None])

    p = jax.nn.softmax(hgrn_lb.astype(f32), axis=0)
    lbs = jnp.cumsum(p, axis=0) - p[0]
    lg = jnp.log1p(-jnp.exp2(-5.0 - jnp.arange(H_C, dtype=f32)))
    ropea_p, roper_p = _rope_tables(jnp.arange(lp) - PAD)
    srow = jnp.arange(SROWS)
    ropea_s, roper_s = _rope_tables(PAST_LEN + srow % dec_seq)
    dec_p, rsc_p, tail_p, gpow_p = _ret_tables(lg, jnp.zeros((BLK,), jnp.int32), jnp.arange(BLK), float(BLK))
    dec_s, rsc_s, tail_s, gpow_s = _ret_tables(lg, srow // dec_seq, srow % dec_seq, float(dec_seq))
    sel = (jnp.arange(1024)[:, None] // HEAD_DIM_A == jnp.arange(128)[None, :]).astype(bf16)
    expand = sel.T
    tri = (jnp.arange(BLK)[:, None] >= jnp.arange(BLK)[None, :]).astype(bf16)
    dsel = (jnp.arange(SUB * 128)[:, None] // 128 == jnp.arange(128)[None, :] % SUB).astype(bf16)
    perm = _rotate_half_perms()

    ck_flat = cache_k.reshape(depth * dec_batch, w, N_KV_A * HEAD_DIM_A)
    cv_flat = cache_v.reshape(depth * dec_batch, w, N_KV_A * HEAD_DIM_A)
    sh_flat = state_hgrn.reshape(depth * dec_batch, H_B, 128, 128)
    sr_flat = state_ret.reshape(depth * dec_batch, H_C, 128, 128)

    outs_p, outs_s = [], []
    for l in range(depth):
        proj = _inproj(h, w_in, l)
        qn = jnp.tile(q_norm[l], N_HEADS_A)[None]
        kn = jnp.tile(k_norm[l], N_KV_A)[None]
        common = (qn, kn, lbs[l][None], hgrn_norm[l][None], ret_norm[l][None], sel, expand, perm)
        mix, ck, cv, sh, sr, wg_b, wu_b, wd_b, wo_b = _mixer_prompt(
            proj, rows, batch, nb, l, attn_sinks[l], gpow_p, ropea_p, roper_p, *common, tri, dsel,
            dec_p, rsc_p, tail_p, w_gate, w_up, w_down.reshape(depth, d // 2, 2 * w_down.shape[1]), w_out)
        wd_b = wd_b.reshape(w_down.shape[1], d)
        mix, nk, nv, nsh, nsr = _mixer_sample(mix, proj, row_s0, l, dec_batch, dec_seq, attn_sinks[l], gpow_s,
                                              ropea_s, roper_s, *common, dec_s, rsc_s, tail_s,
                                              ck_flat, cv_flat, sh_flat, sr_flat)
        ffn_w = (wo_b, norm_ffn[l][None], wg_b, wu_b, wd_b)
        outs_p.append((ck, cv, sh, sr))
        outs_s.append((nk, nv, nsh, nsr))
        if l + 1 < depth:
            x, h = _ffn(x, mix, *ffn_w, norm_mix[l + 1][None], 0, rows // TM_F)
        else:
            (y_main,) = _ffn(x, mix, *ffn_w, None, 0, rows_main // TM_F)
            (y_tail,) = _ffn(x, mix, *ffn_w, None, rows_main // TM_F, 1)

    y_prompt = y_main.reshape(batch, seq, d)
    y_sample = y_tail[row_s0 - rows_main:row_s0 - rows_main + rows_s].reshape(dec_batch, dec_seq, d)
    kv_shape = (depth, -1, w, N_KV_A, HEAD_DIM_A)
    stack = lambda outs, k: jnp.stack([o[k] for o in outs])
    return (y_prompt, y_sample,
            stack(outs_p, 0).reshape(kv_shape), stack(outs_p, 1).reshape(kv_shape), stack(outs_p, 2), stack(outs_p, 3),
            stack(outs_s, 0).reshape(kv_shape), stack(outs_s, 1).reshape(kv_shape), stack(outs_s, 2), stack(outs_s, 3))
```

```python
import functools

import jax
import jax.numpy as jnp
import numpy as np
from jax import lax
from jax.experimental import pallas as pl
from jax.experimental.pallas import tpu as pltpu

f32 = jnp.float32
bf16 = jnp.bfloat16

N_META = 16
EPS = 1e-6
NEG_INF = -1e30
LB_FLOOR = 1e-30
WINDOW = 128
HEAD_DIM_A = 64
N_HEADS_A = 16
N_KV_A = 4
GROUP_A = N_HEADS_A // N_KV_A
N_ROT = 16
ROPE_THETA = 500000.0
H_B = 4
DK_B = 128
H_C = 4
DK_C = 128
RET_THETA = 10000.0
PAST_LEN = 16384

BLK = 128
PAD = BLK - N_META
SUB = 8
SROWS = 16
TM_F = 512
VMEM_LIMIT = 56 * 1024 * 1024

Q_A, K_A, V_A = 0, 1024, 1280
Q_B, F_B, I_B, G_B = 1536, 2048, 2560, 3072
Q_C, K_C, V_C, G_C = 3584, 4096, 4608, 5120
D_IN = 5632
W_A = 1024
NT = (((1,), (1,)), ((), ()))
TN = (((0,), (0,)), ((), ()))


def _dot(a, b, dims=None):
    if dims is None:
        return jnp.dot(a, b, preferred_element_type=f32)
    return lax.dot_general(a, b, dims, preferred_element_type=f32)


def _row_tile(rows, cap):
    best = 8
    for t in range(8, cap + 1, 8):
        if rows % t == 0:
            best = t
    return best


def _silu(x):
    return x * jax.nn.sigmoid(x)


def _rms(x, g):
    return x * lax.rsqrt(jnp.mean(x * x, axis=-1, keepdims=True) + EPS) * g


def _inproj_kernel(h_ref, w_ref, o_ref, wb_ref):
    @pl.when(pl.program_id(1) == 0)
    def _():
        wb_ref[...] = w_ref[...].astype(bf16)

    o_ref[...] = _dot(h_ref[...], wb_ref[...])


def _inproj(h, w_all, layer):
    rows, d = h.shape
    n = w_all.shape[2]
    tm = _row_tile(rows, 1088)
    tn = 1408
    return pl.pallas_call(
        _inproj_kernel,
        grid=(n // tn, rows // tm),
        in_specs=[pl.BlockSpec((tm, d), lambda j, i: (i, 0)),
                  pl.BlockSpec((None, d, tn), lambda j, i: (layer, 0, j))],
        out_specs=pl.BlockSpec((tm, tn), lambda j, i: (i, j)),
        out_shape=jax.ShapeDtypeStruct((rows, n), f32),
        scratch_shapes=[pltpu.VMEM((d, tn), bf16)],
        compiler_params=pltpu.CompilerParams(dimension_semantics=("arbitrary", "arbitrary"),
                                             vmem_limit_bytes=VMEM_LIMIT),
        name="inproj",
    )(h, w_all)


def _embed_kernel(xp_ref, tail_ref, g_ref, x_ref, h_ref):
    i = pl.program_id(0)
    last = pl.num_programs(0) - 1

    def emit(src_ref):
        x = src_ref[...]
        x_ref[...] = x
        h_ref[...] = _rms(x, g_ref[...]).astype(bf16)

    pl.when(i < last)(lambda: emit(xp_ref))
    pl.when(i == last)(lambda: emit(tail_ref))


def _embed(xp, tail, g):
    rows_main, d = xp.shape
    n_main = rows_main // TM_F
    rows = rows_main + TM_F
    return pl.pallas_call(
        _embed_kernel,
        grid=(n_main + 1,),
        in_specs=[pl.BlockSpec((TM_F, d), lambda i: (jnp.minimum(i, n_main - 1), 0)),
                  pl.BlockSpec((TM_F, d), lambda i: (0, 0)),
                  pl.BlockSpec((1, d), lambda i: (0, 0))],
        out_specs=[pl.BlockSpec((TM_F, d), lambda i: (i, 0)),
                   pl.BlockSpec((TM_F, d), lambda i: (i, 0))],
        out_shape=[jax.ShapeDtypeStruct((rows, d), f32), jax.ShapeDtypeStruct((rows, d), bf16)],
        compiler_params=pltpu.CompilerParams(dimension_semantics=("arbitrary",), vmem_limit_bytes=VMEM_LIMIT),
        name="embed_norm",
    )(xp, tail, g)


def _ffn_kernel(*refs, with_next):
    if with_next:
        x_ref, mix_ref, wo_ref, nf_ref, wg_ref, wu_ref, wd_ref, gn_ref, o_ref, hn_ref, h_ref = refs
    else:
        x_ref, mix_ref, wo_ref, nf_ref, wg_ref, wu_ref, wd_ref, o_ref, h_ref = refs
    j = pl.program_id(1)

    @pl.when(j == 0)
    def _():
        x1 = x_ref[...] + _dot(mix_ref[...], wo_ref[...])
        h_ref[...] = _rms(x1, nf_ref[...]).astype(bf16)
        o_ref[...] = x1

    h = h_ref[...]
    a = _silu(_dot(h, wg_ref[...])) * _dot(h, wu_ref[...])
    o_ref[...] += _dot(a.astype(bf16), wd_ref[...])

    if with_next:
        @pl.when(j == pl.num_programs(1) - 1)
        def _():
            hn_ref[...] = _rms(o_ref[...], gn_ref[...]).astype(bf16)


def _ffn(x, mix, wo, nf, wg, wu, wd, g_next, tile0, n_tiles):
    d = x.shape[1]
    dff = wg.shape[1]
    tf = 512
    with_next = g_next is not None
    row_spec = pl.BlockSpec((TM_F, d), lambda i, j: (tile0 + i, 0))
    out_spec = pl.BlockSpec((TM_F, d), lambda i, j: (i, 0))
    in_specs = [row_spec, row_spec,
                pl.BlockSpec((d, d), lambda i, j: (0, 0), pipeline_mode=pl.Buffered(1)),
                pl.BlockSpec((1, d), lambda i, j: (0, 0)),
                pl.BlockSpec((d, tf), lambda i, j: (0, j)),
                pl.BlockSpec((d, tf), lambda i, j: (0, j)),
                pl.BlockSpec((tf, d), lambda i, j: (j, 0))]
    args = [x, mix, wo, nf, wg, wu, wd]
    out_specs = [out_spec]
    out_shape = [jax.ShapeDtypeStruct((n_tiles * TM_F, d), f32)]
    if with_next:
        in_specs.append(pl.BlockSpec((1, d), lambda i, j: (0, 0)))
        args.append(g_next)
        out_specs.append(out_spec)
        out_shape.append(jax.ShapeDtypeStruct((n_tiles * TM_F, d), bf16))
    return pl.pallas_call(
        functools.partial(_ffn_kernel, with_next=with_next),
        grid=(n_tiles, dff // tf),
        in_specs=in_specs,
        out_specs=out_specs,
        out_shape=out_shape,
        scratch_shapes=[pltpu.VMEM((TM_F, d), bf16)],
        compiler_params=pltpu.CompilerParams(dimension_semantics=("arbitrary", "arbitrary"),
                                             vmem_limit_bytes=VMEM_LIMIT),
        name="outproj_ffn",
    )(*args)


def _head_norm(x, sel_ref, exp_ref, g):
    n, w = x.shape
    x2 = x * x
    hi = x2.astype(bf16)
    lo = (x2 - hi.astype(f32)).astype(bf16)
    s = _dot(jnp.concatenate([hi, lo], axis=0), sel_ref[0:w, :])
    return x * _dot_split3(lax.rsqrt((s[0:n] + s[n:2 * n]) * (1.0 / HEAD_DIM_A) + EPS), exp_ref[:, 0:w]) * g


def _dot_split3(x, m):
    n = x.shape[0]
    y = _dot(jnp.concatenate(_split3(x), axis=0), m)
    return y[0:n] + y[n:2 * n] + y[2 * n:3 * n]


def _rope_a(x, tab_ref):
    half = N_ROT // 2
    first = lax.broadcasted_iota(jnp.int32, (x.shape[0], 128), 1) % HEAD_DIM_A < half
    out = []
    for t in range(x.shape[1] // 128):
        xt = x[:, 128 * t:128 * (t + 1)]
        partner = jnp.where(first, -pltpu.roll(xt, 128 - half, 1), pltpu.roll(xt, half, 1))
        out.append(xt * tab_ref[0] + partner * tab_ref[1])
    return jnp.concatenate(out, axis=1)


def _rope_r(x, tab_ref):
    return x * tab_ref[0] + pltpu.roll(x, DK_C // 2, 1) * tab_ref[1]


def _log_forget(fb, lb):
    log_lb = jnp.log(jnp.maximum(lb, LB_FLOOR))
    logsig = jnp.minimum(fb, 0.0) - jnp.log(1.0 + jnp.exp(-jnp.abs(fb)))
    b = jnp.log1p(-lb) + logsig
    return jnp.maximum(log_lb, b) + jnp.log(1.0 + jnp.exp(-jnp.abs(log_lb - b)))


def _split3(x):
    h1 = x.astype(bf16)
    r1 = x - h1.astype(f32)
    h2 = r1.astype(bf16)
    h3 = (r1 - h2.astype(f32)).astype(bf16)
    return h1, h2, h3


def _mixer_prompt_kernel(sink_ref, gpow_ref, proj_ref, ropea_ref, roper_ref, qn_ref, kn_ref, lb_ref, hn_ref, rn_ref,
                         sel_ref, exp_ref, tri_ref, dsel_ref, dec_ref, rsc_ref, tail_ref,
                         wg_ref, wu_ref, wd_ref, wo_ref,
                         mix_ref, ck_ref, cv_ref, sh_ref, sr_ref, wgb_ref, wub_ref, wdb_ref, wob_ref,
                         kk, vv, sht, srs, qs_s, qb_s, kb_s, g_s):
    i = pl.program_id(1)
    nb = pl.num_programs(1)

    @pl.when(i == 0)
    def _():
        kk[...] = jnp.zeros_like(kk)
        vv[...] = jnp.zeros_like(vv)
        for g in range(N_KV_A):
            vv[:, 256 * g + 128:256 * (g + 1)] = jnp.ones((2 * BLK, 128), bf16)
        sht[...] = jnp.zeros_like(sht)
        srs[...] = jnp.zeros_like(srs)

    wgb_ref[...] = wg_ref[...].astype(bf16)
    wub_ref[...] = wu_ref[...].astype(bf16)
    wdb_ref[...] = wd_ref[...].astype(bf16)
    wob_ref[...] = wo_ref[...].astype(bf16)

    rowabs = i * BLK + lax.broadcasted_iota(jnp.int32, (BLK, 1), 0)
    logf = jnp.where(rowabs >= PAD, _log_forget(proj_ref[:, F_B:F_B + 512], lb_ref[...]), 0.0)
    kb_s[...] = 1.0 - jnp.exp(logf)
    qb_s[...] = _silu(proj_ref[:, Q_B:Q_B + 512])
    l1, l2, l3 = _split3(logf)
    tri = tri_ref[...]
    g_s[...] = _dot(tri, l1) + _dot(tri, l2) + _dot(tri, l3)

    r2 = lax.broadcasted_iota(jnp.int32, (BLK, BLK), 0)
    c2 = lax.broadcasted_iota(jnp.int32, (BLK, BLK), 1)
    lane_lo = c2 < HEAD_DIM_A
    qa = _rope_a(_head_norm(proj_ref[:, Q_A:Q_A + 1024], sel_ref, exp_ref, qn_ref[...]), ropea_ref)
    for t in range(N_HEADS_A // 2):
        xt = qa[:, 128 * t:128 * (t + 1)] * (HEAD_DIM_A ** -0.5)
        qs_s[256 * t:256 * t + 128, :] = jnp.where(lane_lo, xt, 0.0).astype(bf16)
        qs_s[256 * t + 128:256 * (t + 1), :] = jnp.where(lane_lo, 0.0, xt).astype(bf16)
    k_cur = _rope_a(_head_norm(proj_ref[:, K_A:K_A + 256], sel_ref, exp_ref, kn_ref[...]), ropea_ref)
    v_cur = proj_ref[:, V_A:V_A + 256]
    ck_ref[0] = k_cur
    cv_ref[0] = v_cur
    for t in range(N_KV_A // 2):
        for src, dst, width in ((k_cur, kk, 128), (v_cur, vv, 256)):
            xt = src[:, 128 * t:128 * (t + 1)]
            xs = pltpu.roll(xt, HEAD_DIM_A, 1)
            dst[BLK:2 * BLK, width * 2 * t:width * 2 * t + 128] = jnp.where(lane_lo, xt, xs).astype(bf16)
            dst[BLK:2 * BLK, width * (2 * t + 1):width * (2 * t + 1) + 128] = jnp.where(lane_lo, xs, xt).astype(bf16)

    prow = 2 * BLK
    r4 = lax.broadcasted_iota(jnp.int32, (prow, BLK), 0) % BLK
    c4 = lax.broadcasted_iota(jnp.int32, (prow, BLK), 1)
    up = c4 > r4
    ok = c4 >= jnp.where(up, PAD - (i - 1) * BLK, PAD - i * BLK)
    for t in range(N_HEADS_A // 2):
        g = (2 * t) // GROUP_A
        s2 = _dot(qs_s[prow * t:prow * (t + 1), :], kk[:, 128 * g:128 * (g + 1)], NT)
        s = jnp.where(ok, jnp.where(up, s2[:, 0:BLK], s2[:, BLK:2 * BLK]), NEG_INF)
        sk = jnp.concatenate([jnp.full((BLK, BLK), sink_ref[2 * t + j], f32) for j in range(2)], axis=0)
        m = jnp.maximum(jnp.max(s, axis=-1, keepdims=True), sk)
        p = jnp.exp(s - m)
        p2 = jnp.concatenate([jnp.where(up, p, 0.0), jnp.where(up, 0.0, p)], axis=1).astype(bf16)
        res = _dot(p2, vv[:, 256 * g:256 * (g + 1)])
        y = res[:, 0:128] / (res[:, 128:256] + jnp.exp(sk - m))
        mix_ref[:, 128 * t:128 * (t + 1)] = jnp.where(lane_lo, y[0:BLK], y[BLK:prow]).astype(bf16)
    kk[0:BLK, :] = kk[BLK:2 * BLK, :]
    vv[0:BLK, :] = vv[BLK:2 * BLK, :]

    levels = []
    bz = BLK
    while bz > SUB:
        levels.append((bz, (r2 % bz) >= bz // 2, None if bz == BLK else (r2 // bz) == (c2 // bz)))
        bz //= 2
    same_sub = (r2 // SUB) == (c2 // SUB)
    sub_ge = [(r2 % SUB) >= s for s in range(SUB)]

    def retention_head(h):
        qc = _rope_r(proj_ref[:, Q_C + 128 * h:Q_C + 128 * (h + 1)], roper_ref)
        kc = _rope_r(proj_ref[:, K_C + 128 * h:K_C + 128 * (h + 1)], roper_ref) * (DK_C ** -0.5)
        vc = proj_ref[:, V_C + 128 * h:V_C + 128 * (h + 1)].astype(bf16)
        qcb = qc.astype(bf16)
        sc = _dot(qcb, kc.astype(bf16), NT) * dec_ref[h]
        oc = _dot(qcb, srs[h].astype(bf16)) * rsc_ref[h] + _dot(sc.astype(bf16), vc)
        srs[h] = gpow_ref[h] * srs[h] + _dot((kc * tail_ref[h]).astype(bf16), vc, TN)
        yc = _rms(oc, rn_ref[...]) * _silu(proj_ref[:, G_C + 128 * h:G_C + 128 * (h + 1)])
        mix_ref[:, W_A + 512 + 128 * h:W_A + 512 + 128 * (h + 1)] = yc.astype(bf16)

    def hgrn_intra(h):
        sl = slice(128 * h, 128 * (h + 1))
        gc = g_s[:, sl]
        qb = qb_s[:, sl]
        kb = kb_s[:, sl]
        inter = _dot((qb * jnp.exp(gc)).astype(bf16), sht[h].astype(bf16), NT)
        amat = None
        for bz, upper, same in levels:
            gref = jnp.concatenate([jnp.broadcast_to(g_s[pl.ds(st + bz // 2 - 1, 1), sl], (bz, 128))
                                    for st in range(0, BLK, bz)], axis=0)
            dg = gc - gref
            e = jnp.exp(jnp.where(upper, dg, -dg))
            qt = jnp.where(upper, qb * e, 0.0).astype(bf16)
            kt = jnp.where(upper, 0.0, kb * e).astype(bf16)
            pm = _dot(qt, kt, NT)
            if same is not None:
                pm = jnp.where(same, pm, 0.0)
            amat = pm if amat is None else amat + pm
        adiag = None
        for s in range(SUB):
            ks = jnp.concatenate([jnp.broadcast_to(kb_s[pl.ds(SUB * j + s, 1), sl], (SUB, 128))
                                  for j in range(BLK // SUB)], axis=0)
            gs = jnp.concatenate([jnp.broadcast_to(g_s[pl.ds(SUB * j + s, 1), sl], (SUB, 128))
                                  for j in range(BLK // SUB)], axis=0)
            ys = (qb * ks * jnp.exp(jnp.where(sub_ge[s], gc - gs, NEG_INF))).astype(bf16)
            pm = _dot(ys, dsel_ref[128 * s:128 * (s + 1), :])
            adiag = pm if adiag is None else adiag + pm
        amat = amat + jnp.where(same_sub, adiag, 0.0)
        return inter, amat.astype(bf16)

    def hgrn_finish(h, inter, amat):
        sl = slice(128 * h, 128 * (h + 1))
        vb = proj_ref[:, I_B + 128 * h:I_B + 128 * (h + 1)].astype(bf16)
        ob = inter + _dot(amat, vb)
        glast = g_s[pl.ds(BLK - 1, 1), sl]
        kt2 = (kb_s[:, sl] * jnp.exp(glast - g_s[:, sl])).astype(bf16)
        sht[h] = sht[h] * jnp.exp(glast) + _dot(vb, kt2, TN)
        yb = _rms(ob, hn_ref[...]) * _silu(proj_ref[:, G_B + 128 * h:G_B + 128 * (h + 1)])
        mix_ref[:, W_A + 128 * h:W_A + 128 * (h + 1)] = yb.astype(bf16)

    pending = None
    for h in range(H_B):
        cur = hgrn_intra(h)
        if pending is not None:
            hgrn_finish(h - 1, *pending)
        retention_head(h)
        pending = cur
    hgrn_finish(H_B - 1, *pending)

    @pl.when(i == nb - 1)
    def _():
        for h in range(H_B):
            sh_ref[0, h] = sht[h].T
            sr_ref[0, h] = srs[h]


def _const_spec(shape):
    nd = len(shape)
    return pl.BlockSpec(shape, lambda *_: (0,) * nd)


def _slab_rows(nrows, steps):
    for r in range(16, nrows + 1, 16):
        if nrows % r == 0 and nrows // r <= steps:
            return r
    return nrows


def _mixer_prompt(proj, rows_total, batch, nb, layer, sinks, gpow, ropea, roper, qn, kn, lb, hn, rn, sel, expand,
                  tri, dsel, dec, rsc, tail, w_gate, w_up, w_down, w_out):
    smem = pl.BlockSpec(memory_space=pltpu.SMEM)
    d, dff = w_gate.shape[1], w_gate.shape[2]

    def row_blk(b, i):
        return jnp.where(i == 0, batch * (nb - 1) + b, b * (nb - 1) + i - 1)

    def slab_in(nrows, ncols):
        r = _slab_rows(nrows, batch * nb)
        return pl.BlockSpec((None, r, ncols), lambda b, i: (layer, jnp.minimum(b * nb + i, nrows // r - 1), 0))

    def slab_out(nrows, ncols):
        r = _slab_rows(nrows, batch * nb)
        return pl.BlockSpec((r, ncols), lambda b, i: (jnp.minimum(b * nb + i, nrows // r - 1), 0))

    return pl.pallas_call(
        _mixer_prompt_kernel,
        grid=(batch, nb),
        in_specs=[smem, smem,
                  pl.BlockSpec((BLK, D_IN), lambda b, i: (row_blk(b, i), 0)),
                  pl.BlockSpec((2, BLK, 128), lambda b, i: (0, i, 0)),
                  pl.BlockSpec((2, BLK, 128), lambda b, i: (0, i, 0)),
                  _const_spec((1, 1024)), _const_spec((1, 256)), _const_spec((1, 512)),
                  _const_spec((1, 128)), _const_spec((1, 128)),
                  _const_spec((1024, 128)), _const_spec((128, 1024)),
                  _const_spec((BLK, BLK)), _const_spec((SUB * 128, 128)),
                  _const_spec((H_C, BLK, BLK)), _const_spec((H_C, BLK, 128)), _const_spec((H_C, BLK, 128)),
                  slab_in(d, dff), slab_in(d, dff), slab_in(dff, d), slab_in(d, d)],
        out_specs=[pl.BlockSpec((BLK, 2048), lambda b, i: (row_blk(b, i), 0)),
                   pl.BlockSpec((1, BLK, 256), lambda b, i: (b, 0, 0)),
                   pl.BlockSpec((1, BLK, 256), lambda b, i: (b, 0, 0)),
                   pl.BlockSpec((1, H_B, 128, 128), lambda b, i: (b, 0, 0, 0)),
                   pl.BlockSpec((1, H_C, 128, 128), lambda b, i: (b, 0, 0, 0)),
                   slab_out(d, dff), slab_out(d, dff), slab_out(dff, d), slab_out(d, d)],
        out_shape=[jax.ShapeDtypeStruct((rows_total, 2048), bf16),
                   jax.ShapeDtypeStruct((batch, BLK, 256), f32),
                   jax.ShapeDtypeStruct((batch, BLK, 256), f32),
                   jax.ShapeDtypeStruct((batch, H_B, 128, 128), f32),
                   jax.ShapeDtypeStruct((batch, H_C, 128, 128), f32),
                   jax.ShapeDtypeStruct((d, dff), bf16), jax.ShapeDtypeStruct((d, dff), bf16),
                   jax.ShapeDtypeStruct((dff, d), bf16), jax.ShapeDtypeStruct((d, d), bf16)],
        scratch_shapes=[pltpu.VMEM((2 * BLK, N_KV_A * 128), bf16), pltpu.VMEM((2 * BLK, N_KV_A * 256), bf16),
                        pltpu.VMEM((H_B, 128, 128), f32), pltpu.VMEM((H_C, 128, 128), f32),
                        pltpu.VMEM((N_HEADS_A * BLK, 128), bf16),
                        pltpu.VMEM((BLK, 512), f32), pltpu.VMEM((BLK, 512), f32), pltpu.VMEM((BLK, 512), f32)],
        compiler_params=pltpu.CompilerParams(dimension_semantics=("arbitrary", "arbitrary"),
                                             vmem_limit_bytes=VMEM_LIMIT),
        name="mixer_prompt",
    )(sinks, gpow, proj, ropea, roper, qn, kn, lb, hn, rn, sel, expand, tri, dsel, dec, rsc, tail,
      w_gate, w_up, w_down, w_out)


def _mixer_sample_kernel(sink_ref, gpow_ref, mixin_ref, *refs, dec_seq, steps, n_prev):
    del mixin_ref
    refs = refs[:17] + refs[17 + n_prev:]
    c = pl.program_id(0)
    mix_ref = refs[17]
    pl.when(c < steps)(lambda: _mixer_sample_body(sink_ref, gpow_ref, *refs, dec_seq=dec_seq))

    @pl.when(c >= steps)
    def _():
        mix_ref[...] = jnp.zeros_like(mix_ref)


def _mixer_sample_body(sink_ref, gpow_ref, proj_ref, ropea_ref, roper_ref, qn_ref, kn_ref, lb_ref,
                       hn_ref, rn_ref, sel_ref, exp_ref, dec_ref, rsc_ref, tail_ref,
                       ck_ref, cv_ref, sh_ref, sr_ref,
                       mix_ref, nk_ref, nv_ref, nsh_ref, nsr_ref, *, dec_seq):
    nbat = SROWS // dec_seq
    w = ck_ref.shape[1]
    rb = lax.broadcasted_iota(jnp.int32, (SROWS, 1), 0) // dec_seq
    rt = lax.broadcasted_iota(jnp.int32, (SROWS, 1), 0) % dec_seq

    def pick(parts):
        out = parts[nbat - 1]
        for b in range(nbat - 2, -1, -1):
            out = jnp.where(rb == b, parts[b], out)
        return out

    qa = _rope_a(_head_norm(proj_ref[:, Q_A:Q_A + 1024], sel_ref, exp_ref, qn_ref[...]), ropea_ref)
    qs = (qa * (HEAD_DIM_A ** -0.5)).astype(bf16)
    k_new = _rope_a(_head_norm(proj_ref[:, K_A:K_A + 256], sel_ref, exp_ref, kn_ref[...]), ropea_ref)
    v_new = proj_ref[:, V_A:V_A + 256]
    for b in range(nbat):
        nk_ref[b, 0:w - dec_seq, :] = ck_ref[b, dec_seq:w, :]
        nk_ref[b, w - dec_seq:w, :] = k_new[dec_seq * b:dec_seq * (b + 1), :]
        nv_ref[b, 0:w - dec_seq, :] = cv_ref[b, dec_seq:w, :]
        nv_ref[b, w - dec_seq:w, :] = v_new[dec_seq * b:dec_seq * (b + 1), :]
    knb = k_new.astype(bf16)
    vnb = v_new.astype(bf16)

    grows = GROUP_A * SROWS
    rt_g = jnp.concatenate([rt] * GROUP_A, axis=0)
    rb_g = jnp.concatenate([rb] * GROUP_A, axis=0)
    jc = lax.broadcasted_iota(jnp.int32, (grows, w), 1)
    valid_c = jc > rt_g + (w - WINDOW)
    cn = lax.broadcasted_iota(jnp.int32, (grows, SROWS), 1)
    valid_n = (cn // dec_seq == rb_g) & (cn % dec_seq <= rt_g)
    heads = [None] * N_HEADS_A
    for g in range(N_KV_A):
        ksl = slice(64 * g, 64 * (g + 1))
        qg = jnp.concatenate([qs[:, 64 * h:64 * (h + 1)] for h in range(GROUP_A * g, GROUP_A * (g + 1))], axis=0)
        sparts = [_dot(qg, ck_ref[b, :, ksl].astype(bf16), NT) for b in range(nbat)]
        s_c = sparts[nbat - 1]
        for b in range(nbat - 2, -1, -1):
            s_c = jnp.where(rb_g == b, sparts[b], s_c)
        s_c = jnp.where(valid_c, s_c, NEG_INF)
        s_n = jnp.where(valid_n, _dot(qg, knb[:, ksl], NT), NEG_INF)
        sk = jnp.concatenate([jnp.full((SROWS, 1), sink_ref[h], f32)
                              for h in range(GROUP_A * g, GROUP_A * (g + 1))], axis=0)
        m = jnp.maximum(jnp.maximum(jnp.max(s_c, axis=-1, keepdims=True), jnp.max(s_n, axis=-1, keepdims=True)), sk)
        p_c = jnp.exp(s_c - m)
        p_n = jnp.exp(s_n - m)
        den = jnp.sum(p_c, axis=-1, keepdims=True) + jnp.sum(p_n, axis=-1, keepdims=True) + jnp.exp(sk - m)
        pcb = p_c.astype(bf16)
        oparts = [_dot(pcb, cv_ref[b, :, ksl].astype(bf16)) for b in range(nbat)]
        o = oparts[nbat - 1]
        for b in range(nbat - 2, -1, -1):
            o = jnp.where(rb_g == b, oparts[b], o)
        o = (o + _dot(p_n.astype(bf16), vnb[:, ksl])) / den
        for hh in range(GROUP_A):
            heads[GROUP_A * g + hh] = o[SROWS * hh:SROWS * (hh + 1), :]
    mix_ref[:, 0:W_A] = jnp.concatenate(heads, axis=1).astype(bf16)

    logf = _log_forget(proj_ref[:, F_B:F_B + 512], lb_ref[...])
    kb = 1.0 - jnp.exp(logf)
    qb = _silu(proj_ref[:, Q_B:Q_B + 512])
    vb = proj_ref[:, I_B:I_B + 512]
    gcum = logf
    for d in range(1, dec_seq):
        gcum = gcum + jnp.where(rt >= d, pltpu.roll(logf, d, 0), 0.0)
    cb = lax.broadcasted_iota(jnp.int32, (1, SROWS), 1) // dec_seq
    ct = lax.broadcasted_iota(jnp.int32, (1, SROWS), 1) % dec_seq
    intra = [jnp.zeros((SROWS, 128), f32) for _ in range(H_B)]
    for s in range(SROWS):
        ok = (rb == s // dec_seq) & (rt >= s % dec_seq)
        y = qb * kb[s:s + 1, :] * jnp.exp(jnp.where(ok, gcum - gcum[s:s + 1, :], NEG_INF))
        for h in range(H_B):
            sl = slice(128 * h, 128 * (h + 1))
            intra[h] = intra[h] + jnp.sum(y[:, sl], axis=-1, keepdims=True) * vb[s:s + 1, sl]
    qg = (qb * jnp.exp(gcum)).astype(bf16)
    vbb = vb.astype(bf16)
    glast = [gcum[dec_seq * (b + 1) - 1:dec_seq * (b + 1), :] for b in range(nbat)]
    for h in range(H_B):
        sl = slice(128 * h, 128 * (h + 1))
        ob = pick([_dot(qg[:, sl], sh_ref[b, h].astype(bf16)) for b in range(nbat)]) + intra[h]
        for b in range(nbat):
            kt2 = jnp.where(rb == b, kb[:, sl] * jnp.exp(glast[b][:, sl] - gcum[:, sl]), 0.0).astype(bf16)
            dcol = jnp.broadcast_to(jnp.exp(glast[b][:, sl]), (128, 128)).T
            nsh_ref[b, h] = dcol * sh_ref[b, h] + _dot(kt2, vbb[:, sl], TN)
        yb = _rms(ob, hn_ref[...]) * _silu(proj_ref[:, G_B + 128 * h:G_B + 128 * (h + 1)])
        mix_ref[:, W_A + 128 * h:W_A + 128 * (h + 1)] = yb.astype(bf16)

    for h in range(H_C):
        qc = _rope_r(proj_ref[:, Q_C + 128 * h:Q_C + 128 * (h + 1)], roper_ref)
        kc = _rope_r(proj_ref[:, K_C + 128 * h:K_C + 128 * (h + 1)], roper_ref) * (DK_C ** -0.5)
        vc = proj_ref[:, V_C + 128 * h:V_C + 128 * (h + 1)]
        qcb = qc.astype(bf16)
        vcb = vc.astype(bf16)
        sc = _dot(qcb, kc.astype(bf16), NT) * dec_ref[h]
        oc = pick([_dot(qcb, sr_ref[b, h].astype(bf16)) for b in range(nbat)]) * rsc_ref[h] \
            + _dot(sc.astype(bf16), vcb)
        kct = kc * tail_ref[h]
        for b in range(nbat):
            nsr_ref[b, h] = gpow_ref[h] * sr_ref[b, h] + _dot(jnp.where(rb == b, kct, 0.0).astype(bf16), vcb, TN)
        yc = _rms(oc, rn_ref[...]) * _silu(proj_ref[:, G_C + 128 * h:G_C + 128 * (h + 1)])
        mix_ref[:, W_A + 512 + 128 * h:W_A + 512 + 128 * (h + 1)] = yc.astype(bf16)


def _mixer_sample(mix, proj, row0, layer, dec_batch, dec_seq, sinks, gpow, ropea, roper, qn, kn, lb, hn, rn, sel, expand,
                  dec, rsc, tail, cache_k, cache_v, state_h, state_r, prev_outs):
    smem = pl.BlockSpec(memory_space=pltpu.SMEM)
    nbat = SROWS // dec_seq
    steps = dec_batch // nbat
    blk0 = row0 // SROWS
    w = cache_k.shape[1]
    n_all = cache_k.shape[0]
    fill_steps = (mix.shape[0] - row0) // SROWS - steps
    cl = lambda c: jnp.minimum(c, steps - 1)
    any_spec = pl.BlockSpec(memory_space=pl.ANY)
    n_in = 20
    return pl.pallas_call(
        functools.partial(_mixer_sample_kernel, dec_seq=dec_seq, steps=steps, n_prev=len(prev_outs)),
        grid=(steps + fill_steps,),
        in_specs=[smem, smem,
                  pl.BlockSpec(memory_space=pl.ANY),
                  pl.BlockSpec((SROWS, D_IN), lambda c: (blk0 + cl(c), 0)),
                  _const_spec((2, SROWS, 128)), _const_spec((2, SROWS, 128)),
                  _const_spec((1, 1024)), _const_spec((1, 256)), _const_spec((1, 512)),
                  _const_spec((1, 128)), _const_spec((1, 128)),
                  _const_spec((1024, 128)), _const_spec((128, 1024)),
                  _const_spec((H_C, SROWS, SROWS)), _const_spec((H_C, SROWS, 128)), _const_spec((H_C, SROWS, 128)),
                  pl.BlockSpec((nbat, w, 256), lambda c: (layer * steps + cl(c), 0, 0)),
                  pl.BlockSpec((nbat, w, 256), lambda c: (layer * steps + cl(c), 0, 0)),
                  pl.BlockSpec((nbat, H_B, 128, 128), lambda c: (layer * steps + cl(c), 0, 0, 0)),
                  pl.BlockSpec((nbat, H_C, 128, 128), lambda c: (layer * steps + cl(c), 0, 0, 0))]
        + [any_spec] * len(prev_outs),
        out_specs=[pl.BlockSpec((SROWS, 2048), lambda c: (blk0 + c, 0)),
                   pl.BlockSpec((nbat, w, 256), lambda c: (layer * steps + cl(c), 0, 0)),
                   pl.BlockSpec((nbat, w, 256), lambda c: (layer * steps + cl(c), 0, 0)),
                   pl.BlockSpec((nbat, H_B, 128, 128), lambda c: (layer * steps + cl(c), 0, 0, 0)),
                   pl.BlockSpec((nbat, H_C, 128, 128), lambda c: (layer * steps + cl(c), 0, 0, 0))],
        out_shape=[jax.ShapeDtypeStruct(mix.shape, bf16),
                   jax.ShapeDtypeStruct((n_all, w, 256), f32),
                   jax.ShapeDtypeStruct((n_all, w, 256), f32),
                   jax.ShapeDtypeStruct((n_all, H_B, 128, 128), f32),
                   jax.ShapeDtypeStruct((n_all, H_C, 128, 128), f32)],
        input_output_aliases={2: 0, **{n_in + k: 1 + k for k in range(len(prev_outs))}},
        compiler_params=pltpu.CompilerParams(dimension_semantics=("arbitrary",), vmem_limit_bytes=VMEM_LIMIT),
        name="mixer_sample",
    )(sinks, gpow, mix, proj, ropea, roper, qn, kn, lb, hn, rn, sel, expand, dec, rsc, tail,
      cache_k, cache_v, state_h, state_r, *prev_outs)


def _rope_tables(pos):
    posf = jnp.asarray(pos).astype(f32)[:, None]
    t = pos.shape[0]
    half = N_ROT // 2
    inv = ROPE_THETA ** (-jnp.arange(half, dtype=f32) * (2.0 / N_ROT))
    ang = posf * inv[None, :]
    cos, sin = jnp.cos(ang), jnp.sin(ang)
    rest0 = jnp.zeros((t, HEAD_DIM_A - N_ROT), f32)
    c64 = jnp.concatenate([cos, cos, rest0 + 1.0], axis=1)
    s64 = jnp.concatenate([sin, sin, rest0], axis=1)
    ropea = jnp.stack([jnp.tile(c64, (1, 2)), jnp.tile(s64, (1, 2))])
    invr = RET_THETA ** (-jnp.arange(DK_C // 2, dtype=f32) * (2.0 / DK_C))
    angr = posf * invr[None, :]
    cr, sr = jnp.cos(angr), jnp.sin(angr)
    roper = jnp.stack([jnp.concatenate([cr, cr], axis=1), jnp.concatenate([-sr, sr], axis=1)])
    return ropea, roper


def _ret_tables(lg, seq_of_row, tok_of_row, length):
    nf = np.float32
    tq = tok_of_row.astype(nf)
    rel = tq[:, None] - tq[None, :]
    ok = (seq_of_row[:, None] == seq_of_row[None, :]) & (rel >= 0)
    dec = np.where(ok[None], np.exp(np.where(ok, rel, nf(0.0))[None] * lg[:, None, None]), nf(0.0))
    n = tq.shape[0]
    rsc = np.broadcast_to(np.exp((tq + nf(1.0))[None, :, None] * lg[:, None, None]), (H_C, n, 128))
    tail = np.broadcast_to(np.exp((nf(length) - nf(1.0) - tq)[None, :, None] * lg[:, None, None]), (H_C, n, 128))
    gpow = np.exp(nf(length) * lg)
    return dec.astype(nf), np.ascontiguousarray(rsc, nf), np.ascontiguousarray(tail, nf), gpow.astype(nf)


def kernel(x_prompt, x_sample, cache_k, cache_v, state_hgrn, state_ret, meta_tokens, norm_mix, norm_ffn, w_in, q_norm, k_norm, attn_sinks, hgrn_lb, hgrn_norm, ret_norm, w_out, w_gate, w_up, w_down):
    batch, seq, d = x_prompt.shape
    dec_batch, dec_seq, _ = x_sample.shape
    depth = w_in.shape[0]
    w = cache_k.shape[2]
    assert d == 2048 and w_in.shape[2] == D_IN and seq % BLK == 0
    assert SROWS % dec_seq == 0 and dec_batch % (SROWS // dec_seq) == 0 and w == WINDOW
    nb = seq // BLK + 1
    lp = nb * BLK
    rows_main = batch * seq
    rows_s = dec_batch * dec_seq
    row_s0 = rows_main + batch * BLK
    rows = rows_main + TM_F
    assert rows_main % TM_F == 0 and batch * BLK + rows_s <= TM_F and (rows - row_s0) % SROWS == 0

    meta_blk = jnp.concatenate([jnp.zeros((PAD, d), f32), meta_tokens.astype(f32)], axis=0)
    tail = jnp.concatenate([jnp.tile(meta_blk, (batch, 1)), x_sample.reshape(rows_s, d),
                            jnp.zeros((rows - row_s0 - rows_s, d), f32)], axis=0)
    x, h = _embed(x_prompt.reshape(rows_main, d), tail, norm_mix[0][None])

    p = jax.nn.softmax(hgrn_lb.astype(f32), axis=0)
    lbs = jnp.cumsum(p, axis=0) - p[0]
    lg = np.log1p(-np.exp2(np.float32(-5.0) - np.arange(H_C, dtype=np.float32))).astype(np.float32)
    ropea_p, roper_p = _rope_tables(np.arange(lp) - PAD)
    srow = np.arange(SROWS)
    ropea_s, roper_s = _rope_tables(PAST_LEN + srow % dec_seq)
    dec_p, rsc_p, tail_p, gpow_p = _ret_tables(lg, np.zeros((BLK,), np.int32), np.arange(BLK), float(BLK))
    dec_s, rsc_s, tail_s, gpow_s = _ret_tables(lg, srow // dec_seq, srow % dec_seq, float(dec_seq))
    sel_np = np.arange(1024)[:, None] // HEAD_DIM_A == np.arange(128)[None, :]
    sel = jnp.asarray(sel_np, bf16)
    expand = jnp.asarray(sel_np.T, bf16)
    tri = jnp.asarray(np.arange(BLK)[:, None] >= np.arange(BLK)[None, :], bf16)
    dsel = jnp.asarray(np.arange(SUB * 128)[:, None] // 128 == np.arange(128)[None, :] % SUB, bf16)

    ck_flat = cache_k.reshape(depth * dec_batch, w, N_KV_A * HEAD_DIM_A)
    cv_flat = cache_v.reshape(depth * dec_batch, w, N_KV_A * HEAD_DIM_A)
    sh_flat = state_hgrn.reshape(depth * dec_batch, H_B, 128, 128)
    sr_flat = state_ret.reshape(depth * dec_batch, H_C, 128, 128)

    outs_p, outs_s = [], ()
    for l in range(depth):
        proj = _inproj(h, w_in, l)
        qn = jnp.tile(q_norm[l], N_HEADS_A)[None]
        kn = jnp.tile(k_norm[l], N_KV_A)[None]
        common = (qn, kn, lbs[l][None], hgrn_norm[l][None], ret_norm[l][None], sel, expand)
        mix, ck, cv, sh, sr, wg_b, wu_b, wd_b, wo_b = _mixer_prompt(
            proj, rows, batch, nb, l, attn_sinks[l], gpow_p, ropea_p, roper_p, *common, tri, dsel,
            dec_p, rsc_p, tail_p, w_gate, w_up, w_down, w_out)
        mix, *outs_s = _mixer_sample(mix, proj, row_s0, l, dec_batch, dec_seq, attn_sinks[l], gpow_s,
                                     ropea_s, roper_s, *common, dec_s, rsc_s, tail_s,
                                     ck_flat, cv_flat, sh_flat, sr_flat, tuple(outs_s))
        ffn_w = (wo_b, norm_ffn[l][None], wg_b, wu_b, wd_b)
        outs_p.append((ck, cv, sh, sr))
        if l + 1 < depth:
            x, h = _ffn(x, mix, *ffn_w, norm_mix[l + 1][None], 0, rows // TM_F)
        else:
            (y_main,) = _ffn(x, mix, *ffn_w, None, 0, rows_main // TM_F)
            (y_tail,) = _ffn(x, mix, *ffn_w, None, rows_main // TM_F, 1)

    y_prompt = y_main.reshape(batch, seq, d)
    y_sample = y_tail[row_s0 - rows_main:row_s0 - rows_main + rows_s].reshape(dec_batch, dec_seq, d)
    kv_shape = (depth, -1, w, N_KV_A, HEAD_DIM_A)
    st_shape = (depth, dec_batch, H_B, 128, 128)
    stack = lambda outs, k: jnp.stack([o[k] for o in outs])
    nk, nv, nsh, nsr = outs_s
    return (y_prompt, y_sample,
            stack(outs_p, 0).reshape(kv_shape), stack(outs_p, 1).reshape(kv_shape), stack(outs_p, 2), stack(outs_p, 3),
            nk.reshape(kv_shape), nv.reshape(kv_shape), nsh.reshape(st_shape), nsr.reshape(st_shape))
```

```python
import functools

import jax
import jax.numpy as jnp
import numpy as np
from jax import lax
from jax.experimental import pallas as pl
from jax.experimental.pallas import tpu as pltpu

f32 = jnp.float32
bf16 = jnp.bfloat16

N_META = 16
EPS = 1e-6
NEG_INF = -1e30
LB_FLOOR = 1e-30
WINDOW = 128
HEAD_DIM_A = 64
N_HEADS_A = 16
N_KV_A = 4
GROUP_A = N_HEADS_A // N_KV_A
N_ROT = 16
ROPE_THETA = 500000.0
H_B = 4
DK_B = 128
H_C = 4
DK_C = 128
RET_THETA = 10000.0
PAST_LEN = 16384

BLK = 128
PAD = BLK - N_META
SUB = 8
SROWS = 16
TM_F = 512
VMEM_LIMIT = 56 * 1024 * 1024

Q_A, K_A, V_A = 0, 1024, 1280
Q_B, F_B, I_B, G_B = 1536, 2048, 2560, 3072
Q_C, K_C, V_C, G_C = 3584, 4096, 4608, 5120
D_IN = 5632
W_A = 1024
NT = (((1,), (1,)), ((), ()))
TN = (((0,), (0,)), ((), ()))


def _dot(a, b, dims=None):
    if dims is None:
        return jnp.dot(a, b, preferred_element_type=f32)
    return lax.dot_general(a, b, dims, preferred_element_type=f32)


def _row_tile(rows, cap):
    best = 8
    for t in range(8, cap + 1, 8):
        if rows % t == 0:
            best = t
    return best


def _silu(x):
    return x * jax.nn.sigmoid(x)


def _rms(x, g):
    return x * lax.rsqrt(jnp.mean(x * x, axis=-1, keepdims=True) + EPS) * g


def _inproj_kernel(h_ref, w_ref, o_ref, wb_ref):
    @pl.when(pl.program_id(1) == 0)
    def _():
        wb_ref[...] = w_ref[...].astype(bf16)

    o_ref[...] = _dot(h_ref[...], wb_ref[...])


def _inproj(h, w_all, layer):
    rows, d = h.shape
    n = w_all.shape[2]
    tm = _row_tile(rows, 2176)
    tn = 512
    return pl.pallas_call(
        _inproj_kernel,
        grid=(n // tn, rows // tm),
        in_specs=[pl.BlockSpec((tm, d), lambda j, i: (i, 0)),
                  pl.BlockSpec((None, d, tn), lambda j, i: (layer, 0, j))],
        out_specs=pl.BlockSpec((tm, tn), lambda j, i: (i, j)),
        out_shape=jax.ShapeDtypeStruct((rows, n), f32),
        scratch_shapes=[pltpu.VMEM((d, tn), bf16)],
        compiler_params=pltpu.CompilerParams(dimension_semantics=("arbitrary", "arbitrary"),
                                             vmem_limit_bytes=VMEM_LIMIT),
        name="inproj",
    )(h, w_all)


def _embed_kernel(xp_ref, tail_ref, g_ref, x_ref, h_ref):
    i = pl.program_id(0)
    last = pl.num_programs(0) - 1

    def emit(src_ref):
        x = src_ref[...]
        x_ref[...] = x
        h_ref[...] = _rms(x, g_ref[...]).astype(bf16)

    pl.when(i < last)(lambda: emit(xp_ref))
    pl.when(i == last)(lambda: emit(tail_ref))


def _embed(xp, tail, g):
    rows_main, d = xp.shape
    tm = tail.shape[0]
    n_main = rows_main // tm
    rows = rows_main + tm
    return pl.pallas_call(
        _embed_kernel,
        grid=(n_main + 1,),
        in_specs=[pl.BlockSpec((tm, d), lambda i: (jnp.minimum(i, n_main - 1), 0)),
                  pl.BlockSpec((tm, d), lambda i: (0, 0)),
                  pl.BlockSpec((1, d), lambda i: (0, 0))],
        out_specs=[pl.BlockSpec((tm, d), lambda i: (i, 0)),
                   pl.BlockSpec((tm, d), lambda i: (i, 0))],
        out_shape=[jax.ShapeDtypeStruct((rows, d), f32), jax.ShapeDtypeStruct((rows, d), bf16)],
        compiler_params=pltpu.CompilerParams(dimension_semantics=("arbitrary",), vmem_limit_bytes=VMEM_LIMIT),
        name="embed_norm",
    )(xp, tail, g)


def _ffn_kernel(*refs, with_next):
    if with_next:
        x_ref, mix_ref, wo_ref, nf_ref, wg_ref, wu_ref, wd_ref, gn_ref, o_ref, hn_ref, h_ref = refs
    else:
        x_ref, mix_ref, wo_ref, nf_ref, wg_ref, wu_ref, wd_ref, o_ref, h_ref = refs
    j = pl.program_id(1)

    @pl.when(j == 0)
    def _():
        x1 = x_ref[...] + _dot(mix_ref[...], wo_ref[...])
        h_ref[...] = _rms(x1, nf_ref[...]).astype(bf16)
        o_ref[...] = x1

    h = h_ref[...]
    a = _silu(_dot(h, wg_ref[...])) * _dot(h, wu_ref[...])
    o_ref[...] += _dot(a.astype(bf16), wd_ref[...])

    if with_next:
        @pl.when(j == pl.num_programs(1) - 1)
        def _():
            hn_ref[...] = _rms(o_ref[...], gn_ref[...]).astype(bf16)


def _ffn(x, mix, wo, nf, wg, wu, wd, g_next, tm, tile0, n_tiles):
    d = x.shape[1]
    dff = wg.shape[1]
    tf = 512
    with_next = g_next is not None
    row_spec = pl.BlockSpec((tm, d), lambda i, j: (tile0 + i, 0))
    out_spec = pl.BlockSpec((tm, d), lambda i, j: (i, 0))
    in_specs = [row_spec, row_spec,
                pl.BlockSpec((d, d), lambda i, j: (0, 0), pipeline_mode=pl.Buffered(1)),
                pl.BlockSpec((1, d), lambda i, j: (0, 0)),
                pl.BlockSpec((d, tf), lambda i, j: (0, j)),
                pl.BlockSpec((d, tf), lambda i, j: (0, j)),
                pl.BlockSpec((tf, d), lambda i, j: (j, 0))]
    args = [x, mix, wo, nf, wg, wu, wd]
    out_specs = [out_spec]
    out_shape = [jax.ShapeDtypeStruct((n_tiles * tm, d), f32)]
    if with_next:
        in_specs.append(pl.BlockSpec((1, d), lambda i, j: (0, 0)))
        args.append(g_next)
        out_specs.append(out_spec)
        out_shape.append(jax.ShapeDtypeStruct((n_tiles * tm, d), bf16))
    return pl.pallas_call(
        functools.partial(_ffn_kernel, with_next=with_next),
        grid=(n_tiles, dff // tf),
        in_specs=in_specs,
        out_specs=out_specs,
        out_shape=out_shape,
        scratch_shapes=[pltpu.VMEM((tm, d), bf16)],
        compiler_params=pltpu.CompilerParams(dimension_semantics=("arbitrary", "arbitrary"),
                                             vmem_limit_bytes=VMEM_LIMIT),
        name="outproj_ffn",
    )(*args)


def _head_norm(x, sel_ref, exp_ref, g):
    n, w = x.shape
    x2 = x * x
    hi = x2.astype(bf16)
    lo = (x2 - hi.astype(f32)).astype(bf16)
    s = _dot(jnp.concatenate([hi, lo], axis=0), sel_ref[0:w, :])
    return x * _dot_split3(lax.rsqrt((s[0:n] + s[n:2 * n]) * (1.0 / HEAD_DIM_A) + EPS), exp_ref[:, 0:w]) * g


def _dot_split3(x, m):
    n = x.shape[0]
    y = _dot(jnp.concatenate(_split3(x), axis=0), m)
    return y[0:n] + y[n:2 * n] + y[2 * n:3 * n]


def _rope_a(x, tab_ref):
    half = N_ROT // 2
    first = lax.broadcasted_iota(jnp.int32, (x.shape[0], 128), 1) % HEAD_DIM_A < half
    out = []
    for t in range(x.shape[1] // 128):
        xt = x[:, 128 * t:128 * (t + 1)]
        partner = jnp.where(first, -pltpu.roll(xt, 128 - half, 1), pltpu.roll(xt, half, 1))
        out.append(xt * tab_ref[0] + partner * tab_ref[1])
    return jnp.concatenate(out, axis=1)


def _rope_r(x, tab_ref):
    return x * tab_ref[0] + pltpu.roll(x, DK_C // 2, 1) * tab_ref[1]


def _log_forget(fb, lb):
    log_lb = jnp.log(jnp.maximum(lb, LB_FLOOR))
    logsig = jnp.minimum(fb, 0.0) - jnp.log(1.0 + jnp.exp(-jnp.abs(fb)))
    b = jnp.log1p(-lb) + logsig
    return jnp.maximum(log_lb, b) + jnp.log(1.0 + jnp.exp(-jnp.abs(log_lb - b)))


def _split3(x):
    h1 = x.astype(bf16)
    r1 = x - h1.astype(f32)
    h2 = r1.astype(bf16)
    h3 = (r1 - h2.astype(f32)).astype(bf16)
    return h1, h2, h3


def _mixer_prompt_kernel(sink_ref, gpow_ref, proj_ref, ropea_ref, roper_ref, qn_ref, kn_ref, lb_ref, hn_ref, rn_ref,
                         sel_ref, exp_ref, tri_ref, dsel_ref, dec_ref, rsc_ref, tail_ref,
                         wg_ref, wu_ref, wd_ref, wo_ref,
                         mix_ref, ck_ref, cv_ref, sh_ref, sr_ref, wgb_ref, wub_ref, wdb_ref, wob_ref,
                         kk, vv, sht, srs, qs_s, qb_s, kb_s, g_s):
    i = pl.program_id(1)
    nb = pl.num_programs(1)

    @pl.when(i == 0)
    def _():
        kk[...] = jnp.zeros_like(kk)
        vv[...] = jnp.zeros_like(vv)
        for g in range(N_KV_A):
            vv[:, 256 * g + 128:256 * (g + 1)] = jnp.ones((2 * BLK, 128), bf16)
        sht[...] = jnp.zeros_like(sht)
        srs[...] = jnp.zeros_like(srs)

    wgb_ref[...] = wg_ref[...].astype(bf16)
    wub_ref[...] = wu_ref[...].astype(bf16)
    wdb_ref[...] = wd_ref[...].astype(bf16)
    wob_ref[...] = wo_ref[...].astype(bf16)

    rowabs = i * BLK + lax.broadcasted_iota(jnp.int32, (BLK, 1), 0)
    logf = jnp.where(rowabs >= PAD, _log_forget(proj_ref[:, F_B:F_B + 512], lb_ref[...]), 0.0)
    kb_s[...] = 1.0 - jnp.exp(logf)
    qb_s[...] = _silu(proj_ref[:, Q_B:Q_B + 512])
    l1, l2, l3 = _split3(logf)
    tri = tri_ref[...]
    g_s[...] = _dot(tri, l1) + _dot(tri, l2) + _dot(tri, l3)

    r2 = lax.broadcasted_iota(jnp.int32, (BLK, BLK), 0)
    c2 = lax.broadcasted_iota(jnp.int32, (BLK, BLK), 1)
    lane_lo = c2 < HEAD_DIM_A
    qa = _rope_a(_head_norm(proj_ref[:, Q_A:Q_A + 1024], sel_ref, exp_ref, qn_ref[...]), ropea_ref)
    for t in range(N_HEADS_A // 2):
        xt = qa[:, 128 * t:128 * (t + 1)] * (HEAD_DIM_A ** -0.5)
        qs_s[256 * t:256 * t + 128, :] = jnp.where(lane_lo, xt, 0.0).astype(bf16)
        qs_s[256 * t + 128:256 * (t + 1), :] = jnp.where(lane_lo, 0.0, xt).astype(bf16)
    k_cur = _rope_a(_head_norm(proj_ref[:, K_A:K_A + 256], sel_ref, exp_ref, kn_ref[...]), ropea_ref)
    v_cur = proj_ref[:, V_A:V_A + 256]
    ck_ref[0] = k_cur
    cv_ref[0] = v_cur
    for t in range(N_KV_A // 2):
        for src, dst, width in ((k_cur, kk, 128), (v_cur, vv, 256)):
            xt = src[:, 128 * t:128 * (t + 1)]
            xs = pltpu.roll(xt, HEAD_DIM_A, 1)
            dst[BLK:2 * BLK, width * 2 * t:width * 2 * t + 128] = jnp.where(lane_lo, xt, xs).astype(bf16)
            dst[BLK:2 * BLK, width * (2 * t + 1):width * (2 * t + 1) + 128] = jnp.where(lane_lo, xs, xt).astype(bf16)

    prow = 2 * BLK
    r4 = lax.broadcasted_iota(jnp.int32, (prow, BLK), 0) % BLK
    c4 = lax.broadcasted_iota(jnp.int32, (prow, BLK), 1)
    up = c4 > r4
    ok = c4 >= jnp.where(up, PAD - (i - 1) * BLK, PAD - i * BLK)

    def attention_pair(t):
        g = (2 * t) // GROUP_A
        s2 = _dot(qs_s[prow * t:prow * (t + 1), :], kk[:, 128 * g:128 * (g + 1)], NT)
        s = jnp.where(ok, jnp.where(up, s2[:, 0:BLK], s2[:, BLK:2 * BLK]), NEG_INF)
        sk = jnp.concatenate([jnp.full((BLK, BLK), sink_ref[2 * t + j], f32) for j in range(2)], axis=0)
        m = jnp.maximum(jnp.max(s, axis=-1, keepdims=True), sk)
        p = jnp.exp(s - m)
        p2 = jnp.concatenate([jnp.where(up, p, 0.0), jnp.where(up, 0.0, p)], axis=1).astype(bf16)
        res = _dot(p2, vv[:, 256 * g:256 * (g + 1)])
        y = res[:, 0:128] / (res[:, 128:256] + jnp.exp(sk - m))
        mix_ref[:, 128 * t:128 * (t + 1)] = jnp.where(lane_lo, y[0:BLK], y[BLK:prow]).astype(bf16)

    for t in range(N_HEADS_A // 2):
        attention_pair(t)
    kk[0:BLK, :] = kk[BLK:2 * BLK, :]
    vv[0:BLK, :] = vv[BLK:2 * BLK, :]

    levels = []
    bz = BLK
    while bz > SUB:
        levels.append((bz, (r2 % bz) >= bz // 2, None if bz == BLK else (r2 // bz) == (c2 // bz)))
        bz //= 2
    same_sub = (r2 // SUB) == (c2 // SUB)
    sub_ge = [(r2 % SUB) >= s for s in range(SUB)]

    def retention_head(h):
        qc = _rope_r(proj_ref[:, Q_C + 128 * h:Q_C + 128 * (h + 1)], roper_ref)
        kc = _rope_r(proj_ref[:, K_C + 128 * h:K_C + 128 * (h + 1)], roper_ref) * (DK_C ** -0.5)
        vc = proj_ref[:, V_C + 128 * h:V_C + 128 * (h + 1)].astype(bf16)
        qcb = qc.astype(bf16)
        sc = _dot(qcb, kc.astype(bf16), NT) * dec_ref[h]
        oc = _dot(qcb, srs[h].astype(bf16)) * rsc_ref[h] + _dot(sc.astype(bf16), vc)
        srs[h] = gpow_ref[h] * srs[h] + _dot((kc * tail_ref[h]).astype(bf16), vc, TN)
        yc = _rms(oc, rn_ref[...]) * _silu(proj_ref[:, G_C + 128 * h:G_C + 128 * (h + 1)])
        mix_ref[:, W_A + 512 + 128 * h:W_A + 512 + 128 * (h + 1)] = yc.astype(bf16)

    def hgrn_intra(h):
        sl = slice(128 * h, 128 * (h + 1))
        gc = g_s[:, sl]
        qb = qb_s[:, sl]
        kb = kb_s[:, sl]
        inter = _dot((qb * jnp.exp(gc)).astype(bf16), sht[h].astype(bf16), NT)
        amat = None
        for bz, upper, same in levels:
            gref = jnp.concatenate([jnp.broadcast_to(g_s[pl.ds(st + bz // 2 - 1, 1), sl], (bz, 128))
                                    for st in range(0, BLK, bz)], axis=0)
            dg = gc - gref
            e = jnp.exp(jnp.where(upper, dg, -dg))
            qt = jnp.where(upper, qb * e, 0.0).astype(bf16)
            kt = jnp.where(upper, 0.0, kb * e).astype(bf16)
            pm = _dot(qt, kt, NT)
            if same is not None:
                pm = jnp.where(same, pm, 0.0)
            amat = pm if amat is None else amat + pm
        ys = []
        for s in range(SUB):
            ks = jnp.concatenate([jnp.broadcast_to(kb_s[pl.ds(SUB * j + s, 1), sl], (SUB, 128))
                                  for j in range(BLK // SUB)], axis=0)
            gs = jnp.concatenate([jnp.broadcast_to(g_s[pl.ds(SUB * j + s, 1), sl], (SUB, 128))
                                  for j in range(BLK // SUB)], axis=0)
            ys.append((qb * ks * jnp.exp(jnp.where(sub_ge[s], gc - gs, NEG_INF))).astype(bf16))
        amat = amat + jnp.where(same_sub, _dot(jnp.concatenate(ys, axis=1), dsel_ref[...]), 0.0)
        return inter, amat.astype(bf16)

    def hgrn_finish(h, inter, amat):
        sl = slice(128 * h, 128 * (h + 1))
        vb = proj_ref[:, I_B + 128 * h:I_B + 128 * (h + 1)].astype(bf16)
        ob = inter + _dot(amat, vb)
        glast = g_s[pl.ds(BLK - 1, 1), sl]
        kt2 = (kb_s[:, sl] * jnp.exp(glast - g_s[:, sl])).astype(bf16)
        sht[h] = sht[h] * jnp.exp(glast) + _dot(vb, kt2, TN)
        yb = _rms(ob, hn_ref[...]) * _silu(proj_ref[:, G_B + 128 * h:G_B + 128 * (h + 1)])
        mix_ref[:, W_A + 128 * h:W_A + 128 * (h + 1)] = yb.astype(bf16)

    pending = None
    for h in range(H_B):
        cur = hgrn_intra(h)
        if pending is not None:
            hgrn_finish(h - 1, *pending)
        retention_head(h)
        pending = cur
    hgrn_finish(H_B - 1, *pending)

    @pl.when(i == nb - 1)
    def _():
        for h in range(H_B):
            sh_ref[0, h] = sht[h].T
            sr_ref[0, h] = srs[h]


def _const_spec(shape):
    nd = len(shape)
    return pl.BlockSpec(shape, lambda *_: (0,) * nd)


def _slab_rows(nrows, steps):
    for r in range(16, nrows + 1, 16):
        if nrows % r == 0 and nrows // r <= steps:
            return r
    return nrows


def _mixer_prompt(proj, rows_total, batch, nb, layer, sinks, gpow, ropea, roper, qn, kn, lb, hn, rn, sel, expand,
                  tri, dsel, dec, rsc, tail, w_gate, w_up, w_down, w_out):
    smem = pl.BlockSpec(memory_space=pltpu.SMEM)
    d, dff = w_gate.shape[1], w_gate.shape[2]

    def row_blk(b, i):
        return jnp.where(i == 0, batch * (nb - 1), b * (nb - 1) + i - 1)

    def mix_blk(b, i):
        return jnp.where((i == 0) & (b > 0), b * (nb - 1), row_blk(b, i))

    def slab_in(nrows, ncols):
        r = _slab_rows(nrows, batch * nb)
        return pl.BlockSpec((None, r, ncols), lambda b, i: (layer, jnp.minimum(b * nb + i, nrows // r - 1), 0))

    def slab_out(nrows, ncols):
        r = _slab_rows(nrows, batch * nb)
        return pl.BlockSpec((r, ncols), lambda b, i: (jnp.minimum(b * nb + i, nrows // r - 1), 0))

    return pl.pallas_call(
        _mixer_prompt_kernel,
        grid=(batch, nb),
        in_specs=[smem, smem,
                  pl.BlockSpec((BLK, D_IN), lambda b, i: (row_blk(b, i), 0)),
                  pl.BlockSpec((2, BLK, 128), lambda b, i: (0, i, 0)),
                  pl.BlockSpec((2, BLK, 128), lambda b, i: (0, i, 0)),
                  _const_spec((1, 1024)), _const_spec((1, 256)), _const_spec((1, 512)),
                  _const_spec((1, 128)), _const_spec((1, 128)),
                  _const_spec((1024, 128)), _const_spec((128, 1024)),
                  _const_spec((BLK, BLK)), _const_spec((SUB * 128, 128)),
                  _const_spec((H_C, BLK, BLK)), _const_spec((H_C, BLK, 128)), _const_spec((H_C, BLK, 128)),
                  slab_in(d, dff), slab_in(d, dff), slab_in(dff, d), slab_in(d, d)],
        out_specs=[pl.BlockSpec((BLK, 2048), lambda b, i: (mix_blk(b, i), 0)),
                   pl.BlockSpec((1, BLK, 256), lambda b, i: (b, 0, 0)),
                   pl.BlockSpec((1, BLK, 256), lambda b, i: (b, 0, 0)),
                   pl.BlockSpec((1, H_B, 128, 128), lambda b, i: (b, 0, 0, 0)),
                   pl.BlockSpec((1, H_C, 128, 128), lambda b, i: (b, 0, 0, 0)),
                   slab_out(d, dff), slab_out(d, dff), slab_out(dff, d), slab_out(d, d)],
        out_shape=[jax.ShapeDtypeStruct((rows_total, 2048), bf16),
                   jax.ShapeDtypeStruct((batch, BLK, 256), f32),
                   jax.ShapeDtypeStruct((batch, BLK, 256), f32),
                   jax.ShapeDtypeStruct((batch, H_B, 128, 128), f32),
                   jax.ShapeDtypeStruct((batch, H_C, 128, 128), f32),
                   jax.ShapeDtypeStruct((d, dff), bf16), jax.ShapeDtypeStruct((d, dff), bf16),
                   jax.ShapeDtypeStruct((dff, d), bf16), jax.ShapeDtypeStruct((d, d), bf16)],
        scratch_shapes=[pltpu.VMEM((2 * BLK, N_KV_A * 128), bf16), pltpu.VMEM((2 * BLK, N_KV_A * 256), bf16),
                        pltpu.VMEM((H_B, 128, 128), f32), pltpu.VMEM((H_C, 128, 128), f32),
                        pltpu.VMEM((N_HEADS_A * BLK, 128), bf16),
                        pltpu.VMEM((BLK, 512), f32), pltpu.VMEM((BLK, 512), f32), pltpu.VMEM((BLK, 512), f32)],
        compiler_params=pltpu.CompilerParams(dimension_semantics=("arbitrary", "arbitrary"),
                                             vmem_limit_bytes=VMEM_LIMIT),
        name="mixer_prompt",
    )(sinks, gpow, proj, ropea, roper, qn, kn, lb, hn, rn, sel, expand, tri, dsel, dec, rsc, tail,
      w_gate, w_up, w_down, w_out)


def _mixer_sample_kernel(sink_ref, gpow_ref, mixin_ref, *refs, dec_seq, steps, n_prev):
    del mixin_ref
    refs = refs[:17] + refs[17 + n_prev:]
    c = pl.program_id(0)
    mix_ref = refs[17]
    pl.when(c < steps)(lambda: _mixer_sample_body(sink_ref, gpow_ref, *refs, dec_seq=dec_seq))

    @pl.when(c >= steps)
    def _():
        mix_ref[...] = jnp.zeros_like(mix_ref)


def _mixer_sample_body(sink_ref, gpow_ref, proj_ref, ropea_ref, roper_ref, qn_ref, kn_ref, lb_ref,
                       hn_ref, rn_ref, sel_ref, exp_ref, dec_ref, rsc_ref, tail_ref,
                       ck_ref, cv_ref, sh_ref, sr_ref,
                       mix_ref, nk_ref, nv_ref, nsh_ref, nsr_ref, *, dec_seq):
    nbat = SROWS // dec_seq
    w = ck_ref.shape[1]
    rb = lax.broadcasted_iota(jnp.int32, (SROWS, 1), 0) // dec_seq
    rt = lax.broadcasted_iota(jnp.int32, (SROWS, 1), 0) % dec_seq

    def pick(parts):
        out = parts[nbat - 1]
        for b in range(nbat - 2, -1, -1):
            out = jnp.where(rb == b, parts[b], out)
        return out

    qa = _rope_a(_head_norm(proj_ref[:, Q_A:Q_A + 1024], sel_ref, exp_ref, qn_ref[...]), ropea_ref)
    qs = (qa * (HEAD_DIM_A ** -0.5)).astype(bf16)
    k_new = _rope_a(_head_norm(proj_ref[:, K_A:K_A + 256], sel_ref, exp_ref, kn_ref[...]), ropea_ref)
    v_new = proj_ref[:, V_A:V_A + 256]
    for b in range(nbat):
        nk_ref[b, 0:w - dec_seq, :] = ck_ref[b, dec_seq:w, :]
        nk_ref[b, w - dec_seq:w, :] = k_new[dec_seq * b:dec_seq * (b + 1), :]
        nv_ref[b, 0:w - dec_seq, :] = cv_ref[b, dec_seq:w, :]
        nv_ref[b, w - dec_seq:w, :] = v_new[dec_seq * b:dec_seq * (b + 1), :]
    knb = k_new.astype(bf16)
    vnb = v_new.astype(bf16)

    grows = GROUP_A * SROWS
    rt_g = jnp.concatenate([rt] * GROUP_A, axis=0)
    rb_g = jnp.concatenate([rb] * GROUP_A, axis=0)
    jc = lax.broadcasted_iota(jnp.int32, (grows, w), 1)
    valid_c = jc > rt_g + (w - WINDOW)
    cn = lax.broadcasted_iota(jnp.int32, (grows, SROWS), 1)
    valid_n = (cn // dec_seq == rb_g) & (cn % dec_seq <= rt_g)
    heads = [None] * N_HEADS_A
    for g in range(N_KV_A):
        ksl = slice(64 * g, 64 * (g + 1))
        qg = jnp.concatenate([qs[:, 64 * h:64 * (h + 1)] for h in range(GROUP_A * g, GROUP_A * (g + 1))], axis=0)
        sparts = [_dot(qg, ck_ref[b, :, ksl].astype(bf16), NT) for b in range(nbat)]
        s_c = sparts[nbat - 1]
        for b in range(nbat - 2, -1, -1):
            s_c = jnp.where(rb_g == b, sparts[b], s_c)
        s_c = jnp.where(valid_c, s_c, NEG_INF)
        s_n = jnp.where(valid_n, _dot(qg, knb[:, ksl], NT), NEG_INF)
        sk = jnp.concatenate([jnp.full((SROWS, 1), sink_ref[h], f32)
                              for h in range(GROUP_A * g, GROUP_A * (g + 1))], axis=0)
        m = jnp.maximum(jnp.maximum(jnp.max(s_c, axis=-1, keepdims=True), jnp.max(s_n, axis=-1, keepdims=True)), sk)
        p_c = jnp.exp(s_c - m)
        p_n = jnp.exp(s_n - m)
        den = jnp.sum(p_c, axis=-1, keepdims=True) + jnp.sum(p_n, axis=-1, keepdims=True) + jnp.exp(sk - m)
        pcb = p_c.astype(bf16)
        oparts = [_dot(pcb, cv_ref[b, :, ksl].astype(bf16)) for b in range(nbat)]
        o = oparts[nbat - 1]
        for b in range(nbat - 2, -1, -1):
            o = jnp.where(rb_g == b, oparts[b], o)
        o = (o + _dot(p_n.astype(bf16), vnb[:, ksl])) / den
        for hh in range(GROUP_A):
            heads[GROUP_A * g + hh] = o[SROWS * hh:SROWS * (hh + 1), :]
    mix_ref[:, 0:W_A] = jnp.concatenate(heads, axis=1).astype(bf16)

    logf = _log_forget(proj_ref[:, F_B:F_B + 512], lb_ref[...])
    kb = 1.0 - jnp.exp(logf)
    qb = _silu(proj_ref[:, Q_B:Q_B + 512])
    vb = proj_ref[:, I_B:I_B + 512]
    gcum = logf
    for d in range(1, dec_seq):
        gcum = gcum + jnp.where(rt >= d, pltpu.roll(logf, d, 0), 0.0)
    cb = lax.broadcasted_iota(jnp.int32, (1, SROWS), 1) // dec_seq
    ct = lax.broadcasted_iota(jnp.int32, (1, SROWS), 1) % dec_seq
    intra = [jnp.zeros((SROWS, 128), f32) for _ in range(H_B)]
    for s in range(SROWS):
        ok = (rb == s // dec_seq) & (rt >= s % dec_seq)
        y = qb * kb[s:s + 1, :] * jnp.exp(jnp.where(ok, gcum - gcum[s:s + 1, :], NEG_INF))
        for h in range(H_B):
            sl = slice(128 * h, 128 * (h + 1))
            intra[h] = intra[h] + jnp.sum(y[:, sl], axis=-1, keepdims=True) * vb[s:s + 1, sl]
    qg = (qb * jnp.exp(gcum)).astype(bf16)
    vbb = vb.astype(bf16)
    glast = [gcum[dec_seq * (b + 1) - 1:dec_seq * (b + 1), :] for b in range(nbat)]
    for h in range(H_B):
        sl = slice(128 * h, 128 * (h + 1))
        ob = pick([_dot(qg[:, sl], sh_ref[b, h].astype(bf16)) for b in range(nbat)]) + intra[h]
        for b in range(nbat):
            kt2 = jnp.where(rb == b, kb[:, sl] * jnp.exp(glast[b][:, sl] - gcum[:, sl]), 0.0).astype(bf16)
            dcol = jnp.broadcast_to(jnp.exp(glast[b][:, sl]), (128, 128)).T
            nsh_ref[b, h] = dcol * sh_ref[b, h] + _dot(kt2, vbb[:, sl], TN)
        yb = _rms(ob, hn_ref[...]) * _silu(proj_ref[:, G_B + 128 * h:G_B + 128 * (h + 1)])
        mix_ref[:, W_A + 128 * h:W_A + 128 * (h + 1)] = yb.astype(bf16)

    for h in range(H_C):
        qc = _rope_r(proj_ref[:, Q_C + 128 * h:Q_C + 128 * (h + 1)], roper_ref)
        kc = _rope_r(proj_ref[:, K_C + 128 * h:K_C + 128 * (h + 1)], roper_ref) * (DK_C ** -0.5)
        vc = proj_ref[:, V_C + 128 * h:V_C + 128 * (h + 1)]
        qcb = qc.astype(bf16)
        vcb = vc.astype(bf16)
        sc = _dot(qcb, kc.astype(bf16), NT) * dec_ref[h]
        oc = pick([_dot(qcb, sr_ref[b, h].astype(bf16)) for b in range(nbat)]) * rsc_ref[h] \
            + _dot(sc.astype(bf16), vcb)
        kct = kc * tail_ref[h]
        for b in range(nbat):
            nsr_ref[b, h] = gpow_ref[h] * sr_ref[b, h] + _dot(jnp.where(rb == b, kct, 0.0).astype(bf16), vcb, TN)
        yc = _rms(oc, rn_ref[...]) * _silu(proj_ref[:, G_C + 128 * h:G_C + 128 * (h + 1)])
        mix_ref[:, W_A + 512 + 128 * h:W_A + 512 + 128 * (h + 1)] = yc.astype(bf16)


def _mixer_sample(mix, proj, row0, layer, dec_batch, dec_seq, sinks, gpow, ropea, roper, qn, kn, lb, hn, rn, sel, expand,
                  dec, rsc, tail, cache_k, cache_v, state_h, state_r, prev_outs):
    smem = pl.BlockSpec(memory_space=pltpu.SMEM)
    nbat = SROWS // dec_seq
    steps = dec_batch // nbat
    blk0 = row0 // SROWS
    w = cache_k.shape[1]
    n_all = cache_k.shape[0]
    fill_steps = (mix.shape[0] - row0) // SROWS - steps
    cl = lambda c: jnp.minimum(c, steps - 1)
    any_spec = pl.BlockSpec(memory_space=pl.ANY)
    n_in = 20
    return pl.pallas_call(
        functools.partial(_mixer_sample_kernel, dec_seq=dec_seq, steps=steps, n_prev=len(prev_outs)),
        grid=(steps + fill_steps,),
        in_specs=[smem, smem,
                  pl.BlockSpec(memory_space=pl.ANY),
                  pl.BlockSpec((SROWS, D_IN), lambda c: (blk0 + cl(c), 0)),
                  _const_spec((2, SROWS, 128)), _const_spec((2, SROWS, 128)),
                  _const_spec((1, 1024)), _const_spec((1, 256)), _const_spec((1, 512)),
                  _const_spec((1, 128)), _const_spec((1, 128)),
                  _const_spec((1024, 128)), _const_spec((128, 1024)),
                  _const_spec((H_C, SROWS, SROWS)), _const_spec((H_C, SROWS, 128)), _const_spec((H_C, SROWS, 128)),
                  pl.BlockSpec((nbat, w, 256), lambda c: (layer * steps + cl(c), 0, 0)),
                  pl.BlockSpec((nbat, w, 256), lambda c: (layer * steps + cl(c), 0, 0)),
                  pl.BlockSpec((nbat, H_B, 128, 128), lambda c: (layer * steps + cl(c), 0, 0, 0)),
                  pl.BlockSpec((nbat, H_C, 128, 128), lambda c: (layer * steps + cl(c), 0, 0, 0))]
        + [any_spec] * len(prev_outs),
        out_specs=[pl.BlockSpec((SROWS, 2048), lambda c: (blk0 + c, 0)),
                   pl.BlockSpec((nbat, w, 256), lambda c: (layer * steps + cl(c), 0, 0)),
                   pl.BlockSpec((nbat, w, 256), lambda c: (layer * steps + cl(c), 0, 0)),
                   pl.BlockSpec((nbat, H_B, 128, 128), lambda c: (layer * steps + cl(c), 0, 0, 0)),
                   pl.BlockSpec((nbat, H_C, 128, 128), lambda c: (layer * steps + cl(c), 0, 0, 0))],
        out_shape=[jax.ShapeDtypeStruct(mix.shape, bf16),
                   jax.ShapeDtypeStruct((n_all, w, 256), f32),
                   jax.ShapeDtypeStruct((n_all, w, 256), f32),
                   jax.ShapeDtypeStruct((n_all, H_B, 128, 128), f32),
                   jax.ShapeDtypeStruct((n_all, H_C, 128, 128), f32)],
        input_output_aliases={2: 0, **{n_in + k: 1 + k for k in range(len(prev_outs))}},
        compiler_params=pltpu.CompilerParams(dimension_semantics=("arbitrary",), vmem_limit_bytes=VMEM_LIMIT),
        name="mixer_sample",
    )(sinks, gpow, mix, proj, ropea, roper, qn, kn, lb, hn, rn, sel, expand, dec, rsc, tail,
      cache_k, cache_v, state_h, state_r, *prev_outs)


def _rope_tables(pos):
    posf = jnp.asarray(pos).astype(f32)[:, None]
    t = pos.shape[0]
    half = N_ROT // 2
    inv = ROPE_THETA ** (-jnp.arange(half, dtype=f32) * (2.0 / N_ROT))
    ang = posf * inv[None, :]
    cos, sin = jnp.cos(ang), jnp.sin(ang)
    rest0 = jnp.zeros((t, HEAD_DIM_A - N_ROT), f32)
    c64 = jnp.concatenate([cos, cos, rest0 + 1.0], axis=1)
    s64 = jnp.concatenate([sin, sin, rest0], axis=1)
    ropea = jnp.stack([jnp.tile(c64, (1, 2)), jnp.tile(s64, (1, 2))])
    invr = RET_THETA ** (-jnp.arange(DK_C // 2, dtype=f32) * (2.0 / DK_C))
    angr = posf * invr[None, :]
    cr, sr = jnp.cos(angr), jnp.sin(angr)
    roper = jnp.stack([jnp.concatenate([cr, cr], axis=1), jnp.concatenate([-sr, sr], axis=1)])
    return ropea, roper


def _ret_tables(lg, seq_of_row, tok_of_row, length):
    nf = np.float32
    tq = tok_of_row.astype(nf)
    rel = tq[:, None] - tq[None, :]
    ok = (seq_of_row[:, None] == seq_of_row[None, :]) & (rel >= 0)
    dec = np.where(ok[None], np.exp(np.where(ok, rel, nf(0.0))[None] * lg[:, None, None]), nf(0.0))
    n = tq.shape[0]
    rsc = np.broadcast_to(np.exp((tq + nf(1.0))[None, :, None] * lg[:, None, None]), (H_C, n, 128))
    tail = np.broadcast_to(np.exp((nf(length) - nf(1.0) - tq)[None, :, None] * lg[:, None, None]), (H_C, n, 128))
    gpow = np.exp(nf(length) * lg)
    return dec.astype(nf), np.ascontiguousarray(rsc, nf), np.ascontiguousarray(tail, nf), gpow.astype(nf)


def kernel(x_prompt, x_sample, cache_k, cache_v, state_hgrn, state_ret, meta_tokens, norm_mix, norm_ffn, w_in, q_norm, k_norm, attn_sinks, hgrn_lb, hgrn_norm, ret_norm, w_out, w_gate, w_up, w_down):
    batch, seq, d = x_prompt.shape
    dec_batch, dec_seq, _ = x_sample.shape
    depth = w_in.shape[0]
    w = cache_k.shape[2]
    assert d == 2048 and w_in.shape[2] == D_IN and seq % BLK == 0
    assert SROWS % dec_seq == 0 and dec_batch % (SROWS // dec_seq) == 0 and w == WINDOW
    nb = seq // BLK + 1
    lp = nb * BLK
    rows_main = batch * seq
    rows_s = dec_batch * dec_seq
    row_s0 = rows_main + BLK
    tail_rows = BLK + rows_s
    rows = rows_main + tail_rows
    n_main = rows_main // TM_F
    tm_all = rows // n_main
    assert rows_main % TM_F == 0 and rows_main % tail_rows == 0 and rows_s % SROWS == 0
    assert rows % n_main == 0 and tm_all % 16 == 0 and tail_rows % 16 == 0

    tail = jnp.concatenate([jnp.zeros((PAD, d), f32), meta_tokens.astype(f32), x_sample.reshape(rows_s, d)], axis=0)
    x, h = _embed(x_prompt.reshape(rows_main, d), tail, norm_mix[0][None])

    p = jax.nn.softmax(hgrn_lb.astype(f32), axis=0)
    lbs = jnp.cumsum(p, axis=0) - p[0]
    lg = np.log1p(-np.exp2(np.float32(-5.0) - np.arange(H_C, dtype=np.float32))).astype(np.float32)
    ropea_p, roper_p = _rope_tables(np.arange(lp) - PAD)
    srow = np.arange(SROWS)
    ropea_s, roper_s = _rope_tables(PAST_LEN + srow % dec_seq)
    dec_p, rsc_p, tail_p, gpow_p = _ret_tables(lg, np.zeros((BLK,), np.int32), np.arange(BLK), float(BLK))
    dec_s, rsc_s, tail_s, gpow_s = _ret_tables(lg, srow // dec_seq, srow % dec_seq, float(dec_seq))
    sel_np = np.arange(1024)[:, None] // HEAD_DIM_A == np.arange(128)[None, :]
    sel = jnp.asarray(sel_np, bf16)
    expand = jnp.asarray(sel_np.T, bf16)
    tri = jnp.asarray(np.arange(BLK)[:, None] >= np.arange(BLK)[None, :], bf16)
    dsel = jnp.asarray(np.arange(SUB * 128)[:, None] // 128 == np.arange(128)[None, :] % SUB, bf16)

    ck_flat = cache_k.reshape(depth * dec_batch, w, N_KV_A * HEAD_DIM_A)
    cv_flat = cache_v.reshape(depth * dec_batch, w, N_KV_A * HEAD_DIM_A)
    sh_flat = state_hgrn.reshape(depth * dec_batch, H_B, 128, 128)
    sr_flat = state_ret.reshape(depth * dec_batch, H_C, 128, 128)

    outs_p, outs_s = [], ()
    for l in range(depth):
        proj = _inproj(h, w_in, l)
        qn = jnp.tile(q_norm[l], N_HEADS_A)[None]
        kn = jnp.tile(k_norm[l], N_KV_A)[None]
        common = (qn, kn, lbs[l][None], hgrn_norm[l][None], ret_norm[l][None], sel, expand)
        mix, ck, cv, sh, sr, wg_b, wu_b, wd_b, wo_b = _mixer_prompt(
            proj, rows, batch, nb, l, attn_sinks[l], gpow_p, ropea_p, roper_p, *common, tri, dsel,
            dec_p, rsc_p, tail_p, w_gate, w_up, w_down, w_out)
        mix, *outs_s = _mixer_sample(mix, proj, row_s0, l, dec_batch, dec_seq, attn_sinks[l], gpow_s,
                                     ropea_s, roper_s, *common, dec_s, rsc_s, tail_s,
                                     ck_flat, cv_flat, sh_flat, sr_flat, tuple(outs_s))
        ffn_w = (wo_b, norm_ffn[l][None], wg_b, wu_b, wd_b)
        outs_p.append((ck, cv, sh, sr))
        if l + 1 < depth:
            x, h = _ffn(x, mix, *ffn_w, norm_mix[l + 1][None], tm_all, 0, n_main)
        else:
            (y_main,) = _ffn(x, mix, *ffn_w, None, TM_F, 0, n_main)
            (y_tail,) = _ffn(x, mix, *ffn_w, None, tail_rows, rows_main // tail_rows, 1)

    y_prompt = y_main.reshape(batch, seq, d)
    y_sample = y_tail[BLK:].reshape(dec_batch, dec_seq, d)
    kv_shape = (depth, -1, w, N_KV_A, HEAD_DIM_A)
    st_shape = (depth, dec_batch, H_B, 128, 128)
    stack = lambda outs, k: jnp.stack([o[k] for o in outs])
    nk, nv, nsh, nsr = outs_s
    return (y_prompt, y_sample,
            stack(outs_p, 0).reshape(kv_shape), stack(outs_p, 1).reshape(kv_shape), stack(outs_p, 2), stack(outs_p, 3),
            nk.reshape(kv_shape), nv.reshape(kv_shape), nsh.reshape(st_shape), nsr.reshape(st_shape))
```

```python
import functools

import jax
import jax.numpy as jnp
import numpy as np
from jax import lax
from jax.experimental import pallas as pl
from jax.experimental.pallas import tpu as pltpu

f32 = jnp.float32
bf16 = jnp.bfloat16

N_META = 16
EPS = 1e-6
NEG_INF = -1e30
LB_FLOOR = 1e-30
WINDOW = 128
HEAD_DIM_A = 64
N_HEADS_A = 16
N_KV_A = 4
GROUP_A = N_HEADS_A // N_KV_A
N_ROT = 16
ROPE_THETA = 500000.0
H_B = 4
DK_B = 128
H_C = 4
DK_C = 128
RET_THETA = 10000.0
PAST_LEN = 16384

BLK = 128
PAD = BLK - N_META
SUB = 8
SROWS = 16
TM_F = 512
VMEM_LIMIT = 56 * 1024 * 1024

Q_A, K_A, V_A = 0, 1024, 1280
Q_B, F_B, I_B, G_B = 1536, 2048, 2560, 3072
Q_C, K_C, V_C, G_C = 3584, 4096, 4608, 5120
D_IN = 5632
W_A = 1024
NT = (((1,), (1,)), ((), ()))
TN = (((0,), (0,)), ((), ()))


def _dot(a, b, dims=None):
    if dims is None:
        return jnp.dot(a, b, preferred_element_type=f32)
    return lax.dot_general(a, b, dims, preferred_element_type=f32)


def _row_tile(rows, cap):
    best = 8
    for t in range(8, cap + 1, 8):
        if rows % t == 0:
            best = t
    return best


def _silu(x):
    return x * jax.nn.sigmoid(x)


def _rms(x, g):
    return x * lax.rsqrt(jnp.mean(x * x, axis=-1, keepdims=True) + EPS) * g


def _inproj_kernel(h_ref, w_ref, o_ref):
    o_ref[...] = _dot(h_ref[...], w_ref[...].astype(bf16))


def _inproj(h, w_all, layer):
    rows, d = h.shape
    n = w_all.shape[2]
    tm = _row_tile(rows, 2176)
    tn = 512
    return pl.pallas_call(
        _inproj_kernel,
        grid=(rows // tm, n // tn),
        in_specs=[pl.BlockSpec((tm, d), lambda i, j: (i, 0)),
                  pl.BlockSpec((None, d, tn), lambda i, j: (layer, 0, j))],
        out_specs=pl.BlockSpec((tm, tn), lambda i, j: (i, j)),
        out_shape=jax.ShapeDtypeStruct((rows, n), f32),
        compiler_params=pltpu.CompilerParams(dimension_semantics=("arbitrary", "arbitrary"),
                                             vmem_limit_bytes=VMEM_LIMIT),
        name="inproj",
    )(h, w_all)


def _embed_kernel(xp_ref, tail_ref, g_ref, x_ref, h_ref):
    i = pl.program_id(0)
    last = pl.num_programs(0) - 1

    def emit(src_ref):
        x = src_ref[...]
        x_ref[...] = x
        h_ref[...] = _rms(x, g_ref[...]).astype(bf16)

    pl.when(i < last)(lambda: emit(xp_ref))
    pl.when(i == last)(lambda: emit(tail_ref))


def _embed(xp, tail, g):
    rows_main, d = xp.shape
    tm = tail.shape[0]
    n_main = rows_main // tm
    rows = rows_main + tm
    return pl.pallas_call(
        _embed_kernel,
        grid=(n_main + 1,),
        in_specs=[pl.BlockSpec((tm, d), lambda i: (jnp.minimum(i, n_main - 1), 0)),
                  pl.BlockSpec((tm, d), lambda i: (0, 0)),
                  pl.BlockSpec((1, d), lambda i: (0, 0))],
        out_specs=[pl.BlockSpec((tm, d), lambda i: (i, 0)),
                   pl.BlockSpec((tm, d), lambda i: (i, 0))],
        out_shape=[jax.ShapeDtypeStruct((rows, d), f32), jax.ShapeDtypeStruct((rows, d), bf16)],
        compiler_params=pltpu.CompilerParams(dimension_semantics=("arbitrary",), vmem_limit_bytes=VMEM_LIMIT),
        name="embed_norm",
    )(xp, tail, g)


def _ffn_kernel(*refs, with_next):
    if with_next:
        x_ref, mix_ref, wo_ref, nf_ref, wg_ref, wu_ref, wd_ref, gn_ref, o_ref, hn_ref, h_ref = refs
    else:
        x_ref, mix_ref, wo_ref, nf_ref, wg_ref, wu_ref, wd_ref, o_ref, h_ref = refs
    j = pl.program_id(1)

    @pl.when(j == 0)
    def _():
        x1 = x_ref[...] + _dot(mix_ref[...], wo_ref[...])
        h_ref[...] = _rms(x1, nf_ref[...]).astype(bf16)
        o_ref[...] = x1

    h = h_ref[...]
    a = _silu(_dot(h, wg_ref[...])) * _dot(h, wu_ref[...])
    o_ref[...] += _dot(a.astype(bf16), wd_ref[...])

    if with_next:
        @pl.when(j == pl.num_programs(1) - 1)
        def _():
            hn_ref[...] = _rms(o_ref[...], gn_ref[...]).astype(bf16)


def _ffn(x, mix, wo, nf, wg, wu, wd, g_next, tm, tile0, n_tiles):
    d = x.shape[1]
    dff = wg.shape[1]
    tf = 512
    with_next = g_next is not None
    row_spec = pl.BlockSpec((tm, d), lambda i, j: (tile0 + i, 0))
    out_spec = pl.BlockSpec((tm, d), lambda i, j: (i, 0))
    in_specs = [row_spec, row_spec,
                pl.BlockSpec((d, d), lambda i, j: (0, 0), pipeline_mode=pl.Buffered(1)),
                pl.BlockSpec((1, d), lambda i, j: (0, 0)),
                pl.BlockSpec((d, tf), lambda i, j: (0, j)),
                pl.BlockSpec((d, tf), lambda i, j: (0, j)),
                pl.BlockSpec((tf, d), lambda i, j: (j, 0))]
    args = [x, mix, wo, nf, wg, wu, wd]
    out_specs = [out_spec]
    out_shape = [jax.ShapeDtypeStruct((n_tiles * tm, d), f32)]
    if with_next:
        in_specs.append(pl.BlockSpec((1, d), lambda i, j: (0, 0)))
        args.append(g_next)
        out_specs.append(out_spec)
        out_shape.append(jax.ShapeDtypeStruct((n_tiles * tm, d), bf16))
    return pl.pallas_call(
        functools.partial(_ffn_kernel, with_next=with_next),
        grid=(n_tiles, dff // tf),
        in_specs=in_specs,
        out_specs=out_specs,
        out_shape=out_shape,
        scratch_shapes=[pltpu.VMEM((tm, d), bf16)],
        compiler_params=pltpu.CompilerParams(dimension_semantics=("arbitrary", "arbitrary"),
                                             vmem_limit_bytes=VMEM_LIMIT),
        name="outproj_ffn",
    )(*args)


def _head_norm(x, sel_ref, exp_ref, g):
    n, w = x.shape
    x2 = x * x
    hi = x2.astype(bf16)
    lo = (x2 - hi.astype(f32)).astype(bf16)
    s = _dot(jnp.concatenate([hi, lo], axis=0), sel_ref[0:w, :])
    return x * _dot_split3(lax.rsqrt((s[0:n] + s[n:2 * n]) * (1.0 / HEAD_DIM_A) + EPS), exp_ref[:, 0:w]) * g


def _dot_split3(x, m):
    n = x.shape[0]
    y = _dot(jnp.concatenate(_split3(x), axis=0), m)
    return y[0:n] + y[n:2 * n] + y[2 * n:3 * n]


def _rope_a(x, tab_ref):
    half = N_ROT // 2
    first = lax.broadcasted_iota(jnp.int32, (x.shape[0], 128), 1) % HEAD_DIM_A < half
    out = []
    for t in range(x.shape[1] // 128):
        xt = x[:, 128 * t:128 * (t + 1)]
        partner = jnp.where(first, -pltpu.roll(xt, 128 - half, 1), pltpu.roll(xt, half, 1))
        out.append(xt * tab_ref[0] + partner * tab_ref[1])
    return jnp.concatenate(out, axis=1)


def _rope_r(x, tab_ref):
    return x * tab_ref[0] + pltpu.roll(x, DK_C // 2, 1) * tab_ref[1]


def _forget(fb, lb):
    f = jnp.maximum(lb, LB_FLOOR) + (1.0 - lb) * jax.nn.sigmoid(fb)
    return f, jnp.log(f)


def _split3(x):
    h1 = x.astype(bf16)
    r1 = x - h1.astype(f32)
    h2 = r1.astype(bf16)
    h3 = (r1 - h2.astype(f32)).astype(bf16)
    return h1, h2, h3


def _mixer_prompt_kernel(sink_ref, gpow_ref, proj_ref, ropea_ref, roper_ref, qn_ref, kn_ref, lb_ref, hn_ref, rn_ref,
                         sel_ref, exp_ref, tri_ref, dsel_ref, dec_ref, rsc_ref, tail_ref,
                         wg_ref, wu_ref, wd_ref, wo_ref,
                         mix_ref, ck_ref, cv_ref, sh_ref, sr_ref, wgb_ref, wub_ref, wdb_ref, wob_ref,
                         kk, vv, sht, srs, qs_s, qb_s, kb_s, g_s):
    i = pl.program_id(1)
    nb = pl.num_programs(1)

    @pl.when(i == 0)
    def _():
        kk[...] = jnp.zeros_like(kk)
        vv[...] = jnp.zeros_like(vv)
        for g in range(N_KV_A):
            vv[:, 256 * g + 128:256 * (g + 1)] = jnp.ones((2 * BLK, 128), bf16)
        sht[...] = jnp.zeros_like(sht)
        srs[...] = jnp.zeros_like(srs)

    wgb_ref[...] = wg_ref[...].astype(bf16)
    wub_ref[...] = wu_ref[...].astype(bf16)
    wdb_ref[...] = wd_ref[...].astype(bf16)
    wob_ref[...] = wo_ref[...].astype(bf16)

    rowabs = i * BLK + lax.broadcasted_iota(jnp.int32, (BLK, 1), 0)
    fgate, logf = _forget(proj_ref[:, F_B:F_B + 512], lb_ref[...])
    logf = jnp.where(rowabs >= PAD, logf, 0.0)
    kb_s[...] = jnp.where(rowabs >= PAD, 1.0 - fgate, 0.0)
    qb_s[...] = _silu(proj_ref[:, Q_B:Q_B + 512])
    l1, l2, l3 = _split3(logf)
    tri = tri_ref[...]
    g_s[...] = _dot(tri, l1) + _dot(tri, l2) + _dot(tri, l3)

    r2 = lax.broadcasted_iota(jnp.int32, (BLK, BLK), 0)
    c2 = lax.broadcasted_iota(jnp.int32, (BLK, BLK), 1)
    lane_lo = c2 < HEAD_DIM_A
    qa = _rope_a(_head_norm(proj_ref[:, Q_A:Q_A + 1024], sel_ref, exp_ref, qn_ref[...]), ropea_ref)
    for t in range(N_HEADS_A // 2):
        xt = qa[:, 128 * t:128 * (t + 1)] * (HEAD_DIM_A ** -0.5)
        qs_s[256 * t:256 * t + 128, :] = jnp.where(lane_lo, xt, 0.0).astype(bf16)
        qs_s[256 * t + 128:256 * (t + 1), :] = jnp.where(lane_lo, 0.0, xt).astype(bf16)
    k_cur = _rope_a(_head_norm(proj_ref[:, K_A:K_A + 256], sel_ref, exp_ref, kn_ref[...]), ropea_ref)
    v_cur = proj_ref[:, V_A:V_A + 256]
    ck_ref[0] = k_cur
    cv_ref[0] = v_cur
    for t in range(N_KV_A // 2):
        for src, dst, width in ((k_cur, kk, 128), (v_cur, vv, 256)):
            xt = src[:, 128 * t:128 * (t + 1)]
            xs = pltpu.roll(xt, HEAD_DIM_A, 1)
            dst[BLK:2 * BLK, width * 2 * t:width * 2 * t + 128] = jnp.where(lane_lo, xt, xs).astype(bf16)
            dst[BLK:2 * BLK, width * (2 * t + 1):width * (2 * t + 1) + 128] = jnp.where(lane_lo, xs, xt).astype(bf16)

    prow = 2 * BLK
    r4 = lax.broadcasted_iota(jnp.int32, (prow, BLK), 0) % BLK
    c4 = lax.broadcasted_iota(jnp.int32, (prow, BLK), 1)
    up = c4 > r4
    ok = c4 >= jnp.where(up, PAD - (i - 1) * BLK, PAD - i * BLK)

    def attention_pair(t):
        g = (2 * t) // GROUP_A
        s2 = _dot(qs_s[prow * t:prow * (t + 1), :], kk[:, 128 * g:128 * (g + 1)], NT)
        s = jnp.where(ok, jnp.where(up, s2[:, 0:BLK], s2[:, BLK:2 * BLK]), NEG_INF)
        sk = jnp.concatenate([jnp.full((BLK, BLK), sink_ref[2 * t + j], f32) for j in range(2)], axis=0)
        m = jnp.maximum(jnp.max(s, axis=-1, keepdims=True), sk)
        p = jnp.exp(s - m)
        p2 = jnp.concatenate([jnp.where(up, p, 0.0), jnp.where(up, 0.0, p)], axis=1).astype(bf16)
        res = _dot(p2, vv[:, 256 * g:256 * (g + 1)])
        y = res[:, 0:128] / (res[:, 128:256] + jnp.exp(sk - m))
        mix_ref[:, 128 * t:128 * (t + 1)] = jnp.where(lane_lo, y[0:BLK], y[BLK:prow]).astype(bf16)

    for t in range(N_HEADS_A // 2):
        attention_pair(t)
    kk[0:BLK, :] = kk[BLK:2 * BLK, :]
    vv[0:BLK, :] = vv[BLK:2 * BLK, :]

    levels = []
    bz = BLK
    while bz > SUB:
        levels.append((bz, (r2 % bz) >= bz // 2, None if bz == BLK else (r2 // bz) == (c2 // bz)))
        bz //= 2
    same_sub = (r2 // SUB) == (c2 // SUB)
    sub_ge = [(r2 % SUB) >= s for s in range(SUB)]

    def retention_head(h):
        qc = _rope_r(proj_ref[:, Q_C + 128 * h:Q_C + 128 * (h + 1)], roper_ref)
        kc = _rope_r(proj_ref[:, K_C + 128 * h:K_C + 128 * (h + 1)], roper_ref) * (DK_C ** -0.5)
        vc = proj_ref[:, V_C + 128 * h:V_C + 128 * (h + 1)].astype(bf16)
        qcb = qc.astype(bf16)
        sc = _dot(qcb, kc.astype(bf16), NT) * dec_ref[h]
        oc = _dot(qcb, srs[h].astype(bf16)) * rsc_ref[h] + _dot(sc.astype(bf16), vc)
        srs[h] = gpow_ref[h] * srs[h] + _dot((kc * tail_ref[h]).astype(bf16), vc, TN)
        yc = _rms(oc, rn_ref[...]) * _silu(proj_ref[:, G_C + 128 * h:G_C + 128 * (h + 1)])
        mix_ref[:, W_A + 512 + 128 * h:W_A + 512 + 128 * (h + 1)] = yc.astype(bf16)

    def hgrn_intra(h):
        sl = slice(128 * h, 128 * (h + 1))
        gc = g_s[:, sl]
        qb = qb_s[:, sl]
        kb = kb_s[:, sl]
        inter = _dot((qb * jnp.exp(gc)).astype(bf16), sht[h].astype(bf16), NT)
        amat = None
        for bz, upper, same in levels:
            gref = jnp.concatenate([jnp.broadcast_to(g_s[pl.ds(st + bz // 2 - 1, 1), sl], (bz, 128))
                                    for st in range(0, BLK, bz)], axis=0)
            dg = gc - gref
            e = jnp.exp(jnp.where(upper, dg, -dg))
            qt = jnp.where(upper, qb * e, 0.0).astype(bf16)
            kt = jnp.where(upper, 0.0, kb * e).astype(bf16)
            pm = _dot(qt, kt, NT)
            if same is not None:
                pm = jnp.where(same, pm, 0.0)
            amat = pm if amat is None else amat + pm
        ys = []
        for s in range(SUB):
            ks = jnp.concatenate([jnp.broadcast_to(kb_s[pl.ds(SUB * j + s, 1), sl], (SUB, 128))
                                  for j in range(BLK // SUB)], axis=0)
            gs = jnp.concatenate([jnp.broadcast_to(g_s[pl.ds(SUB * j + s, 1), sl], (SUB, 128))
                                  for j in range(BLK // SUB)], axis=0)
            ys.append((qb * ks * jnp.exp(jnp.where(sub_ge[s], gc - gs, NEG_INF))).astype(bf16))
        amat = amat + jnp.where(same_sub, _dot(jnp.concatenate(ys, axis=1), dsel_ref[...]), 0.0)
        return inter, amat.astype(bf16)

    def hgrn_finish(h, inter, amat):
        sl = slice(128 * h, 128 * (h + 1))
        vb = proj_ref[:, I_B + 128 * h:I_B + 128 * (h + 1)].astype(bf16)
        ob = inter + _dot(amat, vb)
        glast = g_s[pl.ds(BLK - 1, 1), sl]
        kt2 = (kb_s[:, sl] * jnp.exp(glast - g_s[:, sl])).astype(bf16)
        sht[h] = sht[h] * jnp.exp(glast) + _dot(vb, kt2, TN)
        yb = _rms(ob, hn_ref[...]) * _silu(proj_ref[:, G_B + 128 * h:G_B + 128 * (h + 1)])
        mix_ref[:, W_A + 128 * h:W_A + 128 * (h + 1)] = yb.astype(bf16)

    pending = None
    for h in range(H_B):
        cur = hgrn_intra(h)
        if pending is not None:
            hgrn_finish(h - 1, *pending)
        retention_head(h)
        pending = cur
    hgrn_finish(H_B - 1, *pending)

    @pl.when(i == nb - 1)
    def _():
        for h in range(H_B):
            sh_ref[0, h] = sht[h].T
            sr_ref[0, h] = srs[h]


def _const_spec(shape):
    nd = len(shape)
    return pl.BlockSpec(shape, lambda *_: (0,) * nd)


def _slab_rows(nrows, steps):
    for r in range(16, nrows + 1, 16):
        if nrows % r == 0 and nrows // r <= steps:
            return r
    return nrows


def _mixer_prompt(proj, rows_total, batch, nb, layer, sinks, gpow, ropea, roper, qn, kn, lb, hn, rn, sel, expand,
                  tri, dsel, dec, rsc, tail, w_gate, w_up, w_down, w_out):
    smem = pl.BlockSpec(memory_space=pltpu.SMEM)
    d, dff = w_gate.shape[1], w_gate.shape[2]

    def row_blk(b, i):
        return jnp.where(i == 0, batch * (nb - 1), b * (nb - 1) + i - 1)

    def mix_blk(b, i):
        return jnp.where((i == 0) & (b > 0), b * (nb - 1), row_blk(b, i))

    def slab_in(nrows, ncols):
        r = _slab_rows(nrows, batch * nb)
        return pl.BlockSpec((None, r, ncols), lambda b, i: (layer, jnp.minimum(b * nb + i, nrows // r - 1), 0))

    def slab_out(nrows, ncols):
        r = _slab_rows(nrows, batch * nb)
        return pl.BlockSpec((r, ncols), lambda b, i: (jnp.minimum(b * nb + i, nrows // r - 1), 0))

    return pl.pallas_call(
        _mixer_prompt_kernel,
        grid=(batch, nb),
        in_specs=[smem, smem,
                  pl.BlockSpec((BLK, D_IN), lambda b, i: (row_blk(b, i), 0)),
                  pl.BlockSpec((2, BLK, 128), lambda b, i: (0, i, 0)),
                  pl.BlockSpec((2, BLK, 128), lambda b, i: (0, i, 0)),
                  _const_spec((1, 1024)), _const_spec((1, 256)), _const_spec((1, 512)),
                  _const_spec((1, 128)), _const_spec((1, 128)),
                  _const_spec((1024, 128)), _const_spec((128, 1024)),
                  _const_spec((BLK, BLK)), _const_spec((SUB * 128, 128)),
                  _const_spec((H_C, BLK, BLK)), _const_spec((H_C, BLK, 128)), _const_spec((H_C, BLK, 128)),
                  slab_in(d, dff), slab_in(d, dff), slab_in(dff, d), slab_in(d, d)],
        out_specs=[pl.BlockSpec((BLK, 2048), lambda b, i: (mix_blk(b, i), 0)),
                   pl.BlockSpec((1, BLK, 256), lambda b, i: (b, 0, 0)),
                   pl.BlockSpec((1, BLK, 256), lambda b, i: (b, 0, 0)),
                   pl.BlockSpec((1, H_B, 128, 128), lambda b, i: (b, 0, 0, 0)),
                   pl.BlockSpec((1, H_C, 128, 128), lambda b, i: (b, 0, 0, 0)),
                   slab_out(d, dff), slab_out(d, dff), slab_out(dff, d), slab_out(d, d)],
        out_shape=[jax.ShapeDtypeStruct((rows_total, 2048), bf16),
                   jax.ShapeDtypeStruct((batch, BLK, 256), f32),
                   jax.ShapeDtypeStruct((batch, BLK, 256), f32),
                   jax.ShapeDtypeStruct((batch, H_B, 128, 128), f32),
                   jax.ShapeDtypeStruct((batch, H_C, 128, 128), f32),
                   jax.ShapeDtypeStruct((d, dff), bf16), jax.ShapeDtypeStruct((d, dff), bf16),
                   jax.ShapeDtypeStruct((dff, d), bf16), jax.ShapeDtypeStruct((d, d), bf16)],
        scratch_shapes=[pltpu.VMEM((2 * BLK, N_KV_A * 128), bf16), pltpu.VMEM((2 * BLK, N_KV_A * 256), bf16),
                        pltpu.VMEM((H_B, 128, 128), f32), pltpu.VMEM((H_C, 128, 128), f32),
                        pltpu.VMEM((N_HEADS_A * BLK, 128), bf16),
                        pltpu.VMEM((BLK, 512), f32), pltpu.VMEM((BLK, 512), f32), pltpu.VMEM((BLK, 512), f32)],
        compiler_params=pltpu.CompilerParams(dimension_semantics=("arbitrary", "arbitrary"),
                                             vmem_limit_bytes=VMEM_LIMIT),
        name="mixer_prompt",
    )(sinks, gpow, proj, ropea, roper, qn, kn, lb, hn, rn, sel, expand, tri, dsel, dec, rsc, tail,
      w_gate, w_up, w_down, w_out)


def _mixer_sample_kernel(sink_ref, gpow_ref, mixin_ref, *refs, dec_seq, steps, n_prev):
    del mixin_ref
    refs = refs[:17] + refs[17 + n_prev:]
    c = pl.program_id(0)
    mix_ref = refs[17]
    pl.when(c < steps)(lambda: _mixer_sample_body(sink_ref, gpow_ref, *refs, dec_seq=dec_seq))

    @pl.when(c >= steps)
    def _():
        mix_ref[...] = jnp.zeros_like(mix_ref)


def _mixer_sample_body(sink_ref, gpow_ref, proj_ref, ropea_ref, roper_ref, qn_ref, kn_ref, lb_ref,
                       hn_ref, rn_ref, sel_ref, exp_ref, dec_ref, rsc_ref, tail_ref,
                       ck_ref, cv_ref, sh_ref, sr_ref,
                       mix_ref, nk_ref, nv_ref, nsh_ref, nsr_ref, *, dec_seq):
    nbat = SROWS // dec_seq
    w = ck_ref.shape[1]
    rb = lax.broadcasted_iota(jnp.int32, (SROWS, 1), 0) // dec_seq
    rt = lax.broadcasted_iota(jnp.int32, (SROWS, 1), 0) % dec_seq

    def pick(parts):
        out = parts[nbat - 1]
        for b in range(nbat - 2, -1, -1):
            out = jnp.where(rb == b, parts[b], out)
        return out

    qa = _rope_a(_head_norm(proj_ref[:, Q_A:Q_A + 1024], sel_ref, exp_ref, qn_ref[...]), ropea_ref)
    qs = (qa * (HEAD_DIM_A ** -0.5)).astype(bf16)
    k_new = _rope_a(_head_norm(proj_ref[:, K_A:K_A + 256], sel_ref, exp_ref, kn_ref[...]), ropea_ref)
    v_new = proj_ref[:, V_A:V_A + 256]
    for b in range(nbat):
        nk_ref[b, 0:w - dec_seq, :] = ck_ref[b, dec_seq:w, :]
        nk_ref[b, w - dec_seq:w, :] = k_new[dec_seq * b:dec_seq * (b + 1), :]
        nv_ref[b, 0:w - dec_seq, :] = cv_ref[b, dec_seq:w, :]
        nv_ref[b, w - dec_seq:w, :] = v_new[dec_seq * b:dec_seq * (b + 1), :]
    knb = k_new.astype(bf16)
    vnb = v_new.astype(bf16)

    grows = GROUP_A * SROWS
    rt_g = jnp.concatenate([rt] * GROUP_A, axis=0)
    rb_g = jnp.concatenate([rb] * GROUP_A, axis=0)
    jc = lax.broadcasted_iota(jnp.int32, (grows, w), 1)
    valid_c = jc > rt_g + (w - WINDOW)
    cn = lax.broadcasted_iota(jnp.int32, (grows, SROWS), 1)
    valid_n = (cn // dec_seq == rb_g) & (cn % dec_seq <= rt_g)
    heads = [None] * N_HEADS_A
    for g in range(N_KV_A):
        ksl = slice(64 * g, 64 * (g + 1))
        qg = jnp.concatenate([qs[:, 64 * h:64 * (h + 1)] for h in range(GROUP_A * g, GROUP_A * (g + 1))], axis=0)
        sparts = [_dot(qg, ck_ref[b, :, ksl].astype(bf16), NT) for b in range(nbat)]
        s_c = sparts[nbat - 1]
        for b in range(nbat - 2, -1, -1):
            s_c = jnp.where(rb_g == b, sparts[b], s_c)
        s_c = jnp.where(valid_c, s_c, NEG_INF)
        s_n = jnp.where(valid_n, _dot(qg, knb[:, ksl], NT), NEG_INF)
        sk = jnp.concatenate([jnp.full((SROWS, 1), sink_ref[h], f32)
                              for h in range(GROUP_A * g, GROUP_A * (g + 1))], axis=0)
        m = jnp.maximum(jnp.maximum(jnp.max(s_c, axis=-1, keepdims=True), jnp.max(s_n, axis=-1, keepdims=True)), sk)
        p_c = jnp.exp(s_c - m)
        p_n = jnp.exp(s_n - m)
        den = jnp.sum(p_c, axis=-1, keepdims=True) + jnp.sum(p_n, axis=-1, keepdims=True) + jnp.exp(sk - m)
        pcb = p_c.astype(bf16)
        oparts = [_dot(pcb, cv_ref[b, :, ksl].astype(bf16)) for b in range(nbat)]
        o = oparts[nbat - 1]
        for b in range(nbat - 2, -1, -1):
            o = jnp.where(rb_g == b, oparts[b], o)
        o = (o + _dot(p_n.astype(bf16), vnb[:, ksl])) / den
        for hh in range(GROUP_A):
            heads[GROUP_A * g + hh] = o[SROWS * hh:SROWS * (hh + 1), :]
    mix_ref[:, 0:W_A] = jnp.concatenate(heads, axis=1).astype(bf16)

    fgate, logf = _forget(proj_ref[:, F_B:F_B + 512], lb_ref[...])
    kb = 1.0 - fgate
    qb = _silu(proj_ref[:, Q_B:Q_B + 512])
    vb = proj_ref[:, I_B:I_B + 512]
    gcum = logf
    for d in range(1, dec_seq):
        gcum = gcum + jnp.where(rt >= d, pltpu.roll(logf, d, 0), 0.0)
    cb = lax.broadcasted_iota(jnp.int32, (1, SROWS), 1) // dec_seq
    ct = lax.broadcasted_iota(jnp.int32, (1, SROWS), 1) % dec_seq
    intra = [jnp.zeros((SROWS, 128), f32) for _ in range(H_B)]
    for s in range(SROWS):
        ok = (rb == s // dec_seq) & (rt >= s % dec_seq)
        y = qb * kb[s:s + 1, :] * jnp.exp(jnp.where(ok, gcum - gcum[s:s + 1, :], NEG_INF))
        for h in range(H_B):
            sl = slice(128 * h, 128 * (h + 1))
            intra[h] = intra[h] + jnp.sum(y[:, sl], axis=-1, keepdims=True) * vb[s:s + 1, sl]
    qg = (qb * jnp.exp(gcum)).astype(bf16)
    vbb = vb.astype(bf16)
    glast = [gcum[dec_seq * (b + 1) - 1:dec_seq * (b + 1), :] for b in range(nbat)]
    for h in range(H_B):
        sl = slice(128 * h, 128 * (h + 1))
        ob = pick([_dot(qg[:, sl], sh_ref[b, h].astype(bf16)) for b in range(nbat)]) + intra[h]
        for b in range(nbat):
            kt2 = jnp.where(rb == b, kb[:, sl] * jnp.exp(glast[b][:, sl] - gcum[:, sl]), 0.0).astype(bf16)
            dcol = jnp.broadcast_to(jnp.exp(glast[b][:, sl]), (128, 128)).T
            nsh_ref[b, h] = dcol * sh_ref[b, h] + _dot(kt2, vbb[:, sl], TN)
        yb = _rms(ob, hn_ref[...]) * _silu(proj_ref[:, G_B + 128 * h:G_B + 128 * (h + 1)])
        mix_ref[:, W_A + 128 * h:W_A + 128 * (h + 1)] = yb.astype(bf16)

    for h in range(H_C):
        qc = _rope_r(proj_ref[:, Q_C + 128 * h:Q_C + 128 * (h + 1)], roper_ref)
        kc = _rope_r(proj_ref[:, K_C + 128 * h:K_C + 128 * (h + 1)], roper_ref) * (DK_C ** -0.5)
        vc = proj_ref[:, V_C + 128 * h:V_C + 128 * (h + 1)]
        qcb = qc.astype(bf16)
        vcb = vc.astype(bf16)
        sc = _dot(qcb, kc.astype(bf16), NT) * dec_ref[h]
        oc = pick([_dot(qcb, sr_ref[b, h].astype(bf16)) for b in range(nbat)]) * rsc_ref[h] \
            + _dot(sc.astype(bf16), vcb)
        kct = kc * tail_ref[h]
        for b in range(nbat):
            nsr_ref[b, h] = gpow_ref[h] * sr_ref[b, h] + _dot(jnp.where(rb == b, kct, 0.0).astype(bf16), vcb, TN)
        yc = _rms(oc, rn_ref[...]) * _silu(proj_ref[:, G_C + 128 * h:G_C + 128 * (h + 1)])
        mix_ref[:, W_A + 512 + 128 * h:W_A + 512 + 128 * (h + 1)] = yc.astype(bf16)


def _mixer_sample(mix, proj, row0, layer, dec_batch, dec_seq, sinks, gpow, ropea, roper, qn, kn, lb, hn, rn, sel, expand,
                  dec, rsc, tail, cache_k, cache_v, state_h, state_r, prev_outs):
    smem = pl.BlockSpec(memory_space=pltpu.SMEM)
    nbat = SROWS // dec_seq
    steps = dec_batch // nbat
    blk0 = row0 // SROWS
    w = cache_k.shape[1]
    n_all = cache_k.shape[0]
    fill_steps = (mix.shape[0] - row0) // SROWS - steps
    cl = lambda c: jnp.minimum(c, steps - 1)
    any_spec = pl.BlockSpec(memory_space=pl.ANY)
    n_in = 20
    return pl.pallas_call(
        functools.partial(_mixer_sample_kernel, dec_seq=dec_seq, steps=steps, n_prev=len(prev_outs)),
        grid=(steps + fill_steps,),
        in_specs=[smem, smem,
                  pl.BlockSpec(memory_space=pl.ANY),
                  pl.BlockSpec((SROWS, D_IN), lambda c: (blk0 + cl(c), 0)),
                  _const_spec((2, SROWS, 128)), _const_spec((2, SROWS, 128)),
                  _const_spec((1, 1024)), _const_spec((1, 256)), _const_spec((1, 512)),
                  _const_spec((1, 128)), _const_spec((1, 128)),
                  _const_spec((1024, 128)), _const_spec((128, 1024)),
                  _const_spec((H_C, SROWS, SROWS)), _const_spec((H_C, SROWS, 128)), _const_spec((H_C, SROWS, 128)),
                  pl.BlockSpec((nbat, w, 256), lambda c: (layer * steps + cl(c), 0, 0)),
                  pl.BlockSpec((nbat, w, 256), lambda c: (layer * steps + cl(c), 0, 0)),
                  pl.BlockSpec((nbat, H_B, 128, 128), lambda c: (layer * steps + cl(c), 0, 0, 0)),
                  pl.BlockSpec((nbat, H_C, 128, 128), lambda c: (layer * steps + cl(c), 0, 0, 0))]
        + [any_spec] * len(prev_outs),
        out_specs=[pl.BlockSpec((SROWS, 2048), lambda c: (blk0 + c, 0)),
                   pl.BlockSpec((nbat, w, 256), lambda c: (layer * steps + cl(c), 0, 0)),
                   pl.BlockSpec((nbat, w, 256), lambda c: (layer * steps + cl(c), 0, 0)),
                   pl.BlockSpec((nbat, H_B, 128, 128), lambda c: (layer * steps + cl(c), 0, 0, 0)),
                   pl.BlockSpec((nbat, H_C, 128, 128), lambda c: (layer * steps + cl(c), 0, 0, 0))],
        out_shape=[jax.ShapeDtypeStruct(mix.shape, bf16),
                   jax.ShapeDtypeStruct((n_all, w, 256), f32),
                   jax.ShapeDtypeStruct((n_all, w, 256), f32),
                   jax.ShapeDtypeStruct((n_all, H_B, 128, 128), f32),
                   jax.ShapeDtypeStruct((n_all, H_C, 128, 128), f32)],
        input_output_aliases={2: 0, **{n_in + k: 1 + k for k in range(len(prev_outs))}},
        compiler_params=pltpu.CompilerParams(dimension_semantics=("arbitrary",), vmem_limit_bytes=VMEM_LIMIT),
        name="mixer_sample",
    )(sinks, gpow, mix, proj, ropea, roper, qn, kn, lb, hn, rn, sel, expand, dec, rsc, tail,
      cache_k, cache_v, state_h, state_r, *prev_outs)


def _rope_tables(pos):
    posf = jnp.asarray(pos).astype(f32)[:, None]
    t = pos.shape[0]
    half = N_ROT // 2
    inv = ROPE_THETA ** (-jnp.arange(half, dtype=f32) * (2.0 / N_ROT))
    ang = posf * inv[None, :]
    cos, sin = jnp.cos(ang), jnp.sin(ang)
    rest0 = jnp.zeros((t, HEAD_DIM_A - N_ROT), f32)
    c64 = jnp.concatenate([cos, cos, rest0 + 1.0], axis=1)
    s64 = jnp.concatenate([sin, sin, rest0], axis=1)
    ropea = jnp.stack([jnp.tile(c64, (1, 2)), jnp.tile(s64, (1, 2))])
    invr = RET_THETA ** (-jnp.arange(DK_C // 2, dtype=f32) * (2.0 / DK_C))
    angr = posf * invr[None, :]
    cr, sr = jnp.cos(angr), jnp.sin(angr)
    roper = jnp.stack([jnp.concatenate([cr, cr], axis=1), jnp.concatenate([-sr, sr], axis=1)])
    return ropea, roper


def _ret_tables(lg, seq_of_row, tok_of_row, length):
    nf = np.float32
    tq = tok_of_row.astype(nf)
    rel = tq[:, None] - tq[None, :]
    ok = (seq_of_row[:, None] == seq_of_row[None, :]) & (rel >= 0)
    dec = np.where(ok[None], np.exp(np.where(ok, rel, nf(0.0))[None] * lg[:, None, None]), nf(0.0))
    n = tq.shape[0]
    rsc = np.broadcast_to(np.exp((tq + nf(1.0))[None, :, None] * lg[:, None, None]), (H_C, n, 128))
    tail = np.broadcast_to(np.exp((nf(length) - nf(1.0) - tq)[None, :, None] * lg[:, None, None]), (H_C, n, 128))
    gpow = np.exp(nf(length) * lg)
    return dec.astype(nf), np.ascontiguousarray(rsc, nf), np.ascontiguousarray(tail, nf), gpow.astype(nf)


def kernel(x_prompt, x_sample, cache_k, cache_v, state_hgrn, state_ret, meta_tokens, norm_mix, norm_ffn, w_in, q_norm, k_norm, attn_sinks, hgrn_lb, hgrn_norm, ret_norm, w_out, w_gate, w_up, w_down):
    batch, seq, d = x_prompt.shape
    dec_batch, dec_seq, _ = x_sample.shape
    depth = w_in.shape[0]
    w = cache_k.shape[2]
    assert d == 2048 and w_in.shape[2] == D_IN and seq % BLK == 0
    assert SROWS % dec_seq == 0 and dec_batch % (SROWS // dec_seq) == 0 and w == WINDOW
    nb = seq // BLK + 1
    lp = nb * BLK
    rows_main = batch * seq
    rows_s = dec_batch * dec_seq
    row_s0 = rows_main + BLK
    tail_rows = BLK + rows_s
    rows = rows_main + tail_rows
    n_main = rows_main // TM_F
    tm_all = rows // n_main
    assert rows_main % TM_F == 0 and rows_main % tail_rows == 0 and rows_s % SROWS == 0
    assert rows % n_main == 0 and tm_all % 16 == 0 and tail_rows % 16 == 0

    tail = jnp.concatenate([jnp.zeros((PAD, d), f32), meta_tokens.astype(f32), x_sample.reshape(rows_s, d)], axis=0)
    x, h = _embed(x_prompt.reshape(rows_main, d), tail, norm_mix[0][None])

    p = jax.nn.softmax(hgrn_lb.astype(f32), axis=0)
    lbs = jnp.cumsum(p, axis=0) - p[0]
    lg = np.log1p(-np.exp2(np.float32(-5.0) - np.arange(H_C, dtype=np.float32))).astype(np.float32)
    ropea_p, roper_p = _rope_tables(np.arange(lp) - PAD)
    srow = np.arange(SROWS)
    ropea_s, roper_s = _rope_tables(PAST_LEN + srow % dec_seq)
    dec_p, rsc_p, tail_p, gpow_p = _ret_tables(lg, np.zeros((BLK,), np.int32), np.arange(BLK), float(BLK))
    dec_s, rsc_s, tail_s, gpow_s = _ret_tables(lg, srow // dec_seq, srow % dec_seq, float(dec_seq))
    sel_np = np.arange(1024)[:, None] // HEAD_DIM_A == np.arange(128)[None, :]
    sel = jnp.asarray(sel_np, bf16)
    expand = jnp.asarray(sel_np.T, bf16)
    tri = jnp.asarray(np.arange(BLK)[:, None] >= np.arange(BLK)[None, :], bf16)
    dsel = jnp.asarray(np.arange(SUB * 128)[:, None] // 128 == np.arange(128)[None, :] % SUB, bf16)

    ck_flat = cache_k.reshape(depth * dec_batch, w, N_KV_A * HEAD_DIM_A)
    cv_flat = cache_v.reshape(depth * dec_batch, w, N_KV_A * HEAD_DIM_A)
    sh_flat = state_hgrn.reshape(depth * dec_batch, H_B, 128, 128)
    sr_flat = state_ret.reshape(depth * dec_batch, H_C, 128, 128)

    outs_p, outs_s = [], ()
    for l in range(depth):
        proj = _inproj(h, w_in, l)
        qn = jnp.tile(q_norm[l], N_HEADS_A)[None]
        kn = jnp.tile(k_norm[l], N_KV_A)[None]
        common = (qn, kn, lbs[l][None], hgrn_norm[l][None], ret_norm[l][None], sel, expand)
        mix, ck, cv, sh, sr, wg_b, wu_b, wd_b, wo_b = _mixer_prompt(
            proj, rows, batch, nb, l, attn_sinks[l], gpow_p, ropea_p, roper_p, *common, tri, dsel,
            dec_p, rsc_p, tail_p, w_gate, w_up, w_down, w_out)
        mix, *outs_s = _mixer_sample(mix, proj, row_s0, l, dec_batch, dec_seq, attn_sinks[l], gpow_s,
                                     ropea_s, roper_s, *common, dec_s, rsc_s, tail_s,
                                     ck_flat, cv_flat, sh_flat, sr_flat, tuple(outs_s))
        ffn_w = (wo_b, norm_ffn[l][None], wg_b, wu_b, wd_b)
        outs_p.append((ck, cv, sh, sr))
        if l + 1 < depth:
            x, h = _ffn(x, mix, *ffn_w, norm_mix[l + 1][None], tm_all, 0, n_main)
        else:
            (y_main,) = _ffn(x, mix, *ffn_w, None, TM_F, 0, n_main)
            (y_tail,) = _ffn(x, mix, *ffn_w, None, tail_rows, rows_main // tail_rows, 1)

    y_prompt = y_main.reshape(batch, seq, d)
    y_sample = y_tail[BLK:].reshape(dec_batch, dec_seq, d)
    kv_shape = (depth, -1, w, N_KV_A, HEAD_DIM_A)
    st_shape = (depth, dec_batch, H_B, 128, 128)
    stack = lambda outs, k: jnp.stack([o[k] for o in outs])
    nk, nv, nsh, nsr = outs_s
    return (y_prompt, y_sample,
            stack(outs_p, 0).reshape(kv_shape), stack(outs_p, 1).reshape(kv_shape), stack(outs_p, 2), stack(outs_p, 3),
            nk.reshape(kv_shape), nv.reshape(kv_shape), nsh.reshape(st_shape), nsr.reshape(st_shape))
```

```python
import functools

import jax
import jax.numpy as jnp
import numpy as np
from jax import lax
from jax.experimental import pallas as pl
from jax.experimental.pallas import tpu as pltpu

f32 = jnp.float32
bf16 = jnp.bfloat16

N_META = 16
EPS = 1e-6
NEG_INF = -1e30
LB_FLOOR = 1e-30
WINDOW = 128
HEAD_DIM_A = 64
N_HEADS_A = 16
N_KV_A = 4
GROUP_A = N_HEADS_A // N_KV_A
N_ROT = 16
ROPE_THETA = 500000.0
H_B = 4
DK_B = 128
H_C = 4
DK_C = 128
RET_THETA = 10000.0
PAST_LEN = 16384

BLK = 128
PAD = BLK - N_META
SUB = 8
SROWS = 16
TM_F = 512
VMEM_LIMIT = 56 * 1024 * 1024

Q_A, K_A, V_A = 0, 1024, 1280
Q_B, F_B, I_B, G_B = 1536, 2048, 2560, 3072
Q_C, K_C, V_C, G_C = 3584, 4096, 4608, 5120
D_IN = 5632
W_A = 1024
LOG2E = 1.4426950408889634
Q_SCALE = HEAD_DIM_A ** -0.5 * LOG2E
NT = (((1,), (1,)), ((), ()))
TN = (((0,), (0,)), ((), ()))


def _dot(a, b, dims=None):
    if dims is None:
        return jnp.dot(a, b, preferred_element_type=f32)
    return lax.dot_general(a, b, dims, preferred_element_type=f32)


def _row_tile(rows, cap):
    best = 8
    for t in range(8, cap + 1, 8):
        if rows % t == 0:
            best = t
    return best


def _silu(x):
    return x * jax.nn.sigmoid(x)


def _rms(x, g):
    return x * lax.rsqrt(jnp.mean(x * x, axis=-1, keepdims=True) + EPS) * g


def _inproj_kernel(h_ref, w_ref, o_ref):
    o_ref[...] = _dot(h_ref[...], w_ref[...].astype(bf16))


def _inproj(h, w_all, layer):
    rows, d = h.shape
    n = w_all.shape[2]
    tm = _row_tile(rows, 2176)
    tn = 512
    return pl.pallas_call(
        _inproj_kernel,
        grid=(rows // tm, n // tn),
        in_specs=[pl.BlockSpec((tm, d), lambda i, j: (i, 0)),
                  pl.BlockSpec((None, d, tn), lambda i, j: (layer, 0, j))],
        out_specs=pl.BlockSpec((tm, tn), lambda i, j: (i, j)),
        out_shape=jax.ShapeDtypeStruct((rows, n), f32),
        compiler_params=pltpu.CompilerParams(dimension_semantics=("arbitrary", "arbitrary"),
                                             vmem_limit_bytes=VMEM_LIMIT),
        name="inproj",
    )(h, w_all)


def _embed_kernel(xp_ref, tail_ref, g_ref, x_ref, h_ref):
    i = pl.program_id(0)
    last = pl.num_programs(0) - 1

    def emit(src_ref):
        x = src_ref[...]
        x_ref[...] = x
        h_ref[...] = _rms(x, g_ref[...]).astype(bf16)

    pl.when(i < last)(lambda: emit(xp_ref))
    pl.when(i == last)(lambda: emit(tail_ref))


def _embed(xp, tail, g):
    rows_main, d = xp.shape
    tm = tail.shape[0]
    n_main = rows_main // tm
    rows = rows_main + tm
    return pl.pallas_call(
        _embed_kernel,
        grid=(n_main + 1,),
        in_specs=[pl.BlockSpec((tm, d), lambda i: (jnp.minimum(i, n_main - 1), 0)),
                  pl.BlockSpec((tm, d), lambda i: (0, 0)),
                  pl.BlockSpec((1, d), lambda i: (0, 0))],
        out_specs=[pl.BlockSpec((tm, d), lambda i: (i, 0)),
                   pl.BlockSpec((tm, d), lambda i: (i, 0))],
        out_shape=[jax.ShapeDtypeStruct((rows, d), f32), jax.ShapeDtypeStruct((rows, d), bf16)],
        compiler_params=pltpu.CompilerParams(dimension_semantics=("arbitrary",), vmem_limit_bytes=VMEM_LIMIT),
        name="embed_norm",
    )(xp, tail, g)


def _ffn_kernel(*refs, with_next):
    if with_next:
        x_ref, mix_ref, wo_ref, nf_ref, wg_ref, wu_ref, wd_ref, gn_ref, o_ref, hn_ref, h_ref = refs
    else:
        x_ref, mix_ref, wo_ref, nf_ref, wg_ref, wu_ref, wd_ref, o_ref, h_ref = refs
    j = pl.program_id(1)

    @pl.when(j == 0)
    def _():
        x1 = x_ref[...] + _dot(mix_ref[...], wo_ref[...])
        h_ref[...] = _rms(x1, nf_ref[...]).astype(bf16)
        o_ref[...] = x1

    h = h_ref[...]
    a = _silu(_dot(h, wg_ref[...])) * _dot(h, wu_ref[...])
    o_ref[...] += _dot(a.astype(bf16), wd_ref[...])

    if with_next:
        @pl.when(j == pl.num_programs(1) - 1)
        def _():
            hn_ref[...] = _rms(o_ref[...], gn_ref[...]).astype(bf16)


def _ffn(x, mix, wo, nf, wg, wu, wd, g_next, tm, tile0, n_tiles):
    d = x.shape[1]
    dff = wg.shape[1]
    tf = 512
    with_next = g_next is not None
    row_spec = pl.BlockSpec((tm, d), lambda i, j: (tile0 + i, 0))
    out_spec = pl.BlockSpec((tm, d), lambda i, j: (i, 0))
    in_specs = [row_spec, row_spec,
                pl.BlockSpec((d, d), lambda i, j: (0, 0), pipeline_mode=pl.Buffered(1)),
                pl.BlockSpec((1, d), lambda i, j: (0, 0)),
                pl.BlockSpec((d, tf), lambda i, j: (0, j)),
                pl.BlockSpec((d, tf), lambda i, j: (0, j)),
                pl.BlockSpec((tf, d), lambda i, j: (j, 0))]
    args = [x, mix, wo, nf, wg, wu, wd]
    out_specs = [out_spec]
    out_shape = [jax.ShapeDtypeStruct((n_tiles * tm, d), f32)]
    if with_next:
        in_specs.append(pl.BlockSpec((1, d), lambda i, j: (0, 0)))
        args.append(g_next)
        out_specs.append(out_spec)
        out_shape.append(jax.ShapeDtypeStruct((n_tiles * tm, d), bf16))
    return pl.pallas_call(
        functools.partial(_ffn_kernel, with_next=with_next),
        grid=(n_tiles, dff // tf),
        in_specs=in_specs,
        out_specs=out_specs,
        out_shape=out_shape,
        scratch_shapes=[pltpu.VMEM((tm, d), bf16)],
        compiler_params=pltpu.CompilerParams(dimension_semantics=("arbitrary", "arbitrary"),
                                             vmem_limit_bytes=VMEM_LIMIT),
        name="outproj_ffn",
    )(*args)


def _head_norm(x, sel_ref, exp_ref, g):
    n, w = x.shape
    x2 = x * x
    hi = x2.astype(bf16)
    lo = (x2 - hi.astype(f32)).astype(bf16)
    s = _dot(jnp.concatenate([hi, lo], axis=0), sel_ref[0:w, :])
    return x * _dot_split3(lax.rsqrt((s[0:n] + s[n:2 * n]) * (1.0 / HEAD_DIM_A) + EPS), exp_ref[:, 0:w]) * g


def _dot_split3(x, m):
    n = x.shape[0]
    y = _dot(jnp.concatenate(_split3(x), axis=0), m)
    return y[0:n] + y[n:2 * n] + y[2 * n:3 * n]


def _rope_a(x, tab_ref):
    half = N_ROT // 2
    first = lax.broadcasted_iota(jnp.int32, (x.shape[0], 128), 1) % HEAD_DIM_A < half
    out = []
    for t in range(x.shape[1] // 128):
        xt = x[:, 128 * t:128 * (t + 1)]
        partner = jnp.where(first, -pltpu.roll(xt, 128 - half, 1), pltpu.roll(xt, half, 1))
        out.append(xt * tab_ref[0] + partner * tab_ref[1])
    return jnp.concatenate(out, axis=1)


def _rope_r(x, tab_ref):
    return x * tab_ref[0] + pltpu.roll(x, DK_C // 2, 1) * tab_ref[1]


def _forget(fb, lb):
    f = jnp.maximum(lb, LB_FLOOR) + (1.0 - lb) * jax.nn.sigmoid(fb)
    return f, jnp.log2(f)


def _split3(x):
    h1 = x.astype(bf16)
    r1 = x - h1.astype(f32)
    h2 = r1.astype(bf16)
    h3 = (r1 - h2.astype(f32)).astype(bf16)
    return h1, h2, h3


def _mixer_prompt_kernel(sink_ref, gpow_ref, proj_ref, ropea_ref, roper_ref, qn_ref, kn_ref, lb_ref, hn_ref, rn_ref,
                         sel_ref, exp_ref, tri_ref, dsel_ref, dec_ref, rsc_ref, tail_ref,
                         wg_ref, wu_ref, wd_ref, wo_ref,
                         mix_ref, ck_ref, cv_ref, sh_ref, sr_ref, wgb_ref, wub_ref, wdb_ref, wob_ref,
                         kk, vv, sht, srs, qs_s, qb_s, kb_s, g_s):
    i = pl.program_id(1)
    nb = pl.num_programs(1)

    @pl.when(i == 0)
    def _():
        kk[...] = jnp.zeros_like(kk)
        vv[...] = jnp.zeros_like(vv)
        for g in range(N_KV_A):
            vv[:, 256 * g + 128:256 * (g + 1)] = jnp.ones((2 * BLK, 128), bf16)
        sht[...] = jnp.zeros_like(sht)
        srs[...] = jnp.zeros_like(srs)

    wgb_ref[...] = wg_ref[...].astype(bf16)
    wub_ref[...] = wu_ref[...].astype(bf16)
    wdb_ref[...] = wd_ref[...].astype(bf16)
    wob_ref[...] = wo_ref[...].astype(bf16)

    rowabs = i * BLK + lax.broadcasted_iota(jnp.int32, (BLK, 1), 0)
    fgate, logf = _forget(proj_ref[:, F_B:F_B + 512], lb_ref[...])
    logf = jnp.where(rowabs >= PAD, logf, 0.0)
    kb_s[...] = jnp.where(rowabs >= PAD, 1.0 - fgate, 0.0)
    qb_s[...] = _silu(proj_ref[:, Q_B:Q_B + 512])
    l1, l2, l3 = _split3(logf)
    tri = tri_ref[...]
    g_s[...] = _dot(tri, l1) + _dot(tri, l2) + _dot(tri, l3)

    r2 = lax.broadcasted_iota(jnp.int32, (BLK, BLK), 0)
    c2 = lax.broadcasted_iota(jnp.int32, (BLK, BLK), 1)
    lane_lo = c2 < HEAD_DIM_A

    def attention_prep():
        qa = _rope_a(_head_norm(proj_ref[:, Q_A:Q_A + 1024], sel_ref, exp_ref, qn_ref[...]), ropea_ref)
        for t in range(N_HEADS_A // 2):
            xt = qa[:, 128 * t:128 * (t + 1)] * Q_SCALE
            qs_s[256 * t:256 * t + 128, :] = jnp.where(lane_lo, xt, 0.0).astype(bf16)
            qs_s[256 * t + 128:256 * (t + 1), :] = jnp.where(lane_lo, 0.0, xt).astype(bf16)
        k_cur = _rope_a(_head_norm(proj_ref[:, K_A:K_A + 256], sel_ref, exp_ref, kn_ref[...]), ropea_ref)
        v_cur = proj_ref[:, V_A:V_A + 256]
        ck_ref[0] = k_cur
        cv_ref[0] = v_cur
        for t in range(N_KV_A // 2):
            for src, dst, width in ((k_cur, kk, 128), (v_cur, vv, 256)):
                xt = src[:, 128 * t:128 * (t + 1)]
                xs = pltpu.roll(xt, HEAD_DIM_A, 1)
                dst[BLK:2 * BLK, width * 2 * t:width * 2 * t + 128] = jnp.where(lane_lo, xt, xs).astype(bf16)
                dst[BLK:2 * BLK, width * (2 * t + 1):width * (2 * t + 1) + 128] = jnp.where(lane_lo, xs, xt).astype(bf16)

    prow = 2 * BLK
    r4 = lax.broadcasted_iota(jnp.int32, (prow, BLK), 0) % BLK
    c4 = lax.broadcasted_iota(jnp.int32, (prow, BLK), 1)
    up = c4 > r4
    ok = c4 >= jnp.where(up, PAD - (i - 1) * BLK, PAD - i * BLK)

    def attention_pair(t):
        g = (2 * t) // GROUP_A
        s2 = _dot(qs_s[prow * t:prow * (t + 1), :], kk[:, 128 * g:128 * (g + 1)], NT)
        s = jnp.where(ok, jnp.where(up, s2[:, 0:BLK], s2[:, BLK:2 * BLK]), NEG_INF)
        sk = jnp.concatenate([jnp.full((BLK, BLK), sink_ref[2 * t + j] * LOG2E, f32) for j in range(2)], axis=0)
        m = jnp.maximum(jnp.max(s, axis=-1, keepdims=True), sk)
        p = jnp.exp2(s - m)
        p2 = jnp.concatenate([jnp.where(up, p, 0.0), jnp.where(up, 0.0, p)], axis=1).astype(bf16)
        res = _dot(p2, vv[:, 256 * g:256 * (g + 1)])
        y = res[:, 0:128] / (res[:, 128:256] + jnp.exp2(sk - m))
        mix_ref[:, 128 * t:128 * (t + 1)] = jnp.where(lane_lo, y[0:BLK], y[BLK:prow]).astype(bf16)

    levels = []
    bz = BLK
    while bz > SUB:
        levels.append((bz, (r2 % bz) >= bz // 2, None if bz == BLK else (r2 // bz) == (c2 // bz)))
        bz //= 2
    same_sub = (r2 // SUB) == (c2 // SUB)
    sub_ge = [(r2 % SUB) >= s for s in range(SUB)]

    def retention_head(h):
        qc = _rope_r(proj_ref[:, Q_C + 128 * h:Q_C + 128 * (h + 1)], roper_ref)
        kc = _rope_r(proj_ref[:, K_C + 128 * h:K_C + 128 * (h + 1)], roper_ref) * (DK_C ** -0.5)
        vc = proj_ref[:, V_C + 128 * h:V_C + 128 * (h + 1)].astype(bf16)
        qcb = qc.astype(bf16)
        sc = _dot(qcb, kc.astype(bf16), NT) * dec_ref[h]
        oc = _dot(qcb, srs[h].astype(bf16)) * rsc_ref[h] + _dot(sc.astype(bf16), vc)
        srs[h] = gpow_ref[h] * srs[h] + _dot((kc * tail_ref[h]).astype(bf16), vc, TN)
        yc = _rms(oc, rn_ref[...]) * _silu(proj_ref[:, G_C + 128 * h:G_C + 128 * (h + 1)])
        mix_ref[:, W_A + 512 + 128 * h:W_A + 512 + 128 * (h + 1)] = yc.astype(bf16)

    def hgrn_intra(h):
        sl = slice(128 * h, 128 * (h + 1))
        gc = g_s[:, sl]
        qb = qb_s[:, sl]
        kb = kb_s[:, sl]
        inter = _dot((qb * jnp.exp2(gc)).astype(bf16), sht[h].astype(bf16), NT)
        amat = None
        for bz, upper, same in levels:
            gref = jnp.concatenate([jnp.broadcast_to(g_s[pl.ds(st + bz // 2 - 1, 1), sl], (bz, 128))
                                    for st in range(0, BLK, bz)], axis=0)
            dg = gc - gref
            e = jnp.exp2(jnp.where(upper, dg, -dg))
            qt = jnp.where(upper, qb * e, 0.0).astype(bf16)
            kt = jnp.where(upper, 0.0, kb * e).astype(bf16)
            pm = _dot(qt, kt, NT)
            if same is not None:
                pm = jnp.where(same, pm, 0.0)
            amat = pm if amat is None else amat + pm
        ys = []
        for s in range(SUB):
            ks = jnp.concatenate([jnp.broadcast_to(kb_s[pl.ds(SUB * j + s, 1), sl], (SUB, 128))
                                  for j in range(BLK // SUB)], axis=0)
            gs = jnp.concatenate([jnp.broadcast_to(g_s[pl.ds(SUB * j + s, 1), sl], (SUB, 128))
                                  for j in range(BLK // SUB)], axis=0)
            ys.append((qb * ks * jnp.exp2(jnp.where(sub_ge[s], gc - gs, NEG_INF))).astype(bf16))
        amat = amat + jnp.where(same_sub, _dot(jnp.concatenate(ys, axis=1), dsel_ref[...]), 0.0)
        return inter, amat.astype(bf16)

    def hgrn_finish(h, inter, amat):
        sl = slice(128 * h, 128 * (h + 1))
        vb = proj_ref[:, I_B + 128 * h:I_B + 128 * (h + 1)].astype(bf16)
        ob = inter + _dot(amat, vb)
        glast = g_s[pl.ds(BLK - 1, 1), sl]
        kt2 = (kb_s[:, sl] * jnp.exp2(glast - g_s[:, sl])).astype(bf16)
        sht[h] = sht[h] * jnp.exp2(glast) + _dot(vb, kt2, TN)
        yb = _rms(ob, hn_ref[...]) * _silu(proj_ref[:, G_B + 128 * h:G_B + 128 * (h + 1)])
        mix_ref[:, W_A + 128 * h:W_A + 128 * (h + 1)] = yb.astype(bf16)

    attention_prep()
    pending = None
    for h in range(H_B):
        cur = hgrn_intra(h)
        if pending is not None:
            hgrn_finish(h - 1, *pending)
        retention_head(h)
        pending = cur
    hgrn_finish(H_B - 1, *pending)
    for t in range(N_HEADS_A // 2):
        attention_pair(t)
    kk[0:BLK, :] = kk[BLK:2 * BLK, :]
    vv[0:BLK, :] = vv[BLK:2 * BLK, :]

    @pl.when(i == nb - 1)
    def _():
        for h in range(H_B):
            sh_ref[0, h] = sht[h].T
            sr_ref[0, h] = srs[h]


def _const_spec(shape):
    nd = len(shape)
    return pl.BlockSpec(shape, lambda *_: (0,) * nd)


def _slab_rows(nrows, steps):
    for r in range(16, nrows + 1, 16):
        if nrows % r == 0 and nrows // r <= steps:
            return r
    return nrows


def _mixer_prompt(proj, rows_total, batch, nb, layer, sinks, gpow, ropea, roper, qn, kn, lb, hn, rn, sel, expand,
                  tri, dsel, dec, rsc, tail, w_gate, w_up, w_down, w_out):
    smem = pl.BlockSpec(memory_space=pltpu.SMEM)
    d, dff = w_gate.shape[1], w_gate.shape[2]

    def row_blk(b, i):
        return jnp.where(i == 0, batch * (nb - 1), b * (nb - 1) + i - 1)

    def mix_blk(b, i):
        return jnp.where((i == 0) & (b > 0), b * (nb - 1), row_blk(b, i))

    def slab_in(nrows, ncols):
        r = _slab_rows(nrows, batch * nb)
        return pl.BlockSpec((None, r, ncols), lambda b, i: (layer, jnp.minimum(b * nb + i, nrows // r - 1), 0))

    def slab_out(nrows, ncols):
        r = _slab_rows(nrows, batch * nb)
        return pl.BlockSpec((r, ncols), lambda b, i: (jnp.minimum(b * nb + i, nrows // r - 1), 0))

    return pl.pallas_call(
        _mixer_prompt_kernel,
        grid=(batch, nb),
        in_specs=[smem, smem,
                  pl.BlockSpec((BLK, D_IN), lambda b, i: (row_blk(b, i), 0)),
                  pl.BlockSpec((2, BLK, 128), lambda b, i: (0, i, 0)),
                  pl.BlockSpec((2, BLK, 128), lambda b, i: (0, i, 0)),
                  _const_spec((1, 1024)), _const_spec((1, 256)), _const_spec((1, 512)),
                  _const_spec((1, 128)), _const_spec((1, 128)),
                  _const_spec((1024, 128)), _const_spec((128, 1024)),
                  _const_spec((BLK, BLK)), _const_spec((SUB * 128, 128)),
                  _const_spec((H_C, BLK, BLK)), _const_spec((H_C, BLK, 128)), _const_spec((H_C, BLK, 128)),
                  slab_in(d, dff), slab_in(d, dff), slab_in(dff, d), slab_in(d, d)],
        out_specs=[pl.BlockSpec((BLK, 2048), lambda b, i: (mix_blk(b, i), 0)),
                   pl.BlockSpec((1, BLK, 256), lambda b, i: (b, 0, 0)),
                   pl.BlockSpec((1, BLK, 256), lambda b, i: (b, 0, 0)),
                   pl.BlockSpec((1, H_B, 128, 128), lambda b, i: (b, 0, 0, 0)),
                   pl.BlockSpec((1, H_C, 128, 128), lambda b, i: (b, 0, 0, 0)),
                   slab_out(d, dff), slab_out(d, dff), slab_out(dff, d), slab_out(d, d)],
        out_shape=[jax.ShapeDtypeStruct((rows_total, 2048), bf16),
                   jax.ShapeDtypeStruct((batch, BLK, 256), f32),
                   jax.ShapeDtypeStruct((batch, BLK, 256), f32),
                   jax.ShapeDtypeStruct((batch, H_B, 128, 128), f32),
                   jax.ShapeDtypeStruct((batch, H_C, 128, 128), f32),
                   jax.ShapeDtypeStruct((d, dff), bf16), jax.ShapeDtypeStruct((d, dff), bf16),
                   jax.ShapeDtypeStruct((dff, d), bf16), jax.ShapeDtypeStruct((d, d), bf16)],
        scratch_shapes=[pltpu.VMEM((2 * BLK, N_KV_A * 128), bf16), pltpu.VMEM((2 * BLK, N_KV_A * 256), bf16),
                        pltpu.VMEM((H_B, 128, 128), f32), pltpu.VMEM((H_C, 128, 128), f32),
                        pltpu.VMEM((N_HEADS_A * BLK, 128), bf16),
                        pltpu.VMEM((BLK, 512), f32), pltpu.VMEM((BLK, 512), f32), pltpu.VMEM((BLK, 512), f32)],
        compiler_params=pltpu.CompilerParams(dimension_semantics=("arbitrary", "arbitrary"),
                                             vmem_limit_bytes=VMEM_LIMIT),
        name="mixer_prompt",
    )(sinks, gpow, proj, ropea, roper, qn, kn, lb, hn, rn, sel, expand, tri, dsel, dec, rsc, tail,
      w_gate, w_up, w_down, w_out)


def _mixer_sample_kernel(sink_ref, gpow_ref, mixin_ref, *refs, dec_seq, steps, n_prev):
    del mixin_ref
    refs = refs[:17] + refs[17 + n_prev:]
    c = pl.program_id(0)
    mix_ref = refs[17]
    pl.when(c < steps)(lambda: _mixer_sample_body(sink_ref, gpow_ref, *refs, dec_seq=dec_seq))

    @pl.when(c >= steps)
    def _():
        mix_ref[...] = jnp.zeros_like(mix_ref)


def _mixer_sample_body(sink_ref, gpow_ref, proj_ref, ropea_ref, roper_ref, qn_ref, kn_ref, lb_ref,
                       hn_ref, rn_ref, sel_ref, exp_ref, dec_ref, rsc_ref, tail_ref,
                       ck_ref, cv_ref, sh_ref, sr_ref,
                       mix_ref, nk_ref, nv_ref, nsh_ref, nsr_ref, *, dec_seq):
    nbat = SROWS // dec_seq
    w = ck_ref.shape[1]
    rb = lax.broadcasted_iota(jnp.int32, (SROWS, 1), 0) // dec_seq
    rt = lax.broadcasted_iota(jnp.int32, (SROWS, 1), 0) % dec_seq

    def pick(parts):
        out = parts[nbat - 1]
        for b in range(nbat - 2, -1, -1):
            out = jnp.where(rb == b, parts[b], out)
        return out

    qa = _rope_a(_head_norm(proj_ref[:, Q_A:Q_A + 1024], sel_ref, exp_ref, qn_ref[...]), ropea_ref)
    qs = (qa * Q_SCALE).astype(bf16)
    k_new = _rope_a(_head_norm(proj_ref[:, K_A:K_A + 256], sel_ref, exp_ref, kn_ref[...]), ropea_ref)
    v_new = proj_ref[:, V_A:V_A + 256]
    for b in range(nbat):
        nk_ref[b, 0:w - dec_seq, :] = ck_ref[b, dec_seq:w, :]
        nk_ref[b, w - dec_seq:w, :] = k_new[dec_seq * b:dec_seq * (b + 1), :]
        nv_ref[b, 0:w - dec_seq, :] = cv_ref[b, dec_seq:w, :]
        nv_ref[b, w - dec_seq:w, :] = v_new[dec_seq * b:dec_seq * (b + 1), :]
    knb = k_new.astype(bf16)
    vnb = v_new.astype(bf16)

    grows = GROUP_A * SROWS
    rt_g = jnp.concatenate([rt] * GROUP_A, axis=0)
    rb_g = jnp.concatenate([rb] * GROUP_A, axis=0)
    jc = lax.broadcasted_iota(jnp.int32, (grows, w), 1)
    valid_c = jc > rt_g + (w - WINDOW)
    cn = lax.broadcasted_iota(jnp.int32, (grows, SROWS), 1)
    valid_n = (cn // dec_seq == rb_g) & (cn % dec_seq <= rt_g)
    heads = [None] * N_HEADS_A
    for g in range(N_KV_A):
        ksl = slice(64 * g, 64 * (g + 1))
        qg = jnp.concatenate([qs[:, 64 * h:64 * (h + 1)] for h in range(GROUP_A * g, GROUP_A * (g + 1))], axis=0)
        sparts = [_dot(qg, ck_ref[b, :, ksl].astype(bf16), NT) for b in range(nbat)]
        s_c = sparts[nbat - 1]
        for b in range(nbat - 2, -1, -1):
            s_c = jnp.where(rb_g == b, sparts[b], s_c)
        s_c = jnp.where(valid_c, s_c, NEG_INF)
        s_n = jnp.where(valid_n, _dot(qg, knb[:, ksl], NT), NEG_INF)
        sk = jnp.concatenate([jnp.full((SROWS, 1), sink_ref[h] * LOG2E, f32)
                              for h in range(GROUP_A * g, GROUP_A * (g + 1))], axis=0)
        m = jnp.maximum(jnp.maximum(jnp.max(s_c, axis=-1, keepdims=True), jnp.max(s_n, axis=-1, keepdims=True)), sk)
        p_c = jnp.exp2(s_c - m)
        p_n = jnp.exp2(s_n - m)
        den = jnp.sum(p_c, axis=-1, keepdims=True) + jnp.sum(p_n, axis=-1, keepdims=True) + jnp.exp2(sk - m)
        pcb = p_c.astype(bf16)
        oparts = [_dot(pcb, cv_ref[b, :, ksl].astype(bf16)) for b in range(nbat)]
        o = oparts[nbat - 1]
        for b in range(nbat - 2, -1, -1):
            o = jnp.where(rb_g == b, oparts[b], o)
        o = (o + _dot(p_n.astype(bf16), vnb[:, ksl])) / den
        for hh in range(GROUP_A):
            heads[GROUP_A * g + hh] = o[SROWS * hh:SROWS * (hh + 1), :]
    mix_ref[:, 0:W_A] = jnp.concatenate(heads, axis=1).astype(bf16)

    fgate, logf = _forget(proj_ref[:, F_B:F_B + 512], lb_ref[...])
    kb = 1.0 - fgate
    qb = _silu(proj_ref[:, Q_B:Q_B + 512])
    vb = proj_ref[:, I_B:I_B + 512]
    gcum = logf
    for d in range(1, dec_seq):
        gcum = gcum + jnp.where(rt >= d, pltpu.roll(logf, d, 0), 0.0)
    cb = lax.broadcasted_iota(jnp.int32, (1, SROWS), 1) // dec_seq
    ct = lax.broadcasted_iota(jnp.int32, (1, SROWS), 1) % dec_seq
    intra = [jnp.zeros((SROWS, 128), f32) for _ in range(H_B)]
    for s in range(SROWS):
        ok = (rb == s // dec_seq) & (rt >= s % dec_seq)
        y = qb * kb[s:s + 1, :] * jnp.exp2(jnp.where(ok, gcum - gcum[s:s + 1, :], NEG_INF))
        for h in range(H_B):
            sl = slice(128 * h, 128 * (h + 1))
            intra[h] = intra[h] + jnp.sum(y[:, sl], axis=-1, keepdims=True) * vb[s:s + 1, sl]
    qg = (qb * jnp.exp2(gcum)).astype(bf16)
    vbb = vb.astype(bf16)
    glast = [gcum[dec_seq * (b + 1) - 1:dec_seq * (b + 1), :] for b in range(nbat)]
    for h in range(H_B):
        sl = slice(128 * h, 128 * (h + 1))
        ob = pick([_dot(qg[:, sl], sh_ref[b, h].astype(bf16)) for b in range(nbat)]) + intra[h]
        for b in range(nbat):
            kt2 = jnp.where(rb == b, kb[:, sl] * jnp.exp2(glast[b][:, sl] - gcum[:, sl]), 0.0).astype(bf16)
            dcol = jnp.broadcast_to(jnp.exp2(glast[b][:, sl]), (128, 128)).T
            nsh_ref[b, h] = dcol * sh_ref[b, h] + _dot(kt2, vbb[:, sl], TN)
        yb = _rms(ob, hn_ref[...]) * _silu(proj_ref[:, G_B + 128 * h:G_B + 128 * (h + 1)])
        mix_ref[:, W_A + 128 * h:W_A + 128 * (h + 1)] = yb.astype(bf16)

    for h in range(H_C):
        qc = _rope_r(proj_ref[:, Q_C + 128 * h:Q_C + 128 * (h + 1)], roper_ref)
        kc = _rope_r(proj_ref[:, K_C + 128 * h:K_C + 128 * (h + 1)], roper_ref) * (DK_C ** -0.5)
        vc = proj_ref[:, V_C + 128 * h:V_C + 128 * (h + 1)]
        qcb = qc.astype(bf16)
        vcb = vc.astype(bf16)
        sc = _dot(qcb, kc.astype(bf16), NT) * dec_ref[h]
        oc = pick([_dot(qcb, sr_ref[b, h].astype(bf16)) for b in range(nbat)]) * rsc_ref[h] \
            + _dot(sc.astype(bf16), vcb)
        kct = kc * tail_ref[h]
        for b in range(nbat):
            nsr_ref[b, h] = gpow_ref[h] * sr_ref[b, h] + _dot(jnp.where(rb == b, kct, 0.0).astype(bf16), vcb, TN)
        yc = _rms(oc, rn_ref[...]) * _silu(proj_ref[:, G_C + 128 * h:G_C + 128 * (h + 1)])
        mix_ref[:, W_A + 512 + 128 * h:W_A + 512 + 128 * (h + 1)] = yc.astype(bf16)


def _mixer_sample(mix, proj, row0, layer, dec_batch, dec_seq, sinks, gpow, ropea, roper, qn, kn, lb, hn, rn, sel, expand,
                  dec, rsc, tail, cache_k, cache_v, state_h, state_r, prev_outs):
    smem = pl.BlockSpec(memory_space=pltpu.SMEM)
    nbat = SROWS // dec_seq
    steps = dec_batch // nbat
    blk0 = row0 // SROWS
    w = cache_k.shape[1]
    n_all = cache_k.shape[0]
    fill_steps = (mix.shape[0] - row0) // SROWS - steps
    cl = lambda c: jnp.minimum(c, steps - 1)
    any_spec = pl.BlockSpec(memory_space=pl.ANY)
    n_in = 20
    return pl.pallas_call(
        functools.partial(_mixer_sample_kernel, dec_seq=dec_seq, steps=steps, n_prev=len(prev_outs)),
        grid=(steps + fill_steps,),
        in_specs=[smem, smem,
                  pl.BlockSpec(memory_space=pl.ANY),
                  pl.BlockSpec((SROWS, D_IN), lambda c: (blk0 + cl(c), 0)),
                  _const_spec((2, SROWS, 128)), _const_spec((2, SROWS, 128)),
                  _const_spec((1, 1024)), _const_spec((1, 256)), _const_spec((1, 512)),
                  _const_spec((1, 128)), _const_spec((1, 128)),
                  _const_spec((1024, 128)), _const_spec((128, 1024)),
                  _const_spec((H_C, SROWS, SROWS)), _const_spec((H_C, SROWS, 128)), _const_spec((H_C, SROWS, 128)),
                  pl.BlockSpec((nbat, w, 256), lambda c: (layer * steps + cl(c), 0, 0)),
                  pl.BlockSpec((nbat, w, 256), lambda c: (layer * steps + cl(c), 0, 0)),
                  pl.BlockSpec((nbat, H_B, 128, 128), lambda c: (layer * steps + cl(c), 0, 0, 0)),
                  pl.BlockSpec((nbat, H_C, 128, 128), lambda c: (layer * steps + cl(c), 0, 0, 0))]
        + [any_spec] * len(prev_outs),
        out_specs=[pl.BlockSpec((SROWS, 2048), lambda c: (blk0 + c, 0)),
                   pl.BlockSpec((nbat, w, 256), lambda c: (layer * steps + cl(c), 0, 0)),
                   pl.BlockSpec((nbat, w, 256), lambda c: (layer * steps + cl(c), 0, 0)),
                   pl.BlockSpec((nbat, H_B, 128, 128), lambda c: (layer * steps + cl(c), 0, 0, 0)),
                   pl.BlockSpec((nbat, H_C, 128, 128), lambda c: (layer * steps + cl(c), 0, 0, 0))],
        out_shape=[jax.ShapeDtypeStruct(mix.shape, bf16),
                   jax.ShapeDtypeStruct((n_all, w, 256), f32),
                   jax.ShapeDtypeStruct((n_all, w, 256), f32),
                   jax.ShapeDtypeStruct((n_all, H_B, 128, 128), f32),
                   jax.ShapeDtypeStruct((n_all, H_C, 128, 128), f32)],
        input_output_aliases={2: 0, **{n_in + k: 1 + k for k in range(len(prev_outs))}},
        compiler_params=pltpu.CompilerParams(dimension_semantics=("arbitrary",), vmem_limit_bytes=VMEM_LIMIT),
        name="mixer_sample",
    )(sinks, gpow, mix, proj, ropea, roper, qn, kn, lb, hn, rn, sel, expand, dec, rsc, tail,
      cache_k, cache_v, state_h, state_r, *prev_outs)


def _rope_tables(pos):
    posf = jnp.asarray(pos).astype(f32)[:, None]
    t = pos.shape[0]
    half = N_ROT // 2
    inv = ROPE_THETA ** (-jnp.arange(half, dtype=f32) * (2.0 / N_ROT))
    ang = posf * inv[None, :]
    cos, sin = jnp.cos(ang), jnp.sin(ang)
    rest0 = jnp.zeros((t, HEAD_DIM_A - N_ROT), f32)
    c64 = jnp.concatenate([cos, cos, rest0 + 1.0], axis=1)
    s64 = jnp.concatenate([sin, sin, rest0], axis=1)
    ropea = jnp.stack([jnp.tile(c64, (1, 2)), jnp.tile(s64, (1, 2))])
    invr = RET_THETA ** (-jnp.arange(DK_C // 2, dtype=f32) * (2.0 / DK_C))
    angr = posf * invr[None, :]
    cr, sr = jnp.cos(angr), jnp.sin(angr)
    roper = jnp.stack([jnp.concatenate([cr, cr], axis=1), jnp.concatenate([-sr, sr], axis=1)])
    return ropea, roper


def _ret_tables(lg, seq_of_row, tok_of_row, length):
    nf = np.float32
    tq = tok_of_row.astype(nf)
    rel = tq[:, None] - tq[None, :]
    ok = (seq_of_row[:, None] == seq_of_row[None, :]) & (rel >= 0)
    dec = np.where(ok[None], np.exp(np.where(ok, rel, nf(0.0))[None] * lg[:, None, None]), nf(0.0))
    n = tq.shape[0]
    rsc = np.broadcast_to(np.exp((tq + nf(1.0))[None, :, None] * lg[:, None, None]), (H_C, n, 128))
    tail = np.broadcast_to(np.exp((nf(length) - nf(1.0) - tq)[None, :, None] * lg[:, None, None]), (H_C, n, 128))
    gpow = np.exp(nf(length) * lg)
    return dec.astype(nf), np.ascontiguousarray(rsc, nf), np.ascontiguousarray(tail, nf), gpow.astype(nf)


def kernel(x_prompt, x_sample, cache_k, cache_v, state_hgrn, state_ret, meta_tokens, norm_mix, norm_ffn, w_in, q_norm, k_norm, attn_sinks, hgrn_lb, hgrn_norm, ret_norm, w_out, w_gate, w_up, w_down):
    batch, seq, d = x_prompt.shape
    dec_batch, dec_seq, _ = x_sample.shape
    depth = w_in.shape[0]
    w = cache_k.shape[2]
    assert d == 2048 and w_in.shape[2] == D_IN and seq % BLK == 0
    assert SROWS % dec_seq == 0 and dec_batch % (SROWS // dec_seq) == 0 and w == WINDOW
    nb = seq // BLK + 1
    lp = nb * BLK
    rows_main = batch * seq
    rows_s = dec_batch * dec_seq
    row_s0 = rows_main + BLK
    tail_rows = BLK + rows_s
    rows = rows_main + tail_rows
    n_main = rows_main // TM_F
    tm_all = rows // n_main
    assert rows_main % TM_F == 0 and rows_main % tail_rows == 0 and rows_s % SROWS == 0
    assert rows % n_main == 0 and tm_all % 16 == 0 and tail_rows % 16 == 0

    tail = jnp.concatenate([jnp.zeros((PAD, d), f32), meta_tokens.astype(f32), x_sample.reshape(rows_s, d)], axis=0)
    x, h = _embed(x_prompt.reshape(rows_main, d), tail, norm_mix[0][None])

    p = jax.nn.softmax(hgrn_lb.astype(f32), axis=0)
    lbs = jnp.cumsum(p, axis=0) - p[0]
    lg = np.log1p(-np.exp2(np.float32(-5.0) - np.arange(H_C, dtype=np.float32))).astype(np.float32)
    ropea_p, roper_p = _rope_tables(np.arange(lp) - PAD)
    srow = np.arange(SROWS)
    ropea_s, roper_s = _rope_tables(PAST_LEN + srow % dec_seq)
    dec_p, rsc_p, tail_p, gpow_p = _ret_tables(lg, np.zeros((BLK,), np.int32), np.arange(BLK), float(BLK))
    dec_s, rsc_s, tail_s, gpow_s = _ret_tables(lg, srow // dec_seq, srow % dec_seq, float(dec_seq))
    sel_np = np.arange(1024)[:, None] // HEAD_DIM_A == np.arange(128)[None, :]
    sel = jnp.asarray(sel_np, bf16)
    expand = jnp.asarray(sel_np.T, bf16)
    tri = jnp.asarray(np.arange(BLK)[:, None] >= np.arange(BLK)[None, :], bf16)
    dsel = jnp.asarray(np.arange(SUB * 128)[:, None] // 128 == np.arange(128)[None, :] % SUB, bf16)

    ck_flat = cache_k.reshape(depth * dec_batch, w, N_KV_A * HEAD_DIM_A)
    cv_flat = cache_v.reshape(depth * dec_batch, w, N_KV_A * HEAD_DIM_A)
    sh_flat = state_hgrn.reshape(depth * dec_batch, H_B, 128, 128)
    sr_flat = state_ret.reshape(depth * dec_batch, H_C, 128, 128)

    outs_p, outs_s = [], ()
    for l in range(depth):
        proj = _inproj(h, w_in, l)
        qn = jnp.tile(q_norm[l], N_HEADS_A)[None]
        kn = jnp.tile(k_norm[l], N_KV_A)[None]
        common = (qn, kn, lbs[l][None], hgrn_norm[l][None], ret_norm[l][None], sel, expand)
        mix, ck, cv, sh, sr, wg_b, wu_b, wd_b, wo_b = _mixer_prompt(
            proj, rows, batch, nb, l, attn_sinks[l], gpow_p, ropea_p, roper_p, *common, tri, dsel,
            dec_p, rsc_p, tail_p, w_gate, w_up, w_down, w_out)
        mix, *outs_s = _mixer_sample(mix, proj, row_s0, l, dec_batch, dec_seq, attn_sinks[l], gpow_s,
                                     ropea_s, roper_s, *common, dec_s, rsc_s, tail_s,
                                     ck_flat, cv_flat, sh_flat, sr_flat, tuple(outs_s))
        ffn_w = (wo_b, norm_ffn[l][None], wg_b, wu_b, wd_b)
        outs_p.append((ck, cv, sh, sr))
        if l + 1 < depth:
            x, h = _ffn(x, mix, *ffn_w, norm_mix[l + 1][None], tm_all, 0, n_main)
        else:
            (y_main,) = _ffn(x, mix, *ffn_w, None, TM_F, 0, n_main)
            (y_tail,) = _ffn(x, mix, *ffn_w, None, tail_rows, rows_main // tail_rows, 1)

    y_prompt = y_main.reshape(batch, seq, d)
    y_sample = y_tail[BLK:].reshape(dec_batch, dec_seq, d)
    kv_shape = (depth, -1, w, N_KV_A, HEAD_DIM_A)
    st_shape = (depth, dec_batch, H_B, 128, 128)
    stack = lambda outs, k: jnp.stack([o[k] for o in outs])
    nk, nv, nsh, nsr = outs_s
    return (y_prompt, y_sample,
            stack(outs_p, 0).reshape(kv_shape), stack(outs_p, 1).reshape(kv_shape), stack(outs_p, 2), stack(outs_p, 3),
            nk.reshape(kv_shape), nv.reshape(kv_shape), nsh.reshape(st_shape), nsr.reshape(st_shape))
```

```python
import functools

import jax
import jax.numpy as jnp
import numpy as np
from jax import lax
from jax.experimental import pallas as pl
from jax.experimental.pallas import tpu as pltpu

f32 = jnp.float32
bf16 = jnp.bfloat16

N_META = 16
EPS = 1e-6
NEG_INF = -1e30
LB_FLOOR = 1e-30
WINDOW = 128
HEAD_DIM_A = 64
N_HEADS_A = 16
N_KV_A = 4
GROUP_A = N_HEADS_A // N_KV_A
N_ROT = 16
ROPE_THETA = 500000.0
H_B = 4
DK_B = 128
H_C = 4
DK_C = 128
RET_THETA = 10000.0
PAST_LEN = 16384

BLK = 128
PAD = BLK - N_META
SUB = 8
SROWS = 16
TM_F = 512
VMEM_LIMIT = 56 * 1024 * 1024

Q_A, K_A, V_A = 0, 1024, 1280
Q_B, F_B, I_B, G_B = 1536, 2048, 2560, 3072
Q_C, K_C, V_C, G_C = 3584, 4096, 4608, 5120
D_IN = 5632
W_A = 1024
LOG2E = 1.4426950408889634
Q_SCALE = HEAD_DIM_A ** -0.5 * LOG2E
NT = (((1,), (1,)), ((), ()))
TN = (((0,), (0,)), ((), ()))


def _dot(a, b, dims=None):
    if dims is None:
        return jnp.dot(a, b, preferred_element_type=f32)
    return lax.dot_general(a, b, dims, preferred_element_type=f32)


def _row_tile(rows, cap):
    best = 8
    for t in range(8, cap + 1, 8):
        if rows % t == 0:
            best = t
    return best


def _silu(x):
    return x * jax.nn.sigmoid(x)


def _rms(x, g):
    return x * lax.rsqrt(jnp.mean(x * x, axis=-1, keepdims=True) + EPS) * g


def _inproj_kernel(h_ref, w_ref, o_ref):
    o_ref[...] = _dot(h_ref[...], w_ref[...].astype(bf16))


def _inproj(h, w_all, layer):
    rows, d = h.shape
    n = w_all.shape[2]
    tm = _row_tile(rows, 2176)
    tn = 512
    return pl.pallas_call(
        _inproj_kernel,
        grid=(rows // tm, n // tn),
        in_specs=[pl.BlockSpec((tm, d), lambda i, j: (i, 0)),
                  pl.BlockSpec((None, d, tn), lambda i, j: (layer, 0, j))],
        out_specs=pl.BlockSpec((tm, tn), lambda i, j: (i, j)),
        out_shape=jax.ShapeDtypeStruct((rows, n), f32),
        compiler_params=pltpu.CompilerParams(dimension_semantics=("arbitrary", "arbitrary"),
                                             vmem_limit_bytes=VMEM_LIMIT),
        name="inproj",
    )(h, w_all)


def _embed_kernel(xp_ref, tail_ref, g_ref, x_ref, h_ref):
    i = pl.program_id(0)
    last = pl.num_programs(0) - 1

    def emit(src_ref):
        x = src_ref[...]
        x_ref[...] = x
        h_ref[...] = _rms(x, g_ref[...]).astype(bf16)

    pl.when(i < last)(lambda: emit(xp_ref))
    pl.when(i == last)(lambda: emit(tail_ref))


def _embed(xp, tail, g):
    rows_main, d = xp.shape
    tm = tail.shape[0]
    n_main = rows_main // tm
    rows = rows_main + tm
    return pl.pallas_call(
        _embed_kernel,
        grid=(n_main + 1,),
        in_specs=[pl.BlockSpec((tm, d), lambda i: (jnp.minimum(i, n_main - 1), 0)),
                  pl.BlockSpec((tm, d), lambda i: (0, 0)),
                  pl.BlockSpec((1, d), lambda i: (0, 0))],
        out_specs=[pl.BlockSpec((tm, d), lambda i: (i, 0)),
                   pl.BlockSpec((tm, d), lambda i: (i, 0))],
        out_shape=[jax.ShapeDtypeStruct((rows, d), f32), jax.ShapeDtypeStruct((rows, d), bf16)],
        compiler_params=pltpu.CompilerParams(dimension_semantics=("arbitrary",), vmem_limit_bytes=VMEM_LIMIT),
        name="embed_norm",
    )(xp, tail, g)


def _ffn_kernel(*refs, with_next, tail_rows):
    if with_next:
        x_ref, mix_ref, wo_ref, nf_ref, wg_ref, wu_ref, wd_ref, gn_ref, o_ref, hn_ref, h_ref = refs
    else:
        x_ref, mix_ref, wo_ref, nf_ref, wg_ref, wu_ref, wd_ref, o_ref, tail_ref, h_ref = refs
    j = pl.program_id(1)

    @pl.when(j == 0)
    def _():
        x1 = x_ref[...] + _dot(mix_ref[...], wo_ref[...])
        h_ref[...] = _rms(x1, nf_ref[...]).astype(bf16)
        o_ref[...] = x1

    h = h_ref[...]
    a = _silu(_dot(h, wg_ref[...])) * _dot(h, wu_ref[...])
    o_ref[...] += _dot(a.astype(bf16), wd_ref[...])

    if with_next:
        @pl.when(j == pl.num_programs(1) - 1)
        def _():
            hn_ref[...] = _rms(o_ref[...], gn_ref[...]).astype(bf16)
    else:
        @pl.when((j == pl.num_programs(1) - 1) & (pl.program_id(0) == pl.num_programs(0) - 1))
        def _():
            tail_ref[...] = o_ref[o_ref.shape[0] - tail_rows:, :]


def _ffn(x, mix, wo, nf, wg, wu, wd, g_next, tm, tail_rows):
    rows, d = x.shape
    dff = wg.shape[1]
    tf = 512
    n_tiles = rows // tm
    with_next = g_next is not None
    row_spec = pl.BlockSpec((tm, d), lambda i, j: (i, 0))
    out_spec = row_spec
    in_specs = [row_spec, row_spec,
                pl.BlockSpec((d, d), lambda i, j: (0, 0), pipeline_mode=pl.Buffered(1)),
                pl.BlockSpec((1, d), lambda i, j: (0, 0)),
                pl.BlockSpec((d, tf), lambda i, j: (0, j)),
                pl.BlockSpec((d, tf), lambda i, j: (0, j)),
                pl.BlockSpec((tf, d), lambda i, j: (j, 0))]
    args = [x, mix, wo, nf, wg, wu, wd]
    if with_next:
        in_specs.append(pl.BlockSpec((1, d), lambda i, j: (0, 0)))
        args.append(g_next)
        out_specs = [out_spec, out_spec]
        out_shape = [jax.ShapeDtypeStruct((rows, d), f32), jax.ShapeDtypeStruct((rows, d), bf16)]
    else:
        assert tail_rows % 8 == 0 and tail_rows <= tm
        out_specs = [out_spec, pl.BlockSpec((tail_rows, d), lambda i, j: (0, 0))]
        out_shape = [jax.ShapeDtypeStruct((rows - tail_rows, d), f32), jax.ShapeDtypeStruct((tail_rows, d), f32)]
    return pl.pallas_call(
        functools.partial(_ffn_kernel, with_next=with_next, tail_rows=tail_rows),
        grid=(n_tiles, dff // tf),
        in_specs=in_specs,
        out_specs=out_specs,
        out_shape=out_shape,
        scratch_shapes=[pltpu.VMEM((tm, d), bf16)],
        compiler_params=pltpu.CompilerParams(dimension_semantics=("arbitrary", "arbitrary"),
                                             vmem_limit_bytes=VMEM_LIMIT),
        name="outproj_ffn",
    )(*args)


def _head_norm(x, sel_ref, exp_ref, g):
    n, w = x.shape
    x2 = x * x
    hi = x2.astype(bf16)
    lo = (x2 - hi.astype(f32)).astype(bf16)
    s = _dot(jnp.concatenate([hi, lo], axis=0), sel_ref[0:w, :])
    return x * _dot_split3(lax.rsqrt((s[0:n] + s[n:2 * n]) * (1.0 / HEAD_DIM_A) + EPS), exp_ref[:, 0:w]) * g


def _dot_split3(x, m):
    n = x.shape[0]
    y = _dot(jnp.concatenate(_split3(x), axis=0), m)
    return y[0:n] + y[n:2 * n] + y[2 * n:3 * n]


def _rope_a(x, tab_ref):
    half = N_ROT // 2
    first = lax.broadcasted_iota(jnp.int32, (x.shape[0], 128), 1) % HEAD_DIM_A < half
    out = []
    for t in range(x.shape[1] // 128):
        xt = x[:, 128 * t:128 * (t + 1)]
        partner = jnp.where(first, -pltpu.roll(xt, 128 - half, 1), pltpu.roll(xt, half, 1))
        out.append(xt * tab_ref[0] + partner * tab_ref[1])
    return jnp.concatenate(out, axis=1)


def _rope_r(x, tab_ref):
    return x * tab_ref[0] + pltpu.roll(x, DK_C // 2, 1) * tab_ref[1]


def _forget(fb, lb):
    f = jnp.maximum(lb, LB_FLOOR) + (1.0 - lb) * jax.nn.sigmoid(fb)
    return f, jnp.log2(f)


def _split3(x):
    h1 = x.astype(bf16)
    r1 = x - h1.astype(f32)
    h2 = r1.astype(bf16)
    h3 = (r1 - h2.astype(f32)).astype(bf16)
    return h1, h2, h3


def _mixer_prompt_kernel(sink_ref, gpow_ref, proj_ref, ropea_ref, roper_ref, qn_ref, kn_ref, lb_ref, hn_ref, rn_ref,
                         sel_ref, exp_ref, tri_ref, dsel_ref, dec_ref, rsc_ref, tail_ref,
                         wg_ref, wu_ref, wd_ref, wo_ref,
                         mix_ref, ck_ref, cv_ref, sh_ref, sr_ref, wgb_ref, wub_ref, wdb_ref, wob_ref,
                         kk, vv, sht, srs, qs_s, qb_s, kb_s, g_s):
    i = pl.program_id(1)
    nb = pl.num_programs(1)

    @pl.when(i == 0)
    def _():
        kk[...] = jnp.zeros_like(kk)
        vv[...] = jnp.zeros_like(vv)
        for g in range(N_KV_A):
            vv[:, 256 * g + 128:256 * (g + 1)] = jnp.ones((2 * BLK, 128), bf16)
        sht[...] = jnp.zeros_like(sht)
        srs[...] = jnp.zeros_like(srs)

    wgb_ref[...] = wg_ref[...].astype(bf16)
    wub_ref[...] = wu_ref[...].astype(bf16)
    wdb_ref[...] = wd_ref[...].astype(bf16)
    wob_ref[...] = wo_ref[...].astype(bf16)

    rowabs = i * BLK + lax.broadcasted_iota(jnp.int32, (BLK, 1), 0)
    fgate, logf = _forget(proj_ref[:, F_B:F_B + 512], lb_ref[...])
    logf = jnp.where(rowabs >= PAD, logf, 0.0)
    kb_s[...] = jnp.where(rowabs >= PAD, 1.0 - fgate, 0.0)
    qb_s[...] = _silu(proj_ref[:, Q_B:Q_B + 512])
    l1, l2, l3 = _split3(logf)
    tri = tri_ref[...]
    g_s[...] = _dot(tri, l1) + _dot(tri, l2) + _dot(tri, l3)

    r2 = lax.broadcasted_iota(jnp.int32, (BLK, BLK), 0)
    c2 = lax.broadcasted_iota(jnp.int32, (BLK, BLK), 1)
    lane_lo = c2 < HEAD_DIM_A

    def attention_prep():
        qa = _rope_a(_head_norm(proj_ref[:, Q_A:Q_A + 1024], sel_ref, exp_ref, qn_ref[...]), ropea_ref)
        for t in range(N_HEADS_A // 2):
            xt = qa[:, 128 * t:128 * (t + 1)] * Q_SCALE
            qs_s[256 * t:256 * t + 128, :] = jnp.where(lane_lo, xt, 0.0).astype(bf16)
            qs_s[256 * t + 128:256 * (t + 1), :] = jnp.where(lane_lo, 0.0, xt).astype(bf16)
        k_cur = _rope_a(_head_norm(proj_ref[:, K_A:K_A + 256], sel_ref, exp_ref, kn_ref[...]), ropea_ref)
        v_cur = proj_ref[:, V_A:V_A + 256]
        ck_ref[0] = k_cur
        cv_ref[0] = v_cur
        for t in range(N_KV_A // 2):
            for src, dst, width in ((k_cur, kk, 128), (v_cur, vv, 256)):
                xt = src[:, 128 * t:128 * (t + 1)]
                xs = pltpu.roll(xt, HEAD_DIM_A, 1)
                dst[BLK:2 * BLK, width * 2 * t:width * 2 * t + 128] = jnp.where(lane_lo, xt, xs).astype(bf16)
                dst[BLK:2 * BLK, width * (2 * t + 1):width * (2 * t + 1) + 128] = jnp.where(lane_lo, xs, xt).astype(bf16)

    prow = 2 * BLK
    r4 = lax.broadcasted_iota(jnp.int32, (prow, BLK), 0) % BLK
    c4 = lax.broadcasted_iota(jnp.int32, (prow, BLK), 1)
    up = c4 > r4
    ok = c4 >= jnp.where(up, PAD - (i - 1) * BLK, PAD - i * BLK)

    def attention_pair(t):
        g = (2 * t) // GROUP_A
        s2 = _dot(qs_s[prow * t:prow * (t + 1), :], kk[:, 128 * g:128 * (g + 1)], NT)
        s = jnp.where(ok, jnp.where(up, s2[:, 0:BLK], s2[:, BLK:2 * BLK]), NEG_INF)
        sk = jnp.concatenate([jnp.full((BLK, BLK), sink_ref[2 * t + j] * LOG2E, f32) for j in range(2)], axis=0)
        m = jnp.maximum(jnp.max(s, axis=-1, keepdims=True), sk)
        p = jnp.exp2(s - m)
        p2 = jnp.concatenate([jnp.where(up, p, 0.0), jnp.where(up, 0.0, p)], axis=1).astype(bf16)
        res = _dot(p2, vv[:, 256 * g:256 * (g + 1)])
        y = res[:, 0:128] / (res[:, 128:256] + jnp.exp2(sk - m))
        mix_ref[:, 128 * t:128 * (t + 1)] = jnp.where(lane_lo, y[0:BLK], y[BLK:prow]).astype(bf16)

    levels = []
    bz = BLK
    while bz > SUB:
        levels.append((bz, (r2 % bz) >= bz // 2, None if bz == BLK else (r2 // bz) == (c2 // bz)))
        bz //= 2
    same_sub = (r2 // SUB) == (c2 // SUB)
    sub_ge = [(r2 % SUB) >= s for s in range(SUB)]

    def retention_head(h):
        qc = _rope_r(proj_ref[:, Q_C + 128 * h:Q_C + 128 * (h + 1)], roper_ref)
        kc = _rope_r(proj_ref[:, K_C + 128 * h:K_C + 128 * (h + 1)], roper_ref) * (DK_C ** -0.5)
        vc = proj_ref[:, V_C + 128 * h:V_C + 128 * (h + 1)].astype(bf16)
        qcb = qc.astype(bf16)
        sc = _dot(qcb, kc.astype(bf16), NT) * dec_ref[h]
        oc = _dot(qcb, srs[h].astype(bf16)) * rsc_ref[h] + _dot(sc.astype(bf16), vc)
        srs[h] = gpow_ref[h] * srs[h] + _dot((kc * tail_ref[h]).astype(bf16), vc, TN)
        yc = _rms(oc, rn_ref[...]) * _silu(proj_ref[:, G_C + 128 * h:G_C + 128 * (h + 1)])
        mix_ref[:, W_A + 512 + 128 * h:W_A + 512 + 128 * (h + 1)] = yc.astype(bf16)

    def hgrn_intra(h):
        sl = slice(128 * h, 128 * (h + 1))
        gc = g_s[:, sl]
        qb = qb_s[:, sl]
        kb = kb_s[:, sl]
        inter = _dot((qb * jnp.exp2(gc)).astype(bf16), sht[h].astype(bf16), NT)
        amat = None
        for bz, upper, same in levels:
            gref = jnp.concatenate([jnp.broadcast_to(g_s[pl.ds(st + bz // 2 - 1, 1), sl], (bz, 128))
                                    for st in range(0, BLK, bz)], axis=0)
            dg = gc - gref
            e = jnp.exp2(jnp.where(upper, dg, -dg))
            qt = jnp.where(upper, qb * e, 0.0).astype(bf16)
            kt = jnp.where(upper, 0.0, kb * e).astype(bf16)
            pm = _dot(qt, kt, NT)
            if same is not None:
                pm = jnp.where(same, pm, 0.0)
            amat = pm if amat is None else amat + pm
        ys = []
        for s in range(SUB):
            ks = jnp.concatenate([jnp.broadcast_to(kb_s[pl.ds(SUB * j + s, 1), sl], (SUB, 128))
                                  for j in range(BLK // SUB)], axis=0)
            gs = jnp.concatenate([jnp.broadcast_to(g_s[pl.ds(SUB * j + s, 1), sl], (SUB, 128))
                                  for j in range(BLK // SUB)], axis=0)
            ys.append((qb * ks * jnp.exp2(jnp.where(sub_ge[s], gc - gs, NEG_INF))).astype(bf16))
        amat = amat + jnp.where(same_sub, _dot(jnp.concatenate(ys, axis=1), dsel_ref[...]), 0.0)
        return inter, amat.astype(bf16)

    def hgrn_finish(h, inter, amat):
        sl = slice(128 * h, 128 * (h + 1))
        vb = proj_ref[:, I_B + 128 * h:I_B + 128 * (h + 1)].astype(bf16)
        ob = inter + _dot(amat, vb)
        glast = g_s[pl.ds(BLK - 1, 1), sl]
        kt2 = (kb_s[:, sl] * jnp.exp2(glast - g_s[:, sl])).astype(bf16)
        sht[h] = sht[h] * jnp.exp2(glast) + _dot(vb, kt2, TN)
        yb = _rms(ob, hn_ref[...]) * _silu(proj_ref[:, G_B + 128 * h:G_B + 128 * (h + 1)])
        mix_ref[:, W_A + 128 * h:W_A + 128 * (h + 1)] = yb.astype(bf16)

    attention_prep()
    pending = None
    for h in range(H_B):
        cur = hgrn_intra(h)
        if pending is not None:
            hgrn_finish(h - 1, *pending)
        retention_head(h)
        pending = cur
    hgrn_finish(H_B - 1, *pending)
    for t in range(N_HEADS_A // 2):
        attention_pair(t)
    kk[0:BLK, :] = kk[BLK:2 * BLK, :]
    vv[0:BLK, :] = vv[BLK:2 * BLK, :]

    @pl.when(i == nb - 1)
    def _():
        for h in range(H_B):
            sh_ref[0, h] = sht[h].T
            sr_ref[0, h] = srs[h]


def _const_spec(shape):
    nd = len(shape)
    return pl.BlockSpec(shape, lambda *_: (0,) * nd)


def _slab_rows(nrows, steps):
    for r in range(16, nrows + 1, 16):
        if nrows % r == 0 and nrows // r <= steps:
            return r
    return nrows


def _mixer_prompt(proj, rows_total, batch, nb, layer, sinks, gpow, ropea, roper, qn, kn, lb, hn, rn, sel, expand,
                  tri, dsel, dec, rsc, tail, w_gate, w_up, w_down, w_out):
    smem = pl.BlockSpec(memory_space=pltpu.SMEM)
    d, dff = w_gate.shape[1], w_gate.shape[2]

    def row_blk(b, i):
        return jnp.where(i == 0, batch * (nb - 1), b * (nb - 1) + i - 1)

    def mix_blk(b, i):
        return jnp.where((i == 0) & (b > 0), b * (nb - 1), row_blk(b, i))

    def slab_in(nrows, ncols):
        r = _slab_rows(nrows, batch * nb)
        return pl.BlockSpec((None, r, ncols), lambda b, i: (layer, jnp.minimum(b * nb + i, nrows // r - 1), 0))

    def slab_out(nrows, ncols):
        r = _slab_rows(nrows, batch * nb)
        return pl.BlockSpec((r, ncols), lambda b, i: (jnp.minimum(b * nb + i, nrows // r - 1), 0))

    return pl.pallas_call(
        _mixer_prompt_kernel,
        grid=(batch, nb),
        in_specs=[smem, smem,
                  pl.BlockSpec((BLK, D_IN), lambda b, i: (row_blk(b, i), 0)),
                  pl.BlockSpec((2, BLK, 128), lambda b, i: (0, i, 0)),
                  pl.BlockSpec((2, BLK, 128), lambda b, i: (0, i, 0)),
                  _const_spec((1, 1024)), _const_spec((1, 256)), _const_spec((1, 512)),
                  _const_spec((1, 128)), _const_spec((1, 128)),
                  _const_spec((1024, 128)), _const_spec((128, 1024)),
                  _const_spec((BLK, BLK)), _const_spec((SUB * 128, 128)),
                  _const_spec((H_C, BLK, BLK)), _const_spec((H_C, BLK, 128)), _const_spec((H_C, BLK, 128)),
                  slab_in(d, dff), slab_in(d, dff), slab_in(dff, d), slab_in(d, d)],
        out_specs=[pl.BlockSpec((BLK, 2048), lambda b, i: (mix_blk(b, i), 0)),
                   pl.BlockSpec((1, BLK, 256), lambda b, i: (b, 0, 0)),
                   pl.BlockSpec((1, BLK, 256), lambda b, i: (b, 0, 0)),
                   pl.BlockSpec((1, H_B, 128, 128), lambda b, i: (b, 0, 0, 0)),
                   pl.BlockSpec((1, H_C, 128, 128), lambda b, i: (b, 0, 0, 0)),
                   slab_out(d, dff), slab_out(d, dff), slab_out(dff, d), slab_out(d, d)],
        out_shape=[jax.ShapeDtypeStruct((rows_total, 2048), bf16),
                   jax.ShapeDtypeStruct((batch, BLK, 256), f32),
                   jax.ShapeDtypeStruct((batch, BLK, 256), f32),
                   jax.ShapeDtypeStruct((batch, H_B, 128, 128), f32),
                   jax.ShapeDtypeStruct((batch, H_C, 128, 128), f32),
                   jax.ShapeDtypeStruct((d, dff), bf16), jax.ShapeDtypeStruct((d, dff), bf16),
                   jax.ShapeDtypeStruct((dff, d), bf16), jax.ShapeDtypeStruct((d, d), bf16)],
        scratch_shapes=[pltpu.VMEM((2 * BLK, N_KV_A * 128), bf16), pltpu.VMEM((2 * BLK, N_KV_A * 256), bf16),
                        pltpu.VMEM((H_B, 128, 128), f32), pltpu.VMEM((H_C, 128, 128), f32),
                        pltpu.VMEM((N_HEADS_A * BLK, 128), bf16),
                        pltpu.VMEM((BLK, 512), f32), pltpu.VMEM((BLK, 512), f32), pltpu.VMEM((BLK, 512), f32)],
        compiler_params=pltpu.CompilerParams(dimension_semantics=("arbitrary", "arbitrary"),
                                             vmem_limit_bytes=VMEM_LIMIT),
        name="mixer_prompt",
    )(sinks, gpow, proj, ropea, roper, qn, kn, lb, hn, rn, sel, expand, tri, dsel, dec, rsc, tail,
      w_gate, w_up, w_down, w_out)


def _mixer_sample_kernel(sink_ref, gpow_ref, mixin_ref, *refs, dec_seq, steps, n_prev):
    del mixin_ref
    refs = refs[:17] + refs[17 + n_prev:]
    c = pl.program_id(0)
    mix_ref = refs[17]
    pl.when(c < steps)(lambda: _mixer_sample_body(sink_ref, gpow_ref, *refs, dec_seq=dec_seq))

    @pl.when(c >= steps)
    def _():
        mix_ref[...] = jnp.zeros_like(mix_ref)


def _mixer_sample_body(sink_ref, gpow_ref, proj_ref, ropea_ref, roper_ref, qn_ref, kn_ref, lb_ref,
                       hn_ref, rn_ref, sel_ref, exp_ref, dec_ref, rsc_ref, tail_ref,
                       ck_ref, cv_ref, sh_ref, sr_ref,
                       mix_ref, nk_ref, nv_ref, nsh_ref, nsr_ref, *, dec_seq):
    nbat = SROWS // dec_seq
    w = ck_ref.shape[1]
    rb = lax.broadcasted_iota(jnp.int32, (SROWS, 1), 0) // dec_seq
    rt = lax.broadcasted_iota(jnp.int32, (SROWS, 1), 0) % dec_seq

    def pick(parts):
        out = parts[nbat - 1]
        for b in range(nbat - 2, -1, -1):
            out = jnp.where(rb == b, parts[b], out)
        return out

    qa = _rope_a(_head_norm(proj_ref[:, Q_A:Q_A + 1024], sel_ref, exp_ref, qn_ref[...]), ropea_ref)
    qs = (qa * Q_SCALE).astype(bf16)
    k_new = _rope_a(_head_norm(proj_ref[:, K_A:K_A + 256], sel_ref, exp_ref, kn_ref[...]), ropea_ref)
    v_new = proj_ref[:, V_A:V_A + 256]
    for b in range(nbat):
        nk_ref[b, 0:w - dec_seq, :] = ck_ref[b, dec_seq:w, :]
        nk_ref[b, w - dec_seq:w, :] = k_new[dec_seq * b:dec_seq * (b + 1), :]
        nv_ref[b, 0:w - dec_seq, :] = cv_ref[b, dec_seq:w, :]
        nv_ref[b, w - dec_seq:w, :] = v_new[dec_seq * b:dec_seq * (b + 1), :]
    knb = k_new.astype(bf16)
    vnb = v_new.astype(bf16)

    grows = GROUP_A * SROWS
    rt_g = jnp.concatenate([rt] * GROUP_A, axis=0)
    rb_g = jnp.concatenate([rb] * GROUP_A, axis=0)
    jc = lax.broadcasted_iota(jnp.int32, (grows, w), 1)
    valid_c = jc > rt_g + (w - WINDOW)
    cn = lax.broadcasted_iota(jnp.int32, (grows, SROWS), 1)
    valid_n = (cn // dec_seq == rb_g) & (cn % dec_seq <= rt_g)
    heads = [None] * N_HEADS_A
    for g in range(N_KV_A):
        ksl = slice(64 * g, 64 * (g + 1))
        qg = jnp.concatenate([qs[:, 64 * h:64 * (h + 1)] for h in range(GROUP_A * g, GROUP_A * (g + 1))], axis=0)
        sparts = [_dot(qg, ck_ref[b, :, ksl].astype(bf16), NT) for b in range(nbat)]
        s_c = sparts[nbat - 1]
        for b in range(nbat - 2, -1, -1):
            s_c = jnp.where(rb_g == b, sparts[b], s_c)
        s_c = jnp.where(valid_c, s_c, NEG_INF)
        s_n = jnp.where(valid_n, _dot(qg, knb[:, ksl], NT), NEG_INF)
        sk = jnp.concatenate([jnp.full((SROWS, 1), sink_ref[h] * LOG2E, f32)
                              for h in range(GROUP_A * g, GROUP_A * (g + 1))], axis=0)
        m = jnp.maximum(jnp.maximum(jnp.max(s_c, axis=-1, keepdims=True), jnp.max(s_n, axis=-1, keepdims=True)), sk)
        p_c = jnp.exp2(s_c - m)
        p_n = jnp.exp2(s_n - m)
        den = jnp.sum(p_c, axis=-1, keepdims=True) + jnp.sum(p_n, axis=-1, keepdims=True) + jnp.exp2(sk - m)
        pcb = p_c.astype(bf16)
        oparts = [_dot(pcb, cv_ref[b, :, ksl].astype(bf16)) for b in range(nbat)]
        o = oparts[nbat - 1]
        for b in range(nbat - 2, -1, -1):
            o = jnp.where(rb_g == b, oparts[b], o)
        o = (o + _dot(p_n.astype(bf16), vnb[:, ksl])) / den
        for hh in range(GROUP_A):
            heads[GROUP_A * g + hh] = o[SROWS * hh:SROWS * (hh + 1), :]
    mix_ref[:, 0:W_A] = jnp.concatenate(heads, axis=1).astype(bf16)

    fgate, logf = _forget(proj_ref[:, F_B:F_B + 512], lb_ref[...])
    kb = 1.0 - fgate
    qb = _silu(proj_ref[:, Q_B:Q_B + 512])
    vb = proj_ref[:, I_B:I_B + 512]
    gcum = logf
    for d in range(1, dec_seq):
        gcum = gcum + jnp.where(rt >= d, pltpu.roll(logf, d, 0), 0.0)
    intra = [jnp.zeros((SROWS, 128), f32) for _ in range(H_B)]
    for s in range(SROWS):
        ok = (rb == s // dec_seq) & (rt >= s % dec_seq)
        y = qb * kb[s:s + 1, :] * jnp.exp2(jnp.where(ok, gcum - gcum[s:s + 1, :], NEG_INF))
        for h in range(H_B):
            sl = slice(128 * h, 128 * (h + 1))
            intra[h] = intra[h] + jnp.sum(y[:, sl], axis=-1, keepdims=True) * vb[s:s + 1, sl]
    qg = (qb * jnp.exp2(gcum)).astype(bf16)
    vbb = vb.astype(bf16)
    glast = [gcum[dec_seq * (b + 1) - 1:dec_seq * (b + 1), :] for b in range(nbat)]
    for h in range(H_B):
        sl = slice(128 * h, 128 * (h + 1))
        ob = pick([_dot(qg[:, sl], sh_ref[b, h].astype(bf16)) for b in range(nbat)]) + intra[h]
        for b in range(nbat):
            kt2 = jnp.where(rb == b, kb[:, sl] * jnp.exp2(glast[b][:, sl] - gcum[:, sl]), 0.0).astype(bf16)
            dcol = jnp.broadcast_to(jnp.exp2(glast[b][:, sl]), (128, 128)).T
            nsh_ref[b, h] = dcol * sh_ref[b, h] + _dot(kt2, vbb[:, sl], TN)
        yb = _rms(ob, hn_ref[...]) * _silu(proj_ref[:, G_B + 128 * h:G_B + 128 * (h + 1)])
        mix_ref[:, W_A + 128 * h:W_A + 128 * (h + 1)] = yb.astype(bf16)

    for h in range(H_C):
        qc = _rope_r(proj_ref[:, Q_C + 128 * h:Q_C + 128 * (h + 1)], roper_ref)
        kc = _rope_r(proj_ref[:, K_C + 128 * h:K_C + 128 * (h + 1)], roper_ref) * (DK_C ** -0.5)
        vc = proj_ref[:, V_C + 128 * h:V_C + 128 * (h + 1)]
        qcb = qc.astype(bf16)
        vcb = vc.astype(bf16)
        sc = _dot(qcb, kc.astype(bf16), NT) * dec_ref[h]
        oc = pick([_dot(qcb, sr_ref[b, h].astype(bf16)) for b in range(nbat)]) * rsc_ref[h] \
            + _dot(sc.astype(bf16), vcb)
        kct = kc * tail_ref[h]
        for b in range(nbat):
            nsr_ref[b, h] = gpow_ref[h] * sr_ref[b, h] + _dot(jnp.where(rb == b, kct, 0.0).astype(bf16), vcb, TN)
        yc = _rms(oc, rn_ref[...]) * _silu(proj_ref[:, G_C + 128 * h:G_C + 128 * (h + 1)])
        mix_ref[:, W_A + 512 + 128 * h:W_A + 512 + 128 * (h + 1)] = yc.astype(bf16)


def _mixer_sample(mix, proj, row0, layer, dec_batch, dec_seq, sinks, gpow, ropea, roper, qn, kn, lb, hn, rn, sel, expand,
                  dec, rsc, tail, cache_k, cache_v, state_h, state_r, prev_outs):
    smem = pl.BlockSpec(memory_space=pltpu.SMEM)
    nbat = SROWS // dec_seq
    steps = dec_batch // nbat
    blk0 = row0 // SROWS
    w = cache_k.shape[1]
    n_all = cache_k.shape[0]
    fill_steps = (mix.shape[0] - row0) // SROWS - steps
    cl = lambda c: jnp.minimum(c, steps - 1)
    any_spec = pl.BlockSpec(memory_space=pl.ANY)
    n_in = 20
    return pl.pallas_call(
        functools.partial(_mixer_sample_kernel, dec_seq=dec_seq, steps=steps, n_prev=len(prev_outs)),
        grid=(steps + fill_steps,),
        in_specs=[smem, smem,
                  pl.BlockSpec(memory_space=pl.ANY),
                  pl.BlockSpec((SROWS, D_IN), lambda c: (blk0 + cl(c), 0)),
                  _const_spec((2, SROWS, 128)), _const_spec((2, SROWS, 128)),
                  _const_spec((1, 1024)), _const_spec((1, 256)), _const_spec((1, 512)),
                  _const_spec((1, 128)), _const_spec((1, 128)),
                  _const_spec((1024, 128)), _const_spec((128, 1024)),
                  _const_spec((H_C, SROWS, SROWS)), _const_spec((H_C, SROWS, 128)), _const_spec((H_C, SROWS, 128)),
                  pl.BlockSpec((nbat, w, 256), lambda c: (layer * steps + cl(c), 0, 0)),
                  pl.BlockSpec((nbat, w, 256), lambda c: (layer * steps + cl(c), 0, 0)),
                  pl.BlockSpec((nbat, H_B, 128, 128), lambda c: (layer * steps + cl(c), 0, 0, 0)),
                  pl.BlockSpec((nbat, H_C, 128, 128), lambda c: (layer * steps + cl(c), 0, 0, 0))]
        + [any_spec] * len(prev_outs),
        out_specs=[pl.BlockSpec((SROWS, 2048), lambda c: (blk0 + c, 0)),
                   pl.BlockSpec((nbat, w, 256), lambda c: (layer * steps + cl(c), 0, 0)),
                   pl.BlockSpec((nbat, w, 256), lambda c: (layer * steps + cl(c), 0, 0)),
                   pl.BlockSpec((nbat, H_B, 128, 128), lambda c: (layer * steps + cl(c), 0, 0, 0)),
                   pl.BlockSpec((nbat, H_C, 128, 128), lambda c: (layer * steps + cl(c), 0, 0, 0))],
        out_shape=[jax.ShapeDtypeStruct(mix.shape, bf16),
                   jax.ShapeDtypeStruct((n_all, w, 256), f32),
                   jax.ShapeDtypeStruct((n_all, w, 256), f32),
                   jax.ShapeDtypeStruct((n_all, H_B, 128, 128), f32),
                   jax.ShapeDtypeStruct((n_all, H_C, 128, 128), f32)],
        input_output_aliases={2: 0, **{n_in + k: 1 + k for k in range(len(prev_outs))}},
        compiler_params=pltpu.CompilerParams(dimension_semantics=("arbitrary",), vmem_limit_bytes=VMEM_LIMIT),
        name="mixer_sample",
    )(sinks, gpow, mix, proj, ropea, roper, qn, kn, lb, hn, rn, sel, expand, dec, rsc, tail,
      cache_k, cache_v, state_h, state_r, *prev_outs)


def _rope_tables(pos):
    posf = jnp.asarray(pos).astype(f32)[:, None]
    t = pos.shape[0]
    half = N_ROT // 2
    inv = ROPE_THETA ** (-jnp.arange(half, dtype=f32) * (2.0 / N_ROT))
    ang = posf * inv[None, :]
    cos, sin = jnp.cos(ang), jnp.sin(ang)
    rest0 = jnp.zeros((t, HEAD_DIM_A - N_ROT), f32)
    c64 = jnp.concatenate([cos, cos, rest0 + 1.0], axis=1)
    s64 = jnp.concatenate([sin, sin, rest0], axis=1)
    ropea = jnp.stack([jnp.tile(c64, (1, 2)), jnp.tile(s64, (1, 2))])
    invr = RET_THETA ** (-jnp.arange(DK_C // 2, dtype=f32) * (2.0 / DK_C))
    angr = posf * invr[None, :]
    cr, sr = jnp.cos(angr), jnp.sin(angr)
    roper = jnp.stack([jnp.concatenate([cr, cr], axis=1), jnp.concatenate([-sr, sr], axis=1)])
    return ropea, roper


def _ret_tables(lg, seq_of_row, tok_of_row, length):
    nf = np.float32
    tq = tok_of_row.astype(nf)
    rel = tq[:, None] - tq[None, :]
    ok = (seq_of_row[:, None] == seq_of_row[None, :]) & (rel >= 0)
    dec = np.where(ok[None], np.exp(np.where(ok, rel, nf(0.0))[None] * lg[:, None, None]), nf(0.0))
    n = tq.shape[0]
    rsc = np.broadcast_to(np.exp((tq + nf(1.0))[None, :, None] * lg[:, None, None]), (H_C, n, 128))
    tail = np.broadcast_to(np.exp((nf(length) - nf(1.0) - tq)[None, :, None] * lg[:, None, None]), (H_C, n, 128))
    gpow = np.exp(nf(length) * lg)
    return dec.astype(nf), np.ascontiguousarray(rsc, nf), np.ascontiguousarray(tail, nf), gpow.astype(nf)


def kernel(x_prompt, x_sample, cache_k, cache_v, state_hgrn, state_ret, meta_tokens, norm_mix, norm_ffn, w_in, q_norm, k_norm, attn_sinks, hgrn_lb, hgrn_norm, ret_norm, w_out, w_gate, w_up, w_down):
    batch, seq, d = x_prompt.shape
    dec_batch, dec_seq, _ = x_sample.shape
    depth = w_in.shape[0]
    w = cache_k.shape[2]
    assert d == 2048 and w_in.shape[2] == D_IN and seq % BLK == 0
    assert SROWS % dec_seq == 0 and dec_batch % (SROWS // dec_seq) == 0 and w == WINDOW
    nb = seq // BLK + 1
    lp = nb * BLK
    rows_main = batch * seq
    rows_s = dec_batch * dec_seq
    row_s0 = rows_main + BLK
    tail_rows = BLK + rows_s
    rows = rows_main + tail_rows
    n_main = rows_main // TM_F
    tm_all = rows // n_main
    assert rows_main % TM_F == 0 and rows_main % tail_rows == 0 and rows_s % SROWS == 0
    assert rows % n_main == 0 and tm_all % 16 == 0 and tail_rows % 16 == 0

    tail = jnp.concatenate([jnp.zeros((PAD, d), f32), meta_tokens.astype(f32), x_sample.reshape(rows_s, d)], axis=0)
    x, h = _embed(x_prompt.reshape(rows_main, d), tail, norm_mix[0][None])

    p = jax.nn.softmax(hgrn_lb.astype(f32), axis=0)
    lbs = jnp.cumsum(p, axis=0) - p[0]
    lg = np.log1p(-np.exp2(np.float32(-5.0) - np.arange(H_C, dtype=np.float32))).astype(np.float32)
    ropea_p, roper_p = _rope_tables(np.arange(lp) - PAD)
    srow = np.arange(SROWS)
    ropea_s, roper_s = _rope_tables(PAST_LEN + srow % dec_seq)
    dec_p, rsc_p, tail_p, gpow_p = _ret_tables(lg, np.zeros((BLK,), np.int32), np.arange(BLK), float(BLK))
    dec_s, rsc_s, tail_s, gpow_s = _ret_tables(lg, srow // dec_seq, srow % dec_seq, float(dec_seq))
    sel_np = np.arange(1024)[:, None] // HEAD_DIM_A == np.arange(128)[None, :]
    sel = jnp.asarray(sel_np, bf16)
    expand = jnp.asarray(sel_np.T, bf16)
    tri = jnp.asarray(np.arange(BLK)[:, None] >= np.arange(BLK)[None, :], bf16)
    dsel = jnp.asarray(np.arange(SUB * 128)[:, None] // 128 == np.arange(128)[None, :] % SUB, bf16)

    ck_flat = cache_k.reshape(depth * dec_batch, w, N_KV_A * HEAD_DIM_A)
    cv_flat = cache_v.reshape(depth * dec_batch, w, N_KV_A * HEAD_DIM_A)
    sh_flat = state_hgrn.reshape(depth * dec_batch, H_B, 128, 128)
    sr_flat = state_ret.reshape(depth * dec_batch, H_C, 128, 128)

    outs_p, outs_s = [], ()
    for l in range(depth):
        proj = _inproj(h, w_in, l)
        qn = jnp.tile(q_norm[l], N_HEADS_A)[None]
        kn = jnp.tile(k_norm[l], N_KV_A)[None]
        common = (qn, kn, lbs[l][None], hgrn_norm[l][None], ret_norm[l][None], sel, expand)
        mix, ck, cv, sh, sr, wg_b, wu_b, wd_b, wo_b = _mixer_prompt(
            proj, rows, batch, nb, l, attn_sinks[l], gpow_p, ropea_p, roper_p, *common, tri, dsel,
            dec_p, rsc_p, tail_p, w_gate, w_up, w_down, w_out)
        mix, *outs_s = _mixer_sample(mix, proj, row_s0, l, dec_batch, dec_seq, attn_sinks[l], gpow_s,
                                     ropea_s, roper_s, *common, dec_s, rsc_s, tail_s,
                                     ck_flat, cv_flat, sh_flat, sr_flat, tuple(outs_s))
        ffn_w = (wo_b, norm_ffn[l][None], wg_b, wu_b, wd_b)
        outs_p.append((ck, cv, sh, sr))
        if l + 1 < depth:
            x, h = _ffn(x, mix, *ffn_w, norm_mix[l + 1][None], tm_all, tail_rows)
        else:
            y_main, y_tail = _ffn(x, mix, *ffn_w, None, tm_all, tail_rows)

    y_prompt = y_main.reshape(batch, seq, d)
    y_sample = y_tail[BLK:].reshape(dec_batch, dec_seq, d)
    kv_shape = (depth, -1, w, N_KV_A, HEAD_DIM_A)
    st_shape = (depth, dec_batch, H_B, 128, 128)
    stack = lambda outs, k: jnp.stack([o[k] for o in outs])
    nk, nv, nsh, nsr = outs_s
    return (y_prompt, y_sample,
            stack(outs_p, 0).reshape(kv_shape), stack(outs_p, 1).reshape(kv_shape), stack(outs_p, 2), stack(outs_p, 3),
            nk.reshape(kv_shape), nv.reshape(kv_shape), nsh.reshape(st_shape), nsr.reshape(st_shape))
```

```python
import functools

import jax
import jax.numpy as jnp
import numpy as np
from jax import lax
from jax.experimental import pallas as pl
from jax.experimental.pallas import tpu as pltpu

f32 = jnp.float32
bf16 = jnp.bfloat16

N_META = 16
EPS = 1e-6
NEG_INF = -1e30
LB_FLOOR = 1e-30
WINDOW = 128
HEAD_DIM_A = 64
N_HEADS_A = 16
N_KV_A = 4
GROUP_A = N_HEADS_A // N_KV_A
N_ROT = 16
ROPE_THETA = 500000.0
H_B = 4
DK_B = 128
H_C = 4
DK_C = 128
RET_THETA = 10000.0
PAST_LEN = 16384

BLK = 128
PAD = BLK - N_META
SUB = 8
SROWS = 16
TM_FFN = 528
TM_FFN_LAST = 704
VMEM_LIMIT = 56 * 1024 * 1024
VMEM_LIMIT_FFN = 60 * 1024 * 1024

Q_A, K_A, V_A = 0, 1024, 1280
Q_B, F_B, I_B, G_B = 1536, 2048, 2560, 3072
Q_C, K_C, V_C, G_C = 3584, 4096, 4608, 5120
D_IN = 5632
W_A = 1024
LOG2E = 1.4426950408889634
Q_SCALE = HEAD_DIM_A ** -0.5 * LOG2E
NT = (((1,), (1,)), ((), ()))
TN = (((0,), (0,)), ((), ()))


def _dot(a, b, dims=None):
    if dims is None:
        return jnp.dot(a, b, preferred_element_type=f32)
    return lax.dot_general(a, b, dims, preferred_element_type=f32)


def _row_tile(rows, cap, align=8):
    best = align
    for t in range(align, cap + 1, align):
        if rows % t == 0:
            best = t
    return best


def _silu(x):
    return x * jax.nn.sigmoid(x)


def _rms(x, g):
    return x * lax.rsqrt(jnp.mean(x * x, axis=-1, keepdims=True) + EPS) * g


def _inproj_kernel(h_ref, w_ref, o_ref):
    o_ref[...] = _dot(h_ref[...], w_ref[...].astype(bf16))


def _inproj(h, w_all, layer):
    rows, d = h.shape
    n = w_all.shape[2]
    tm = _row_tile(rows, 2176, 16)
    tn = 512
    return pl.pallas_call(
        _inproj_kernel,
        grid=(rows // tm, n // tn),
        in_specs=[pl.BlockSpec((tm, d), lambda i, j: (i, 0)),
                  pl.BlockSpec((None, d, tn), lambda i, j: (layer, 0, j))],
        out_specs=pl.BlockSpec((tm, tn), lambda i, j: (i, j)),
        out_shape=jax.ShapeDtypeStruct((rows, n), f32),
        compiler_params=pltpu.CompilerParams(dimension_semantics=("arbitrary", "arbitrary"),
                                             vmem_limit_bytes=VMEM_LIMIT),
        name="inproj",
    )(h, w_all)


def _embed_kernel(xp_ref, tail_ref, g_ref, x_ref, h_ref):
    i = pl.program_id(0)
    last = pl.num_programs(0) - 1

    def emit(src_ref):
        x = src_ref[...]
        x_ref[...] = x
        h_ref[...] = _rms(x, g_ref[...]).astype(bf16)

    pl.when(i < last)(lambda: emit(xp_ref))
    pl.when(i == last)(lambda: emit(tail_ref))


def _embed(xp, tail, g):
    rows_main, d = xp.shape
    tm = tail.shape[0]
    n_main = rows_main // tm
    rows = rows_main + tm
    return pl.pallas_call(
        _embed_kernel,
        grid=(n_main + 1,),
        in_specs=[pl.BlockSpec((tm, d), lambda i: (jnp.minimum(i, n_main - 1), 0)),
                  pl.BlockSpec((tm, d), lambda i: (0, 0)),
                  pl.BlockSpec((1, d), lambda i: (0, 0))],
        out_specs=[pl.BlockSpec((tm, d), lambda i: (i, 0)),
                   pl.BlockSpec((tm, d), lambda i: (i, 0))],
        out_shape=[jax.ShapeDtypeStruct((rows, d), f32), jax.ShapeDtypeStruct((rows, d), bf16)],
        compiler_params=pltpu.CompilerParams(dimension_semantics=("arbitrary",), vmem_limit_bytes=VMEM_LIMIT),
        name="embed_norm",
    )(xp, tail, g)


def _ffn_kernel(*refs, with_next, tail_rows):
    if with_next:
        x_ref, mix_ref, wo_ref, nf_ref, wg_ref, wu_ref, wd_ref, gn_ref, o_ref, hn_ref, h_ref = refs
    else:
        x_ref, mix_ref, wo_ref, nf_ref, wg_ref, wu_ref, wd_ref, o_ref, tail_ref, h_ref = refs
    j = pl.program_id(1)

    @pl.when(j == 0)
    def _():
        x1 = x_ref[...] + _dot(mix_ref[...], wo_ref[...])
        h_ref[...] = _rms(x1, nf_ref[...]).astype(bf16)
        o_ref[...] = x1

    h = h_ref[...]
    a = _silu(_dot(h, wg_ref[...])) * _dot(h, wu_ref[...])
    o_ref[...] += _dot(a.astype(bf16), wd_ref[...])

    if with_next:
        @pl.when(j == pl.num_programs(1) - 1)
        def _():
            hn_ref[...] = _rms(o_ref[...], gn_ref[...]).astype(bf16)
    else:
        @pl.when((j == pl.num_programs(1) - 1) & (pl.program_id(0) == pl.num_programs(0) - 1))
        def _():
            tail_ref[...] = o_ref[o_ref.shape[0] - tail_rows:, :]


def _ffn(x, mix, wo, nf, wg, wu, wd, g_next, tm, tail_rows):
    rows, d = x.shape
    dff = wg.shape[1]
    tf = 512
    n_tiles = rows // tm
    with_next = g_next is not None
    row_spec = pl.BlockSpec((tm, d), lambda i, j: (i, 0))
    out_spec = row_spec
    in_specs = [row_spec, row_spec,
                pl.BlockSpec((d, d), lambda i, j: (0, 0), pipeline_mode=pl.Buffered(1)),
                pl.BlockSpec((1, d), lambda i, j: (0, 0)),
                pl.BlockSpec((d, tf), lambda i, j: (0, j)),
                pl.BlockSpec((d, tf), lambda i, j: (0, j)),
                pl.BlockSpec((tf, d), lambda i, j: (j, 0))]
    args = [x, mix, wo, nf, wg, wu, wd]
    if with_next:
        in_specs.append(pl.BlockSpec((1, d), lambda i, j: (0, 0)))
        args.append(g_next)
        out_specs = [out_spec, out_spec]
        out_shape = [jax.ShapeDtypeStruct((rows, d), f32), jax.ShapeDtypeStruct((rows, d), bf16)]
    else:
        assert tail_rows % 8 == 0 and tail_rows <= tm
        out_specs = [out_spec, pl.BlockSpec((tail_rows, d), lambda i, j: (0, 0))]
        out_shape = [jax.ShapeDtypeStruct((rows - tail_rows, d), f32), jax.ShapeDtypeStruct((tail_rows, d), f32)]
    return pl.pallas_call(
        functools.partial(_ffn_kernel, with_next=with_next, tail_rows=tail_rows),
        grid=(n_tiles, dff // tf),
        in_specs=in_specs,
        out_specs=out_specs,
        out_shape=out_shape,
        scratch_shapes=[pltpu.VMEM((tm, d), bf16)],
        compiler_params=pltpu.CompilerParams(dimension_semantics=("arbitrary", "arbitrary"),
                                             vmem_limit_bytes=VMEM_LIMIT_FFN),
        name="outproj_ffn",
    )(*args)


def _head_norm(x, sel_ref, exp_ref, g):
    n, w = x.shape
    x2 = x * x
    hi = x2.astype(bf16)
    lo = (x2 - hi.astype(f32)).astype(bf16)
    s = _dot(jnp.concatenate([hi, lo], axis=0), sel_ref[0:w, :])
    return x * _dot_split3(lax.rsqrt((s[0:n] + s[n:2 * n]) * (1.0 / HEAD_DIM_A) + EPS), exp_ref[:, 0:w]) * g


def _dot_split3(x, m):
    n = x.shape[0]
    y = _dot(jnp.concatenate(_split3(x), axis=0), m)
    return y[0:n] + y[n:2 * n] + y[2 * n:3 * n]


def _rope_a(x, tab_ref):
    half = N_ROT // 2
    first = lax.broadcasted_iota(jnp.int32, (x.shape[0], 128), 1) % HEAD_DIM_A < half
    out = []
    for t in range(x.shape[1] // 128):
        xt = x[:, 128 * t:128 * (t + 1)]
        partner = jnp.where(first, -pltpu.roll(xt, 128 - half, 1), pltpu.roll(xt, half, 1))
        out.append(xt * tab_ref[0] + partner * tab_ref[1])
    return jnp.concatenate(out, axis=1)


def _rope_r(x, tab_ref):
    return x * tab_ref[0] + pltpu.roll(x, DK_C // 2, 1) * tab_ref[1]


def _forget(fb, lb):
    f = jnp.maximum(lb, LB_FLOOR) + (1.0 - lb) * jax.nn.sigmoid(fb)
    return f, jnp.log2(f)


def _split3(x):
    h1 = x.astype(bf16)
    r1 = x - h1.astype(f32)
    h2 = r1.astype(bf16)
    h3 = (r1 - h2.astype(f32)).astype(bf16)
    return h1, h2, h3


def _mixer_prompt_kernel(sink_ref, gpow_ref, proj_ref, ropea_ref, roper_ref, qn_ref, kn_ref, lb_ref, hn_ref, rn_ref,
                         sel_ref, exp_ref, tri_ref, dsel_ref, dec_ref, rsc_ref, tail_ref,
                         wg_ref, wu_ref, wd_ref, wo_ref,
                         mix_ref, ck_ref, cv_ref, sh_ref, sr_ref, wgb_ref, wub_ref, wdb_ref, wob_ref,
                         kk, vv, sht, srs, qs_s, qb_s, kb_s, g_s):
    i = pl.program_id(1)
    nb = pl.num_programs(1)

    @pl.when(i == 0)
    def _():
        kk[...] = jnp.zeros_like(kk)
        vv[...] = jnp.zeros_like(vv)
        for g in range(N_KV_A):
            vv[:, 256 * g + 128:256 * (g + 1)] = jnp.ones((2 * BLK, 128), bf16)
        sht[...] = jnp.zeros_like(sht)
        srs[...] = jnp.zeros_like(srs)

    wgb_ref[...] = wg_ref[...].astype(bf16)
    wub_ref[...] = wu_ref[...].astype(bf16)
    wdb_ref[...] = wd_ref[...].astype(bf16)
    wob_ref[...] = wo_ref[...].astype(bf16)

    rowabs = i * BLK + lax.broadcasted_iota(jnp.int32, (BLK, 1), 0)
    fgate, logf = _forget(proj_ref[:, F_B:F_B + 512], lb_ref[...])
    logf = jnp.where(rowabs >= PAD, logf, 0.0)
    kb_s[...] = jnp.where(rowabs >= PAD, 1.0 - fgate, 0.0)
    qb_s[...] = _silu(proj_ref[:, Q_B:Q_B + 512])
    l1, l2, l3 = _split3(logf)
    tri = tri_ref[...]
    g_s[...] = _dot(tri, l1) + _dot(tri, l2) + _dot(tri, l3)

    r2 = lax.broadcasted_iota(jnp.int32, (BLK, BLK), 0)
    c2 = lax.broadcasted_iota(jnp.int32, (BLK, BLK), 1)
    lane_lo = c2 < HEAD_DIM_A

    def attention_prep():
        qa = _rope_a(_head_norm(proj_ref[:, Q_A:Q_A + 1024], sel_ref, exp_ref, qn_ref[...]), ropea_ref)
        for t in range(N_HEADS_A // 2):
            xt = qa[:, 128 * t:128 * (t + 1)] * Q_SCALE
            qs_s[256 * t:256 * t + 128, :] = jnp.where(lane_lo, xt, 0.0).astype(bf16)
            qs_s[256 * t + 128:256 * (t + 1), :] = jnp.where(lane_lo, 0.0, xt).astype(bf16)
        k_cur = _rope_a(_head_norm(proj_ref[:, K_A:K_A + 256], sel_ref, exp_ref, kn_ref[...]), ropea_ref)
        v_cur = proj_ref[:, V_A:V_A + 256]
        ck_ref[0] = k_cur
        cv_ref[0] = v_cur
        for t in range(N_KV_A // 2):
            for src, dst, width in ((k_cur, kk, 128), (v_cur, vv, 256)):
                xt = src[:, 128 * t:128 * (t + 1)]
                xs = pltpu.roll(xt, HEAD_DIM_A, 1)
                dst[BLK:2 * BLK, width * 2 * t:width * 2 * t + 128] = jnp.where(lane_lo, xt, xs).astype(bf16)
                dst[BLK:2 * BLK, width * (2 * t + 1):width * (2 * t + 1) + 128] = jnp.where(lane_lo, xs, xt).astype(bf16)

    prow = 2 * BLK
    r4 = lax.broadcasted_iota(jnp.int32, (prow, BLK), 0) % BLK
    c4 = lax.broadcasted_iota(jnp.int32, (prow, BLK), 1)
    up = c4 > r4
    ok = c4 >= jnp.where(up, PAD - (i - 1) * BLK, PAD - i * BLK)

    def attention_pair(t):
        g = (2 * t) // GROUP_A
        s2 = _dot(qs_s[prow * t:prow * (t + 1), :], kk[:, 128 * g:128 * (g + 1)], NT)
        s = jnp.where(ok, jnp.where(up, s2[:, 0:BLK], s2[:, BLK:2 * BLK]), NEG_INF)
        sk = jnp.concatenate([jnp.full((BLK, BLK), sink_ref[2 * t + j] * LOG2E, f32) for j in range(2)], axis=0)
        m = jnp.maximum(jnp.max(s, axis=-1, keepdims=True), sk)
        p = jnp.exp2(s - m)
        p2 = jnp.concatenate([jnp.where(up, p, 0.0), jnp.where(up, 0.0, p)], axis=1).astype(bf16)
        res = _dot(p2, vv[:, 256 * g:256 * (g + 1)])
        y = res[:, 0:128] / (res[:, 128:256] + jnp.exp2(sk - m))
        mix_ref[:, 128 * t:128 * (t + 1)] = jnp.where(lane_lo, y[0:BLK], y[BLK:prow]).astype(bf16)

    levels = []
    bz = BLK
    while bz > SUB:
        levels.append((bz, (r2 % bz) >= bz // 2, None if bz == BLK else (r2 // bz) == (c2 // bz)))
        bz //= 2
    same_sub = (r2 // SUB) == (c2 // SUB)
    sub_ge = [(r2 % SUB) >= s for s in range(SUB)]

    def retention_head(h):
        qc = _rope_r(proj_ref[:, Q_C + 128 * h:Q_C + 128 * (h + 1)], roper_ref)
        kc = _rope_r(proj_ref[:, K_C + 128 * h:K_C + 128 * (h + 1)], roper_ref) * (DK_C ** -0.5)
        vc = proj_ref[:, V_C + 128 * h:V_C + 128 * (h + 1)].astype(bf16)
        qcb = qc.astype(bf16)
        sc = _dot(qcb, kc.astype(bf16), NT) * dec_ref[h]
        oc = _dot(qcb, srs[h].astype(bf16)) * rsc_ref[h] + _dot(sc.astype(bf16), vc)
        srs[h] = gpow_ref[h] * srs[h] + _dot((kc * tail_ref[h]).astype(bf16), vc, TN)
        yc = _rms(oc, rn_ref[...]) * _silu(proj_ref[:, G_C + 128 * h:G_C + 128 * (h + 1)])
        mix_ref[:, W_A + 512 + 128 * h:W_A + 512 + 128 * (h + 1)] = yc.astype(bf16)

    def hgrn_intra(h):
        sl = slice(128 * h, 128 * (h + 1))
        gc = g_s[:, sl]
        qb = qb_s[:, sl]
        kb = kb_s[:, sl]
        inter = _dot((qb * jnp.exp2(gc)).astype(bf16), sht[h].astype(bf16), NT)
        amat = None
        for bz, upper, same in levels:
            gref = jnp.concatenate([jnp.broadcast_to(g_s[pl.ds(st + bz // 2 - 1, 1), sl], (bz, 128))
                                    for st in range(0, BLK, bz)], axis=0)
            dg = gc - gref
            e = jnp.exp2(jnp.where(upper, dg, -dg))
            qt = jnp.where(upper, qb * e, 0.0).astype(bf16)
            kt = jnp.where(upper, 0.0, kb * e).astype(bf16)
            pm = _dot(qt, kt, NT)
            if same is not None:
                pm = jnp.where(same, pm, 0.0)
            amat = pm if amat is None else amat + pm
        ys = []
        for s in range(SUB):
            ks = jnp.concatenate([jnp.broadcast_to(kb_s[pl.ds(SUB * j + s, 1), sl], (SUB, 128))
                                  for j in range(BLK // SUB)], axis=0)
            gs = jnp.concatenate([jnp.broadcast_to(g_s[pl.ds(SUB * j + s, 1), sl], (SUB, 128))
                                  for j in range(BLK // SUB)], axis=0)
            ys.append((qb * ks * jnp.exp2(jnp.where(sub_ge[s], gc - gs, NEG_INF))).astype(bf16))
        amat = amat + jnp.where(same_sub, _dot(jnp.concatenate(ys, axis=1), dsel_ref[...]), 0.0)
        return inter, amat.astype(bf16)

    def hgrn_finish(h, inter, amat):
        sl = slice(128 * h, 128 * (h + 1))
        vb = proj_ref[:, I_B + 128 * h:I_B + 128 * (h + 1)].astype(bf16)
        ob = inter + _dot(amat, vb)
        glast = g_s[pl.ds(BLK - 1, 1), sl]
        kt2 = (kb_s[:, sl] * jnp.exp2(glast - g_s[:, sl])).astype(bf16)
        sht[h] = sht[h] * jnp.exp2(glast) + _dot(vb, kt2, TN)
        yb = _rms(ob, hn_ref[...]) * _silu(proj_ref[:, G_B + 128 * h:G_B + 128 * (h + 1)])
        mix_ref[:, W_A + 128 * h:W_A + 128 * (h + 1)] = yb.astype(bf16)

    attention_prep()
    pending = None
    for h in range(H_B):
        cur = hgrn_intra(h)
        if pending is not None:
            hgrn_finish(h - 1, *pending)
        retention_head(h)
        pending = cur
    hgrn_finish(H_B - 1, *pending)
    for t in range(N_HEADS_A // 2):
        attention_pair(t)
    kk[0:BLK, :] = kk[BLK:2 * BLK, :]
    vv[0:BLK, :] = vv[BLK:2 * BLK, :]

    @pl.when(i == nb - 1)
    def _():
        for h in range(H_B):
            sh_ref[0, h] = sht[h].T
            sr_ref[0, h] = srs[h]


def _const_spec(shape):
    nd = len(shape)
    return pl.BlockSpec(shape, lambda *_: (0,) * nd)


def _slab_rows(nrows, steps):
    for r in range(16, nrows + 1, 16):
        if nrows % r == 0 and nrows // r <= steps:
            return r
    return nrows


def _mixer_prompt(proj, rows_total, batch, nb, layer, sinks, gpow, ropea, roper, qn, kn, lb, hn, rn, sel, expand,
                  tri, dsel, dec, rsc, tail, w_gate, w_up, w_down, w_out):
    smem = pl.BlockSpec(memory_space=pltpu.SMEM)
    d, dff = w_gate.shape[1], w_gate.shape[2]

    def row_blk(b, i):
        return jnp.where(i == 0, batch * (nb - 1), b * (nb - 1) + i - 1)

    def mix_blk(b, i):
        return jnp.where((i == 0) & (b > 0), b * (nb - 1), row_blk(b, i))

    def slab_in(nrows, ncols):
        r = _slab_rows(nrows, batch * nb)
        return pl.BlockSpec((None, r, ncols), lambda b, i: (layer, jnp.minimum(b * nb + i, nrows // r - 1), 0))

    def slab_out(nrows, ncols):
        r = _slab_rows(nrows, batch * nb)
        return pl.BlockSpec((r, ncols), lambda b, i: (jnp.minimum(b * nb + i, nrows // r - 1), 0))

    return pl.pallas_call(
        _mixer_prompt_kernel,
        grid=(batch, nb),
        in_specs=[smem, smem,
                  pl.BlockSpec((BLK, D_IN), lambda b, i: (row_blk(b, i), 0)),
                  pl.BlockSpec((2, BLK, 128), lambda b, i: (0, i, 0)),
                  pl.BlockSpec((2, BLK, 128), lambda b, i: (0, i, 0)),
                  _const_spec((1, 1024)), _const_spec((1, 256)), _const_spec((1, 512)),
                  _const_spec((1, 128)), _const_spec((1, 128)),
                  _const_spec((1024, 128)), _const_spec((128, 1024)),
                  _const_spec((BLK, BLK)), _const_spec((SUB * 128, 128)),
                  _const_spec((H_C, BLK, BLK)), _const_spec((H_C, BLK, 128)), _const_spec((H_C, BLK, 128)),
                  slab_in(d, dff), slab_in(d, dff), slab_in(dff, d), slab_in(d, d)],
        out_specs=[pl.BlockSpec((BLK, 2048), lambda b, i: (mix_blk(b, i), 0)),
                   pl.BlockSpec((1, BLK, 256), lambda b, i: (b, 0, 0)),
                   pl.BlockSpec((1, BLK, 256), lambda b, i: (b, 0, 0)),
                   pl.BlockSpec((1, H_B, 128, 128), lambda b, i: (b, 0, 0, 0)),
                   pl.BlockSpec((1, H_C, 128, 128), lambda b, i: (b, 0, 0, 0)),
                   slab_out(d, dff), slab_out(d, dff), slab_out(dff, d), slab_out(d, d)],
        out_shape=[jax.ShapeDtypeStruct((rows_total, 2048), bf16),
                   jax.ShapeDtypeStruct((batch, BLK, 256), f32),
                   jax.ShapeDtypeStruct((batch, BLK, 256), f32),
                   jax.ShapeDtypeStruct((batch, H_B, 128, 128), f32),
                   jax.ShapeDtypeStruct((batch, H_C, 128, 128), f32),
                   jax.ShapeDtypeStruct((d, dff), bf16), jax.ShapeDtypeStruct((d, dff), bf16),
                   jax.ShapeDtypeStruct((dff, d), bf16), jax.ShapeDtypeStruct((d, d), bf16)],
        scratch_shapes=[pltpu.VMEM((2 * BLK, N_KV_A * 128), bf16), pltpu.VMEM((2 * BLK, N_KV_A * 256), bf16),
                        pltpu.VMEM((H_B, 128, 128), f32), pltpu.VMEM((H_C, 128, 128), f32),
                        pltpu.VMEM((N_HEADS_A * BLK, 128), bf16),
                        pltpu.VMEM((BLK, 512), f32), pltpu.VMEM((BLK, 512), f32), pltpu.VMEM((BLK, 512), f32)],
        compiler_params=pltpu.CompilerParams(dimension_semantics=("arbitrary", "arbitrary"),
                                             vmem_limit_bytes=VMEM_LIMIT),
        name="mixer_prompt",
    )(sinks, gpow, proj, ropea, roper, qn, kn, lb, hn, rn, sel, expand, tri, dsel, dec, rsc, tail,
      w_gate, w_up, w_down, w_out)


def _mixer_sample_kernel(sink_ref, gpow_ref, mixin_ref, *refs, dec_seq, steps, n_prev):
    del mixin_ref
    refs = refs[:17] + refs[17 + n_prev:]
    c = pl.program_id(0)
    mix_ref = refs[17]
    pl.when(c < steps)(lambda: _mixer_sample_body(sink_ref, gpow_ref, *refs, dec_seq=dec_seq))

    @pl.when(c >= steps)
    def _():
        mix_ref[...] = jnp.zeros_like(mix_ref)


def _mixer_sample_body(sink_ref, gpow_ref, proj_ref, ropea_ref, roper_ref, qn_ref, kn_ref, lb_ref,
                       hn_ref, rn_ref, sel_ref, exp_ref, dec_ref, rsc_ref, tail_ref,
                       ck_ref, cv_ref, sh_ref, sr_ref,
                       mix_ref, nk_ref, nv_ref, nsh_ref, nsr_ref, *, dec_seq):
    nbat = SROWS // dec_seq
    w = ck_ref.shape[1]
    rb = lax.broadcasted_iota(jnp.int32, (SROWS, 1), 0) // dec_seq
    rt = lax.broadcasted_iota(jnp.int32, (SROWS, 1), 0) % dec_seq

    def pick(parts):
        out = parts[nbat - 1]
        for b in range(nbat - 2, -1, -1):
            out = jnp.where(rb == b, parts[b], out)
        return out

    fgate, logf = _forget(proj_ref[:, F_B:F_B + 512], lb_ref[...])
    kb = 1.0 - fgate
    qb = _silu(proj_ref[:, Q_B:Q_B + 512])
    vb = proj_ref[:, I_B:I_B + 512]
    gcum = logf
    for d in range(1, dec_seq):
        gcum = gcum + jnp.where(rt >= d, pltpu.roll(logf, d, 0), 0.0)
    intra = [jnp.zeros((SROWS, 128), f32) for _ in range(H_B)]
    for s in range(SROWS):
        ok = (rb == s // dec_seq) & (rt >= s % dec_seq)
        y = qb * kb[s:s + 1, :] * jnp.exp2(jnp.where(ok, gcum - gcum[s:s + 1, :], NEG_INF))
        for h in range(H_B):
            sl = slice(128 * h, 128 * (h + 1))
            intra[h] = intra[h] + jnp.sum(y[:, sl], axis=-1, keepdims=True) * vb[s:s + 1, sl]

    qa = _rope_a(_head_norm(proj_ref[:, Q_A:Q_A + 1024], sel_ref, exp_ref, qn_ref[...]), ropea_ref)
    qs = (qa * Q_SCALE).astype(bf16)
    k_new = _rope_a(_head_norm(proj_ref[:, K_A:K_A + 256], sel_ref, exp_ref, kn_ref[...]), ropea_ref)
    v_new = proj_ref[:, V_A:V_A + 256]
    for b in range(nbat):
        nk_ref[b, 0:w - dec_seq, :] = ck_ref[b, dec_seq:w, :]
        nk_ref[b, w - dec_seq:w, :] = k_new[dec_seq * b:dec_seq * (b + 1), :]
        nv_ref[b, 0:w - dec_seq, :] = cv_ref[b, dec_seq:w, :]
        nv_ref[b, w - dec_seq:w, :] = v_new[dec_seq * b:dec_seq * (b + 1), :]
    knb = k_new.astype(bf16)
    vnb = v_new.astype(bf16)

    grows = GROUP_A * SROWS
    rt_g = jnp.concatenate([rt] * GROUP_A, axis=0)
    rb_g = jnp.concatenate([rb] * GROUP_A, axis=0)
    jc = lax.broadcasted_iota(jnp.int32, (grows, w), 1)
    valid_c = jc > rt_g + (w - WINDOW)
    cn = lax.broadcasted_iota(jnp.int32, (grows, SROWS), 1)
    valid_n = (cn // dec_seq == rb_g) & (cn % dec_seq <= rt_g)
    heads = [None] * N_HEADS_A
    for g in range(N_KV_A):
        ksl = slice(64 * g, 64 * (g + 1))
        qg = jnp.concatenate([qs[:, 64 * h:64 * (h + 1)] for h in range(GROUP_A * g, GROUP_A * (g + 1))], axis=0)
        sparts = [_dot(qg, ck_ref[b, :, ksl].astype(bf16), NT) for b in range(nbat)]
        s_c = sparts[nbat - 1]
        for b in range(nbat - 2, -1, -1):
            s_c = jnp.where(rb_g == b, sparts[b], s_c)
        s_c = jnp.where(valid_c, s_c, NEG_INF)
        s_n = jnp.where(valid_n, _dot(qg, knb[:, ksl], NT), NEG_INF)
        sk = jnp.concatenate([jnp.full((SROWS, 1), sink_ref[h] * LOG2E, f32)
                              for h in range(GROUP_A * g, GROUP_A * (g + 1))], axis=0)
        m = jnp.maximum(jnp.maximum(jnp.max(s_c, axis=-1, keepdims=True), jnp.max(s_n, axis=-1, keepdims=True)), sk)
        p_c = jnp.exp2(s_c - m)
        p_n = jnp.exp2(s_n - m)
        den = jnp.sum(p_c, axis=-1, keepdims=True) + jnp.sum(p_n, axis=-1, keepdims=True) + jnp.exp2(sk - m)
        pcb = p_c.astype(bf16)
        oparts = [_dot(pcb, cv_ref[b, :, ksl].astype(bf16)) for b in range(nbat)]
        o = oparts[nbat - 1]
        for b in range(nbat - 2, -1, -1):
            o = jnp.where(rb_g == b, oparts[b], o)
        o = (o + _dot(p_n.astype(bf16), vnb[:, ksl])) / den
        for hh in range(GROUP_A):
            heads[GROUP_A * g + hh] = o[SROWS * hh:SROWS * (hh + 1), :]
    mix_ref[:, 0:W_A] = jnp.concatenate(heads, axis=1).astype(bf16)

    qg = (qb * jnp.exp2(gcum)).astype(bf16)
    vbb = vb.astype(bf16)
    glast = [gcum[dec_seq * (b + 1) - 1:dec_seq * (b + 1), :] for b in range(nbat)]
    for h in range(H_B):
        sl = slice(128 * h, 128 * (h + 1))
        ob = pick([_dot(qg[:, sl], sh_ref[b, h].astype(bf16)) for b in range(nbat)]) + intra[h]
        for b in range(nbat):
            kt2 = jnp.where(rb == b, kb[:, sl] * jnp.exp2(glast[b][:, sl] - gcum[:, sl]), 0.0).astype(bf16)
            dcol = jnp.broadcast_to(jnp.exp2(glast[b][:, sl]), (128, 128)).T
            nsh_ref[b, h] = dcol * sh_ref[b, h] + _dot(kt2, vbb[:, sl], TN)
        yb = _rms(ob, hn_ref[...]) * _silu(proj_ref[:, G_B + 128 * h:G_B + 128 * (h + 1)])
        mix_ref[:, W_A + 128 * h:W_A + 128 * (h + 1)] = yb.astype(bf16)

    for h in range(H_C):
        qc = _rope_r(proj_ref[:, Q_C + 128 * h:Q_C + 128 * (h + 1)], roper_ref)
        kc = _rope_r(proj_ref[:, K_C + 128 * h:K_C + 128 * (h + 1)], roper_ref) * (DK_C ** -0.5)
        vc = proj_ref[:, V_C + 128 * h:V_C + 128 * (h + 1)]
        qcb = qc.astype(bf16)
        vcb = vc.astype(bf16)
        sc = _dot(qcb, kc.astype(bf16), NT) * dec_ref[h]
        oc = pick([_dot(qcb, sr_ref[b, h].astype(bf16)) for b in range(nbat)]) * rsc_ref[h] \
            + _dot(sc.astype(bf16), vcb)
        kct = kc * tail_ref[h]
        for b in range(nbat):
            nsr_ref[b, h] = gpow_ref[h] * sr_ref[b, h] + _dot(jnp.where(rb == b, kct, 0.0).astype(bf16), vcb, TN)
        yc = _rms(oc, rn_ref[...]) * _silu(proj_ref[:, G_C + 128 * h:G_C + 128 * (h + 1)])
        mix_ref[:, W_A + 512 + 128 * h:W_A + 512 + 128 * (h + 1)] = yc.astype(bf16)


def _mixer_sample(mix, proj, row0, layer, dec_batch, dec_seq, sinks, gpow, ropea, roper, qn, kn, lb, hn, rn, sel, expand,
                  dec, rsc, tail, cache_k, cache_v, state_h, state_r, prev_outs):
    smem = pl.BlockSpec(memory_space=pltpu.SMEM)
    nbat = SROWS // dec_seq
    steps = dec_batch // nbat
    blk0 = row0 // SROWS
    w = cache_k.shape[1]
    n_all = cache_k.shape[0]
    fill_steps = (mix.shape[0] - row0) // SROWS - steps
    cl = lambda c: jnp.minimum(c, steps - 1)
    any_spec = pl.BlockSpec(memory_space=pl.ANY)
    n_in = 20
    return pl.pallas_call(
        functools.partial(_mixer_sample_kernel, dec_seq=dec_seq, steps=steps, n_prev=len(prev_outs)),
        grid=(steps + fill_steps,),
        in_specs=[smem, smem,
                  pl.BlockSpec(memory_space=pl.ANY),
                  pl.BlockSpec((SROWS, D_IN), lambda c: (blk0 + cl(c), 0)),
                  _const_spec((2, SROWS, 128)), _const_spec((2, SROWS, 128)),
                  _const_spec((1, 1024)), _const_spec((1, 256)), _const_spec((1, 512)),
                  _const_spec((1, 128)), _const_spec((1, 128)),
                  _const_spec((1024, 128)), _const_spec((128, 1024)),
                  _const_spec((H_C, SROWS, SROWS)), _const_spec((H_C, SROWS, 128)), _const_spec((H_C, SROWS, 128)),
                  pl.BlockSpec((nbat, w, 256), lambda c: (layer * steps + cl(c), 0, 0)),
                  pl.BlockSpec((nbat, w, 256), lambda c: (layer * steps + cl(c), 0, 0)),
                  pl.BlockSpec((nbat, H_B, 128, 128), lambda c: (layer * steps + cl(c), 0, 0, 0)),
                  pl.BlockSpec((nbat, H_C, 128, 128), lambda c: (layer * steps + cl(c), 0, 0, 0))]
        + [any_spec] * len(prev_outs),
        out_specs=[pl.BlockSpec((SROWS, 2048), lambda c: (blk0 + c, 0)),
                   pl.BlockSpec((nbat, w, 256), lambda c: (layer * steps + cl(c), 0, 0)),
                   pl.BlockSpec((nbat, w, 256), lambda c: (layer * steps + cl(c), 0, 0)),
                   pl.BlockSpec((nbat, H_B, 128, 128), lambda c: (layer * steps + cl(c), 0, 0, 0)),
                   pl.BlockSpec((nbat, H_C, 128, 128), lambda c: (layer * steps + cl(c), 0, 0, 0))],
        out_shape=[jax.ShapeDtypeStruct(mix.shape, bf16),
                   jax.ShapeDtypeStruct((n_all, w, 256), f32),
                   jax.ShapeDtypeStruct((n_all, w, 256), f32),
                   jax.ShapeDtypeStruct((n_all, H_B, 128, 128), f32),
                   jax.ShapeDtypeStruct((n_all, H_C, 128, 128), f32)],
        input_output_aliases={2: 0, **{n_in + k: 1 + k for k in range(len(prev_outs))}},
        compiler_params=pltpu.CompilerParams(dimension_semantics=("arbitrary",), vmem_limit_bytes=VMEM_LIMIT),
        name="mixer_sample",
    )(sinks, gpow, mix, proj, ropea, roper, qn, kn, lb, hn, rn, sel, expand, dec, rsc, tail,
      cache_k, cache_v, state_h, state_r, *prev_outs)


def _rope_tables(pos):
    posf = jnp.asarray(pos).astype(f32)[:, None]
    t = pos.shape[0]
    half = N_ROT // 2
    inv = ROPE_THETA ** (-jnp.arange(half, dtype=f32) * (2.0 / N_ROT))
    ang = posf * inv[None, :]
    cos, sin = jnp.cos(ang), jnp.sin(ang)
    rest0 = jnp.zeros((t, HEAD_DIM_A - N_ROT), f32)
    c64 = jnp.concatenate([cos, cos, rest0 + 1.0], axis=1)
    s64 = jnp.concatenate([sin, sin, rest0], axis=1)
    ropea = jnp.stack([jnp.tile(c64, (1, 2)), jnp.tile(s64, (1, 2))])
    invr = RET_THETA ** (-jnp.arange(DK_C // 2, dtype=f32) * (2.0 / DK_C))
    angr = posf * invr[None, :]
    cr, sr = jnp.cos(angr), jnp.sin(angr)
    roper = jnp.stack([jnp.concatenate([cr, cr], axis=1), jnp.concatenate([-sr, sr], axis=1)])
    return ropea, roper


def _ret_tables(lg, seq_of_row, tok_of_row, length):
    nf = np.float32
    tq = tok_of_row.astype(nf)
    rel = tq[:, None] - tq[None, :]
    ok = (seq_of_row[:, None] == seq_of_row[None, :]) & (rel >= 0)
    dec = np.where(ok[None], np.exp(np.where(ok, rel, nf(0.0))[None] * lg[:, None, None]), nf(0.0))
    n = tq.shape[0]
    rsc = np.broadcast_to(np.exp((tq + nf(1.0))[None, :, None] * lg[:, None, None]), (H_C, n, 128))
    tail = np.broadcast_to(np.exp((nf(length) - nf(1.0) - tq)[None, :, None] * lg[:, None, None]), (H_C, n, 128))
    gpow = np.exp(nf(length) * lg)
    return dec.astype(nf), np.ascontiguousarray(rsc, nf), np.ascontiguousarray(tail, nf), gpow.astype(nf)


def kernel(x_prompt, x_sample, cache_k, cache_v, state_hgrn, state_ret, meta_tokens, norm_mix, norm_ffn, w_in, q_norm, k_norm, attn_sinks, hgrn_lb, hgrn_norm, ret_norm, w_out, w_gate, w_up, w_down):
    batch, seq, d = x_prompt.shape
    dec_batch, dec_seq, _ = x_sample.shape
    depth = w_in.shape[0]
    w = cache_k.shape[2]
    assert d == 2048 and w_in.shape[2] == D_IN and seq % BLK == 0
    assert SROWS % dec_seq == 0 and dec_batch % (SROWS // dec_seq) == 0 and w == WINDOW
    nb = seq // BLK + 1
    lp = nb * BLK
    rows_main = batch * seq
    rows_s = dec_batch * dec_seq
    row_s0 = rows_main + BLK
    tail_rows = BLK + rows_s
    rows = rows_main + tail_rows
    tm_ffn = _row_tile(rows, TM_FFN, 16)
    tm_ffn_last = _row_tile(rows, TM_FFN_LAST, 16)
    assert rows_main % tail_rows == 0 and rows_s % SROWS == 0 and tail_rows % 16 == 0
    assert min(tm_ffn, tm_ffn_last) >= tail_rows

    tail = jnp.concatenate([jnp.zeros((PAD, d), f32), meta_tokens.astype(f32), x_sample.reshape(rows_s, d)], axis=0)
    x, h = _embed(x_prompt.reshape(rows_main, d), tail, norm_mix[0][None])

    p = jax.nn.softmax(hgrn_lb.astype(f32), axis=0)
    lbs = jnp.cumsum(p, axis=0) - p[0]
    lg = np.log1p(-np.exp2(np.float32(-5.0) - np.arange(H_C, dtype=np.float32))).astype(np.float32)
    ropea_p, roper_p = _rope_tables(np.arange(lp) - PAD)
    srow = np.arange(SROWS)
    ropea_s, roper_s = _rope_tables(PAST_LEN + srow % dec_seq)
    dec_p, rsc_p, tail_p, gpow_p = _ret_tables(lg, np.zeros((BLK,), np.int32), np.arange(BLK), float(BLK))
    dec_s, rsc_s, tail_s, gpow_s = _ret_tables(lg, srow // dec_seq, srow % dec_seq, float(dec_seq))
    sel_np = np.arange(1024)[:, None] // HEAD_DIM_A == np.arange(128)[None, :]
    sel = jnp.asarray(sel_np, bf16)
    expand = jnp.asarray(sel_np.T, bf16)
    tri = jnp.asarray(np.arange(BLK)[:, None] >= np.arange(BLK)[None, :], bf16)
    dsel = jnp.asarray(np.arange(SUB * 128)[:, None] // 128 == np.arange(128)[None, :] % SUB, bf16)

    ck_flat = cache_k.reshape(depth * dec_batch, w, N_KV_A * HEAD_DIM_A)
    cv_flat = cache_v.reshape(depth * dec_batch, w, N_KV_A * HEAD_DIM_A)
    sh_flat = state_hgrn.reshape(depth * dec_batch, H_B, 128, 128)
    sr_flat = state_ret.reshape(depth * dec_batch, H_C, 128, 128)

    outs_p, outs_s = [], ()
    for l in range(depth):
        proj = _inproj(h, w_in, l)
        qn = jnp.tile(q_norm[l], N_HEADS_A)[None]
        kn = jnp.tile(k_norm[l], N_KV_A)[None]
        common = (qn, kn, lbs[l][None], hgrn_norm[l][None], ret_norm[l][None], sel, expand)
        mix, ck, cv, sh, sr, wg_b, wu_b, wd_b, wo_b = _mixer_prompt(
            proj, rows, batch, nb, l, attn_sinks[l], gpow_p, ropea_p, roper_p, *common, tri, dsel,
            dec_p, rsc_p, tail_p, w_gate, w_up, w_down, w_out)
        mix, *outs_s = _mixer_sample(mix, proj, row_s0, l, dec_batch, dec_seq, attn_sinks[l], gpow_s,
                                     ropea_s, roper_s, *common, dec_s, rsc_s, tail_s,
                                     ck_flat, cv_flat, sh_flat, sr_flat, tuple(outs_s))
        ffn_w = (wo_b, norm_ffn[l][None], wg_b, wu_b, wd_b)
        outs_p.append((ck, cv, sh, sr))
        if l + 1 < depth:
            x, h = _ffn(x, mix, *ffn_w, norm_mix[l + 1][None], tm_ffn, tail_rows)
        else:
            y_main, y_tail = _ffn(x, mix, *ffn_w, None, tm_ffn_last, tail_rows)

    y_prompt = y_main.reshape(batch, seq, d)
    y_sample = y_tail[BLK:].reshape(dec_batch, dec_seq, d)
    kv_shape = (depth, -1, w, N_KV_A, HEAD_DIM_A)
    st_shape = (depth, dec_batch, H_B, 128, 128)
    stack = lambda outs, k: jnp.stack([o[k] for o in outs])
    nk, nv, nsh, nsr = outs_s
    return (y_prompt, y_sample,
            stack(outs_p, 0).reshape(kv_shape), stack(outs_p, 1).reshape(kv_shape), stack(outs_p, 2), stack(outs_p, 3),
            nk.reshape(kv_shape), nv.reshape(kv_shape), nsh.reshape(st_shape), nsr.reshape(st_shape))
```

```python
import functools

import jax
import jax.numpy as jnp
import numpy as np
from jax import lax
from jax.experimental import pallas as pl
from jax.experimental.pallas import tpu as pltpu

f32 = jnp.float32
bf16 = jnp.bfloat16

N_META = 16
EPS = 1e-6
NEG_INF = -1e30
LB_FLOOR = 1e-30
WINDOW = 128
HEAD_DIM_A = 64
N_HEADS_A = 16
N_KV_A = 4
GROUP_A = N_HEADS_A // N_KV_A
N_ROT = 16
ROPE_THETA = 500000.0
H_B = 4
DK_B = 128
H_C = 4
DK_C = 128
RET_THETA = 10000.0
PAST_LEN = 16384

BLK = 128
PAD = BLK - N_META
SUB = 8
SROWS = 16
TM_FFN = 704
VMEM_LIMIT = 56 * 1024 * 1024
VMEM_LIMIT_FFN = 60 * 1024 * 1024

Q_A, K_A, V_A = 0, 1024, 1280
Q_B, F_B, I_B, G_B = 1536, 2048, 2560, 3072
Q_C, K_C, V_C, G_C = 3584, 4096, 4608, 5120
D_IN = 5632
W_A = 1024
LOG2E = 1.4426950408889634
Q_SCALE = HEAD_DIM_A ** -0.5 * LOG2E
NT = (((1,), (1,)), ((), ()))
TN = (((0,), (0,)), ((), ()))


def _dot(a, b, dims=None):
    if dims is None:
        return jnp.dot(a, b, preferred_element_type=f32)
    return lax.dot_general(a, b, dims, preferred_element_type=f32)


def _row_tile(rows, cap, align=8):
    best = align
    for t in range(align, cap + 1, align):
        if rows % t == 0:
            best = t
    return best


def _silu(x):
    return x * jax.nn.sigmoid(x)


def _rms(x, g):
    return x * lax.rsqrt(jnp.mean(x * x, axis=-1, keepdims=True) + EPS) * g


def _inproj_kernel(h_ref, w_ref, o_ref):
    o_ref[...] = _dot(h_ref[...], w_ref[...].astype(bf16))


def _inproj(h, w_all, layer):
    rows, d = h.shape
    n = w_all.shape[2]
    tm = _row_tile(rows, 2176, 16)
    tn = 512
    return pl.pallas_call(
        _inproj_kernel,
        grid=(rows // tm, n // tn),
        in_specs=[pl.BlockSpec((tm, d), lambda i, j: (i, 0)),
                  pl.BlockSpec((None, d, tn), lambda i, j: (layer, 0, j))],
        out_specs=pl.BlockSpec((tm, tn), lambda i, j: (i, j)),
        out_shape=jax.ShapeDtypeStruct((rows, n), f32),
        compiler_params=pltpu.CompilerParams(dimension_semantics=("arbitrary", "arbitrary"),
                                             vmem_limit_bytes=VMEM_LIMIT),
        name="inproj",
    )(h, w_all)


def _embed_kernel(xp_ref, tail_ref, g_ref, x_ref, h_ref):
    i = pl.program_id(0)
    last = pl.num_programs(0) - 1

    def emit(src_ref):
        x = src_ref[...]
        x_ref[...] = x
        h_ref[...] = _rms(x, g_ref[...]).astype(bf16)

    pl.when(i < last)(lambda: emit(xp_ref))
    pl.when(i == last)(lambda: emit(tail_ref))


def _embed(xp, tail, g):
    rows_main, d = xp.shape
    tm = tail.shape[0]
    n_main = rows_main // tm
    rows = rows_main + tm
    return pl.pallas_call(
        _embed_kernel,
        grid=(n_main + 1,),
        in_specs=[pl.BlockSpec((tm, d), lambda i: (jnp.minimum(i, n_main - 1), 0)),
                  pl.BlockSpec((tm, d), lambda i: (0, 0)),
                  pl.BlockSpec((1, d), lambda i: (0, 0))],
        out_specs=[pl.BlockSpec((tm, d), lambda i: (i, 0)),
                   pl.BlockSpec((tm, d), lambda i: (i, 0))],
        out_shape=[jax.ShapeDtypeStruct((rows, d), f32), jax.ShapeDtypeStruct((rows, d), bf16)],
        compiler_params=pltpu.CompilerParams(dimension_semantics=("arbitrary",), vmem_limit_bytes=VMEM_LIMIT),
        name="embed_norm",
    )(xp, tail, g)


def _ffn_kernel(*refs, with_next, tail_rows):
    if with_next:
        x_ref, mix_ref, wo_ref, nf_ref, wg_ref, wu_ref, wd_ref, gn_ref, o_ref, hn_ref, h_ref = refs
    else:
        x_ref, mix_ref, wo_ref, nf_ref, wg_ref, wu_ref, wd_ref, o_ref, tail_ref, h_ref = refs
    j = pl.program_id(1)

    @pl.when(j == 0)
    def _():
        x1 = x_ref[...] + _dot(mix_ref[...], wo_ref[...])
        h_ref[...] = _rms(x1, nf_ref[...]).astype(bf16)
        o_ref[...] = x1

    h = h_ref[...]
    a = _silu(_dot(h, wg_ref[...])) * _dot(h, wu_ref[...])
    o_ref[...] += _dot(a.astype(bf16), wd_ref[...])

    if with_next:
        @pl.when(j == pl.num_programs(1) - 1)
        def _():
            hn_ref[...] = _rms(o_ref[...], gn_ref[...]).astype(bf16)
    else:
        @pl.when((j == pl.num_programs(1) - 1) & (pl.program_id(0) == pl.num_programs(0) - 1))
        def _():
            tail_ref[...] = o_ref[o_ref.shape[0] - tail_rows:, :]


def _ffn(x, mix, wo, nf, wg, wu, wd, g_next, tm, tail_rows):
    rows, d = x.shape
    dff = wg.shape[1]
    tf = 512
    n_tiles = rows // tm
    with_next = g_next is not None
    row_spec = pl.BlockSpec((tm, d), lambda i, j: (i, 0))
    out_spec = row_spec
    mix_spec = pl.BlockSpec((tm, d), lambda i, j: (i, 0), pipeline_mode=pl.Buffered(1)) if with_next else row_spec
    in_specs = [row_spec, mix_spec,
                pl.BlockSpec((d, d), lambda i, j: (0, 0), pipeline_mode=pl.Buffered(1)),
                pl.BlockSpec((1, d), lambda i, j: (0, 0)),
                pl.BlockSpec((d, tf), lambda i, j: (0, j)),
                pl.BlockSpec((d, tf), lambda i, j: (0, j)),
                pl.BlockSpec((tf, d), lambda i, j: (j, 0))]
    args = [x, mix, wo, nf, wg, wu, wd]
    if with_next:
        in_specs.append(pl.BlockSpec((1, d), lambda i, j: (0, 0)))
        args.append(g_next)
        out_specs = [out_spec, out_spec]
        out_shape = [jax.ShapeDtypeStruct((rows, d), f32), jax.ShapeDtypeStruct((rows, d), bf16)]
    else:
        assert tail_rows % 8 == 0 and tail_rows <= tm
        out_specs = [out_spec, pl.BlockSpec((tail_rows, d), lambda i, j: (0, 0))]
        out_shape = [jax.ShapeDtypeStruct((rows - tail_rows, d), f32), jax.ShapeDtypeStruct((tail_rows, d), f32)]
    return pl.pallas_call(
        functools.partial(_ffn_kernel, with_next=with_next, tail_rows=tail_rows),
        grid=(n_tiles, dff // tf),
        in_specs=in_specs,
        out_specs=out_specs,
        out_shape=out_shape,
        scratch_shapes=[pltpu.VMEM((tm, d), bf16)],
        compiler_params=pltpu.CompilerParams(dimension_semantics=("arbitrary", "arbitrary"),
                                             vmem_limit_bytes=VMEM_LIMIT_FFN),
        name="outproj_ffn",
    )(*args)


def _head_norm(x, sel_ref, exp_ref, g):
    n, w = x.shape
    x2 = x * x
    hi = x2.astype(bf16)
    lo = (x2 - hi.astype(f32)).astype(bf16)
    s = _dot(jnp.concatenate([hi, lo], axis=0), sel_ref[0:w, :])
    return x * _dot_split3(lax.rsqrt((s[0:n] + s[n:2 * n]) * (1.0 / HEAD_DIM_A) + EPS), exp_ref[:, 0:w]) * g


def _dot_split3(x, m):
    n = x.shape[0]
    y = _dot(jnp.concatenate(_split3(x), axis=0), m)
    return y[0:n] + y[n:2 * n] + y[2 * n:3 * n]


def _rope_a(x, tab_ref):
    half = N_ROT // 2
    first = lax.broadcasted_iota(jnp.int32, (x.shape[0], 128), 1) % HEAD_DIM_A < half
    out = []
    for t in range(x.shape[1] // 128):
        xt = x[:, 128 * t:128 * (t + 1)]
        partner = jnp.where(first, -pltpu.roll(xt, 128 - half, 1), pltpu.roll(xt, half, 1))
        out.append(xt * tab_ref[0] + partner * tab_ref[1])
    return jnp.concatenate(out, axis=1)


def _rope_r(x, tab_ref):
    return x * tab_ref[0] + pltpu.roll(x, DK_C // 2, 1) * tab_ref[1]


def _forget(fb, lb):
    f = jnp.maximum(lb, LB_FLOOR) + (1.0 - lb) * jax.nn.sigmoid(fb)
    return f, jnp.log2(f)


def _split3(x):
    h1 = x.astype(bf16)
    r1 = x - h1.astype(f32)
    h2 = r1.astype(bf16)
    h3 = (r1 - h2.astype(f32)).astype(bf16)
    return h1, h2, h3


def _mixer_prompt_kernel(sink_ref, gpow_ref, proj_ref, ropea_ref, roper_ref, qn_ref, kn_ref, lb_ref, hn_ref, rn_ref,
                         sel_ref, exp_ref, tri_ref, dsel_ref, dec_ref, rsc_ref, tail_ref,
                         wg_ref, wu_ref, wd_ref, wo_ref,
                         mix_ref, ck_ref, cv_ref, sh_ref, sr_ref, wgb_ref, wub_ref, wdb_ref, wob_ref,
                         kk, vv, sht, srs, qs_s, qb_s, kb_s, g_s):
    i = pl.program_id(1)
    nb = pl.num_programs(1)

    @pl.when(i == 0)
    def _():
        kk[...] = jnp.zeros_like(kk)
        vv[...] = jnp.zeros_like(vv)
        for g in range(N_KV_A):
            vv[:, 256 * g + 128:256 * (g + 1)] = jnp.ones((2 * BLK, 128), bf16)
        sht[...] = jnp.zeros_like(sht)
        srs[...] = jnp.zeros_like(srs)

    wgb_ref[...] = wg_ref[...].astype(bf16)
    wub_ref[...] = wu_ref[...].astype(bf16)
    wdb_ref[...] = wd_ref[...].astype(bf16)
    wob_ref[...] = wo_ref[...].astype(bf16)

    rowabs = i * BLK + lax.broadcasted_iota(jnp.int32, (BLK, 1), 0)
    fgate, logf = _forget(proj_ref[:, F_B:F_B + 512], lb_ref[...])
    logf = jnp.where(rowabs >= PAD, logf, 0.0)
    kb_s[...] = jnp.where(rowabs >= PAD, 1.0 - fgate, 0.0)
    qb_s[...] = _silu(proj_ref[:, Q_B:Q_B + 512])
    l1, l2, l3 = _split3(logf)
    tri = tri_ref[...]
    g_s[...] = _dot(tri, l1) + _dot(tri, l2) + _dot(tri, l3)

    r2 = lax.broadcasted_iota(jnp.int32, (BLK, BLK), 0)
    c2 = lax.broadcasted_iota(jnp.int32, (BLK, BLK), 1)
    lane_lo = c2 < HEAD_DIM_A

    def attention_prep():
        qa = _rope_a(_head_norm(proj_ref[:, Q_A:Q_A + 1024], sel_ref, exp_ref, qn_ref[...]), ropea_ref)
        for t in range(N_HEADS_A // 2):
            xt = qa[:, 128 * t:128 * (t + 1)] * Q_SCALE
            qs_s[256 * t:256 * t + 128, :] = jnp.where(lane_lo, xt, 0.0).astype(bf16)
            qs_s[256 * t + 128:256 * (t + 1), :] = jnp.where(lane_lo, 0.0, xt).astype(bf16)
        k_cur = _rope_a(_head_norm(proj_ref[:, K_A:K_A + 256], sel_ref, exp_ref, kn_ref[...]), ropea_ref)
        v_cur = proj_ref[:, V_A:V_A + 256]
        ck_ref[0] = k_cur
        cv_ref[0] = v_cur
        for t in range(N_KV_A // 2):
            for src, dst, width in ((k_cur, kk, 128), (v_cur, vv, 256)):
                xt = src[:, 128 * t:128 * (t + 1)]
                xs = pltpu.roll(xt, HEAD_DIM_A, 1)
                dst[BLK:2 * BLK, width * 2 * t:width * 2 * t + 128] = jnp.where(lane_lo, xt, xs).astype(bf16)
                dst[BLK:2 * BLK, width * (2 * t + 1):width * (2 * t + 1) + 128] = jnp.where(lane_lo, xs, xt).astype(bf16)

    prow = 2 * BLK
    r4 = lax.broadcasted_iota(jnp.int32, (prow, BLK), 0) % BLK
    c4 = lax.broadcasted_iota(jnp.int32, (prow, BLK), 1)
    up = c4 > r4
    ok = c4 >= jnp.where(up, PAD - (i - 1) * BLK, PAD - i * BLK)

    def attention_pair(t):
        g = (2 * t) // GROUP_A
        s2 = _dot(qs_s[prow * t:prow * (t + 1), :], kk[:, 128 * g:128 * (g + 1)], NT)
        s = jnp.where(ok, jnp.where(up, s2[:, 0:BLK], s2[:, BLK:2 * BLK]), NEG_INF)
        sk = jnp.concatenate([jnp.full((BLK, BLK), sink_ref[2 * t + j] * LOG2E, f32) for j in range(2)], axis=0)
        m = jnp.maximum(jnp.max(s, axis=-1, keepdims=True), sk)
        p = jnp.exp2(s - m)
        p2 = jnp.concatenate([jnp.where(up, p, 0.0), jnp.where(up, 0.0, p)], axis=1).astype(bf16)
        res = _dot(p2, vv[:, 256 * g:256 * (g + 1)])
        y = res[:, 0:128] / (res[:, 128:256] + jnp.exp2(sk - m))
        mix_ref[:, 128 * t:128 * (t + 1)] = jnp.where(lane_lo, y[0:BLK], y[BLK:prow]).astype(bf16)

    levels = []
    bz = BLK
    while bz > SUB:
        levels.append((bz, (r2 % bz) >= bz // 2, None if bz == BLK else (r2 // bz) == (c2 // bz)))
        bz //= 2
    same_sub = (r2 // SUB) == (c2 // SUB)
    sub_ge = [(r2 % SUB) >= s for s in range(SUB)]

    def retention_head(h):
        qc = _rope_r(proj_ref[:, Q_C + 128 * h:Q_C + 128 * (h + 1)], roper_ref)
        kc = _rope_r(proj_ref[:, K_C + 128 * h:K_C + 128 * (h + 1)], roper_ref) * (DK_C ** -0.5)
        vc = proj_ref[:, V_C + 128 * h:V_C + 128 * (h + 1)].astype(bf16)
        qcb = qc.astype(bf16)
        sc = _dot(qcb, kc.astype(bf16), NT) * dec_ref[h]
        oc = _dot(qcb, srs[h].astype(bf16)) * rsc_ref[h] + _dot(sc.astype(bf16), vc)
        srs[h] = gpow_ref[h] * srs[h] + _dot((kc * tail_ref[h]).astype(bf16), vc, TN)
        yc = _rms(oc, rn_ref[...]) * _silu(proj_ref[:, G_C + 128 * h:G_C + 128 * (h + 1)])
        mix_ref[:, W_A + 512 + 128 * h:W_A + 512 + 128 * (h + 1)] = yc.astype(bf16)

    def hgrn_intra(h):
        sl = slice(128 * h, 128 * (h + 1))
        gc = g_s[:, sl]
        qb = qb_s[:, sl]
        kb = kb_s[:, sl]
        inter = _dot((qb * jnp.exp2(gc)).astype(bf16), sht[h].astype(bf16), NT)
        amat = None
        for bz, upper, same in levels:
            gref = jnp.concatenate([jnp.broadcast_to(g_s[pl.ds(st + bz // 2 - 1, 1), sl], (bz, 128))
                                    for st in range(0, BLK, bz)], axis=0)
            dg = gc - gref
            e = jnp.exp2(jnp.where(upper, dg, -dg))
            qt = jnp.where(upper, qb * e, 0.0).astype(bf16)
            kt = jnp.where(upper, 0.0, kb * e).astype(bf16)
            pm = _dot(qt, kt, NT)
            if same is not None:
                pm = jnp.where(same, pm, 0.0)
            amat = pm if amat is None else amat + pm
        ys = []
        for s in range(SUB):
            ks = jnp.concatenate([jnp.broadcast_to(kb_s[pl.ds(SUB * j + s, 1), sl], (SUB, 128))
                                  for j in range(BLK // SUB)], axis=0)
            gs = jnp.concatenate([jnp.broadcast_to(g_s[pl.ds(SUB * j + s, 1), sl], (SUB, 128))
                                  for j in range(BLK // SUB)], axis=0)
            ys.append((qb * ks * jnp.exp2(jnp.where(sub_ge[s], gc - gs, NEG_INF))).astype(bf16))
        amat = amat + jnp.where(same_sub, _dot(jnp.concatenate(ys, axis=1), dsel_ref[...]), 0.0)
        return inter, amat.astype(bf16)

    def hgrn_finish(h, inter, amat):
        sl = slice(128 * h, 128 * (h + 1))
        vb = proj_ref[:, I_B + 128 * h:I_B + 128 * (h + 1)].astype(bf16)
        ob = inter + _dot(amat, vb)
        glast = g_s[pl.ds(BLK - 1, 1), sl]
        kt2 = (kb_s[:, sl] * jnp.exp2(glast - g_s[:, sl])).astype(bf16)
        sht[h] = sht[h] * jnp.exp2(glast) + _dot(vb, kt2, TN)
        yb = _rms(ob, hn_ref[...]) * _silu(proj_ref[:, G_B + 128 * h:G_B + 128 * (h + 1)])
        mix_ref[:, W_A + 128 * h:W_A + 128 * (h + 1)] = yb.astype(bf16)

    attention_prep()
    pending = None
    for h in range(H_B):
        cur = hgrn_intra(h)
        if pending is not None:
            hgrn_finish(h - 1, *pending)
        retention_head(h)
        pending = cur
    hgrn_finish(H_B - 1, *pending)
    for t in range(N_HEADS_A // 2):
        attention_pair(t)
    kk[0:BLK, :] = kk[BLK:2 * BLK, :]
    vv[0:BLK, :] = vv[BLK:2 * BLK, :]

    @pl.when(i == nb - 1)
    def _():
        for h in range(H_B):
            sh_ref[0, h] = sht[h].T
            sr_ref[0, h] = srs[h]


def _const_spec(shape):
    nd = len(shape)
    return pl.BlockSpec(shape, lambda *_: (0,) * nd)


def _slab_rows(nrows, steps):
    for r in range(16, nrows + 1, 16):
        if nrows % r == 0 and nrows // r <= steps:
            return r
    return nrows


def _mixer_prompt(proj, rows_total, batch, nb, layer, sinks, gpow, ropea, roper, qn, kn, lb, hn, rn, sel, expand,
                  tri, dsel, dec, rsc, tail, w_gate, w_up, w_down, w_out):
    smem = pl.BlockSpec(memory_space=pltpu.SMEM)
    d, dff = w_gate.shape[1], w_gate.shape[2]

    def row_blk(b, i):
        return jnp.where(i == 0, batch * (nb - 1), b * (nb - 1) + i - 1)

    def mix_blk(b, i):
        return jnp.where((i == 0) & (b > 0), b * (nb - 1), row_blk(b, i))

    def slab_in(nrows, ncols):
        r = _slab_rows(nrows, batch * nb)
        return pl.BlockSpec((None, r, ncols), lambda b, i: (layer, jnp.minimum(b * nb + i, nrows // r - 1), 0))

    def slab_out(nrows, ncols):
        r = _slab_rows(nrows, batch * nb)
        return pl.BlockSpec((r, ncols), lambda b, i: (jnp.minimum(b * nb + i, nrows // r - 1), 0))

    return pl.pallas_call(
        _mixer_prompt_kernel,
        grid=(batch, nb),
        in_specs=[smem, smem,
                  pl.BlockSpec((BLK, D_IN), lambda b, i: (row_blk(b, i), 0)),
                  pl.BlockSpec((2, BLK, 128), lambda b, i: (0, i, 0)),
                  pl.BlockSpec((2, BLK, 128), lambda b, i: (0, i, 0)),
                  _const_spec((1, 1024)), _const_spec((1, 256)), _const_spec((1, 512)),
                  _const_spec((1, 128)), _const_spec((1, 128)),
                  _const_spec((1024, 128)), _const_spec((128, 1024)),
                  _const_spec((BLK, BLK)), _const_spec((SUB * 128, 128)),
                  _const_spec((H_C, BLK, BLK)), _const_spec((H_C, BLK, 128)), _const_spec((H_C, BLK, 128)),
                  slab_in(d, dff), slab_in(d, dff), slab_in(dff, d), slab_in(d, d)],
        out_specs=[pl.BlockSpec((BLK, 2048), lambda b, i: (mix_blk(b, i), 0)),
                   pl.BlockSpec((1, BLK, 256), lambda b, i: (b, 0, 0)),
                   pl.BlockSpec((1, BLK, 256), lambda b, i: (b, 0, 0)),
                   pl.BlockSpec((1, H_B, 128, 128), lambda b, i: (b, 0, 0, 0)),
                   pl.BlockSpec((1, H_C, 128, 128), lambda b, i: (b, 0, 0, 0)),
                   slab_out(d, dff), slab_out(d, dff), slab_out(dff, d), slab_out(d, d)],
        out_shape=[jax.ShapeDtypeStruct((rows_total, 2048), bf16),
                   jax.ShapeDtypeStruct((batch, BLK, 256), f32),
                   jax.ShapeDtypeStruct((batch, BLK, 256), f32),
                   jax.ShapeDtypeStruct((batch, H_B, 128, 128), f32),
                   jax.ShapeDtypeStruct((batch, H_C, 128, 128), f32),
                   jax.ShapeDtypeStruct((d, dff), bf16), jax.ShapeDtypeStruct((d, dff), bf16),
                   jax.ShapeDtypeStruct((dff, d), bf16), jax.ShapeDtypeStruct((d, d), bf16)],
        scratch_shapes=[pltpu.VMEM((2 * BLK, N_KV_A * 128), bf16), pltpu.VMEM((2 * BLK, N_KV_A * 256), bf16),
                        pltpu.VMEM((H_B, 128, 128), f32), pltpu.VMEM((H_C, 128, 128), f32),
                        pltpu.VMEM((N_HEADS_A * BLK, 128), bf16),
                        pltpu.VMEM((BLK, 512), f32), pltpu.VMEM((BLK, 512), f32), pltpu.VMEM((BLK, 512), f32)],
        compiler_params=pltpu.CompilerParams(dimension_semantics=("arbitrary", "arbitrary"),
                                             vmem_limit_bytes=VMEM_LIMIT),
        name="mixer_prompt",
    )(sinks, gpow, proj, ropea, roper, qn, kn, lb, hn, rn, sel, expand, tri, dsel, dec, rsc, tail,
      w_gate, w_up, w_down, w_out)


def _mixer_sample_kernel(sink_ref, gpow_ref, mixin_ref, *refs, dec_seq, n_prev):
    del mixin_ref
    _mixer_sample_body(sink_ref, gpow_ref, *refs[:17], *refs[17 + n_prev:], dec_seq=dec_seq)


def _mixer_sample_body(sink_ref, gpow_ref, proj_ref, ropea_ref, roper_ref, qn_ref, kn_ref, lb_ref,
                       hn_ref, rn_ref, sel_ref, exp_ref, dec_ref, rsc_ref, tail_ref,
                       ck_ref, cv_ref, sh_ref, sr_ref,
                       mix_ref, nk_ref, nv_ref, nsh_ref, nsr_ref, *, dec_seq):
    nbat = SROWS // dec_seq
    w = ck_ref.shape[1]
    rb = lax.broadcasted_iota(jnp.int32, (SROWS, 1), 0) // dec_seq
    rt = lax.broadcasted_iota(jnp.int32, (SROWS, 1), 0) % dec_seq

    def pick(parts):
        out = parts[nbat - 1]
        for b in range(nbat - 2, -1, -1):
            out = jnp.where(rb == b, parts[b], out)
        return out

    fgate, logf = _forget(proj_ref[:, F_B:F_B + 512], lb_ref[...])
    kb = 1.0 - fgate
    qb = _silu(proj_ref[:, Q_B:Q_B + 512])
    vb = proj_ref[:, I_B:I_B + 512]
    gcum = logf
    for d in range(1, dec_seq):
        gcum = gcum + jnp.where(rt >= d, pltpu.roll(logf, d, 0), 0.0)
    intra = [jnp.zeros((SROWS, 128), f32) for _ in range(H_B)]
    for s in range(SROWS):
        ok = (rb == s // dec_seq) & (rt >= s % dec_seq)
        y = qb * kb[s:s + 1, :] * jnp.exp2(jnp.where(ok, gcum - gcum[s:s + 1, :], NEG_INF))
        for h in range(H_B):
            sl = slice(128 * h, 128 * (h + 1))
            intra[h] = intra[h] + jnp.sum(y[:, sl], axis=-1, keepdims=True) * vb[s:s + 1, sl]

    qa = _rope_a(_head_norm(proj_ref[:, Q_A:Q_A + 1024], sel_ref, exp_ref, qn_ref[...]), ropea_ref)
    qs = (qa * Q_SCALE).astype(bf16)
    k_new = _rope_a(_head_norm(proj_ref[:, K_A:K_A + 256], sel_ref, exp_ref, kn_ref[...]), ropea_ref)
    v_new = proj_ref[:, V_A:V_A + 256]
    for b in range(nbat):
        nk_ref[b, 0:w - dec_seq, :] = ck_ref[b, dec_seq:w, :]
        nk_ref[b, w - dec_seq:w, :] = k_new[dec_seq * b:dec_seq * (b + 1), :]
        nv_ref[b, 0:w - dec_seq, :] = cv_ref[b, dec_seq:w, :]
        nv_ref[b, w - dec_seq:w, :] = v_new[dec_seq * b:dec_seq * (b + 1), :]
    knb = k_new.astype(bf16)
    vnb = v_new.astype(bf16)

    grows = GROUP_A * SROWS
    rt_g = jnp.concatenate([rt] * GROUP_A, axis=0)
    rb_g = jnp.concatenate([rb] * GROUP_A, axis=0)
    jc = lax.broadcasted_iota(jnp.int32, (grows, w), 1)
    valid_c = jc > rt_g + (w - WINDOW)
    cn = lax.broadcasted_iota(jnp.int32, (grows, SROWS), 1)
    valid_n = (cn // dec_seq == rb_g) & (cn % dec_seq <= rt_g)
    heads = [None] * N_HEADS_A
    for g in range(N_KV_A):
        ksl = slice(64 * g, 64 * (g + 1))
        qg = jnp.concatenate([qs[:, 64 * h:64 * (h + 1)] for h in range(GROUP_A * g, GROUP_A * (g + 1))], axis=0)
        sparts = [_dot(qg, ck_ref[b, :, ksl].astype(bf16), NT) for b in range(nbat)]
        s_c = sparts[nbat - 1]
        for b in range(nbat - 2, -1, -1):
            s_c = jnp.where(rb_g == b, sparts[b], s_c)
        s_c = jnp.where(valid_c, s_c, NEG_INF)
        s_n = jnp.where(valid_n, _dot(qg, knb[:, ksl], NT), NEG_INF)
        sk = jnp.concatenate([jnp.full((SROWS, 1), sink_ref[h] * LOG2E, f32)
                              for h in range(GROUP_A * g, GROUP_A * (g + 1))], axis=0)
        m = jnp.maximum(jnp.maximum(jnp.max(s_c, axis=-1, keepdims=True), jnp.max(s_n, axis=-1, keepdims=True)), sk)
        p_c = jnp.exp2(s_c - m)
        p_n = jnp.exp2(s_n - m)
        den = jnp.sum(p_c, axis=-1, keepdims=True) + jnp.sum(p_n, axis=-1, keepdims=True) + jnp.exp2(sk - m)
        pcb = p_c.astype(bf16)
        oparts = [_dot(pcb, cv_ref[b, :, ksl].astype(bf16)) for b in range(nbat)]
        o = oparts[nbat - 1]
        for b in range(nbat - 2, -1, -1):
            o = jnp.where(rb_g == b, oparts[b], o)
        o = (o + _dot(p_n.astype(bf16), vnb[:, ksl])) / den
        for hh in range(GROUP_A):
            heads[GROUP_A * g + hh] = o[SROWS * hh:SROWS * (hh + 1), :]
    mix_ref[:, 0:W_A] = jnp.concatenate(heads, axis=1).astype(bf16)

    qg = (qb * jnp.exp2(gcum)).astype(bf16)
    vbb = vb.astype(bf16)
    glast = [gcum[dec_seq * (b + 1) - 1:dec_seq * (b + 1), :] for b in range(nbat)]
    for h in range(H_B):
        sl = slice(128 * h, 128 * (h + 1))
        ob = pick([_dot(qg[:, sl], sh_ref[b, h].astype(bf16)) for b in range(nbat)]) + intra[h]
        for b in range(nbat):
            kt2 = jnp.where(rb == b, kb[:, sl] * jnp.exp2(glast[b][:, sl] - gcum[:, sl]), 0.0).astype(bf16)
            dcol = jnp.broadcast_to(jnp.exp2(glast[b][:, sl]), (128, 128)).T
            nsh_ref[b, h] = dcol * sh_ref[b, h] + _dot(kt2, vbb[:, sl], TN)
        yb = _rms(ob, hn_ref[...]) * _silu(proj_ref[:, G_B + 128 * h:G_B + 128 * (h + 1)])
        mix_ref[:, W_A + 128 * h:W_A + 128 * (h + 1)] = yb.astype(bf16)

    for h in range(H_C):
        qc = _rope_r(proj_ref[:, Q_C + 128 * h:Q_C + 128 * (h + 1)], roper_ref)
        kc = _rope_r(proj_ref[:, K_C + 128 * h:K_C + 128 * (h + 1)], roper_ref) * (DK_C ** -0.5)
        vc = proj_ref[:, V_C + 128 * h:V_C + 128 * (h + 1)]
        qcb = qc.astype(bf16)
        vcb = vc.astype(bf16)
        sc = _dot(qcb, kc.astype(bf16), NT) * dec_ref[h]
        oc = pick([_dot(qcb, sr_ref[b, h].astype(bf16)) for b in range(nbat)]) * rsc_ref[h] \
            + _dot(sc.astype(bf16), vcb)
        kct = kc * tail_ref[h]
        for b in range(nbat):
            nsr_ref[b, h] = gpow_ref[h] * sr_ref[b, h] + _dot(jnp.where(rb == b, kct, 0.0).astype(bf16), vcb, TN)
        yc = _rms(oc, rn_ref[...]) * _silu(proj_ref[:, G_C + 128 * h:G_C + 128 * (h + 1)])
        mix_ref[:, W_A + 512 + 128 * h:W_A + 512 + 128 * (h + 1)] = yc.astype(bf16)


def _mixer_sample(mix, proj, row0, layer, dec_batch, dec_seq, sinks, gpow, ropea, roper, qn, kn, lb, hn, rn, sel, expand,
                  dec, rsc, tail, cache_k, cache_v, state_h, state_r, prev_outs):
    smem = pl.BlockSpec(memory_space=pltpu.SMEM)
    nbat = SROWS // dec_seq
    steps = dec_batch // nbat
    blk0 = row0 // SROWS
    w = cache_k.shape[1]
    n_all = cache_k.shape[0]
    assert mix.shape[0] == row0 + steps * SROWS
    any_spec = pl.BlockSpec(memory_space=pl.ANY)
    n_in = 20
    return pl.pallas_call(
        functools.partial(_mixer_sample_kernel, dec_seq=dec_seq, n_prev=len(prev_outs)),
        grid=(steps,),
        in_specs=[smem, smem,
                  pl.BlockSpec(memory_space=pl.ANY),
                  pl.BlockSpec((SROWS, D_IN), lambda c: (blk0 + c, 0)),
                  _const_spec((2, SROWS, 128)), _const_spec((2, SROWS, 128)),
                  _const_spec((1, 1024)), _const_spec((1, 256)), _const_spec((1, 512)),
                  _const_spec((1, 128)), _const_spec((1, 128)),
                  _const_spec((1024, 128)), _const_spec((128, 1024)),
                  _const_spec((H_C, SROWS, SROWS)), _const_spec((H_C, SROWS, 128)), _const_spec((H_C, SROWS, 128)),
                  pl.BlockSpec((nbat, w, 256), lambda c: (layer * steps + c, 0, 0)),
                  pl.BlockSpec((nbat, w, 256), lambda c: (layer * steps + c, 0, 0)),
                  pl.BlockSpec((nbat, H_B, 128, 128), lambda c: (layer * steps + c, 0, 0, 0)),
                  pl.BlockSpec((nbat, H_C, 128, 128), lambda c: (layer * steps + c, 0, 0, 0))]
        + [any_spec] * len(prev_outs),
        out_specs=[pl.BlockSpec((SROWS, 2048), lambda c: (blk0 + c, 0)),
                   pl.BlockSpec((nbat, w, 256), lambda c: (layer * steps + c, 0, 0)),
                   pl.BlockSpec((nbat, w, 256), lambda c: (layer * steps + c, 0, 0)),
                   pl.BlockSpec((nbat, H_B, 128, 128), lambda c: (layer * steps + c, 0, 0, 0)),
                   pl.BlockSpec((nbat, H_C, 128, 128), lambda c: (layer * steps + c, 0, 0, 0))],
        out_shape=[jax.ShapeDtypeStruct(mix.shape, bf16),
                   jax.ShapeDtypeStruct((n_all, w, 256), f32),
                   jax.ShapeDtypeStruct((n_all, w, 256), f32),
                   jax.ShapeDtypeStruct((n_all, H_B, 128, 128), f32),
                   jax.ShapeDtypeStruct((n_all, H_C, 128, 128), f32)],
        input_output_aliases={2: 0, **{n_in + k: 1 + k for k in range(len(prev_outs))}},
        compiler_params=pltpu.CompilerParams(dimension_semantics=("arbitrary",), vmem_limit_bytes=VMEM_LIMIT),
        name="mixer_sample",
    )(sinks, gpow, mix, proj, ropea, roper, qn, kn, lb, hn, rn, sel, expand, dec, rsc, tail,
      cache_k, cache_v, state_h, state_r, *prev_outs)


def _rope_tables(pos):
    posf = jnp.asarray(pos).astype(f32)[:, None]
    t = pos.shape[0]
    half = N_ROT // 2
    inv = ROPE_THETA ** (-jnp.arange(half, dtype=f32) * (2.0 / N_ROT))
    ang = posf * inv[None, :]
    cos, sin = jnp.cos(ang), jnp.sin(ang)
    rest0 = jnp.zeros((t, HEAD_DIM_A - N_ROT), f32)
    c64 = jnp.concatenate([cos, cos, rest0 + 1.0], axis=1)
    s64 = jnp.concatenate([sin, sin, rest0], axis=1)
    ropea = jnp.stack([jnp.tile(c64, (1, 2)), jnp.tile(s64, (1, 2))])
    invr = RET_THETA ** (-jnp.arange(DK_C // 2, dtype=f32) * (2.0 / DK_C))
    angr = posf * invr[None, :]
    cr, sr = jnp.cos(angr), jnp.sin(angr)
    roper = jnp.stack([jnp.concatenate([cr, cr], axis=1), jnp.concatenate([-sr, sr], axis=1)])
    return ropea, roper


def _ret_tables(lg, seq_of_row, tok_of_row, length):
    nf = np.float32
    tq = tok_of_row.astype(nf)
    rel = tq[:, None] - tq[None, :]
    ok = (seq_of_row[:, None] == seq_of_row[None, :]) & (rel >= 0)
    dec = np.where(ok[None], np.exp(np.where(ok, rel, nf(0.0))[None] * lg[:, None, None]), nf(0.0))
    n = tq.shape[0]
    rsc = np.broadcast_to(np.exp((tq + nf(1.0))[None, :, None] * lg[:, None, None]), (H_C, n, 128))
    tail = np.broadcast_to(np.exp((nf(length) - nf(1.0) - tq)[None, :, None] * lg[:, None, None]), (H_C, n, 128))
    gpow = np.exp(nf(length) * lg)
    return dec.astype(nf), np.ascontiguousarray(rsc, nf), np.ascontiguousarray(tail, nf), gpow.astype(nf)


def kernel(x_prompt, x_sample, cache_k, cache_v, state_hgrn, state_ret, meta_tokens, norm_mix, norm_ffn, w_in, q_norm, k_norm, attn_sinks, hgrn_lb, hgrn_norm, ret_norm, w_out, w_gate, w_up, w_down):
    batch, seq, d = x_prompt.shape
    dec_batch, dec_seq, _ = x_sample.shape
    depth = w_in.shape[0]
    w = cache_k.shape[2]
    assert d == 2048 and w_in.shape[2] == D_IN and seq % BLK == 0
    assert SROWS % dec_seq == 0 and dec_batch % (SROWS // dec_seq) == 0 and w == WINDOW
    nb = seq // BLK + 1
    lp = nb * BLK
    rows_main = batch * seq
    rows_s = dec_batch * dec_seq
    row_s0 = rows_main + BLK
    tail_rows = BLK + rows_s
    rows = rows_main + tail_rows
    tm_ffn = _row_tile(rows, TM_FFN, 16)
    assert rows_main % tail_rows == 0 and rows_s % SROWS == 0 and tail_rows % 16 == 0 and tm_ffn >= tail_rows

    tail = jnp.concatenate([jnp.zeros((PAD, d), f32), meta_tokens.astype(f32), x_sample.reshape(rows_s, d)], axis=0)
    x, h = _embed(x_prompt.reshape(rows_main, d), tail, norm_mix[0][None])

    p = jax.nn.softmax(hgrn_lb.astype(f32), axis=0)
    lbs = jnp.cumsum(p, axis=0) - p[0]
    lg = np.log1p(-np.exp2(np.float32(-5.0) - np.arange(H_C, dtype=np.float32))).astype(np.float32)
    ropea_p, roper_p = _rope_tables(np.arange(lp) - PAD)
    srow = np.arange(SROWS)
    ropea_s, roper_s = _rope_tables(PAST_LEN + srow % dec_seq)
    dec_p, rsc_p, tail_p, gpow_p = _ret_tables(lg, np.zeros((BLK,), np.int32), np.arange(BLK), float(BLK))
    dec_s, rsc_s, tail_s, gpow_s = _ret_tables(lg, srow // dec_seq, srow % dec_seq, float(dec_seq))
    sel_np = np.arange(1024)[:, None] // HEAD_DIM_A == np.arange(128)[None, :]
    sel = jnp.asarray(sel_np, bf16)
    expand = jnp.asarray(sel_np.T, bf16)
    tri = jnp.asarray(np.arange(BLK)[:, None] >= np.arange(BLK)[None, :], bf16)
    dsel = jnp.asarray(np.arange(SUB * 128)[:, None] // 128 == np.arange(128)[None, :] % SUB, bf16)

    ck_flat = cache_k.reshape(depth * dec_batch, w, N_KV_A * HEAD_DIM_A)
    cv_flat = cache_v.reshape(depth * dec_batch, w, N_KV_A * HEAD_DIM_A)
    sh_flat = state_hgrn.reshape(depth * dec_batch, H_B, 128, 128)
    sr_flat = state_ret.reshape(depth * dec_batch, H_C, 128, 128)

    outs_p, outs_s = [], ()
    for l in range(depth):
        proj = _inproj(h, w_in, l)
        qn = jnp.tile(q_norm[l], N_HEADS_A)[None]
        kn = jnp.tile(k_norm[l], N_KV_A)[None]
        common = (qn, kn, lbs[l][None], hgrn_norm[l][None], ret_norm[l][None], sel, expand)
        mix, ck, cv, sh, sr, wg_b, wu_b, wd_b, wo_b = _mixer_prompt(
            proj, rows, batch, nb, l, attn_sinks[l], gpow_p, ropea_p, roper_p, *common, tri, dsel,
            dec_p, rsc_p, tail_p, w_gate, w_up, w_down, w_out)
        mix, *outs_s = _mixer_sample(mix, proj, row_s0, l, dec_batch, dec_seq, attn_sinks[l], gpow_s,
                                     ropea_s, roper_s, *common, dec_s, rsc_s, tail_s,
                                     ck_flat, cv_flat, sh_flat, sr_flat, tuple(outs_s))
        ffn_w = (wo_b, norm_ffn[l][None], wg_b, wu_b, wd_b)
        outs_p.append((ck, cv, sh, sr))
        if l + 1 < depth:
            x, h = _ffn(x, mix, *ffn_w, norm_mix[l + 1][None], tm_ffn, tail_rows)
        else:
            y_main, y_tail = _ffn(x, mix, *ffn_w, None, tm_ffn, tail_rows)

    y_prompt = y_main.reshape(batch, seq, d)
    y_sample = y_tail[BLK:].reshape(dec_batch, dec_seq, d)
    kv_shape = (depth, -1, w, N_KV_A, HEAD_DIM_A)
    st_shape = (depth, dec_batch, H_B, 128, 128)
    stack = lambda outs, k: jnp.stack([o[k] for o in outs])
    nk, nv, nsh, nsr = outs_s
    return (y_prompt, y_sample,
            stack(outs_p, 0).reshape(kv_shape), stack(outs_p, 1).reshape(kv_shape), stack(outs_p, 2), stack(outs_p, 3),
            nk.reshape(kv_shape), nv.reshape(kv_shape), nsh.reshape(st_shape), nsr.reshape(st_shape))
```

```python
import functools

import jax
import jax.numpy as jnp
import numpy as np
from jax import lax
from jax.experimental import pallas as pl
from jax.experimental.pallas import tpu as pltpu

f32 = jnp.float32
bf16 = jnp.bfloat16

N_META = 16
EPS = 1e-6
NEG_INF = -1e30
LB_FLOOR = 1e-30
WINDOW = 128
HEAD_DIM_A = 64
N_HEADS_A = 16
N_KV_A = 4
GROUP_A = N_HEADS_A // N_KV_A
N_ROT = 16
ROPE_THETA = 500000.0
H_B = 4
DK_B = 128
H_C = 4
DK_C = 128
RET_THETA = 10000.0
PAST_LEN = 16384

BLK = 128
PAD = BLK - N_META
SUB = 8
SROWS = 16
TM_FFN = 528
TM_FFN_LAST = 704
VMEM_LIMIT = 56 * 1024 * 1024
VMEM_LIMIT_FFN = 60 * 1024 * 1024

Q_A, K_A, V_A = 0, 1024, 1280
Q_B, F_B, I_B, G_B = 1536, 2048, 2560, 3072
Q_C, K_C, V_C, G_C = 3584, 4096, 4608, 5120
D_IN = 5632
W_A = 1024
LOG2E = 1.4426950408889634
Q_SCALE = HEAD_DIM_A ** -0.5 * LOG2E
NT = (((1,), (1,)), ((), ()))
TN = (((0,), (0,)), ((), ()))


def _dot(a, b, dims=None):
    if dims is None:
        return jnp.dot(a, b, preferred_element_type=f32)
    return lax.dot_general(a, b, dims, preferred_element_type=f32)


def _row_tile(rows, cap, align=8):
    best = align
    for t in range(align, cap + 1, align):
        if rows % t == 0:
            best = t
    return best


def _silu(x):
    return x * jax.nn.sigmoid(x)


def _rms(x, g):
    return x * lax.rsqrt(jnp.mean(x * x, axis=-1, keepdims=True) + EPS) * g


def _inproj_kernel(h_ref, w_ref, o_ref):
    o_ref[...] = _dot(h_ref[...], w_ref[...].astype(bf16))


def _inproj(h, w_all, layer):
    rows, d = h.shape
    n = w_all.shape[2]
    tm = _row_tile(rows, 2176, 16)
    tn = 512
    return pl.pallas_call(
        _inproj_kernel,
        grid=(rows // tm, n // tn),
        in_specs=[pl.BlockSpec((tm, d), lambda i, j: (i, 0)),
                  pl.BlockSpec((None, d, tn), lambda i, j: (layer, 0, j))],
        out_specs=pl.BlockSpec((tm, tn), lambda i, j: (i, j)),
        out_shape=jax.ShapeDtypeStruct((rows, n), f32),
        compiler_params=pltpu.CompilerParams(dimension_semantics=("arbitrary", "arbitrary"),
                                             vmem_limit_bytes=VMEM_LIMIT),
        name="inproj",
    )(h, w_all)


def _embed_kernel(xp_ref, tail_ref, g_ref, x_ref, h_ref):
    i = pl.program_id(0)
    last = pl.num_programs(0) - 1

    def emit(src_ref):
        x = src_ref[...]
        x_ref[...] = x
        h_ref[...] = _rms(x, g_ref[...]).astype(bf16)

    pl.when(i < last)(lambda: emit(xp_ref))
    pl.when(i == last)(lambda: emit(tail_ref))


def _embed(xp, tail, g):
    rows_main, d = xp.shape
    tm = tail.shape[0]
    n_main = rows_main // tm
    rows = rows_main + tm
    return pl.pallas_call(
        _embed_kernel,
        grid=(n_main + 1,),
        in_specs=[pl.BlockSpec((tm, d), lambda i: (jnp.minimum(i, n_main - 1), 0)),
                  pl.BlockSpec((tm, d), lambda i: (0, 0)),
                  pl.BlockSpec((1, d), lambda i: (0, 0))],
        out_specs=[pl.BlockSpec((tm, d), lambda i: (i, 0)),
                   pl.BlockSpec((tm, d), lambda i: (i, 0))],
        out_shape=[jax.ShapeDtypeStruct((rows, d), f32), jax.ShapeDtypeStruct((rows, d), bf16)],
        compiler_params=pltpu.CompilerParams(dimension_semantics=("arbitrary",), vmem_limit_bytes=VMEM_LIMIT),
        name="embed_norm",
    )(xp, tail, g)


def _ffn_kernel(*refs, with_next, tail_rows):
    if with_next:
        x_ref, mix_ref, wo_ref, nf_ref, wg_ref, wu_ref, wd_ref, gn_ref, o_ref, hn_ref, h_ref = refs
    else:
        x_ref, mix_ref, wo_ref, nf_ref, wg_ref, wu_ref, wd_ref, o_ref, tail_ref, h_ref = refs
    j = pl.program_id(1)

    @pl.when(j == 0)
    def _():
        x1 = x_ref[...] + _dot(mix_ref[...], wo_ref[...])
        h_ref[...] = _rms(x1, nf_ref[...]).astype(bf16)
        o_ref[...] = x1

    h = h_ref[...]
    a = _silu(_dot(h, wg_ref[...])) * _dot(h, wu_ref[...])
    o_ref[...] += _dot(a.astype(bf16), wd_ref[...])

    if with_next:
        @pl.when(j == pl.num_programs(1) - 1)
        def _():
            hn_ref[...] = _rms(o_ref[...], gn_ref[...]).astype(bf16)
    else:
        @pl.when((j == pl.num_programs(1) - 1) & (pl.program_id(0) == pl.num_programs(0) - 1))
        def _():
            tail_ref[...] = o_ref[o_ref.shape[0] - tail_rows:, :]


def _ffn(x, mix, wo, nf, wg, wu, wd, g_next, tm, tail_rows):
    rows, d = x.shape
    dff = wg.shape[1]
    tf = 512
    n_tiles = rows // tm
    with_next = g_next is not None
    row_spec = pl.BlockSpec((tm, d), lambda i, j: (i, 0))
    out_spec = row_spec
    in_specs = [row_spec, row_spec,
                pl.BlockSpec((d, d), lambda i, j: (0, 0), pipeline_mode=pl.Buffered(1)),
                pl.BlockSpec((1, d), lambda i, j: (0, 0)),
                pl.BlockSpec((d, tf), lambda i, j: (0, j)),
                pl.BlockSpec((d, tf), lambda i, j: (0, j)),
                pl.BlockSpec((tf, d), lambda i, j: (j, 0))]
    args = [x, mix, wo, nf, wg, wu, wd]
    if with_next:
        in_specs.append(pl.BlockSpec((1, d), lambda i, j: (0, 0)))
        args.append(g_next)
        out_specs = [out_spec, out_spec]
        out_shape = [jax.ShapeDtypeStruct((rows, d), f32), jax.ShapeDtypeStruct((rows, d), bf16)]
    else:
        assert tail_rows % 8 == 0 and tail_rows <= tm
        out_specs = [out_spec, pl.BlockSpec((tail_rows, d), lambda i, j: (0, 0))]
        out_shape = [jax.ShapeDtypeStruct((rows - tail_rows, d), f32), jax.ShapeDtypeStruct((tail_rows, d), f32)]
    return pl.pallas_call(
        functools.partial(_ffn_kernel, with_next=with_next, tail_rows=tail_rows),
        grid=(n_tiles, dff // tf),
        in_specs=in_specs,
        out_specs=out_specs,
        out_shape=out_shape,
        scratch_shapes=[pltpu.VMEM((tm, d), bf16)],
        compiler_params=pltpu.CompilerParams(dimension_semantics=("arbitrary", "arbitrary"),
                                             vmem_limit_bytes=VMEM_LIMIT_FFN),
        name="outproj_ffn",
    )(*args)


def _head_norm(x, sel_ref, exp_ref, g):
    n, w = x.shape
    x2 = x * x
    hi = x2.astype(bf16)
    lo = (x2 - hi.astype(f32)).astype(bf16)
    s = _dot(jnp.concatenate([hi, lo], axis=0), sel_ref[0:w, :])
    return x * _dot_split3(lax.rsqrt((s[0:n] + s[n:2 * n]) * (1.0 / HEAD_DIM_A) + EPS), exp_ref[:, 0:w]) * g


def _dot_split3(x, m):
    n = x.shape[0]
    y = _dot(jnp.concatenate(_split3(x), axis=0), m)
    return y[0:n] + y[n:2 * n] + y[2 * n:3 * n]


def _rope_a(x, tab_ref):
    half = N_ROT // 2
    first = lax.broadcasted_iota(jnp.int32, (x.shape[0], 128), 1) % HEAD_DIM_A < half
    out = []
    for t in range(x.shape[1] // 128):
        xt = x[:, 128 * t:128 * (t + 1)]
        partner = jnp.where(first, -pltpu.roll(xt, 128 - half, 1), pltpu.roll(xt, half, 1))
        out.append(xt * tab_ref[0] + partner * tab_ref[1])
    return jnp.concatenate(out, axis=1)


def _rope_r(x, tab_ref):
    return x * tab_ref[0] + pltpu.roll(x, DK_C // 2, 1) * tab_ref[1]


def _forget(fb, lb):
    f = jnp.maximum(lb, LB_FLOOR) + (1.0 - lb) * jax.nn.sigmoid(fb)
    return f, jnp.log2(f)


def _split3(x):
    h1 = x.astype(bf16)
    r1 = x - h1.astype(f32)
    h2 = r1.astype(bf16)
    h3 = (r1 - h2.astype(f32)).astype(bf16)
    return h1, h2, h3


def _mixer_prompt_kernel(sink_ref, gpow_ref, proj_ref, ropea_ref, roper_ref, qn_ref, kn_ref, lb_ref, hn_ref, rn_ref,
                         sel_ref, exp_ref, tri_ref, dsel_ref, dec_ref, rsc_ref, tail_ref,
                         wg_ref, wu_ref, wd_ref, wo_ref,
                         mix_ref, ck_ref, cv_ref, sh_ref, sr_ref, wgb_ref, wub_ref, wdb_ref, wob_ref,
                         kk, vv, sht, srs, qs_s, qb_s, kb_s, g_s):
    i = pl.program_id(1)
    nb = pl.num_programs(1)

    @pl.when(i == 0)
    def _():
        kk[...] = jnp.zeros_like(kk)
        vv[...] = jnp.zeros_like(vv)
        for g in range(N_KV_A):
            vv[:, 256 * g + 128:256 * (g + 1)] = jnp.ones((2 * BLK, 128), bf16)
        sht[...] = jnp.zeros_like(sht)
        srs[...] = jnp.zeros_like(srs)

    wgb_ref[...] = wg_ref[...].astype(bf16)
    wub_ref[...] = wu_ref[...].astype(bf16)
    wdb_ref[...] = wd_ref[...].astype(bf16)
    wob_ref[...] = wo_ref[...].astype(bf16)

    rowabs = i * BLK + lax.broadcasted_iota(jnp.int32, (BLK, 1), 0)
    fgate, logf = _forget(proj_ref[:, F_B:F_B + 512], lb_ref[...])
    logf = jnp.where(rowabs >= PAD, logf, 0.0)
    kb_s[...] = jnp.where(rowabs >= PAD, 1.0 - fgate, 0.0)
    qb_s[...] = _silu(proj_ref[:, Q_B:Q_B + 512])
    l1, l2, l3 = _split3(logf)
    tri = tri_ref[...]
    g_s[...] = _dot(tri, l1) + _dot(tri, l2) + _dot(tri, l3)

    r2 = lax.broadcasted_iota(jnp.int32, (BLK, BLK), 0)
    c2 = lax.broadcasted_iota(jnp.int32, (BLK, BLK), 1)
    lane_lo = c2 < HEAD_DIM_A

    def attention_prep():
        qa = _rope_a(_head_norm(proj_ref[:, Q_A:Q_A + 1024], sel_ref, exp_ref, qn_ref[...]), ropea_ref)
        for t in range(N_HEADS_A // 2):
            xt = qa[:, 128 * t:128 * (t + 1)] * Q_SCALE
            qs_s[256 * t:256 * t + 128, :] = jnp.where(lane_lo, xt, 0.0).astype(bf16)
            qs_s[256 * t + 128:256 * (t + 1), :] = jnp.where(lane_lo, 0.0, xt).astype(bf16)
        k_cur = _rope_a(_head_norm(proj_ref[:, K_A:K_A + 256], sel_ref, exp_ref, kn_ref[...]), ropea_ref)
        v_cur = proj_ref[:, V_A:V_A + 256]
        ck_ref[0] = k_cur
        cv_ref[0] = v_cur
        for t in range(N_KV_A // 2):
            for src, dst, width in ((k_cur, kk, 128), (v_cur, vv, 256)):
                xt = src[:, 128 * t:128 * (t + 1)]
                xs = pltpu.roll(xt, HEAD_DIM_A, 1)
                dst[BLK:2 * BLK, width * 2 * t:width * 2 * t + 128] = jnp.where(lane_lo, xt, xs).astype(bf16)
                dst[BLK:2 * BLK, width * (2 * t + 1):width * (2 * t + 1) + 128] = jnp.where(lane_lo, xs, xt).astype(bf16)

    prow = 2 * BLK
    r4 = lax.broadcasted_iota(jnp.int32, (prow, BLK), 0) % BLK
    c4 = lax.broadcasted_iota(jnp.int32, (prow, BLK), 1)
    up = c4 > r4
    ok = c4 >= jnp.where(up, PAD - (i - 1) * BLK, PAD - i * BLK)

    def attention_pair(t):
        g = (2 * t) // GROUP_A
        s2 = _dot(qs_s[prow * t:prow * (t + 1), :], kk[:, 128 * g:128 * (g + 1)], NT)
        s = jnp.where(ok, jnp.where(up, s2[:, 0:BLK], s2[:, BLK:2 * BLK]), NEG_INF)
        sk = jnp.concatenate([jnp.full((BLK, BLK), sink_ref[2 * t + j] * LOG2E, f32) for j in range(2)], axis=0)
        m = jnp.maximum(jnp.max(s, axis=-1, keepdims=True), sk)
        p = jnp.exp2(s - m)
        p2 = jnp.concatenate([jnp.where(up, p, 0.0), jnp.where(up, 0.0, p)], axis=1).astype(bf16)
        res = _dot(p2, vv[:, 256 * g:256 * (g + 1)])
        y = res[:, 0:128] / (res[:, 128:256] + jnp.exp2(sk - m))
        mix_ref[:, 128 * t:128 * (t + 1)] = jnp.where(lane_lo, y[0:BLK], y[BLK:prow]).astype(bf16)

    levels = []
    bz = BLK
    while bz > SUB:
        levels.append((bz, (r2 % bz) >= bz // 2, None if bz == BLK else (r2 // bz) == (c2 // bz)))
        bz //= 2
    same_sub = (r2 // SUB) == (c2 // SUB)
    sub_ge = [(r2 % SUB) >= s for s in range(SUB)]

    def retention_head(h):
        qc = _rope_r(proj_ref[:, Q_C + 128 * h:Q_C + 128 * (h + 1)], roper_ref)
        kc = _rope_r(proj_ref[:, K_C + 128 * h:K_C + 128 * (h + 1)], roper_ref) * (DK_C ** -0.5)
        vc = proj_ref[:, V_C + 128 * h:V_C + 128 * (h + 1)].astype(bf16)
        qcb = qc.astype(bf16)
        sc = _dot(qcb, kc.astype(bf16), NT) * dec_ref[h]
        oc = _dot(qcb, srs[h].astype(bf16)) * rsc_ref[h] + _dot(sc.astype(bf16), vc)
        srs[h] = gpow_ref[h] * srs[h] + _dot((kc * tail_ref[h]).astype(bf16), vc, TN)
        yc = _rms(oc, rn_ref[...]) * _silu(proj_ref[:, G_C + 128 * h:G_C + 128 * (h + 1)])
        mix_ref[:, W_A + 512 + 128 * h:W_A + 512 + 128 * (h + 1)] = yc.astype(bf16)

    def hgrn_intra(h):
        sl = slice(128 * h, 128 * (h + 1))
        gc = g_s[:, sl]
        qb = qb_s[:, sl]
        kb = kb_s[:, sl]
        inter = _dot((qb * jnp.exp2(gc)).astype(bf16), sht[h].astype(bf16), NT)
        amat = None
        for bz, upper, same in levels:
            gref = jnp.concatenate([jnp.broadcast_to(g_s[pl.ds(st + bz // 2 - 1, 1), sl], (bz, 128))
                                    for st in range(0, BLK, bz)], axis=0)
            dg = gc - gref
            e = jnp.exp2(jnp.where(upper, dg, -dg))
            qt = jnp.where(upper, qb * e, 0.0).astype(bf16)
            kt = jnp.where(upper, 0.0, kb * e).astype(bf16)
            pm = _dot(qt, kt, NT)
            if same is not None:
                pm = jnp.where(same, pm, 0.0)
            amat = pm if amat is None else amat + pm
        ys = []
        for s in range(SUB):
            ks = jnp.concatenate([jnp.broadcast_to(kb_s[pl.ds(SUB * j + s, 1), sl], (SUB, 128))
                                  for j in range(BLK // SUB)], axis=0)
            gs = jnp.concatenate([jnp.broadcast_to(g_s[pl.ds(SUB * j + s, 1), sl], (SUB, 128))
                                  for j in range(BLK // SUB)], axis=0)
            ys.append((qb * ks * jnp.exp2(jnp.where(sub_ge[s], gc - gs, NEG_INF))).astype(bf16))
        amat = amat + jnp.where(same_sub, _dot(jnp.concatenate(ys, axis=1), dsel_ref[...]), 0.0)
        return inter, amat.astype(bf16)

    def hgrn_finish(h, inter, amat):
        sl = slice(128 * h, 128 * (h + 1))
        vb = proj_ref[:, I_B + 128 * h:I_B + 128 * (h + 1)].astype(bf16)
        ob = inter + _dot(amat, vb)
        glast = g_s[pl.ds(BLK - 1, 1), sl]
        kt2 = (kb_s[:, sl] * jnp.exp2(glast - g_s[:, sl])).astype(bf16)
        sht[h] = sht[h] * jnp.exp2(glast) + _dot(vb, kt2, TN)
        yb = _rms(ob, hn_ref[...]) * _silu(proj_ref[:, G_B + 128 * h:G_B + 128 * (h + 1)])
        mix_ref[:, W_A + 128 * h:W_A + 128 * (h + 1)] = yb.astype(bf16)

    attention_prep()
    pending = None
    for h in range(H_B):
        cur = hgrn_intra(h)
        if pending is not None:
            hgrn_finish(h - 1, *pending)
        retention_head(h)
        pending = cur
    hgrn_finish(H_B - 1, *pending)
    for t in range(N_HEADS_A // 2):
        attention_pair(t)
    kk[0:BLK, :] = kk[BLK:2 * BLK, :]
    vv[0:BLK, :] = vv[BLK:2 * BLK, :]

    @pl.when(i == nb - 1)
    def _():
        for h in range(H_B):
            sh_ref[0, h] = sht[h].T
            sr_ref[0, h] = srs[h]


def _const_spec(shape):
    nd = len(shape)
    return pl.BlockSpec(shape, lambda *_: (0,) * nd)


def _slab_rows(nrows, steps):
    for r in range(16, nrows + 1, 16):
        if nrows % r == 0 and nrows // r <= steps:
            return r
    return nrows


def _mixer_prompt(proj, rows_total, batch, nb, layer, sinks, gpow, ropea, roper, qn, kn, lb, hn, rn, sel, expand,
                  tri, dsel, dec, rsc, tail, w_gate, w_up, w_down, w_out):
    smem = pl.BlockSpec(memory_space=pltpu.SMEM)
    d, dff = w_gate.shape[1], w_gate.shape[2]

    def row_blk(b, i):
        return jnp.where(i == 0, batch * (nb - 1), b * (nb - 1) + i - 1)

    def mix_blk(b, i):
        return jnp.where((i == 0) & (b > 0), b * (nb - 1), row_blk(b, i))

    def slab_in(nrows, ncols):
        r = _slab_rows(nrows, batch * nb)
        return pl.BlockSpec((None, r, ncols), lambda b, i: (layer, jnp.minimum(b * nb + i, nrows // r - 1), 0))

    def slab_out(nrows, ncols):
        r = _slab_rows(nrows, batch * nb)
        return pl.BlockSpec((r, ncols), lambda b, i: (jnp.minimum(b * nb + i, nrows // r - 1), 0))

    return pl.pallas_call(
        _mixer_prompt_kernel,
        grid=(batch, nb),
        in_specs=[smem, smem,
                  pl.BlockSpec((BLK, D_IN), lambda b, i: (row_blk(b, i), 0)),
                  pl.BlockSpec((2, BLK, 128), lambda b, i: (0, i, 0)),
                  pl.BlockSpec((2, BLK, 128), lambda b, i: (0, i, 0)),
                  _const_spec((1, 1024)), _const_spec((1, 256)), _const_spec((1, 512)),
                  _const_spec((1, 128)), _const_spec((1, 128)),
                  _const_spec((1024, 128)), _const_spec((128, 1024)),
                  _const_spec((BLK, BLK)), _const_spec((SUB * 128, 128)),
                  _const_spec((H_C, BLK, BLK)), _const_spec((H_C, BLK, 128)), _const_spec((H_C, BLK, 128)),
                  slab_in(d, dff), slab_in(d, dff), slab_in(dff, d), slab_in(d, d)],
        out_specs=[pl.BlockSpec((BLK, 2048), lambda b, i: (mix_blk(b, i), 0)),
                   pl.BlockSpec((1, BLK, 256), lambda b, i: (b, 0, 0)),
                   pl.BlockSpec((1, BLK, 256), lambda b, i: (b, 0, 0)),
                   pl.BlockSpec((1, H_B, 128, 128), lambda b, i: (b, 0, 0, 0)),
                   pl.BlockSpec((1, H_C, 128, 128), lambda b, i: (b, 0, 0, 0)),
                   slab_out(d, dff), slab_out(d, dff), slab_out(dff, d), slab_out(d, d)],
        out_shape=[jax.ShapeDtypeStruct((rows_total, 2048), bf16),
                   jax.ShapeDtypeStruct((batch, BLK, 256), f32),
                   jax.ShapeDtypeStruct((batch, BLK, 256), f32),
                   jax.ShapeDtypeStruct((batch, H_B, 128, 128), f32),
                   jax.ShapeDtypeStruct((batch, H_C, 128, 128), f32),
                   jax.ShapeDtypeStruct((d, dff), bf16), jax.ShapeDtypeStruct((d, dff), bf16),
                   jax.ShapeDtypeStruct((dff, d), bf16), jax.ShapeDtypeStruct((d, d), bf16)],
        scratch_shapes=[pltpu.VMEM((2 * BLK, N_KV_A * 128), bf16), pltpu.VMEM((2 * BLK, N_KV_A * 256), bf16),
                        pltpu.VMEM((H_B, 128, 128), f32), pltpu.VMEM((H_C, 128, 128), f32),
                        pltpu.VMEM((N_HEADS_A * BLK, 128), bf16),
                        pltpu.VMEM((BLK, 512), f32), pltpu.VMEM((BLK, 512), f32), pltpu.VMEM((BLK, 512), f32)],
        compiler_params=pltpu.CompilerParams(dimension_semantics=("arbitrary", "arbitrary"),
                                             vmem_limit_bytes=VMEM_LIMIT),
        name="mixer_prompt",
    )(sinks, gpow, proj, ropea, roper, qn, kn, lb, hn, rn, sel, expand, tri, dsel, dec, rsc, tail,
      w_gate, w_up, w_down, w_out)


def _mixer_sample_kernel(sink_ref, gpow_ref, mixin_ref, *refs, dec_seq, n_prev):
    del mixin_ref
    _mixer_sample_body(sink_ref, gpow_ref, *refs[:17], *refs[17 + n_prev:], dec_seq=dec_seq)


def _mixer_sample_body(sink_ref, gpow_ref, proj_ref, ropea_ref, roper_ref, qn_ref, kn_ref, lb_ref,
                       hn_ref, rn_ref, sel_ref, exp_ref, dec_ref, rsc_ref, tail_ref,
                       ck_ref, cv_ref, sh_ref, sr_ref,
                       mix_ref, nk_ref, nv_ref, nsh_ref, nsr_ref, *, dec_seq):
    nbat = SROWS // dec_seq
    w = ck_ref.shape[1]
    rb = lax.broadcasted_iota(jnp.int32, (SROWS, 1), 0) // dec_seq
    rt = lax.broadcasted_iota(jnp.int32, (SROWS, 1), 0) % dec_seq

    def pick(parts):
        out = parts[nbat - 1]
        for b in range(nbat - 2, -1, -1):
            out = jnp.where(rb == b, parts[b], out)
        return out

    fgate, logf = _forget(proj_ref[:, F_B:F_B + 512], lb_ref[...])
    kb = 1.0 - fgate
    qb = _silu(proj_ref[:, Q_B:Q_B + 512])
    vb = proj_ref[:, I_B:I_B + 512]
    gcum = logf
    for d in range(1, dec_seq):
        gcum = gcum + jnp.where(rt >= d, pltpu.roll(logf, d, 0), 0.0)
    intra = [jnp.zeros((SROWS, 128), f32) for _ in range(H_B)]
    for s in range(SROWS):
        ok = (rb == s // dec_seq) & (rt >= s % dec_seq)
        y = qb * kb[s:s + 1, :] * jnp.exp2(jnp.where(ok, gcum - gcum[s:s + 1, :], NEG_INF))
        for h in range(H_B):
            sl = slice(128 * h, 128 * (h + 1))
            intra[h] = intra[h] + jnp.sum(y[:, sl], axis=-1, keepdims=True) * vb[s:s + 1, sl]

    qa = _rope_a(_head_norm(proj_ref[:, Q_A:Q_A + 1024], sel_ref, exp_ref, qn_ref[...]), ropea_ref)
    qs = (qa * Q_SCALE).astype(bf16)
    k_new = _rope_a(_head_norm(proj_ref[:, K_A:K_A + 256], sel_ref, exp_ref, kn_ref[...]), ropea_ref)
    v_new = proj_ref[:, V_A:V_A + 256]
    for b in range(nbat):
        nk_ref[b, 0:w - dec_seq, :] = ck_ref[b, dec_seq:w, :]
        nk_ref[b, w - dec_seq:w, :] = k_new[dec_seq * b:dec_seq * (b + 1), :]
        nv_ref[b, 0:w - dec_seq, :] = cv_ref[b, dec_seq:w, :]
        nv_ref[b, w - dec_seq:w, :] = v_new[dec_seq * b:dec_seq * (b + 1), :]
    knb = k_new.astype(bf16)
    vnb = v_new.astype(bf16)

    grows = GROUP_A * SROWS
    rt_g = jnp.concatenate([rt] * GROUP_A, axis=0)
    rb_g = jnp.concatenate([rb] * GROUP_A, axis=0)
    jc = lax.broadcasted_iota(jnp.int32, (grows, w), 1)
    valid_c = jc > rt_g + (w - WINDOW)
    cn = lax.broadcasted_iota(jnp.int32, (grows, SROWS), 1)
    valid_n = (cn // dec_seq == rb_g) & (cn % dec_seq <= rt_g)
    heads = [None] * N_HEADS_A
    for g in range(N_KV_A):
        ksl = slice(64 * g, 64 * (g + 1))
        qg = jnp.concatenate([qs[:, 64 * h:64 * (h + 1)] for h in range(GROUP_A * g, GROUP_A * (g + 1))], axis=0)
        sparts = [_dot(qg, ck_ref[b, :, ksl].astype(bf16), NT) for b in range(nbat)]
        s_c = sparts[nbat - 1]
        for b in range(nbat - 2, -1, -1):
            s_c = jnp.where(rb_g == b, sparts[b], s_c)
        s_c = jnp.where(valid_c, s_c, NEG_INF)
        s_n = jnp.where(valid_n, _dot(qg, knb[:, ksl], NT), NEG_INF)
        sk = jnp.concatenate([jnp.full((SROWS, 1), sink_ref[h] * LOG2E, f32)
                              for h in range(GROUP_A * g, GROUP_A * (g + 1))], axis=0)
        m = jnp.maximum(jnp.maximum(jnp.max(s_c, axis=-1, keepdims=True), jnp.max(s_n, axis=-1, keepdims=True)), sk)
        p_c = jnp.exp2(s_c - m)
        p_n = jnp.exp2(s_n - m)
        den = jnp.sum(p_c, axis=-1, keepdims=True) + jnp.sum(p_n, axis=-1, keepdims=True) + jnp.exp2(sk - m)
        pcb = p_c.astype(bf16)
        oparts = [_dot(pcb, cv_ref[b, :, ksl].astype(bf16)) for b in range(nbat)]
        o = oparts[nbat - 1]
        for b in range(nbat - 2, -1, -1):
            o = jnp.where(rb_g == b, oparts[b], o)
        o = (o + _dot(p_n.astype(bf16), vnb[:, ksl])) / den
        for hh in range(GROUP_A):
            heads[GROUP_A * g + hh] = o[SROWS * hh:SROWS * (hh + 1), :]
    mix_ref[:, 0:W_A] = jnp.concatenate(heads, axis=1).astype(bf16)

    qg = (qb * jnp.exp2(gcum)).astype(bf16)
    vbb = vb.astype(bf16)
    glast = [gcum[dec_seq * (b + 1) - 1:dec_seq * (b + 1), :] for b in range(nbat)]
    for h in range(H_B):
        sl = slice(128 * h, 128 * (h + 1))
        ob = pick([_dot(qg[:, sl], sh_ref[b, h].astype(bf16)) for b in range(nbat)]) + intra[h]
        for b in range(nbat):
            kt2 = jnp.where(rb == b, kb[:, sl] * jnp.exp2(glast[b][:, sl] - gcum[:, sl]), 0.0).astype(bf16)
            dcol = jnp.broadcast_to(jnp.exp2(glast[b][:, sl]), (128, 128)).T
            nsh_ref[b, h] = dcol * sh_ref[b, h] + _dot(kt2, vbb[:, sl], TN)
        yb = _rms(ob, hn_ref[...]) * _silu(proj_ref[:, G_B + 128 * h:G_B + 128 * (h + 1)])
        mix_ref[:, W_A + 128 * h:W_A + 128 * (h + 1)] = yb.astype(bf16)

    for h in range(H_C):
        qc = _rope_r(proj_ref[:, Q_C + 128 * h:Q_C + 128 * (h + 1)], roper_ref)
        kc = _rope_r(proj_ref[:, K_C + 128 * h:K_C + 128 * (h + 1)], roper_ref) * (DK_C ** -0.5)
        vc = proj_ref[:, V_C + 128 * h:V_C + 128 * (h + 1)]
        qcb = qc.astype(bf16)
        vcb = vc.astype(bf16)
        sc = _dot(qcb, kc.astype(bf16), NT) * dec_ref[h]
        oc = pick([_dot(qcb, sr_ref[b, h].astype(bf16)) for b in range(nbat)]) * rsc_ref[h] \
            + _dot(sc.astype(bf16), vcb)
        kct = kc * tail_ref[h]
        for b in range(nbat):
            nsr_ref[b, h] = gpow_ref[h] * sr_ref[b, h] + _dot(jnp.where(rb == b, kct, 0.0).astype(bf16), vcb, TN)
        yc = _rms(oc, rn_ref[...]) * _silu(proj_ref[:, G_C + 128 * h:G_C + 128 * (h + 1)])
        mix_ref[:, W_A + 512 + 128 * h:W_A + 512 + 128 * (h + 1)] = yc.astype(bf16)


def _mixer_sample(mix, proj, row0, layer, dec_batch, dec_seq, sinks, gpow, ropea, roper, qn, kn, lb, hn, rn, sel, expand,
                  dec, rsc, tail, cache_k, cache_v, state_h, state_r, prev_outs):
    smem = pl.BlockSpec(memory_space=pltpu.SMEM)
    nbat = SROWS // dec_seq
    steps = dec_batch // nbat
    blk0 = row0 // SROWS
    w = cache_k.shape[1]
    n_all = cache_k.shape[0]
    assert mix.shape[0] == row0 + steps * SROWS
    any_spec = pl.BlockSpec(memory_space=pl.ANY)
    n_in = 20
    return pl.pallas_call(
        functools.partial(_mixer_sample_kernel, dec_seq=dec_seq, n_prev=len(prev_outs)),
        grid=(steps,),
        in_specs=[smem, smem,
                  pl.BlockSpec(memory_space=pl.ANY),
                  pl.BlockSpec((SROWS, D_IN), lambda c: (blk0 + c, 0)),
                  _const_spec((2, SROWS, 128)), _const_spec((2, SROWS, 128)),
                  _const_spec((1, 1024)), _const_spec((1, 256)), _const_spec((1, 512)),
                  _const_spec((1, 128)), _const_spec((1, 128)),
                  _const_spec((1024, 128)), _const_spec((128, 1024)),
                  _const_spec((H_C, SROWS, SROWS)), _const_spec((H_C, SROWS, 128)), _const_spec((H_C, SROWS, 128)),
                  pl.BlockSpec((nbat, w, 256), lambda c: (layer * steps + c, 0, 0)),
                  pl.BlockSpec((nbat, w, 256), lambda c: (layer * steps + c, 0, 0)),
                  pl.BlockSpec((nbat, H_B, 128, 128), lambda c: (layer * steps + c, 0, 0, 0)),
                  pl.BlockSpec((nbat, H_C, 128, 128), lambda c: (layer * steps + c, 0, 0, 0))]
        + [any_spec] * len(prev_outs),
        out_specs=[pl.BlockSpec((SROWS, 2048), lambda c: (blk0 + c, 0)),
                   pl.BlockSpec((nbat, w, 256), lambda c: (layer * steps + c, 0, 0)),
                   pl.BlockSpec((nbat, w, 256), lambda c: (layer * steps + c, 0, 0)),
                   pl.BlockSpec((nbat, H_B, 128, 128), lambda c: (layer * steps + c, 0, 0, 0)),
                   pl.BlockSpec((nbat, H_C, 128, 128), lambda c: (layer * steps + c, 0, 0, 0))],
        out_shape=[jax.ShapeDtypeStruct(mix.shape, bf16),
                   jax.ShapeDtypeStruct((n_all, w, 256), f32),
                   jax.ShapeDtypeStruct((n_all, w, 256), f32),
                   jax.ShapeDtypeStruct((n_all, H_B, 128, 128), f32),
                   jax.ShapeDtypeStruct((n_all, H_C, 128, 128), f32)],
        input_output_aliases={2: 0, **{n_in + k: 1 + k for k in range(len(prev_outs))}},
        compiler_params=pltpu.CompilerParams(dimension_semantics=("arbitrary",), vmem_limit_bytes=VMEM_LIMIT),
        name="mixer_sample",
    )(sinks, gpow, mix, proj, ropea, roper, qn, kn, lb, hn, rn, sel, expand, dec, rsc, tail,
      cache_k, cache_v, state_h, state_r, *prev_outs)


def _rope_tables(pos):
    posf = jnp.asarray(pos).astype(f32)[:, None]
    t = pos.shape[0]
    half = N_ROT // 2
    inv = ROPE_THETA ** (-jnp.arange(half, dtype=f32) * (2.0 / N_ROT))
    ang = posf * inv[None, :]
    cos, sin = jnp.cos(ang), jnp.sin(ang)
    rest0 = jnp.zeros((t, HEAD_DIM_A - N_ROT), f32)
    c64 = jnp.concatenate([cos, cos, rest0 + 1.0], axis=1)
    s64 = jnp.concatenate([sin, sin, rest0], axis=1)
    ropea = jnp.stack([jnp.tile(c64, (1, 2)), jnp.tile(s64, (1, 2))])
    invr = RET_THETA ** (-jnp.arange(DK_C // 2, dtype=f32) * (2.0 / DK_C))
    angr = posf * invr[None, :]
    cr, sr = jnp.cos(angr), jnp.sin(angr)
    roper = jnp.stack([jnp.concatenate([cr, cr], axis=1), jnp.concatenate([-sr, sr], axis=1)])
    return ropea, roper


def _ret_tables(lg, seq_of_row, tok_of_row, length):
    nf = np.float32
    tq = tok_of_row.astype(nf)
    rel = tq[:, None] - tq[None, :]
    ok = (seq_of_row[:, None] == seq_of_row[None, :]) & (rel >= 0)
    dec = np.where(ok[None], np.exp(np.where(ok, rel, nf(0.0))[None] * lg[:, None, None]), nf(0.0))
    n = tq.shape[0]
    rsc = np.broadcast_to(np.exp((tq + nf(1.0))[None, :, None] * lg[:, None, None]), (H_C, n, 128))
    tail = np.broadcast_to(np.exp((nf(length) - nf(1.0) - tq)[None, :, None] * lg[:, None, None]), (H_C, n, 128))
    gpow = np.exp(nf(length) * lg)
    return dec.astype(nf), np.ascontiguousarray(rsc, nf), np.ascontiguousarray(tail, nf), gpow.astype(nf)


def kernel(x_prompt, x_sample, cache_k, cache_v, state_hgrn, state_ret, meta_tokens, norm_mix, norm_ffn, w_in, q_norm, k_norm, attn_sinks, hgrn_lb, hgrn_norm, ret_norm, w_out, w_gate, w_up, w_down):
    batch, seq, d = x_prompt.shape
    dec_batch, dec_seq, _ = x_sample.shape
    depth = w_in.shape[0]
    w = cache_k.shape[2]
    assert d == 2048 and w_in.shape[2] == D_IN and seq % BLK == 0
    assert SROWS % dec_seq == 0 and dec_batch % (SROWS // dec_seq) == 0 and w == WINDOW
    nb = seq // BLK + 1
    lp = nb * BLK
    rows_main = batch * seq
    rows_s = dec_batch * dec_seq
    row_s0 = rows_main + BLK
    tail_rows = BLK + rows_s
    rows = rows_main + tail_rows
    tm_ffn = _row_tile(rows, TM_FFN, 16)
    tm_ffn_last = _row_tile(rows, TM_FFN_LAST, 16)
    assert rows_main % tail_rows == 0 and rows_s % SROWS == 0 and tail_rows % 16 == 0
    assert min(tm_ffn, tm_ffn_last) >= tail_rows

    tail = jnp.concatenate([jnp.zeros((PAD, d), f32), meta_tokens.astype(f32), x_sample.reshape(rows_s, d)], axis=0)
    x, h = _embed(x_prompt.reshape(rows_main, d), tail, norm_mix[0][None])

    p = jax.nn.softmax(hgrn_lb.astype(f32), axis=0)
    lbs = jnp.cumsum(p, axis=0) - p[0]
    lg = np.log1p(-np.exp2(np.float32(-5.0) - np.arange(H_C, dtype=np.float32))).astype(np.float32)
    ropea_p, roper_p = _rope_tables(np.arange(lp) - PAD)
    srow = np.arange(SROWS)
    ropea_s, roper_s = _rope_tables(PAST_LEN + srow % dec_seq)
    dec_p, rsc_p, tail_p, gpow_p = _ret_tables(lg, np.zeros((BLK,), np.int32), np.arange(BLK), float(BLK))
    dec_s, rsc_s, tail_s, gpow_s = _ret_tables(lg, srow // dec_seq, srow % dec_seq, float(dec_seq))
    sel_np = np.arange(1024)[:, None] // HEAD_DIM_A == np.arange(128)[None, :]
    sel = jnp.asarray(sel_np, bf16)
    expand = jnp.asarray(sel_np.T, bf16)
    tri = jnp.asarray(np.arange(BLK)[:, None] >= np.arange(BLK)[None, :], bf16)
    dsel = jnp.asarray(np.arange(SUB * 128)[:, None] // 128 == np.arange(128)[None, :] % SUB, bf16)

    ck_flat = cache_k.reshape(depth * dec_batch, w, N_KV_A * HEAD_DIM_A)
    cv_flat = cache_v.reshape(depth * dec_batch, w, N_KV_A * HEAD_DIM_A)
    sh_flat = state_hgrn.reshape(depth * dec_batch, H_B, 128, 128)
    sr_flat = state_ret.reshape(depth * dec_batch, H_C, 128, 128)

    outs_p, outs_s = [], ()
    for l in range(depth):
        proj = _inproj(h, w_in, l)
        qn = jnp.tile(q_norm[l], N_HEADS_A)[None]
        kn = jnp.tile(k_norm[l], N_KV_A)[None]
        common = (qn, kn, lbs[l][None], hgrn_norm[l][None], ret_norm[l][None], sel, expand)
        mix, ck, cv, sh, sr, wg_b, wu_b, wd_b, wo_b = _mixer_prompt(
            proj, rows, batch, nb, l, attn_sinks[l], gpow_p, ropea_p, roper_p, *common, tri, dsel,
            dec_p, rsc_p, tail_p, w_gate, w_up, w_down, w_out)
        mix, *outs_s = _mixer_sample(mix, proj, row_s0, l, dec_batch, dec_seq, attn_sinks[l], gpow_s,
                                     ropea_s, roper_s, *common, dec_s, rsc_s, tail_s,
                                     ck_flat, cv_flat, sh_flat, sr_flat, tuple(outs_s))
        ffn_w = (wo_b, norm_ffn[l][None], wg_b, wu_b, wd_b)
        outs_p.append((ck, cv, sh, sr))
        if l + 1 < depth:
            x, h = _ffn(x, mix, *ffn_w, norm_mix[l + 1][None], tm_ffn, tail_rows)
        else:
            y_main, y_tail = _ffn(x, mix, *ffn_w, None, tm_ffn_last, tail_rows)

    y_prompt = y_main.reshape(batch, seq, d)
    y_sample = y_tail[BLK:].reshape(dec_batch, dec_seq, d)
    kv_shape = (depth, -1, w, N_KV_A, HEAD_DIM_A)
    st_shape = (depth, dec_batch, H_B, 128, 128)
    stack = lambda outs, k: jnp.stack([o[k] for o in outs])
    nk, nv, nsh, nsr = outs_s
    return (y_prompt, y_sample,
            stack(outs_p, 0).reshape(kv_shape), stack(outs_p, 1).reshape(kv_shape), stack(outs_p, 2), stack(outs_p, 3),
            nk.reshape(kv_shape), nv.reshape(kv_shape), nsh.reshape(st_shape), nsr.reshape(st_shape))
```

```python
import functools

import jax
import jax.numpy as jnp
import numpy as np
from jax import lax
from jax.experimental import pallas as pl
from jax.experimental.pallas import tpu as pltpu

f32 = jnp.float32
bf16 = jnp.bfloat16

N_META = 16
EPS = 1e-6
NEG_INF = -1e30
LB_FLOOR = 1e-30
WINDOW = 128
HEAD_DIM_A = 64
N_HEADS_A = 16
N_KV_A = 4
GROUP_A = N_HEADS_A // N_KV_A
N_ROT = 16
ROPE_THETA = 500000.0
H_B = 4
DK_B = 128
H_C = 4
DK_C = 128
RET_THETA = 10000.0
PAST_LEN = 16384

BLK = 128
PAD = BLK - N_META
SUB = 8
SROWS = 16
TM_FFN = 704
VMEM_LIMIT = 56 * 1024 * 1024
VMEM_LIMIT_FFN = 60 * 1024 * 1024

Q_A, K_A, V_A = 0, 1024, 1280
Q_B, F_B, I_B, G_B = 1536, 2048, 2560, 3072
Q_C, K_C, V_C, G_C = 3584, 4096, 4608, 5120
D_IN = 5632
W_A = 1024
LOG2E = 1.4426950408889634
Q_SCALE = HEAD_DIM_A ** -0.5 * LOG2E
NT = (((1,), (1,)), ((), ()))
TN = (((0,), (0,)), ((), ()))


def _dot(a, b, dims=None):
    if dims is None:
        return jnp.dot(a, b, preferred_element_type=f32)
    return lax.dot_general(a, b, dims, preferred_element_type=f32)


def _row_tile(rows, cap, align=8):
    best = align
    for t in range(align, cap + 1, align):
        if rows % t == 0:
            best = t
    return best


def _silu(x):
    return x * jax.nn.sigmoid(x)


def _rms(x, g):
    return x * lax.rsqrt(jnp.mean(x * x, axis=-1, keepdims=True) + EPS) * g


def _inproj_kernel(h_ref, w_ref, o_ref):
    o_ref[...] = _dot(h_ref[...], w_ref[...].astype(bf16))


def _inproj(h, w_all, layer):
    rows, d = h.shape
    n = w_all.shape[2]
    tm = _row_tile(rows, 2176, 16)
    tn = 512
    return pl.pallas_call(
        _inproj_kernel,
        grid=(rows // tm, n // tn),
        in_specs=[pl.BlockSpec((tm, d), lambda i, j: (i, 0)),
                  pl.BlockSpec((None, d, tn), lambda i, j: (layer, 0, j))],
        out_specs=pl.BlockSpec((tm, tn), lambda i, j: (i, j)),
        out_shape=jax.ShapeDtypeStruct((rows, n), f32),
        compiler_params=pltpu.CompilerParams(dimension_semantics=("arbitrary", "arbitrary"),
                                             vmem_limit_bytes=VMEM_LIMIT),
        name="inproj",
    )(h, w_all)


def _embed_kernel(xp_ref, tail_ref, g_ref, x_ref, h_ref):
    i = pl.program_id(0)
    last = pl.num_programs(0) - 1

    def emit(src_ref):
        x = src_ref[...]
        x_ref[...] = x
        h_ref[...] = _rms(x, g_ref[...]).astype(bf16)

    pl.when(i < last)(lambda: emit(xp_ref))
    pl.when(i == last)(lambda: emit(tail_ref))


def _embed(xp, tail, g):
    rows_main, d = xp.shape
    tm = tail.shape[0]
    n_main = rows_main // tm
    rows = rows_main + tm
    return pl.pallas_call(
        _embed_kernel,
        grid=(n_main + 1,),
        in_specs=[pl.BlockSpec((tm, d), lambda i: (jnp.minimum(i, n_main - 1), 0)),
                  pl.BlockSpec((tm, d), lambda i: (0, 0)),
                  pl.BlockSpec((1, d), lambda i: (0, 0))],
        out_specs=[pl.BlockSpec((tm, d), lambda i: (i, 0)),
                   pl.BlockSpec((tm, d), lambda i: (i, 0))],
        out_shape=[jax.ShapeDtypeStruct((rows, d), f32), jax.ShapeDtypeStruct((rows, d), bf16)],
        compiler_params=pltpu.CompilerParams(dimension_semantics=("arbitrary",), vmem_limit_bytes=VMEM_LIMIT),
        name="embed_norm",
    )(xp, tail, g)


def _ffn_kernel(*refs, with_next, tail_rows):
    if with_next:
        x_ref, mix_ref, wo_ref, nf_ref, wg_ref, wu_ref, wd_ref, gn_ref, o_ref, hn_ref = refs
        h_ref = hn_ref
    else:
        x_ref, mix_ref, wo_ref, nf_ref, wg_ref, wu_ref, wd_ref, o_ref, tail_ref, h_ref = refs
    j = pl.program_id(1)

    @pl.when(j == 0)
    def _():
        x1 = x_ref[...] + _dot(mix_ref[...], wo_ref[...])
        h_ref[...] = _rms(x1, nf_ref[...]).astype(bf16)
        o_ref[...] = x1

    h = h_ref[...]
    a = _silu(_dot(h, wg_ref[...])) * _dot(h, wu_ref[...])
    o_ref[...] += _dot(a.astype(bf16), wd_ref[...])

    if with_next:
        @pl.when(j == pl.num_programs(1) - 1)
        def _():
            hn_ref[...] = _rms(o_ref[...], gn_ref[...]).astype(bf16)
    else:
        @pl.when((j == pl.num_programs(1) - 1) & (pl.program_id(0) == pl.num_programs(0) - 1))
        def _():
            tail_ref[...] = o_ref[o_ref.shape[0] - tail_rows:, :]


def _ffn(x, mix, wo, nf, wg, wu, wd, g_next, tm, tail_rows):
    rows, d = x.shape
    dff = wg.shape[1]
    tf = 512
    n_tiles = rows // tm
    with_next = g_next is not None
    row_spec = pl.BlockSpec((tm, d), lambda i, j: (i, 0))
    out_spec = row_spec
    in_specs = [row_spec, row_spec,
                pl.BlockSpec((d, d), lambda i, j: (0, 0), pipeline_mode=pl.Buffered(1)),
                pl.BlockSpec((1, d), lambda i, j: (0, 0)),
                pl.BlockSpec((d, tf), lambda i, j: (0, j)),
                pl.BlockSpec((d, tf), lambda i, j: (0, j)),
                pl.BlockSpec((tf, d), lambda i, j: (j, 0))]
    args = [x, mix, wo, nf, wg, wu, wd]
    if with_next:
        in_specs.append(pl.BlockSpec((1, d), lambda i, j: (0, 0)))
        args.append(g_next)
        out_specs = [out_spec, out_spec]
        out_shape = [jax.ShapeDtypeStruct((rows, d), f32), jax.ShapeDtypeStruct((rows, d), bf16)]
    else:
        assert tail_rows % 8 == 0 and tail_rows <= tm
        out_specs = [out_spec, pl.BlockSpec((tail_rows, d), lambda i, j: (0, 0))]
        out_shape = [jax.ShapeDtypeStruct((rows - tail_rows, d), f32), jax.ShapeDtypeStruct((tail_rows, d), f32)]
    return pl.pallas_call(
        functools.partial(_ffn_kernel, with_next=with_next, tail_rows=tail_rows),
        grid=(n_tiles, dff // tf),
        in_specs=in_specs,
        out_specs=out_specs,
        out_shape=out_shape,
        scratch_shapes=[] if with_next else [pltpu.VMEM((tm, d), bf16)],
        compiler_params=pltpu.CompilerParams(dimension_semantics=("arbitrary", "arbitrary"),
                                             vmem_limit_bytes=VMEM_LIMIT_FFN),
        name="outproj_ffn",
    )(*args)


def _head_norm(x, sel_ref, exp_ref, g):
    n, w = x.shape
    x2 = x * x
    hi = x2.astype(bf16)
    lo = (x2 - hi.astype(f32)).astype(bf16)
    s = _dot(jnp.concatenate([hi, lo], axis=0), sel_ref[0:w, :])
    return x * _dot_split3(lax.rsqrt((s[0:n] + s[n:2 * n]) * (1.0 / HEAD_DIM_A) + EPS), exp_ref[:, 0:w]) * g


def _dot_split3(x, m):
    n = x.shape[0]
    y = _dot(jnp.concatenate(_split3(x), axis=0), m)
    return y[0:n] + y[n:2 * n] + y[2 * n:3 * n]


def _rope_a(x, tab_ref):
    half = N_ROT // 2
    first = lax.broadcasted_iota(jnp.int32, (x.shape[0], 128), 1) % HEAD_DIM_A < half
    out = []
    for t in range(x.shape[1] // 128):
        xt = x[:, 128 * t:128 * (t + 1)]
        partner = jnp.where(first, -pltpu.roll(xt, 128 - half, 1), pltpu.roll(xt, half, 1))
        out.append(xt * tab_ref[0] + partner * tab_ref[1])
    return jnp.concatenate(out, axis=1)


def _rope_r(x, tab_ref):
    return x * tab_ref[0] + pltpu.roll(x, DK_C // 2, 1) * tab_ref[1]


def _forget(fb, lb):
    f = jnp.maximum(lb, LB_FLOOR) + (1.0 - lb) * jax.nn.sigmoid(fb)
    return f, jnp.log2(f)


def _split3(x):
    h1 = x.astype(bf16)
    r1 = x - h1.astype(f32)
    h2 = r1.astype(bf16)
    h3 = (r1 - h2.astype(f32)).astype(bf16)
    return h1, h2, h3


def _mixer_prompt_kernel(sink_ref, gpow_ref, proj_ref, ropea_ref, roper_ref, qn_ref, kn_ref, lb_ref, hn_ref, rn_ref,
                         sel_ref, exp_ref, tri_ref, dsel_ref, dec_ref, rsc_ref, tail_ref,
                         wg_ref, wu_ref, wd_ref, wo_ref,
                         mix_ref, ck_ref, cv_ref, sh_ref, sr_ref, wgb_ref, wub_ref, wdb_ref, wob_ref,
                         kk, vv, sht, srs, qs_s, qb_s, kb_s, g_s):
    i = pl.program_id(1)
    nb = pl.num_programs(1)

    @pl.when(i == 0)
    def _():
        kk[...] = jnp.zeros_like(kk)
        vv[...] = jnp.zeros_like(vv)
        for g in range(N_KV_A):
            vv[:, 256 * g + 128:256 * (g + 1)] = jnp.ones((2 * BLK, 128), bf16)
        sht[...] = jnp.zeros_like(sht)
        srs[...] = jnp.zeros_like(srs)

    wgb_ref[...] = wg_ref[...].astype(bf16)
    wub_ref[...] = wu_ref[...].astype(bf16)
    wdb_ref[...] = wd_ref[...].astype(bf16)
    wob_ref[...] = wo_ref[...].astype(bf16)

    rowabs = i * BLK + lax.broadcasted_iota(jnp.int32, (BLK, 1), 0)
    fgate, logf = _forget(proj_ref[:, F_B:F_B + 512], lb_ref[...])
    logf = jnp.where(rowabs >= PAD, logf, 0.0)
    kb_s[...] = jnp.where(rowabs >= PAD, 1.0 - fgate, 0.0)
    qb_s[...] = _silu(proj_ref[:, Q_B:Q_B + 512])
    l1, l2, l3 = _split3(logf)
    tri = tri_ref[...]
    g_s[...] = _dot(tri, l1) + _dot(tri, l2) + _dot(tri, l3)

    r2 = lax.broadcasted_iota(jnp.int32, (BLK, BLK), 0)
    c2 = lax.broadcasted_iota(jnp.int32, (BLK, BLK), 1)
    lane_lo = c2 < HEAD_DIM_A

    def attention_prep():
        qa = _rope_a(_head_norm(proj_ref[:, Q_A:Q_A + 1024], sel_ref, exp_ref, qn_ref[...]), ropea_ref)
        for t in range(N_HEADS_A // 2):
            xt = qa[:, 128 * t:128 * (t + 1)] * Q_SCALE
            qs_s[256 * t:256 * t + 128, :] = jnp.where(lane_lo, xt, 0.0).astype(bf16)
            qs_s[256 * t + 128:256 * (t + 1), :] = jnp.where(lane_lo, 0.0, xt).astype(bf16)
        k_cur = _rope_a(_head_norm(proj_ref[:, K_A:K_A + 256], sel_ref, exp_ref, kn_ref[...]), ropea_ref)
        v_cur = proj_ref[:, V_A:V_A + 256]
        ck_ref[0] = k_cur
        cv_ref[0] = v_cur
        for t in range(N_KV_A // 2):
            for src, dst, width in ((k_cur, kk, 128), (v_cur, vv, 256)):
                xt = src[:, 128 * t:128 * (t + 1)]
                xs = pltpu.roll(xt, HEAD_DIM_A, 1)
                dst[BLK:2 * BLK, width * 2 * t:width * 2 * t + 128] = jnp.where(lane_lo, xt, xs).astype(bf16)
                dst[BLK:2 * BLK, width * (2 * t + 1):width * (2 * t + 1) + 128] = jnp.where(lane_lo, xs, xt).astype(bf16)

    prow = 2 * BLK
    r4 = lax.broadcasted_iota(jnp.int32, (prow, BLK), 0) % BLK
    c4 = lax.broadcasted_iota(jnp.int32, (prow, BLK), 1)
    up = c4 > r4
    ok = c4 >= jnp.where(up, PAD - (i - 1) * BLK, PAD - i * BLK)

    def attention_pair(t):
        g = (2 * t) // GROUP_A
        s2 = _dot(qs_s[prow * t:prow * (t + 1), :], kk[:, 128 * g:128 * (g + 1)], NT)
        s = jnp.where(ok, jnp.where(up, s2[:, 0:BLK], s2[:, BLK:2 * BLK]), NEG_INF)
        sk = jnp.concatenate([jnp.full((BLK, BLK), sink_ref[2 * t + j] * LOG2E, f32) for j in range(2)], axis=0)
        m = jnp.maximum(jnp.max(s, axis=-1, keepdims=True), sk)
        p = jnp.exp2(s - m)
        p2 = jnp.concatenate([jnp.where(up, p, 0.0), jnp.where(up, 0.0, p)], axis=1).astype(bf16)
        res = _dot(p2, vv[:, 256 * g:256 * (g + 1)])
        y = res[:, 0:128] / (res[:, 128:256] + jnp.exp2(sk - m))
        mix_ref[:, 128 * t:128 * (t + 1)] = jnp.where(lane_lo, y[0:BLK], y[BLK:prow]).astype(bf16)

    levels = []
    bz = BLK
    while bz > SUB:
        levels.append((bz, (r2 % bz) >= bz // 2, None if bz == BLK else (r2 // bz) == (c2 // bz)))
        bz //= 2
    same_sub = (r2 // SUB) == (c2 // SUB)
    sub_ge = [(r2 % SUB) >= s for s in range(SUB)]

    def retention_head(h):
        qc = _rope_r(proj_ref[:, Q_C + 128 * h:Q_C + 128 * (h + 1)], roper_ref)
        kc = _rope_r(proj_ref[:, K_C + 128 * h:K_C + 128 * (h + 1)], roper_ref) * (DK_C ** -0.5)
        vc = proj_ref[:, V_C + 128 * h:V_C + 128 * (h + 1)].astype(bf16)
        qcb = qc.astype(bf16)
        sc = _dot(qcb, kc.astype(bf16), NT) * dec_ref[h]
        oc = _dot(qcb, srs[h].astype(bf16)) * rsc_ref[h] + _dot(sc.astype(bf16), vc)
        srs[h] = gpow_ref[h] * srs[h] + _dot((kc * tail_ref[h]).astype(bf16), vc, TN)
        yc = _rms(oc, rn_ref[...]) * _silu(proj_ref[:, G_C + 128 * h:G_C + 128 * (h + 1)])
        mix_ref[:, W_A + 512 + 128 * h:W_A + 512 + 128 * (h + 1)] = yc.astype(bf16)

    def hgrn_intra(h):
        sl = slice(128 * h, 128 * (h + 1))
        gc = g_s[:, sl]
        qb = qb_s[:, sl]
        kb = kb_s[:, sl]
        inter = _dot((qb * jnp.exp2(gc)).astype(bf16), sht[h].astype(bf16), NT)
        amat = None
        for bz, upper, same in levels:
            gref = jnp.concatenate([jnp.broadcast_to(g_s[pl.ds(st + bz // 2 - 1, 1), sl], (bz, 128))
                                    for st in range(0, BLK, bz)], axis=0)
            dg = gc - gref
            e = jnp.exp2(jnp.where(upper, dg, -dg))
            qt = jnp.where(upper, qb * e, 0.0).astype(bf16)
            kt = jnp.where(upper, 0.0, kb * e).astype(bf16)
            pm = _dot(qt, kt, NT)
            if same is not None:
                pm = jnp.where(same, pm, 0.0)
            amat = pm if amat is None else amat + pm
        ys = []
        for s in range(SUB):
            ks = jnp.concatenate([jnp.broadcast_to(kb_s[pl.ds(SUB * j + s, 1), sl], (SUB, 128))
                                  for j in range(BLK // SUB)], axis=0)
            gs = jnp.concatenate([jnp.broadcast_to(g_s[pl.ds(SUB * j + s, 1), sl], (SUB, 128))
                                  for j in range(BLK // SUB)], axis=0)
            ys.append((qb * ks * jnp.exp2(jnp.where(sub_ge[s], gc - gs, NEG_INF))).astype(bf16))
        amat = amat + jnp.where(same_sub, _dot(jnp.concatenate(ys, axis=1), dsel_ref[...]), 0.0)
        return inter, amat.astype(bf16)

    def hgrn_finish(h, inter, amat):
        sl = slice(128 * h, 128 * (h + 1))
        vb = proj_ref[:, I_B + 128 * h:I_B + 128 * (h + 1)].astype(bf16)
        ob = inter + _dot(amat, vb)
        glast = g_s[pl.ds(BLK - 1, 1), sl]
        kt2 = (kb_s[:, sl] * jnp.exp2(glast - g_s[:, sl])).astype(bf16)
        sht[h] = sht[h] * jnp.exp2(glast) + _dot(vb, kt2, TN)
        yb = _rms(ob, hn_ref[...]) * _silu(proj_ref[:, G_B + 128 * h:G_B + 128 * (h + 1)])
        mix_ref[:, W_A + 128 * h:W_A + 128 * (h + 1)] = yb.astype(bf16)

    attention_prep()
    pending = None
    for h in range(H_B):
        cur = hgrn_intra(h)
        if pending is not None:
            hgrn_finish(h - 1, *pending)
        retention_head(h)
        pending = cur
    hgrn_finish(H_B - 1, *pending)
    for t in range(N_HEADS_A // 2):
        attention_pair(t)
    kk[0:BLK, :] = kk[BLK:2 * BLK, :]
    vv[0:BLK, :] = vv[BLK:2 * BLK, :]

    @pl.when(i == nb - 1)
    def _():
        for h in range(H_B):
            sh_ref[0, h] = sht[h].T
            sr_ref[0, h] = srs[h]


def _const_spec(shape):
    nd = len(shape)
    return pl.BlockSpec(shape, lambda *_: (0,) * nd)


def _slab_rows(nrows, steps):
    for r in range(16, nrows + 1, 16):
        if nrows % r == 0 and nrows // r <= steps:
            return r
    return nrows


def _mixer_prompt(proj, rows_total, batch, nb, layer, sinks, gpow, ropea, roper, qn, kn, lb, hn, rn, sel, expand,
                  tri, dsel, dec, rsc, tail, w_gate, w_up, w_down, w_out):
    smem = pl.BlockSpec(memory_space=pltpu.SMEM)
    d, dff = w_gate.shape[1], w_gate.shape[2]

    def row_blk(b, i):
        return jnp.where(i == 0, batch * (nb - 1), b * (nb - 1) + i - 1)

    def mix_blk(b, i):
        return jnp.where((i == 0) & (b > 0), b * (nb - 1), row_blk(b, i))

    def slab_in(nrows, ncols):
        r = _slab_rows(nrows, batch * nb)
        return pl.BlockSpec((None, r, ncols), lambda b, i: (layer, jnp.minimum(b * nb + i, nrows // r - 1), 0))

    def slab_out(nrows, ncols):
        r = _slab_rows(nrows, batch * nb)
        return pl.BlockSpec((r, ncols), lambda b, i: (jnp.minimum(b * nb + i, nrows // r - 1), 0))

    return pl.pallas_call(
        _mixer_prompt_kernel,
        grid=(batch, nb),
        in_specs=[smem, smem,
                  pl.BlockSpec((BLK, D_IN), lambda b, i: (row_blk(b, i), 0)),
                  pl.BlockSpec((2, BLK, 128), lambda b, i: (0, i, 0)),
                  pl.BlockSpec((2, BLK, 128), lambda b, i: (0, i, 0)),
                  _const_spec((1, 1024)), _const_spec((1, 256)), _const_spec((1, 512)),
                  _const_spec((1, 128)), _const_spec((1, 128)),
                  _const_spec((1024, 128)), _const_spec((128, 1024)),
                  _const_spec((BLK, BLK)), _const_spec((SUB * 128, 128)),
                  _const_spec((H_C, BLK, BLK)), _const_spec((H_C, BLK, 128)), _const_spec((H_C, BLK, 128)),
                  slab_in(d, dff), slab_in(d, dff), slab_in(dff, d), slab_in(d, d)],
        out_specs=[pl.BlockSpec((BLK, 2048), lambda b, i: (mix_blk(b, i), 0)),
                   pl.BlockSpec((1, BLK, 256), lambda b, i: (b, 0, 0)),
                   pl.BlockSpec((1, BLK, 256), lambda b, i: (b, 0, 0)),
                   pl.BlockSpec((1, H_B, 128, 128), lambda b, i: (b, 0, 0, 0)),
                   pl.BlockSpec((1, H_C, 128, 128), lambda b, i: (b, 0, 0, 0)),
                   slab_out(d, dff), slab_out(d, dff), slab_out(dff, d), slab_out(d, d)],
        out_shape=[jax.ShapeDtypeStruct((rows_total, 2048), bf16),
                   jax.ShapeDtypeStruct((batch, BLK, 256), f32),
                   jax.ShapeDtypeStruct((batch, BLK, 256), f32),
                   jax.ShapeDtypeStruct((batch, H_B, 128, 128), f32),
                   jax.ShapeDtypeStruct((batch, H_C, 128, 128), f32),
                   jax.ShapeDtypeStruct((d, dff), bf16), jax.ShapeDtypeStruct((d, dff), bf16),
                   jax.ShapeDtypeStruct((dff, d), bf16), jax.ShapeDtypeStruct((d, d), bf16)],
        scratch_shapes=[pltpu.VMEM((2 * BLK, N_KV_A * 128), bf16), pltpu.VMEM((2 * BLK, N_KV_A * 256), bf16),
                        pltpu.VMEM((H_B, 128, 128), f32), pltpu.VMEM((H_C, 128, 128), f32),
                        pltpu.VMEM((N_HEADS_A * BLK, 128), bf16),
                        pltpu.VMEM((BLK, 512), f32), pltpu.VMEM((BLK, 512), f32), pltpu.VMEM((BLK, 512), f32)],
        compiler_params=pltpu.CompilerParams(dimension_semantics=("arbitrary", "arbitrary"),
                                             vmem_limit_bytes=VMEM_LIMIT),
        name="mixer_prompt",
    )(sinks, gpow, proj, ropea, roper, qn, kn, lb, hn, rn, sel, expand, tri, dsel, dec, rsc, tail,
      w_gate, w_up, w_down, w_out)


def _mixer_sample_kernel(sink_ref, gpow_ref, mixin_ref, *refs, dec_seq, n_prev):
    del mixin_ref
    _mixer_sample_body(sink_ref, gpow_ref, *refs[:17], *refs[17 + n_prev:], dec_seq=dec_seq)


def _mixer_sample_body(sink_ref, gpow_ref, proj_ref, ropea_ref, roper_ref, qn_ref, kn_ref, lb_ref,
                       hn_ref, rn_ref, sel_ref, exp_ref, dec_ref, rsc_ref, tail_ref,
                       ck_ref, cv_ref, sh_ref, sr_ref,
                       mix_ref, nk_ref, nv_ref, nsh_ref, nsr_ref, *, dec_seq):
    nbat = SROWS // dec_seq
    w = ck_ref.shape[1]
    rb = lax.broadcasted_iota(jnp.int32, (SROWS, 1), 0) // dec_seq
    rt = lax.broadcasted_iota(jnp.int32, (SROWS, 1), 0) % dec_seq

    def pick(parts):
        out = parts[nbat - 1]
        for b in range(nbat - 2, -1, -1):
            out = jnp.where(rb == b, parts[b], out)
        return out

    fgate, logf = _forget(proj_ref[:, F_B:F_B + 512], lb_ref[...])
    kb = 1.0 - fgate
    qb = _silu(proj_ref[:, Q_B:Q_B + 512])
    vb = proj_ref[:, I_B:I_B + 512]
    gcum = logf
    for d in range(1, dec_seq):
        gcum = gcum + jnp.where(rt >= d, pltpu.roll(logf, d, 0), 0.0)
    intra = [jnp.zeros((SROWS, 128), f32) for _ in range(H_B)]
    for s in range(SROWS):
        ok = (rb == s // dec_seq) & (rt >= s % dec_seq)
        y = qb * kb[s:s + 1, :] * jnp.exp2(jnp.where(ok, gcum - gcum[s:s + 1, :], NEG_INF))
        for h in range(H_B):
            sl = slice(128 * h, 128 * (h + 1))
            intra[h] = intra[h] + jnp.sum(y[:, sl], axis=-1, keepdims=True) * vb[s:s + 1, sl]

    qa = _rope_a(_head_norm(proj_ref[:, Q_A:Q_A + 1024], sel_ref, exp_ref, qn_ref[...]), ropea_ref)
    qs = (qa * Q_SCALE).astype(bf16)
    k_new = _rope_a(_head_norm(proj_ref[:, K_A:K_A + 256], sel_ref, exp_ref, kn_ref[...]), ropea_ref)
    v_new = proj_ref[:, V_A:V_A + 256]
    for b in range(nbat):
        nk_ref[b, 0:w - dec_seq, :] = ck_ref[b, dec_seq:w, :]
        nk_ref[b, w - dec_seq:w, :] = k_new[dec_seq * b:dec_seq * (b + 1), :]
        nv_ref[b, 0:w - dec_seq, :] = cv_ref[b, dec_seq:w, :]
        nv_ref[b, w - dec_seq:w, :] = v_new[dec_seq * b:dec_seq * (b + 1), :]
    knb = k_new.astype(bf16)
    vnb = v_new.astype(bf16)

    grows = GROUP_A * SROWS
    rt_g = jnp.concatenate([rt] * GROUP_A, axis=0)
    rb_g = jnp.concatenate([rb] * GROUP_A, axis=0)
    jc = lax.broadcasted_iota(jnp.int32, (grows, w), 1)
    valid_c = jc > rt_g + (w - WINDOW)
    cn = lax.broadcasted_iota(jnp.int32, (grows, SROWS), 1)
    valid_n = (cn // dec_seq == rb_g) & (cn % dec_seq <= rt_g)
    heads = [None] * N_HEADS_A
    for g in range(N_KV_A):
        ksl = slice(64 * g, 64 * (g + 1))
        qg = jnp.concatenate([qs[:, 64 * h:64 * (h + 1)] for h in range(GROUP_A * g, GROUP_A * (g + 1))], axis=0)
        sparts = [_dot(qg, ck_ref[b, :, ksl].astype(bf16), NT) for b in range(nbat)]
        s_c = sparts[nbat - 1]
        for b in range(nbat - 2, -1, -1):
            s_c = jnp.where(rb_g == b, sparts[b], s_c)
        s_c = jnp.where(valid_c, s_c, NEG_INF)
        s_n = jnp.where(valid_n, _dot(qg, knb[:, ksl], NT), NEG_INF)
        sk = jnp.concatenate([jnp.full((SROWS, 1), sink_ref[h] * LOG2E, f32)
                              for h in range(GROUP_A * g, GROUP_A * (g + 1))], axis=0)
        m = jnp.maximum(jnp.maximum(jnp.max(s_c, axis=-1, keepdims=True), jnp.max(s_n, axis=-1, keepdims=True)), sk)
        p_c = jnp.exp2(s_c - m)
        p_n = jnp.exp2(s_n - m)
        den = jnp.sum(p_c, axis=-1, keepdims=True) + jnp.sum(p_n, axis=-1, keepdims=True) + jnp.exp2(sk - m)
        pcb = p_c.astype(bf16)
        oparts = [_dot(pcb, cv_ref[b, :, ksl].astype(bf16)) for b in range(nbat)]
        o = oparts[nbat - 1]
        for b in range(nbat - 2, -1, -1):
            o = jnp.where(rb_g == b, oparts[b], o)
        o = (o + _dot(p_n.astype(bf16), vnb[:, ksl])) / den
        for hh in range(GROUP_A):
            heads[GROUP_A * g + hh] = o[SROWS * hh:SROWS * (hh + 1), :]
    mix_ref[:, 0:W_A] = jnp.concatenate(heads, axis=1).astype(bf16)

    qg = (qb * jnp.exp2(gcum)).astype(bf16)
    vbb = vb.astype(bf16)
    glast = [gcum[dec_seq * (b + 1) - 1:dec_seq * (b + 1), :] for b in range(nbat)]
    for h in range(H_B):
        sl = slice(128 * h, 128 * (h + 1))
        ob = pick([_dot(qg[:, sl], sh_ref[b, h].astype(bf16)) for b in range(nbat)]) + intra[h]
        for b in range(nbat):
            kt2 = jnp.where(rb == b, kb[:, sl] * jnp.exp2(glast[b][:, sl] - gcum[:, sl]), 0.0).astype(bf16)
            dcol = jnp.broadcast_to(jnp.exp2(glast[b][:, sl]), (128, 128)).T
            nsh_ref[b, h] = dcol * sh_ref[b, h] + _dot(kt2, vbb[:, sl], TN)
        yb = _rms(ob, hn_ref[...]) * _silu(proj_ref[:, G_B + 128 * h:G_B + 128 * (h + 1)])
        mix_ref[:, W_A + 128 * h:W_A + 128 * (h + 1)] = yb.astype(bf16)

    for h in range(H_C):
        qc = _rope_r(proj_ref[:, Q_C + 128 * h:Q_C + 128 * (h + 1)], roper_ref)
        kc = _rope_r(proj_ref[:, K_C + 128 * h:K_C + 128 * (h + 1)], roper_ref) * (DK_C ** -0.5)
        vc = proj_ref[:, V_C + 128 * h:V_C + 128 * (h + 1)]
        qcb = qc.astype(bf16)
        vcb = vc.astype(bf16)
        sc = _dot(qcb, kc.astype(bf16), NT) * dec_ref[h]
        oc = pick([_dot(qcb, sr_ref[b, h].astype(bf16)) for b in range(nbat)]) * rsc_ref[h] \
            + _dot(sc.astype(bf16), vcb)
        kct = kc * tail_ref[h]
        for b in range(nbat):
            nsr_ref[b, h] = gpow_ref[h] * sr_ref[b, h] + _dot(jnp.where(rb == b, kct, 0.0).astype(bf16), vcb, TN)
        yc = _rms(oc, rn_ref[...]) * _silu(proj_ref[:, G_C + 128 * h:G_C + 128 * (h + 1)])
        mix_ref[:, W_A + 512 + 128 * h:W_A + 512 + 128 * (h + 1)] = yc.astype(bf16)


def _mixer_sample(mix, proj, row0, layer, dec_batch, dec_seq, sinks, gpow, ropea, roper, qn, kn, lb, hn, rn, sel, expand,
                  dec, rsc, tail, cache_k, cache_v, state_h, state_r, prev_outs):
    smem = pl.BlockSpec(memory_space=pltpu.SMEM)
    nbat = SROWS // dec_seq
    steps = dec_batch // nbat
    blk0 = row0 // SROWS
    w = cache_k.shape[1]
    n_all = cache_k.shape[0]
    assert mix.shape[0] == row0 + steps * SROWS
    any_spec = pl.BlockSpec(memory_space=pl.ANY)
    n_in = 20
    return pl.pallas_call(
        functools.partial(_mixer_sample_kernel, dec_seq=dec_seq, n_prev=len(prev_outs)),
        grid=(steps,),
        in_specs=[smem, smem,
                  pl.BlockSpec(memory_space=pl.ANY),
                  pl.BlockSpec((SROWS, D_IN), lambda c: (blk0 + c, 0)),
                  _const_spec((2, SROWS, 128)), _const_spec((2, SROWS, 128)),
                  _const_spec((1, 1024)), _const_spec((1, 256)), _const_spec((1, 512)),
                  _const_spec((1, 128)), _const_spec((1, 128)),
                  _const_spec((1024, 128)), _const_spec((128, 1024)),
                  _const_spec((H_C, SROWS, SROWS)), _const_spec((H_C, SROWS, 128)), _const_spec((H_C, SROWS, 128)),
                  pl.BlockSpec((nbat, w, 256), lambda c: (layer * steps + c, 0, 0)),
                  pl.BlockSpec((nbat, w, 256), lambda c: (layer * steps + c, 0, 0)),
                  pl.BlockSpec((nbat, H_B, 128, 128), lambda c: (layer * steps + c, 0, 0, 0)),
                  pl.BlockSpec((nbat, H_C, 128, 128), lambda c: (layer * steps + c, 0, 0, 0))]
        + [any_spec] * len(prev_outs),
        out_specs=[pl.BlockSpec((SROWS, 2048), lambda c: (blk0 + c, 0)),
                   pl.BlockSpec((nbat, w, 256), lambda c: (layer * steps + c, 0, 0)),
                   pl.BlockSpec((nbat, w, 256), lambda c: (layer * steps + c, 0, 0)),
                   pl.BlockSpec((nbat, H_B, 128, 128), lambda c: (layer * steps + c, 0, 0, 0)),
                   pl.BlockSpec((nbat, H_C, 128, 128), lambda c: (layer * steps + c, 0, 0, 0))],
        out_shape=[jax.ShapeDtypeStruct(mix.shape, bf16),
                   jax.ShapeDtypeStruct((n_all, w, 256), f32),
                   jax.ShapeDtypeStruct((n_all, w, 256), f32),
                   jax.ShapeDtypeStruct((n_all, H_B, 128, 128), f32),
                   jax.ShapeDtypeStruct((n_all, H_C, 128, 128), f32)],
        input_output_aliases={2: 0, **{n_in + k: 1 + k for k in range(len(prev_outs))}},
        compiler_params=pltpu.CompilerParams(dimension_semantics=("arbitrary",), vmem_limit_bytes=VMEM_LIMIT),
        name="mixer_sample",
    )(sinks, gpow, mix, proj, ropea, roper, qn, kn, lb, hn, rn, sel, expand, dec, rsc, tail,
      cache_k, cache_v, state_h, state_r, *prev_outs)


def _rope_tables(pos):
    posf = jnp.asarray(pos).astype(f32)[:, None]
    t = pos.shape[0]
    half = N_ROT // 2
    inv = ROPE_THETA ** (-jnp.arange(half, dtype=f32) * (2.0 / N_ROT))
    ang = posf * inv[None, :]
    cos, sin = jnp.cos(ang), jnp.sin(ang)
    rest0 = jnp.zeros((t, HEAD_DIM_A - N_ROT), f32)
    c64 = jnp.concatenate([cos, cos, rest0 + 1.0], axis=1)
    s64 = jnp.concatenate([sin, sin, rest0], axis=1)
    ropea = jnp.stack([jnp.tile(c64, (1, 2)), jnp.tile(s64, (1, 2))])
    invr = RET_THETA ** (-jnp.arange(DK_C // 2, dtype=f32) * (2.0 / DK_C))
    angr = posf * invr[None, :]
    cr, sr = jnp.cos(angr), jnp.sin(angr)
    roper = jnp.stack([jnp.concatenate([cr, cr], axis=1), jnp.concatenate([-sr, sr], axis=1)])
    return ropea, roper


def _ret_tables(lg, seq_of_row, tok_of_row, length):
    nf = np.float32
    tq = tok_of_row.astype(nf)
    rel = tq[:, None] - tq[None, :]
    ok = (seq_of_row[:, None] == seq_of_row[None, :]) & (rel >= 0)
    dec = np.where(ok[None], np.exp(np.where(ok, rel, nf(0.0))[None] * lg[:, None, None]), nf(0.0))
    n = tq.shape[0]
    rsc = np.broadcast_to(np.exp((tq + nf(1.0))[None, :, None] * lg[:, None, None]), (H_C, n, 128))
    tail = np.broadcast_to(np.exp((nf(length) - nf(1.0) - tq)[None, :, None] * lg[:, None, None]), (H_C, n, 128))
    gpow = np.exp(nf(length) * lg)
    return dec.astype(nf), np.ascontiguousarray(rsc, nf), np.ascontiguousarray(tail, nf), gpow.astype(nf)


def kernel(x_prompt, x_sample, cache_k, cache_v, state_hgrn, state_ret, meta_tokens, norm_mix, norm_ffn, w_in, q_norm, k_norm, attn_sinks, hgrn_lb, hgrn_norm, ret_norm, w_out, w_gate, w_up, w_down):
    batch, seq, d = x_prompt.shape
    dec_batch, dec_seq, _ = x_sample.shape
    depth = w_in.shape[0]
    w = cache_k.shape[2]
    assert d == 2048 and w_in.shape[2] == D_IN and seq % BLK == 0
    assert SROWS % dec_seq == 0 and dec_batch % (SROWS // dec_seq) == 0 and w == WINDOW
    nb = seq // BLK + 1
    lp = nb * BLK
    rows_main = batch * seq
    rows_s = dec_batch * dec_seq
    row_s0 = rows_main + BLK
    tail_rows = BLK + rows_s
    rows = rows_main + tail_rows
    tm_ffn = _row_tile(rows, TM_FFN, 16)
    assert rows_main % tail_rows == 0 and rows_s % SROWS == 0 and tail_rows % 16 == 0 and tm_ffn >= tail_rows

    tail = jnp.concatenate([jnp.zeros((PAD, d), f32), meta_tokens.astype(f32), x_sample.reshape(rows_s, d)], axis=0)
    x, h = _embed(x_prompt.reshape(rows_main, d), tail, norm_mix[0][None])

    p = jax.nn.softmax(hgrn_lb.astype(f32), axis=0)
    lbs = jnp.cumsum(p, axis=0) - p[0]
    lg = np.log1p(-np.exp2(np.float32(-5.0) - np.arange(H_C, dtype=np.float32))).astype(np.float32)
    ropea_p, roper_p = _rope_tables(np.arange(lp) - PAD)
    srow = np.arange(SROWS)
    ropea_s, roper_s = _rope_tables(PAST_LEN + srow % dec_seq)
    dec_p, rsc_p, tail_p, gpow_p = _ret_tables(lg, np.zeros((BLK,), np.int32), np.arange(BLK), float(BLK))
    dec_s, rsc_s, tail_s, gpow_s = _ret_tables(lg, srow // dec_seq, srow % dec_seq, float(dec_seq))
    sel_np = np.arange(1024)[:, None] // HEAD_DIM_A == np.arange(128)[None, :]
    sel = jnp.asarray(sel_np, bf16)
    expand = jnp.asarray(sel_np.T, bf16)
    tri = jnp.asarray(np.arange(BLK)[:, None] >= np.arange(BLK)[None, :], bf16)
    dsel = jnp.asarray(np.arange(SUB * 128)[:, None] // 128 == np.arange(128)[None, :] % SUB, bf16)

    ck_flat = cache_k.reshape(depth * dec_batch, w, N_KV_A * HEAD_DIM_A)
    cv_flat = cache_v.reshape(depth * dec_batch, w, N_KV_A * HEAD_DIM_A)
    sh_flat = state_hgrn.reshape(depth * dec_batch, H_B, 128, 128)
    sr_flat = state_ret.reshape(depth * dec_batch, H_C, 128, 128)

    outs_p, outs_s = [], ()
    for l in range(depth):
        proj = _inproj(h, w_in, l)
        qn = jnp.tile(q_norm[l], N_HEADS_A)[None]
        kn = jnp.tile(k_norm[l], N_KV_A)[None]
        common = (qn, kn, lbs[l][None], hgrn_norm[l][None], ret_norm[l][None], sel, expand)
        mix, ck, cv, sh, sr, wg_b, wu_b, wd_b, wo_b = _mixer_prompt(
            proj, rows, batch, nb, l, attn_sinks[l], gpow_p, ropea_p, roper_p, *common, tri, dsel,
            dec_p, rsc_p, tail_p, w_gate, w_up, w_down, w_out)
        mix, *outs_s = _mixer_sample(mix, proj, row_s0, l, dec_batch, dec_seq, attn_sinks[l], gpow_s,
                                     ropea_s, roper_s, *common, dec_s, rsc_s, tail_s,
                                     ck_flat, cv_flat, sh_flat, sr_flat, tuple(outs_s))
        ffn_w = (wo_b, norm_ffn[l][None], wg_b, wu_b, wd_b)
        outs_p.append((ck, cv, sh, sr))
        if l + 1 < depth:
            x, h = _ffn(x, mix, *ffn_w, norm_mix[l + 1][None], tm_ffn, tail_rows)
        else:
            y_main, y_tail = _ffn(x, mix, *ffn_w, None, tm_ffn, tail_rows)

    y_prompt = y_main.reshape(batch, seq, d)
    y_sample = y_tail[BLK:].reshape(dec_batch, dec_seq, d)
    kv_shape = (depth, -1, w, N_KV_A, HEAD_DIM_A)
    st_shape = (depth, dec_batch, H_B, 128, 128)
    stack = lambda outs, k: jnp.stack([o[k] for o in outs])
    nk, nv, nsh, nsr = outs_s
    return (y_prompt, y_sample,
            stack(outs_p, 0).reshape(kv_shape), stack(outs_p, 1).reshape(kv_shape), stack(outs_p, 2), stack(outs_p, 3),
            nk.reshape(kv_shape), nv.reshape(kv_shape), nsh.reshape(st_shape), nsr.reshape(st_shape))
```

```python
import functools

import jax
import jax.numpy as jnp
import numpy as np
from jax import lax
from jax.experimental import pallas as pl
from jax.experimental.pallas import tpu as pltpu

f32 = jnp.float32
bf16 = jnp.bfloat16

N_META = 16
EPS = 1e-6
NEG_INF = -1e30
LB_FLOOR = 1e-30
WINDOW = 128
HEAD_DIM_A = 64
N_HEADS_A = 16
N_KV_A = 4
GROUP_A = N_HEADS_A // N_KV_A
N_ROT = 16
ROPE_THETA = 500000.0
H_B = 4
DK_B = 128
H_C = 4
DK_C = 128
RET_THETA = 10000.0
PAST_LEN = 16384

BLK = 128
PAD = BLK - N_META
SUB = 8
SROWS = 32
TM_FFN = 704
VMEM_LIMIT = 56 * 1024 * 1024
VMEM_LIMIT_FFN = 60 * 1024 * 1024

Q_A, K_A, V_A = 0, 1024, 1280
Q_B, F_B, I_B, G_B = 1536, 2048, 2560, 3072
Q_C, K_C, V_C, G_C = 3584, 4096, 4608, 5120
D_IN = 5632
W_A = 1024
LOG2E = 1.4426950408889634
Q_SCALE = HEAD_DIM_A ** -0.5 * LOG2E
NT = (((1,), (1,)), ((), ()))
TN = (((0,), (0,)), ((), ()))


def _dot(a, b, dims=None):
    if dims is None:
        return jnp.dot(a, b, preferred_element_type=f32)
    return lax.dot_general(a, b, dims, preferred_element_type=f32)


def _row_tile(rows, cap, align=8):
    best = align
    for t in range(align, cap + 1, align):
        if rows % t == 0:
            best = t
    return best


def _silu(x):
    return x * jax.nn.sigmoid(x)


def _rms(x, g):
    return x * lax.rsqrt(jnp.mean(x * x, axis=-1, keepdims=True) + EPS) * g


def _inproj_kernel(h_ref, w_ref, o_ref):
    o_ref[...] = _dot(h_ref[...], w_ref[...].astype(bf16))


def _inproj(h, w_all, layer):
    rows, d = h.shape
    n = w_all.shape[2]
    tm = _row_tile(rows, 2816, 16)
    tn = 512
    return pl.pallas_call(
        _inproj_kernel,
        grid=(rows // tm, n // tn),
        in_specs=[pl.BlockSpec((tm, d), lambda i, j: (i, 0)),
                  pl.BlockSpec((None, d, tn), lambda i, j: (layer, 0, j))],
        out_specs=pl.BlockSpec((tm, tn), lambda i, j: (i, j)),
        out_shape=jax.ShapeDtypeStruct((rows, n), f32),
        compiler_params=pltpu.CompilerParams(dimension_semantics=("arbitrary", "arbitrary"),
                                             vmem_limit_bytes=VMEM_LIMIT),
        name="inproj",
    )(h, w_all)


def _embed_kernel(xp_ref, tail_ref, g_ref, x_ref, h_ref):
    i = pl.program_id(0)
    last = pl.num_programs(0) - 1

    def emit(src_ref):
        x = src_ref[...]
        x_ref[...] = x
        h_ref[...] = _rms(x, g_ref[...]).astype(bf16)

    pl.when(i < last)(lambda: emit(xp_ref))
    pl.when(i == last)(lambda: emit(tail_ref))


def _embed(xp, tail, g):
    rows_main, d = xp.shape
    tm = tail.shape[0]
    n_main = rows_main // tm
    rows = rows_main + tm
    return pl.pallas_call(
        _embed_kernel,
        grid=(n_main + 1,),
        in_specs=[pl.BlockSpec((tm, d), lambda i: (jnp.minimum(i, n_main - 1), 0)),
                  pl.BlockSpec((tm, d), lambda i: (0, 0)),
                  pl.BlockSpec((1, d), lambda i: (0, 0))],
        out_specs=[pl.BlockSpec((tm, d), lambda i: (i, 0)),
                   pl.BlockSpec((tm, d), lambda i: (i, 0))],
        out_shape=[jax.ShapeDtypeStruct((rows, d), f32), jax.ShapeDtypeStruct((rows, d), bf16)],
        compiler_params=pltpu.CompilerParams(dimension_semantics=("arbitrary",), vmem_limit_bytes=VMEM_LIMIT),
        name="embed_norm",
    )(xp, tail, g)


def _ffn_kernel(*refs, with_next, tail_rows):
    if with_next:
        x_ref, mix_ref, wo_ref, nf_ref, wg_ref, wu_ref, wd_ref, gn_ref, o_ref, hn_ref = refs
        h_ref = hn_ref
    else:
        x_ref, mix_ref, wo_ref, nf_ref, wg_ref, wu_ref, wd_ref, o_ref, tail_ref, h_ref = refs
    j = pl.program_id(1)

    @pl.when(j == 0)
    def _():
        x1 = x_ref[...] + _dot(mix_ref[...], wo_ref[...])
        h_ref[...] = _rms(x1, nf_ref[...]).astype(bf16)
        o_ref[...] = x1

    h = h_ref[...]
    a = _silu(_dot(h, wg_ref[...])) * _dot(h, wu_ref[...])
    o_ref[...] += _dot(a.astype(bf16), wd_ref[...])

    if with_next:
        @pl.when(j == pl.num_programs(1) - 1)
        def _():
            hn_ref[...] = _rms(o_ref[...], gn_ref[...]).astype(bf16)
    else:
        @pl.when((j == pl.num_programs(1) - 1) & (pl.program_id(0) == pl.num_programs(0) - 1))
        def _():
            tail_ref[...] = o_ref[o_ref.shape[0] - tail_rows:, :]


def _ffn(x, mix, wo, nf, wg, wu, wd, g_next, tm, tail_rows):
    rows, d = x.shape
    dff = wg.shape[1]
    tf = 512
    n_tiles = rows // tm
    with_next = g_next is not None
    row_spec = pl.BlockSpec((tm, d), lambda i, j: (i, 0))
    out_spec = row_spec
    in_specs = [row_spec, row_spec,
                pl.BlockSpec((d, d), lambda i, j: (0, 0), pipeline_mode=pl.Buffered(1)),
                pl.BlockSpec((1, d), lambda i, j: (0, 0)),
                pl.BlockSpec((d, tf), lambda i, j: (0, j)),
                pl.BlockSpec((d, tf), lambda i, j: (0, j)),
                pl.BlockSpec((tf, d), lambda i, j: (j, 0))]
    args = [x, mix, wo, nf, wg, wu, wd]
    if with_next:
        in_specs.append(pl.BlockSpec((1, d), lambda i, j: (0, 0)))
        args.append(g_next)
        out_specs = [out_spec, out_spec]
        out_shape = [jax.ShapeDtypeStruct((rows, d), f32), jax.ShapeDtypeStruct((rows, d), bf16)]
    else:
        assert tail_rows % 8 == 0 and tail_rows <= tm
        out_specs = [out_spec, pl.BlockSpec((tail_rows, d), lambda i, j: (0, 0))]
        out_shape = [jax.ShapeDtypeStruct((rows - tail_rows, d), f32), jax.ShapeDtypeStruct((tail_rows, d), f32)]
    return pl.pallas_call(
        functools.partial(_ffn_kernel, with_next=with_next, tail_rows=tail_rows),
        grid=(n_tiles, dff // tf),
        in_specs=in_specs,
        out_specs=out_specs,
        out_shape=out_shape,
        scratch_shapes=[] if with_next else [pltpu.VMEM((tm, d), bf16)],
        compiler_params=pltpu.CompilerParams(dimension_semantics=("arbitrary", "arbitrary"),
                                             vmem_limit_bytes=VMEM_LIMIT_FFN),
        name="outproj_ffn",
    )(*args)


def _head_norm(x, sel_ref, exp_ref, g):
    n, w = x.shape
    x2 = x * x
    hi = x2.astype(bf16)
    lo = (x2 - hi.astype(f32)).astype(bf16)
    s = _dot(jnp.concatenate([hi, lo], axis=0), sel_ref[0:w, :])
    return x * _dot_split3(lax.rsqrt((s[0:n] + s[n:2 * n]) * (1.0 / HEAD_DIM_A) + EPS), exp_ref[:, 0:w]) * g


def _dot_split3(x, m):
    n = x.shape[0]
    y = _dot(jnp.concatenate(_split3(x), axis=0), m)
    return y[0:n] + y[n:2 * n] + y[2 * n:3 * n]


def _rope_a(x, tab_ref):
    half = N_ROT // 2
    first = lax.broadcasted_iota(jnp.int32, (x.shape[0], 128), 1) % HEAD_DIM_A < half
    out = []
    for t in range(x.shape[1] // 128):
        xt = x[:, 128 * t:128 * (t + 1)]
        partner = jnp.where(first, -pltpu.roll(xt, 128 - half, 1), pltpu.roll(xt, half, 1))
        out.append(xt * tab_ref[0] + partner * tab_ref[1])
    return jnp.concatenate(out, axis=1)


def _rope_r(x, tab_ref):
    return x * tab_ref[0] + pltpu.roll(x, DK_C // 2, 1) * tab_ref[1]


def _forget(fb, lb):
    f = jnp.maximum(lb, LB_FLOOR) + (1.0 - lb) * jax.nn.sigmoid(fb)
    return f, jnp.log2(f)


def _split3(x):
    h1 = x.astype(bf16)
    r1 = x - h1.astype(f32)
    h2 = r1.astype(bf16)
    h3 = (r1 - h2.astype(f32)).astype(bf16)
    return h1, h2, h3


def _mixer_prompt_kernel(sink_ref, gpow_ref, proj_ref, ropea_ref, roper_ref, qn_ref, kn_ref, lb_ref, hn_ref, rn_ref,
                         sel_ref, exp_ref, tri_ref, dsel_ref, dec_ref, rsc_ref, tail_ref,
                         wg_ref, wu_ref, wd_ref, wo_ref,
                         mix_ref, ck_ref, cv_ref, sh_ref, sr_ref, wgb_ref, wub_ref, wdb_ref, wob_ref,
                         kk, vv, sht, srs, qs_s, qb_s, kb_s, g_s):
    i = pl.program_id(1)
    nb = pl.num_programs(1)

    @pl.when(i == 0)
    def _():
        kk[...] = jnp.zeros_like(kk)
        vv[...] = jnp.zeros_like(vv)
        for g in range(N_KV_A):
            vv[:, 256 * g + 128:256 * (g + 1)] = jnp.ones((2 * BLK, 128), bf16)
        sht[...] = jnp.zeros_like(sht)
        srs[...] = jnp.zeros_like(srs)

    wgb_ref[...] = wg_ref[...].astype(bf16)
    wub_ref[...] = wu_ref[...].astype(bf16)
    wdb_ref[...] = wd_ref[...].astype(bf16)
    wob_ref[...] = wo_ref[...].astype(bf16)

    rowabs = i * BLK + lax.broadcasted_iota(jnp.int32, (BLK, 1), 0)
    fgate, logf = _forget(proj_ref[:, F_B:F_B + 512], lb_ref[...])
    logf = jnp.where(rowabs >= PAD, logf, 0.0)
    kb_s[...] = jnp.where(rowabs >= PAD, 1.0 - fgate, 0.0)
    qb_s[...] = _silu(proj_ref[:, Q_B:Q_B + 512])
    l1, l2, l3 = _split3(logf)
    tri = tri_ref[...]
    g_s[...] = _dot(tri, l1) + _dot(tri, l2) + _dot(tri, l3)

    r2 = lax.broadcasted_iota(jnp.int32, (BLK, BLK), 0)
    c2 = lax.broadcasted_iota(jnp.int32, (BLK, BLK), 1)
    lane_lo = c2 < HEAD_DIM_A

    def attention_prep():
        qa = _rope_a(_head_norm(proj_ref[:, Q_A:Q_A + 1024], sel_ref, exp_ref, qn_ref[...]), ropea_ref)
        for t in range(N_HEADS_A // 2):
            xt = qa[:, 128 * t:128 * (t + 1)] * Q_SCALE
            qs_s[256 * t:256 * t + 128, :] = jnp.where(lane_lo, xt, 0.0).astype(bf16)
            qs_s[256 * t + 128:256 * (t + 1), :] = jnp.where(lane_lo, 0.0, xt).astype(bf16)
        k_cur = _rope_a(_head_norm(proj_ref[:, K_A:K_A + 256], sel_ref, exp_ref, kn_ref[...]), ropea_ref)
        v_cur = proj_ref[:, V_A:V_A + 256]
        ck_ref[0] = k_cur
        cv_ref[0] = v_cur
        for t in range(N_KV_A // 2):
            for src, dst, width in ((k_cur, kk, 128), (v_cur, vv, 256)):
                xt = src[:, 128 * t:128 * (t + 1)]
                xs = pltpu.roll(xt, HEAD_DIM_A, 1)
                dst[BLK:2 * BLK, width * 2 * t:width * 2 * t + 128] = jnp.where(lane_lo, xt, xs).astype(bf16)
                dst[BLK:2 * BLK, width * (2 * t + 1):width * (2 * t + 1) + 128] = jnp.where(lane_lo, xs, xt).astype(bf16)

    prow = 2 * BLK
    r4 = lax.broadcasted_iota(jnp.int32, (prow, BLK), 0) % BLK
    c4 = lax.broadcasted_iota(jnp.int32, (prow, BLK), 1)
    up = c4 > r4
    ok = c4 >= jnp.where(up, PAD - (i - 1) * BLK, PAD - i * BLK)

    def attention_pair(t):
        g = (2 * t) // GROUP_A
        s2 = _dot(qs_s[prow * t:prow * (t + 1), :], kk[:, 128 * g:128 * (g + 1)], NT)
        s = jnp.where(ok, jnp.where(up, s2[:, 0:BLK], s2[:, BLK:2 * BLK]), NEG_INF)
        sk = jnp.concatenate([jnp.full((BLK, BLK), sink_ref[2 * t + j] * LOG2E, f32) for j in range(2)], axis=0)
        m = jnp.maximum(jnp.max(s, axis=-1, keepdims=True), sk)
        p = jnp.exp2(s - m)
        p2 = jnp.concatenate([jnp.where(up, p, 0.0), jnp.where(up, 0.0, p)], axis=1).astype(bf16)
        res = _dot(p2, vv[:, 256 * g:256 * (g + 1)])
        y = res[:, 0:128] / (res[:, 128:256] + jnp.exp2(sk - m))
        mix_ref[:, 128 * t:128 * (t + 1)] = jnp.where(lane_lo, y[0:BLK], y[BLK:prow]).astype(bf16)

    levels = []
    bz = BLK
    while bz > SUB:
        levels.append((bz, (r2 % bz) >= bz // 2, None if bz == BLK else (r2 // bz) == (c2 // bz)))
        bz //= 2
    same_sub = (r2 // SUB) == (c2 // SUB)
    sub_ge = [(r2 % SUB) >= s for s in range(SUB)]

    def retention_head(h):
        qc = _rope_r(proj_ref[:, Q_C + 128 * h:Q_C + 128 * (h + 1)], roper_ref)
        kc = _rope_r(proj_ref[:, K_C + 128 * h:K_C + 128 * (h + 1)], roper_ref) * (DK_C ** -0.5)
        vc = proj_ref[:, V_C + 128 * h:V_C + 128 * (h + 1)].astype(bf16)
        qcb = qc.astype(bf16)
        sc = _dot(qcb, kc.astype(bf16), NT) * dec_ref[h]
        oc = _dot(qcb, srs[h].astype(bf16)) * rsc_ref[h] + _dot(sc.astype(bf16), vc)
        srs[h] = gpow_ref[h] * srs[h] + _dot((kc * tail_ref[h]).astype(bf16), vc, TN)
        yc = _rms(oc, rn_ref[...]) * _silu(proj_ref[:, G_C + 128 * h:G_C + 128 * (h + 1)])
        mix_ref[:, W_A + 512 + 128 * h:W_A + 512 + 128 * (h + 1)] = yc.astype(bf16)

    def hgrn_intra(h):
        sl = slice(128 * h, 128 * (h + 1))
        gc = g_s[:, sl]
        qb = qb_s[:, sl]
        kb = kb_s[:, sl]
        inter = _dot((qb * jnp.exp2(gc)).astype(bf16), sht[h].astype(bf16), NT)
        amat = None
        for bz, upper, same in levels:
            gref = jnp.concatenate([jnp.broadcast_to(g_s[pl.ds(st + bz // 2 - 1, 1), sl], (bz, 128))
                                    for st in range(0, BLK, bz)], axis=0)
            dg = gc - gref
            e = jnp.exp2(jnp.where(upper, dg, -dg))
            qt = jnp.where(upper, qb * e, 0.0).astype(bf16)
            kt = jnp.where(upper, 0.0, kb * e).astype(bf16)
            pm = _dot(qt, kt, NT)
            if same is not None:
                pm = jnp.where(same, pm, 0.0)
            amat = pm if amat is None else amat + pm
        ys = []
        for s in range(SUB):
            ks = jnp.concatenate([jnp.broadcast_to(kb_s[pl.ds(SUB * j + s, 1), sl], (SUB, 128))
                                  for j in range(BLK // SUB)], axis=0)
            gs = jnp.concatenate([jnp.broadcast_to(g_s[pl.ds(SUB * j + s, 1), sl], (SUB, 128))
                                  for j in range(BLK // SUB)], axis=0)
            ys.append((qb * ks * jnp.exp2(jnp.where(sub_ge[s], gc - gs, NEG_INF))).astype(bf16))
        amat = amat + jnp.where(same_sub, _dot(jnp.concatenate(ys, axis=1), dsel_ref[...]), 0.0)
        return inter, amat.astype(bf16)

    def hgrn_finish(h, inter, amat):
        sl = slice(128 * h, 128 * (h + 1))
        vb = proj_ref[:, I_B + 128 * h:I_B + 128 * (h + 1)].astype(bf16)
        ob = inter + _dot(amat, vb)
        glast = g_s[pl.ds(BLK - 1, 1), sl]
        kt2 = (kb_s[:, sl] * jnp.exp2(glast - g_s[:, sl])).astype(bf16)
        sht[h] = sht[h] * jnp.exp2(glast) + _dot(vb, kt2, TN)
        yb = _rms(ob, hn_ref[...]) * _silu(proj_ref[:, G_B + 128 * h:G_B + 128 * (h + 1)])
        mix_ref[:, W_A + 128 * h:W_A + 128 * (h + 1)] = yb.astype(bf16)

    attention_prep()
    pending = None
    for h in range(H_B):
        cur = hgrn_intra(h)
        if pending is not None:
            hgrn_finish(h - 1, *pending)
        retention_head(h)
        pending = cur
    hgrn_finish(H_B - 1, *pending)
    for t in range(N_HEADS_A // 2):
        attention_pair(t)
    kk[0:BLK, :] = kk[BLK:2 * BLK, :]
    vv[0:BLK, :] = vv[BLK:2 * BLK, :]

    @pl.when(i == nb - 1)
    def _():
        for h in range(H_B):
            sh_ref[0, h] = sht[h].T
            sr_ref[0, h] = srs[h]


def _const_spec(shape):
    nd = len(shape)
    return pl.BlockSpec(shape, lambda *_: (0,) * nd)


def _slab_rows(nrows, steps):
    for r in range(16, nrows + 1, 16):
        if nrows % r == 0 and nrows // r <= steps:
            return r
    return nrows


def _mixer_prompt(proj, rows_total, batch, nb, layer, sinks, gpow, ropea, roper, qn, kn, lb, hn, rn, sel, expand,
                  tri, dsel, dec, rsc, tail, w_gate, w_up, w_down, w_out):
    smem = pl.BlockSpec(memory_space=pltpu.SMEM)
    d, dff = w_gate.shape[1], w_gate.shape[2]

    def row_blk(b, i):
        return jnp.where(i == 0, batch * (nb - 1), b * (nb - 1) + i - 1)

    def mix_blk(b, i):
        return jnp.where((i == 0) & (b > 0), b * (nb - 1), row_blk(b, i))

    def slab_in(nrows, ncols):
        r = _slab_rows(nrows, batch * nb)
        return pl.BlockSpec((None, r, ncols), lambda b, i: (layer, jnp.minimum(b * nb + i, nrows // r - 1), 0))

    def slab_out(nrows, ncols):
        r = _slab_rows(nrows, batch * nb)
        return pl.BlockSpec((r, ncols), lambda b, i: (jnp.minimum(b * nb + i, nrows // r - 1), 0))

    return pl.pallas_call(
        _mixer_prompt_kernel,
        grid=(batch, nb),
        in_specs=[smem, smem,
                  pl.BlockSpec((BLK, D_IN), lambda b, i: (row_blk(b, i), 0)),
                  pl.BlockSpec((2, BLK, 128), lambda b, i: (0, i, 0)),
                  pl.BlockSpec((2, BLK, 128), lambda b, i: (0, i, 0)),
                  _const_spec((1, 1024)), _const_spec((1, 256)), _const_spec((1, 512)),
                  _const_spec((1, 128)), _const_spec((1, 128)),
                  _const_spec((1024, 128)), _const_spec((128, 1024)),
                  _const_spec((BLK, BLK)), _const_spec((SUB * 128, 128)),
                  _const_spec((H_C, BLK, BLK)), _const_spec((H_C, BLK, 128)), _const_spec((H_C, BLK, 128)),
                  slab_in(d, dff), slab_in(d, dff), slab_in(dff, d), slab_in(d, d)],
        out_specs=[pl.BlockSpec((BLK, 2048), lambda b, i: (mix_blk(b, i), 0)),
                   pl.BlockSpec((1, BLK, 256), lambda b, i: (b, 0, 0)),
                   pl.BlockSpec((1, BLK, 256), lambda b, i: (b, 0, 0)),
                   pl.BlockSpec((1, H_B, 128, 128), lambda b, i: (b, 0, 0, 0)),
                   pl.BlockSpec((1, H_C, 128, 128), lambda b, i: (b, 0, 0, 0)),
                   slab_out(d, dff), slab_out(d, dff), slab_out(dff, d), slab_out(d, d)],
        out_shape=[jax.ShapeDtypeStruct((rows_total, 2048), bf16),
                   jax.ShapeDtypeStruct((batch, BLK, 256), f32),
                   jax.ShapeDtypeStruct((batch, BLK, 256), f32),
                   jax.ShapeDtypeStruct((batch, H_B, 128, 128), f32),
                   jax.ShapeDtypeStruct((batch, H_C, 128, 128), f32),
                   jax.ShapeDtypeStruct((d, dff), bf16), jax.ShapeDtypeStruct((d, dff), bf16),
                   jax.ShapeDtypeStruct((dff, d), bf16), jax.ShapeDtypeStruct((d, d), bf16)],
        scratch_shapes=[pltpu.VMEM((2 * BLK, N_KV_A * 128), bf16), pltpu.VMEM((2 * BLK, N_KV_A * 256), bf16),
                        pltpu.VMEM((H_B, 128, 128), f32), pltpu.VMEM((H_C, 128, 128), f32),
                        pltpu.VMEM((N_HEADS_A * BLK, 128), bf16),
                        pltpu.VMEM((BLK, 512), f32), pltpu.VMEM((BLK, 512), f32), pltpu.VMEM((BLK, 512), f32)],
        compiler_params=pltpu.CompilerParams(dimension_semantics=("arbitrary", "arbitrary"),
                                             vmem_limit_bytes=VMEM_LIMIT),
        name="mixer_prompt",
    )(sinks, gpow, proj, ropea, roper, qn, kn, lb, hn, rn, sel, expand, tri, dsel, dec, rsc, tail,
      w_gate, w_up, w_down, w_out)


def _mixer_sample_kernel(sink_ref, gpow_ref, mixin_ref, *refs, dec_seq, n_prev):
    del mixin_ref
    _mixer_sample_body(sink_ref, gpow_ref, *refs[:17], *refs[17 + n_prev:], dec_seq=dec_seq)


def _mixer_sample_body(sink_ref, gpow_ref, proj_ref, ropea_ref, roper_ref, qn_ref, kn_ref, lb_ref,
                       hn_ref, rn_ref, sel_ref, exp_ref, dec_ref, rsc_ref, tail_ref,
                       ck_ref, cv_ref, sh_ref, sr_ref,
                       mix_ref, nk_ref, nv_ref, nsh_ref, nsr_ref, *, dec_seq):
    nbat = SROWS // dec_seq
    w = ck_ref.shape[1]
    rb = lax.broadcasted_iota(jnp.int32, (SROWS, 1), 0) // dec_seq
    rt = lax.broadcasted_iota(jnp.int32, (SROWS, 1), 0) % dec_seq

    def pick(parts):
        out = parts[nbat - 1]
        for b in range(nbat - 2, -1, -1):
            out = jnp.where(rb == b, parts[b], out)
        return out

    fgate, logf = _forget(proj_ref[:, F_B:F_B + 512], lb_ref[...])
    kb = 1.0 - fgate
    qb = _silu(proj_ref[:, Q_B:Q_B + 512])
    vb = proj_ref[:, I_B:I_B + 512]
    gcum = logf
    for d in range(1, dec_seq):
        gcum = gcum + jnp.where(rt >= d, pltpu.roll(logf, d, 0), 0.0)
    intra = [jnp.zeros((SROWS, 128), f32) for _ in range(H_B)]
    for s in range(SROWS):
        ok = (rb == s // dec_seq) & (rt >= s % dec_seq)
        y = qb * kb[s:s + 1, :] * jnp.exp2(jnp.where(ok, gcum - gcum[s:s + 1, :], NEG_INF))
        for h in range(H_B):
            sl = slice(128 * h, 128 * (h + 1))
            intra[h] = intra[h] + jnp.sum(y[:, sl], axis=-1, keepdims=True) * vb[s:s + 1, sl]

    qa = _rope_a(_head_norm(proj_ref[:, Q_A:Q_A + 1024], sel_ref, exp_ref, qn_ref[...]), ropea_ref)
    qs = (qa * Q_SCALE).astype(bf16)
    k_new = _rope_a(_head_norm(proj_ref[:, K_A:K_A + 256], sel_ref, exp_ref, kn_ref[...]), ropea_ref)
    v_new = proj_ref[:, V_A:V_A + 256]
    for b in range(nbat):
        nk_ref[b, 0:w - dec_seq, :] = ck_ref[b, dec_seq:w, :]
        nk_ref[b, w - dec_seq:w, :] = k_new[dec_seq * b:dec_seq * (b + 1), :]
        nv_ref[b, 0:w - dec_seq, :] = cv_ref[b, dec_seq:w, :]
        nv_ref[b, w - dec_seq:w, :] = v_new[dec_seq * b:dec_seq * (b + 1), :]
    knb = k_new.astype(bf16)
    vnb = v_new.astype(bf16)

    grows = GROUP_A * SROWS
    rt_g = jnp.concatenate([rt] * GROUP_A, axis=0)
    rb_g = jnp.concatenate([rb] * GROUP_A, axis=0)
    jc = lax.broadcasted_iota(jnp.int32, (grows, w), 1)
    valid_c = jc > rt_g + (w - WINDOW)
    cn = lax.broadcasted_iota(jnp.int32, (grows, SROWS), 1)
    valid_n = (cn // dec_seq == rb_g) & (cn % dec_seq <= rt_g)
    heads = [None] * N_HEADS_A
    for g in range(N_KV_A):
        ksl = slice(64 * g, 64 * (g + 1))
        qg = jnp.concatenate([qs[:, 64 * h:64 * (h + 1)] for h in range(GROUP_A * g, GROUP_A * (g + 1))], axis=0)
        sparts = [_dot(qg, ck_ref[b, :, ksl].astype(bf16), NT) for b in range(nbat)]
        s_c = sparts[nbat - 1]
        for b in range(nbat - 2, -1, -1):
            s_c = jnp.where(rb_g == b, sparts[b], s_c)
        s_c = jnp.where(valid_c, s_c, NEG_INF)
        s_n = jnp.where(valid_n, _dot(qg, knb[:, ksl], NT), NEG_INF)
        sk = jnp.concatenate([jnp.full((SROWS, 1), sink_ref[h] * LOG2E, f32)
                              for h in range(GROUP_A * g, GROUP_A * (g + 1))], axis=0)
        m = jnp.maximum(jnp.maximum(jnp.max(s_c, axis=-1, keepdims=True), jnp.max(s_n, axis=-1, keepdims=True)), sk)
        p_c = jnp.exp2(s_c - m)
        p_n = jnp.exp2(s_n - m)
        den = jnp.sum(p_c, axis=-1, keepdims=True) + jnp.sum(p_n, axis=-1, keepdims=True) + jnp.exp2(sk - m)
        pcb = p_c.astype(bf16)
        oparts = [_dot(pcb, cv_ref[b, :, ksl].astype(bf16)) for b in range(nbat)]
        o = oparts[nbat - 1]
        for b in range(nbat - 2, -1, -1):
            o = jnp.where(rb_g == b, oparts[b], o)
        o = (o + _dot(p_n.astype(bf16), vnb[:, ksl])) / den
        for hh in range(GROUP_A):
            heads[GROUP_A * g + hh] = o[SROWS * hh:SROWS * (hh + 1), :]
    mix_ref[:, 0:W_A] = jnp.concatenate(heads, axis=1).astype(bf16)

    qg = (qb * jnp.exp2(gcum)).astype(bf16)
    vbb = vb.astype(bf16)
    glast = [gcum[dec_seq * (b + 1) - 1:dec_seq * (b + 1), :] for b in range(nbat)]
    for h in range(H_B):
        sl = slice(128 * h, 128 * (h + 1))
        ob = pick([_dot(qg[:, sl], sh_ref[b, h].astype(bf16)) for b in range(nbat)]) + intra[h]
        for b in range(nbat):
            kt2 = jnp.where(rb == b, kb[:, sl] * jnp.exp2(glast[b][:, sl] - gcum[:, sl]), 0.0).astype(bf16)
            dcol = jnp.broadcast_to(jnp.exp2(glast[b][:, sl]), (128, 128)).T
            nsh_ref[b, h] = dcol * sh_ref[b, h] + _dot(kt2, vbb[:, sl], TN)
        yb = _rms(ob, hn_ref[...]) * _silu(proj_ref[:, G_B + 128 * h:G_B + 128 * (h + 1)])
        mix_ref[:, W_A + 128 * h:W_A + 128 * (h + 1)] = yb.astype(bf16)

    for h in range(H_C):
        qc = _rope_r(proj_ref[:, Q_C + 128 * h:Q_C + 128 * (h + 1)], roper_ref)
        kc = _rope_r(proj_ref[:, K_C + 128 * h:K_C + 128 * (h + 1)], roper_ref) * (DK_C ** -0.5)
        vc = proj_ref[:, V_C + 128 * h:V_C + 128 * (h + 1)]
        qcb = qc.astype(bf16)
        vcb = vc.astype(bf16)
        sc = _dot(qcb, kc.astype(bf16), NT) * dec_ref[h]
        oc = pick([_dot(qcb, sr_ref[b, h].astype(bf16)) for b in range(nbat)]) * rsc_ref[h] \
            + _dot(sc.astype(bf16), vcb)
        kct = kc * tail_ref[h]
        for b in range(nbat):
            nsr_ref[b, h] = gpow_ref[h] * sr_ref[b, h] + _dot(jnp.where(rb == b, kct, 0.0).astype(bf16), vcb, TN)
        yc = _rms(oc, rn_ref[...]) * _silu(proj_ref[:, G_C + 128 * h:G_C + 128 * (h + 1)])
        mix_ref[:, W_A + 512 + 128 * h:W_A + 512 + 128 * (h + 1)] = yc.astype(bf16)


def _mixer_sample(mix, proj, row0, layer, dec_batch, dec_seq, sinks, gpow, ropea, roper, qn, kn, lb, hn, rn, sel, expand,
                  dec, rsc, tail, cache_k, cache_v, state_h, state_r, prev_outs):
    smem = pl.BlockSpec(memory_space=pltpu.SMEM)
    nbat = SROWS // dec_seq
    steps = dec_batch // nbat
    blk0 = row0 // SROWS
    w = cache_k.shape[1]
    n_all = cache_k.shape[0]
    assert mix.shape[0] == row0 + steps * SROWS
    any_spec = pl.BlockSpec(memory_space=pl.ANY)
    n_in = 20
    return pl.pallas_call(
        functools.partial(_mixer_sample_kernel, dec_seq=dec_seq, n_prev=len(prev_outs)),
        grid=(steps,),
        in_specs=[smem, smem,
                  pl.BlockSpec(memory_space=pl.ANY),
                  pl.BlockSpec((SROWS, D_IN), lambda c: (blk0 + c, 0)),
                  _const_spec((2, SROWS, 128)), _const_spec((2, SROWS, 128)),
                  _const_spec((1, 1024)), _const_spec((1, 256)), _const_spec((1, 512)),
                  _const_spec((1, 128)), _const_spec((1, 128)),
                  _const_spec((1024, 128)), _const_spec((128, 1024)),
                  _const_spec((H_C, SROWS, SROWS)), _const_spec((H_C, SROWS, 128)), _const_spec((H_C, SROWS, 128)),
                  pl.BlockSpec((nbat, w, 256), lambda c: (layer * steps + c, 0, 0)),
                  pl.BlockSpec((nbat, w, 256), lambda c: (layer * steps + c, 0, 0)),
                  pl.BlockSpec((nbat, H_B, 128, 128), lambda c: (layer * steps + c, 0, 0, 0)),
                  pl.BlockSpec((nbat, H_C, 128, 128), lambda c: (layer * steps + c, 0, 0, 0))]
        + [any_spec] * len(prev_outs),
        out_specs=[pl.BlockSpec((SROWS, 2048), lambda c: (blk0 + c, 0)),
                   pl.BlockSpec((nbat, w, 256), lambda c: (layer * steps + c, 0, 0)),
                   pl.BlockSpec((nbat, w, 256), lambda c: (layer * steps + c, 0, 0)),
                   pl.BlockSpec((nbat, H_B, 128, 128), lambda c: (layer * steps + c, 0, 0, 0)),
                   pl.BlockSpec((nbat, H_C, 128, 128), lambda c: (layer * steps + c, 0, 0, 0))],
        out_shape=[jax.ShapeDtypeStruct(mix.shape, bf16),
                   jax.ShapeDtypeStruct((n_all, w, 256), f32),
                   jax.ShapeDtypeStruct((n_all, w, 256), f32),
                   jax.ShapeDtypeStruct((n_all, H_B, 128, 128), f32),
                   jax.ShapeDtypeStruct((n_all, H_C, 128, 128), f32)],
        input_output_aliases={2: 0, **{n_in + k: 1 + k for k in range(len(prev_outs))}},
        compiler_params=pltpu.CompilerParams(dimension_semantics=("arbitrary",), vmem_limit_bytes=VMEM_LIMIT),
        name="mixer_sample",
    )(sinks, gpow, mix, proj, ropea, roper, qn, kn, lb, hn, rn, sel, expand, dec, rsc, tail,
      cache_k, cache_v, state_h, state_r, *prev_outs)


def _rope_tables(pos):
    posf = jnp.asarray(pos).astype(f32)[:, None]
    t = pos.shape[0]
    half = N_ROT // 2
    inv = ROPE_THETA ** (-jnp.arange(half, dtype=f32) * (2.0 / N_ROT))
    ang = posf * inv[None, :]
    cos, sin = jnp.cos(ang), jnp.sin(ang)
    rest0 = jnp.zeros((t, HEAD_DIM_A - N_ROT), f32)
    c64 = jnp.concatenate([cos, cos, rest0 + 1.0], axis=1)
    s64 = jnp.concatenate([sin, sin, rest0], axis=1)
    ropea = jnp.stack([jnp.tile(c64, (1, 2)), jnp.tile(s64, (1, 2))])
    invr = RET_THETA ** (-jnp.arange(DK_C // 2, dtype=f32) * (2.0 / DK_C))
    angr = posf * invr[None, :]
    cr, sr = jnp.cos(angr), jnp.sin(angr)
    roper = jnp.stack([jnp.concatenate([cr, cr], axis=1), jnp.concatenate([-sr, sr], axis=1)])
    return ropea, roper


def _ret_tables(lg, seq_of_row, tok_of_row, length):
    nf = np.float32
    tq = tok_of_row.astype(nf)
    rel = tq[:, None] - tq[None, :]
    ok = (seq_of_row[:, None] == seq_of_row[None, :]) & (rel >= 0)
    dec = np.where(ok[None], np.exp(np.where(ok, rel, nf(0.0))[None] * lg[:, None, None]), nf(0.0))
    n = tq.shape[0]
    rsc = np.broadcast_to(np.exp((tq + nf(1.0))[None, :, None] * lg[:, None, None]), (H_C, n, 128))
    tail = np.broadcast_to(np.exp((nf(length) - nf(1.0) - tq)[None, :, None] * lg[:, None, None]), (H_C, n, 128))
    gpow = np.exp(nf(length) * lg)
    return dec.astype(nf), np.ascontiguousarray(rsc, nf), np.ascontiguousarray(tail, nf), gpow.astype(nf)


def kernel(x_prompt, x_sample, cache_k, cache_v, state_hgrn, state_ret, meta_tokens, norm_mix, norm_ffn, w_in, q_norm, k_norm, attn_sinks, hgrn_lb, hgrn_norm, ret_norm, w_out, w_gate, w_up, w_down):
    batch, seq, d = x_prompt.shape
    dec_batch, dec_seq, _ = x_sample.shape
    depth = w_in.shape[0]
    w = cache_k.shape[2]
    assert d == 2048 and w_in.shape[2] == D_IN and seq % BLK == 0
    assert SROWS % dec_seq == 0 and dec_batch % (SROWS // dec_seq) == 0 and w == WINDOW
    nb = seq // BLK + 1
    lp = nb * BLK
    rows_main = batch * seq
    rows_s = dec_batch * dec_seq
    row_s0 = rows_main + BLK
    tail_rows = BLK + rows_s
    rows = rows_main + tail_rows
    tm_ffn = _row_tile(rows, TM_FFN, 16)
    assert rows_main % tail_rows == 0 and rows_s % SROWS == 0 and tail_rows % 16 == 0 and tm_ffn >= tail_rows

    tail = jnp.concatenate([jnp.zeros((PAD, d), f32), meta_tokens.astype(f32), x_sample.reshape(rows_s, d)], axis=0)
    x, h = _embed(x_prompt.reshape(rows_main, d), tail, norm_mix[0][None])

    p = jax.nn.softmax(hgrn_lb.astype(f32), axis=0)
    lbs = jnp.cumsum(p, axis=0) - p[0]
    lg = np.log1p(-np.exp2(np.float32(-5.0) - np.arange(H_C, dtype=np.float32))).astype(np.float32)
    ropea_p, roper_p = _rope_tables(np.arange(lp) - PAD)
    srow = np.arange(SROWS)
    ropea_s, roper_s = _rope_tables(PAST_LEN + srow % dec_seq)
    dec_p, rsc_p, tail_p, gpow_p = _ret_tables(lg, np.zeros((BLK,), np.int32), np.arange(BLK), float(BLK))
    dec_s, rsc_s, tail_s, gpow_s = _ret_tables(lg, srow // dec_seq, srow % dec_seq, float(dec_seq))
    sel_np = np.arange(1024)[:, None] // HEAD_DIM_A == np.arange(128)[None, :]
    sel = jnp.asarray(sel_np, bf16)
    expand = jnp.asarray(sel_np.T, bf16)
    tri = jnp.asarray(np.arange(BLK)[:, None] >= np.arange(BLK)[None, :], bf16)
    dsel = jnp.asarray(np.arange(SUB * 128)[:, None] // 128 == np.arange(128)[None, :] % SUB, bf16)

    ck_flat = cache_k.reshape(depth * dec_batch, w, N_KV_A * HEAD_DIM_A)
    cv_flat = cache_v.reshape(depth * dec_batch, w, N_KV_A * HEAD_DIM_A)
    sh_flat = state_hgrn.reshape(depth * dec_batch, H_B, 128, 128)
    sr_flat = state_ret.reshape(depth * dec_batch, H_C, 128, 128)

    outs_p, outs_s = [], ()
    for l in range(depth):
        proj = _inproj(h, w_in, l)
        qn = jnp.tile(q_norm[l], N_HEADS_A)[None]
        kn = jnp.tile(k_norm[l], N_KV_A)[None]
        common = (qn, kn, lbs[l][None], hgrn_norm[l][None], ret_norm[l][None], sel, expand)
        mix, ck, cv, sh, sr, wg_b, wu_b, wd_b, wo_b = _mixer_prompt(
            proj, rows, batch, nb, l, attn_sinks[l], gpow_p, ropea_p, roper_p, *common, tri, dsel,
            dec_p, rsc_p, tail_p, w_gate, w_up, w_down, w_out)
        mix, *outs_s = _mixer_sample(mix, proj, row_s0, l, dec_batch, dec_seq, attn_sinks[l], gpow_s,
                                     ropea_s, roper_s, *common, dec_s, rsc_s, tail_s,
                                     ck_flat, cv_flat, sh_flat, sr_flat, tuple(outs_s))
        ffn_w = (wo_b, norm_ffn[l][None], wg_b, wu_b, wd_b)
        outs_p.append((ck, cv, sh, sr))
        if l + 1 < depth:
            x, h = _ffn(x, mix, *ffn_w, norm_mix[l + 1][None], tm_ffn, tail_rows)
        else:
            y_main, y_tail = _ffn(x, mix, *ffn_w, None, tm_ffn, tail_rows)

    y_prompt = y_main.reshape(batch, seq, d)
    y_sample = y_tail[BLK:].reshape(dec_batch, dec_seq, d)
    kv_shape = (depth, -1, w, N_KV_A, HEAD_DIM_A)
    st_shape = (depth, dec_batch, H_B, 128, 128)
    stack = lambda outs, k: jnp.stack([o[k] for o in outs])
    nk, nv, nsh, nsr = outs_s
    return (y_prompt, y_sample,
            stack(outs_p, 0).reshape(kv_shape), stack(outs_p, 1).reshape(kv_shape), stack(outs_p, 2), stack(outs_p, 3),
            nk.reshape(kv_shape), nv.reshape(kv_shape), nsh.reshape(st_shape), nsr.reshape(st_shape))
```

```python
import functools

import jax
import jax.numpy as jnp
import numpy as np
from jax import lax
from jax.experimental import pallas as pl
from jax.experimental.pallas import tpu as pltpu

f32 = jnp.float32
bf16 = jnp.bfloat16

N_META = 16
EPS = 1e-6
NEG_INF = -1e30
LB_FLOOR = 1e-30
WINDOW = 128
HEAD_DIM_A = 64
N_HEADS_A = 16
N_KV_A = 4
GROUP_A = N_HEADS_A // N_KV_A
N_ROT = 16
ROPE_THETA = 500000.0
H_B = 4
DK_B = 128
H_C = 4
DK_C = 128
RET_THETA = 10000.0
PAST_LEN = 16384

BLK = 128
PAD = BLK - N_META
SUB = 8
SROWS = 32
TM_FFN = 704
VMEM_LIMIT = 56 * 1024 * 1024
VMEM_LIMIT_FFN = 60 * 1024 * 1024

Q_A, K_A, V_A = 0, 1024, 1280
Q_B, F_B, I_B, G_B = 1536, 2048, 2560, 3072
Q_C, K_C, V_C, G_C = 3584, 4096, 4608, 5120
D_IN = 5632
W_A = 1024
LOG2E = 1.4426950408889634
Q_SCALE = HEAD_DIM_A ** -0.5 * LOG2E
NT = (((1,), (1,)), ((), ()))
TN = (((0,), (0,)), ((), ()))


def _dot(a, b, dims=None):
    if dims is None:
        return jnp.dot(a, b, preferred_element_type=f32)
    return lax.dot_general(a, b, dims, preferred_element_type=f32)


def _row_tile(rows, cap, align=8):
    best = align
    for t in range(align, cap + 1, align):
        if rows % t == 0:
            best = t
    return best


def _silu(x):
    return x * jax.nn.sigmoid(x)


def _rms(x, g):
    return x * lax.rsqrt(jnp.mean(x * x, axis=-1, keepdims=True) + EPS) * g


def _inproj_kernel(h_ref, w_ref, o_ref):
    o_ref[...] = _dot(h_ref[...], w_ref[...].astype(bf16))


def _inproj(h, w_all, layer):
    rows, d = h.shape
    n = w_all.shape[2]
    tm = _row_tile(rows, 2816, 16)
    tn = 512
    return pl.pallas_call(
        _inproj_kernel,
        grid=(rows // tm, n // tn),
        in_specs=[pl.BlockSpec((tm, d), lambda i, j: (i, 0)),
                  pl.BlockSpec((None, d, tn), lambda i, j: (layer, 0, j))],
        out_specs=pl.BlockSpec((tm, tn), lambda i, j: (i, j)),
        out_shape=jax.ShapeDtypeStruct((rows, n), f32),
        compiler_params=pltpu.CompilerParams(dimension_semantics=("arbitrary", "arbitrary"),
                                             vmem_limit_bytes=VMEM_LIMIT),
        name="inproj",
    )(h, w_all)


def _embed_kernel(xp_ref, tail_ref, g_ref, x_ref, h_ref):
    i = pl.program_id(0)
    last = pl.num_programs(0) - 1

    def emit(src_ref):
        x = src_ref[...]
        x_ref[...] = x
        h_ref[...] = _rms(x, g_ref[...]).astype(bf16)

    pl.when(i < last)(lambda: emit(xp_ref))
    pl.when(i == last)(lambda: emit(tail_ref))


def _embed(xp, tail, g):
    rows_main, d = xp.shape
    tm = tail.shape[0]
    n_main = rows_main // tm
    rows = rows_main + tm
    return pl.pallas_call(
        _embed_kernel,
        grid=(n_main + 1,),
        in_specs=[pl.BlockSpec((tm, d), lambda i: (jnp.minimum(i, n_main - 1), 0)),
                  pl.BlockSpec((tm, d), lambda i: (0, 0)),
                  pl.BlockSpec((1, d), lambda i: (0, 0))],
        out_specs=[pl.BlockSpec((tm, d), lambda i: (i, 0)),
                   pl.BlockSpec((tm, d), lambda i: (i, 0))],
        out_shape=[jax.ShapeDtypeStruct((rows, d), f32), jax.ShapeDtypeStruct((rows, d), bf16)],
        compiler_params=pltpu.CompilerParams(dimension_semantics=("arbitrary",), vmem_limit_bytes=VMEM_LIMIT),
        name="embed_norm",
    )(xp, tail, g)


def _ffn_kernel(*refs, with_next, tail_rows, n_ff, tf):
    if with_next:
        (x_ref, mix_ref, wo_ref, nf_ref, wg_hbm, wu_hbm, wd_hbm, gn_ref, o_ref, hn_ref,
         wg_buf, wu_buf, wd_buf, sem) = refs
        h_ref = hn_ref
    else:
        (x_ref, mix_ref, wo_ref, nf_ref, wg_hbm, wu_hbm, wd_hbm, o_ref, tail_ref,
         h_ref, wg_buf, wu_buf, wd_buf, sem) = refs
    i = pl.program_id(0)
    n_chunks = pl.num_programs(0) * n_ff

    def chunk_copies(c):
        jt = c % n_ff
        slot = c % 2
        col = pl.multiple_of(jt * tf, tf)
        return (pltpu.make_async_copy(wg_hbm.at[:, pl.ds(col, tf)], wg_buf.at[slot], sem.at[0, slot]),
                pltpu.make_async_copy(wu_hbm.at[:, pl.ds(col, tf)], wu_buf.at[slot], sem.at[1, slot]),
                pltpu.make_async_copy(wd_hbm.at[pl.ds(col, tf), :], wd_buf.at[slot], sem.at[2, slot]))

    @pl.when(i == 0)
    def _():
        for cp in chunk_copies(0):
            cp.start()

    x1 = x_ref[...] + _dot(mix_ref[...], wo_ref[...])
    h_ref[...] = _rms(x1, nf_ref[...]).astype(bf16)
    o_ref[...] = x1

    def ff_step(j, carry):
        c = i * n_ff + j
        for cp in chunk_copies(c):
            cp.wait()

        @pl.when(c + 1 < n_chunks)
        def _():
            for cp in chunk_copies(c + 1):
                cp.start()

        slot = c % 2
        h = h_ref[...]
        a = _silu(_dot(h, wg_buf[slot])) * _dot(h, wu_buf[slot])
        o_ref[...] += _dot(a.astype(bf16), wd_buf[slot])
        return carry

    lax.fori_loop(0, n_ff, ff_step, 0)

    if with_next:
        hn_ref[...] = _rms(o_ref[...], gn_ref[...]).astype(bf16)
    else:
        @pl.when(i == pl.num_programs(0) - 1)
        def _():
            tail_ref[...] = o_ref[o_ref.shape[0] - tail_rows:, :]


def _ffn(x, mix, wo, nf, wg, wu, wd, g_next, tm, tail_rows):
    rows, d = x.shape
    dff = wg.shape[1]
    tf = 512
    n_tiles = rows // tm
    with_next = g_next is not None
    row_spec = pl.BlockSpec((tm, d), lambda i: (i, 0))
    out_spec = row_spec
    hbm = pl.BlockSpec(memory_space=pl.ANY)
    in_specs = [row_spec, row_spec,
                pl.BlockSpec((d, d), lambda i: (0, 0), pipeline_mode=pl.Buffered(1)),
                pl.BlockSpec((1, d), lambda i: (0, 0)),
                hbm, hbm, hbm]
    args = [x, mix, wo, nf, wg, wu, wd]
    if with_next:
        in_specs.append(pl.BlockSpec((1, d), lambda i: (0, 0)))
        args.append(g_next)
        out_specs = [out_spec, out_spec]
        out_shape = [jax.ShapeDtypeStruct((rows, d), f32), jax.ShapeDtypeStruct((rows, d), bf16)]
    else:
        assert tail_rows % 8 == 0 and tail_rows <= tm
        out_specs = [out_spec, pl.BlockSpec((tail_rows, d), lambda i: (0, 0))]
        out_shape = [jax.ShapeDtypeStruct((rows - tail_rows, d), f32), jax.ShapeDtypeStruct((tail_rows, d), f32)]
    weight_ring = [pltpu.VMEM((2, d, tf), bf16), pltpu.VMEM((2, d, tf), bf16), pltpu.VMEM((2, tf, d), bf16),
                   pltpu.SemaphoreType.DMA((3, 2))]
    return pl.pallas_call(
        functools.partial(_ffn_kernel, with_next=with_next, tail_rows=tail_rows, n_ff=dff // tf, tf=tf),
        grid=(n_tiles,),
        in_specs=in_specs,
        out_specs=out_specs,
        out_shape=out_shape,
        scratch_shapes=([] if with_next else [pltpu.VMEM((tm, d), bf16)]) + weight_ring,
        compiler_params=pltpu.CompilerParams(dimension_semantics=("arbitrary",),
                                             vmem_limit_bytes=VMEM_LIMIT_FFN),
        name="outproj_ffn",
    )(*args)


def _head_norm(x, sel_ref, exp_ref, g):
    n, w = x.shape
    x2 = x * x
    hi = x2.astype(bf16)
    lo = (x2 - hi.astype(f32)).astype(bf16)
    s = _dot(jnp.concatenate([hi, lo], axis=0), sel_ref[0:w, :])
    return x * _dot_split3(lax.rsqrt((s[0:n] + s[n:2 * n]) * (1.0 / HEAD_DIM_A) + EPS), exp_ref[:, 0:w]) * g


def _dot_split3(x, m):
    n = x.shape[0]
    y = _dot(jnp.concatenate(_split3(x), axis=0), m)
    return y[0:n] + y[n:2 * n] + y[2 * n:3 * n]


def _rope_a(x, tab_ref):
    half = N_ROT // 2
    first = lax.broadcasted_iota(jnp.int32, (x.shape[0], 128), 1) % HEAD_DIM_A < half
    out = []
    for t in range(x.shape[1] // 128):
        xt = x[:, 128 * t:128 * (t + 1)]
        partner = jnp.where(first, -pltpu.roll(xt, 128 - half, 1), pltpu.roll(xt, half, 1))
        out.append(xt * tab_ref[0] + partner * tab_ref[1])
    return jnp.concatenate(out, axis=1)


def _rope_r(x, tab_ref):
    return x * tab_ref[0] + pltpu.roll(x, DK_C // 2, 1) * tab_ref[1]


def _forget(fb, lb):
    f = jnp.maximum(lb, LB_FLOOR) + (1.0 - lb) * jax.nn.sigmoid(fb)
    return f, jnp.log2(f)


def _split3(x):
    h1 = x.astype(bf16)
    r1 = x - h1.astype(f32)
    h2 = r1.astype(bf16)
    h3 = (r1 - h2.astype(f32)).astype(bf16)
    return h1, h2, h3


def _mixer_prompt_kernel(sink_ref, gpow_ref, proj_ref, ropea_ref, roper_ref, qn_ref, kn_ref, lb_ref, hn_ref, rn_ref,
                         sel_ref, exp_ref, tri_ref, dsel_ref, dec_ref, rsc_ref, tail_ref,
                         wg_ref, wu_ref, wd_ref, wo_ref,
                         mix_ref, ck_ref, cv_ref, sh_ref, sr_ref, wgb_ref, wub_ref, wdb_ref, wob_ref,
                         kk, vv, sht, srs, qs_s, qb_s, kb_s, g_s):
    i = pl.program_id(1)
    nb = pl.num_programs(1)

    @pl.when(i == 0)
    def _():
        kk[...] = jnp.zeros_like(kk)
        vv[...] = jnp.zeros_like(vv)
        for g in range(N_KV_A):
            vv[:, 256 * g + 128:256 * (g + 1)] = jnp.ones((2 * BLK, 128), bf16)
        sht[...] = jnp.zeros_like(sht)
        srs[...] = jnp.zeros_like(srs)

    wgb_ref[...] = wg_ref[...].astype(bf16)
    wub_ref[...] = wu_ref[...].astype(bf16)
    wdb_ref[...] = wd_ref[...].astype(bf16)
    wob_ref[...] = wo_ref[...].astype(bf16)

    rowabs = i * BLK + lax.broadcasted_iota(jnp.int32, (BLK, 1), 0)
    fgate, logf = _forget(proj_ref[:, F_B:F_B + 512], lb_ref[...])
    logf = jnp.where(rowabs >= PAD, logf, 0.0)
    kb_s[...] = jnp.where(rowabs >= PAD, 1.0 - fgate, 0.0)
    qb_s[...] = _silu(proj_ref[:, Q_B:Q_B + 512])
    l1, l2, l3 = _split3(logf)
    tri = tri_ref[...]
    g_s[...] = _dot(tri, l1) + _dot(tri, l2) + _dot(tri, l3)

    r2 = lax.broadcasted_iota(jnp.int32, (BLK, BLK), 0)
    c2 = lax.broadcasted_iota(jnp.int32, (BLK, BLK), 1)
    lane_lo = c2 < HEAD_DIM_A

    def attention_prep():
        qa = _rope_a(_head_norm(proj_ref[:, Q_A:Q_A + 1024], sel_ref, exp_ref, qn_ref[...]), ropea_ref)
        for t in range(N_HEADS_A // 2):
            xt = qa[:, 128 * t:128 * (t + 1)] * Q_SCALE
            qs_s[256 * t:256 * t + 128, :] = jnp.where(lane_lo, xt, 0.0).astype(bf16)
            qs_s[256 * t + 128:256 * (t + 1), :] = jnp.where(lane_lo, 0.0, xt).astype(bf16)
        k_cur = _rope_a(_head_norm(proj_ref[:, K_A:K_A + 256], sel_ref, exp_ref, kn_ref[...]), ropea_ref)
        v_cur = proj_ref[:, V_A:V_A + 256]
        ck_ref[0] = k_cur
        cv_ref[0] = v_cur
        for t in range(N_KV_A // 2):
            for src, dst, width in ((k_cur, kk, 128), (v_cur, vv, 256)):
                xt = src[:, 128 * t:128 * (t + 1)]
                xs = pltpu.roll(xt, HEAD_DIM_A, 1)
                dst[BLK:2 * BLK, width * 2 * t:width * 2 * t + 128] = jnp.where(lane_lo, xt, xs).astype(bf16)
                dst[BLK:2 * BLK, width * (2 * t + 1):width * (2 * t + 1) + 128] = jnp.where(lane_lo, xs, xt).astype(bf16)

    prow = 2 * BLK
    r4 = lax.broadcasted_iota(jnp.int32, (prow, BLK), 0) % BLK
    c4 = lax.broadcasted_iota(jnp.int32, (prow, BLK), 1)
    up = c4 > r4
    ok = c4 >= jnp.where(up, PAD - (i - 1) * BLK, PAD - i * BLK)

    def attention_pair(t):
        g = (2 * t) // GROUP_A
        s2 = _dot(qs_s[prow * t:prow * (t + 1), :], kk[:, 128 * g:128 * (g + 1)], NT)
        s = jnp.where(ok, jnp.where(up, s2[:, 0:BLK], s2[:, BLK:2 * BLK]), NEG_INF)
        sk = jnp.concatenate([jnp.full((BLK, BLK), sink_ref[2 * t + j] * LOG2E, f32) for j in range(2)], axis=0)
        m = jnp.maximum(jnp.max(s, axis=-1, keepdims=True), sk)
        p = jnp.exp2(s - m)
        p2 = jnp.concatenate([jnp.where(up, p, 0.0), jnp.where(up, 0.0, p)], axis=1).astype(bf16)
        res = _dot(p2, vv[:, 256 * g:256 * (g + 1)])
        y = res[:, 0:128] / (res[:, 128:256] + jnp.exp2(sk - m))
        mix_ref[:, 128 * t:128 * (t + 1)] = jnp.where(lane_lo, y[0:BLK], y[BLK:prow]).astype(bf16)

    levels = []
    bz = BLK
    while bz > SUB:
        levels.append((bz, (r2 % bz) >= bz // 2, None if bz == BLK else (r2 // bz) == (c2 // bz)))
        bz //= 2
    same_sub = (r2 // SUB) == (c2 // SUB)
    sub_ge = [(r2 % SUB) >= s for s in range(SUB)]

    def retention_head(h):
        qc = _rope_r(proj_ref[:, Q_C + 128 * h:Q_C + 128 * (h + 1)], roper_ref)
        kc = _rope_r(proj_ref[:, K_C + 128 * h:K_C + 128 * (h + 1)], roper_ref) * (DK_C ** -0.5)
        vc = proj_ref[:, V_C + 128 * h:V_C + 128 * (h + 1)].astype(bf16)
        qcb = qc.astype(bf16)
        sc = _dot(qcb, kc.astype(bf16), NT) * dec_ref[h]
        oc = _dot(qcb, srs[h].astype(bf16)) * rsc_ref[h] + _dot(sc.astype(bf16), vc)
        srs[h] = gpow_ref[h] * srs[h] + _dot((kc * tail_ref[h]).astype(bf16), vc, TN)
        yc = _rms(oc, rn_ref[...]) * _silu(proj_ref[:, G_C + 128 * h:G_C + 128 * (h + 1)])
        mix_ref[:, W_A + 512 + 128 * h:W_A + 512 + 128 * (h + 1)] = yc.astype(bf16)

    def hgrn_intra(h):
        sl = slice(128 * h, 128 * (h + 1))
        gc = g_s[:, sl]
        qb = qb_s[:, sl]
        kb = kb_s[:, sl]
        inter = _dot((qb * jnp.exp2(gc)).astype(bf16), sht[h].astype(bf16), NT)
        amat = None
        for bz, upper, same in levels:
            gref = jnp.concatenate([jnp.broadcast_to(g_s[pl.ds(st + bz // 2 - 1, 1), sl], (bz, 128))
                                    for st in range(0, BLK, bz)], axis=0)
            dg = gc - gref
            e = jnp.exp2(jnp.where(upper, dg, -dg))
            qt = jnp.where(upper, qb * e, 0.0).astype(bf16)
            kt = jnp.where(upper, 0.0, kb * e).astype(bf16)
            pm = _dot(qt, kt, NT)
            if same is not None:
                pm = jnp.where(same, pm, 0.0)
            amat = pm if amat is None else amat + pm
        ys = []
        for s in range(SUB):
            ks = jnp.concatenate([jnp.broadcast_to(kb_s[pl.ds(SUB * j + s, 1), sl], (SUB, 128))
                                  for j in range(BLK // SUB)], axis=0)
            gs = jnp.concatenate([jnp.broadcast_to(g_s[pl.ds(SUB * j + s, 1), sl], (SUB, 128))
                                  for j in range(BLK // SUB)], axis=0)
            ys.append((qb * ks * jnp.exp2(jnp.where(sub_ge[s], gc - gs, NEG_INF))).astype(bf16))
        amat = amat + jnp.where(same_sub, _dot(jnp.concatenate(ys, axis=1), dsel_ref[...]), 0.0)
        return inter, amat.astype(bf16)

    def hgrn_finish(h, inter, amat):
        sl = slice(128 * h, 128 * (h + 1))
        vb = proj_ref[:, I_B + 128 * h:I_B + 128 * (h + 1)].astype(bf16)
        ob = inter + _dot(amat, vb)
        glast = g_s[pl.ds(BLK - 1, 1), sl]
        kt2 = (kb_s[:, sl] * jnp.exp2(glast - g_s[:, sl])).astype(bf16)
        sht[h] = sht[h] * jnp.exp2(glast) + _dot(vb, kt2, TN)
        yb = _rms(ob, hn_ref[...]) * _silu(proj_ref[:, G_B + 128 * h:G_B + 128 * (h + 1)])
        mix_ref[:, W_A + 128 * h:W_A + 128 * (h + 1)] = yb.astype(bf16)

    attention_prep()
    pending = None
    for h in range(H_B):
        cur = hgrn_intra(h)
        if pending is not None:
            hgrn_finish(h - 1, *pending)
        retention_head(h)
        pending = cur
    hgrn_finish(H_B - 1, *pending)
    for t in range(N_HEADS_A // 2):
        attention_pair(t)
    kk[0:BLK, :] = kk[BLK:2 * BLK, :]
    vv[0:BLK, :] = vv[BLK:2 * BLK, :]

    @pl.when(i == nb - 1)
    def _():
        for h in range(H_B):
            sh_ref[0, h] = sht[h].T
            sr_ref[0, h] = srs[h]


def _const_spec(shape):
    nd = len(shape)
    return pl.BlockSpec(shape, lambda *_: (0,) * nd)


def _slab_rows(nrows, steps):
    for r in range(16, nrows + 1, 16):
        if nrows % r == 0 and nrows // r <= steps:
            return r
    return nrows


def _mixer_prompt(proj, rows_total, batch, nb, layer, sinks, gpow, ropea, roper, qn, kn, lb, hn, rn, sel, expand,
                  tri, dsel, dec, rsc, tail, w_gate, w_up, w_down, w_out):
    smem = pl.BlockSpec(memory_space=pltpu.SMEM)
    d, dff = w_gate.shape[1], w_gate.shape[2]

    def row_blk(b, i):
        return jnp.where(i == 0, batch * (nb - 1), b * (nb - 1) + i - 1)

    def mix_blk(b, i):
        return jnp.where((i == 0) & (b > 0), b * (nb - 1), row_blk(b, i))

    def slab_in(nrows, ncols):
        r = _slab_rows(nrows, batch * nb)
        return pl.BlockSpec((None, r, ncols), lambda b, i: (layer, jnp.minimum(b * nb + i, nrows // r - 1), 0))

    def slab_out(nrows, ncols):
        r = _slab_rows(nrows, batch * nb)
        return pl.BlockSpec((r, ncols), lambda b, i: (jnp.minimum(b * nb + i, nrows // r - 1), 0))

    return pl.pallas_call(
        _mixer_prompt_kernel,
        grid=(batch, nb),
        in_specs=[smem, smem,
                  pl.BlockSpec((BLK, D_IN), lambda b, i: (row_blk(b, i), 0)),
                  pl.BlockSpec((2, BLK, 128), lambda b, i: (0, i, 0)),
                  pl.BlockSpec((2, BLK, 128), lambda b, i: (0, i, 0)),
                  _const_spec((1, 1024)), _const_spec((1, 256)), _const_spec((1, 512)),
                  _const_spec((1, 128)), _const_spec((1, 128)),
                  _const_spec((1024, 128)), _const_spec((128, 1024)),
                  _const_spec((BLK, BLK)), _const_spec((SUB * 128, 128)),
                  _const_spec((H_C, BLK, BLK)), _const_spec((H_C, BLK, 128)), _const_spec((H_C, BLK, 128)),
                  slab_in(d, dff), slab_in(d, dff), slab_in(dff, d), slab_in(d, d)],
        out_specs=[pl.BlockSpec((BLK, 2048), lambda b, i: (mix_blk(b, i), 0)),
                   pl.BlockSpec((1, BLK, 256), lambda b, i: (b, 0, 0)),
                   pl.BlockSpec((1, BLK, 256), lambda b, i: (b, 0, 0)),
                   pl.BlockSpec((1, H_B, 128, 128), lambda b, i: (b, 0, 0, 0)),
                   pl.BlockSpec((1, H_C, 128, 128), lambda b, i: (b, 0, 0, 0)),
                   slab_out(d, dff), slab_out(d, dff), slab_out(dff, d), slab_out(d, d)],
        out_shape=[jax.ShapeDtypeStruct((rows_total, 2048), bf16),
                   jax.ShapeDtypeStruct((batch, BLK, 256), f32),
                   jax.ShapeDtypeStruct((batch, BLK, 256), f32),
                   jax.ShapeDtypeStruct((batch, H_B, 128, 128), f32),
                   jax.ShapeDtypeStruct((batch, H_C, 128, 128), f32),
                   jax.ShapeDtypeStruct((d, dff), bf16), jax.ShapeDtypeStruct((d, dff), bf16),
                   jax.ShapeDtypeStruct((dff, d), bf16), jax.ShapeDtypeStruct((d, d), bf16)],
        scratch_shapes=[pltpu.VMEM((2 * BLK, N_KV_A * 128), bf16), pltpu.VMEM((2 * BLK, N_KV_A * 256), bf16),
                        pltpu.VMEM((H_B, 128, 128), f32), pltpu.VMEM((H_C, 128, 128), f32),
                        pltpu.VMEM((N_HEADS_A * BLK, 128), bf16),
                        pltpu.VMEM((BLK, 512), f32), pltpu.VMEM((BLK, 512), f32), pltpu.VMEM((BLK, 512), f32)],
        compiler_params=pltpu.CompilerParams(dimension_semantics=("arbitrary", "arbitrary"),
                                             vmem_limit_bytes=VMEM_LIMIT),
        name="mixer_prompt",
    )(sinks, gpow, proj, ropea, roper, qn, kn, lb, hn, rn, sel, expand, tri, dsel, dec, rsc, tail,
      w_gate, w_up, w_down, w_out)


def _mixer_sample_kernel(sink_ref, gpow_ref, mixin_ref, *refs, dec_seq, n_prev):
    del mixin_ref
    _mixer_sample_body(sink_ref, gpow_ref, *refs[:17], *refs[17 + n_prev:], dec_seq=dec_seq)


def _mixer_sample_body(sink_ref, gpow_ref, proj_ref, ropea_ref, roper_ref, qn_ref, kn_ref, lb_ref,
                       hn_ref, rn_ref, sel_ref, exp_ref, dec_ref, rsc_ref, tail_ref,
                       ck_ref, cv_ref, sh_ref, sr_ref,
                       mix_ref, nk_ref, nv_ref, nsh_ref, nsr_ref, *, dec_seq):
    nbat = SROWS // dec_seq
    w = ck_ref.shape[1]
    rb = lax.broadcasted_iota(jnp.int32, (SROWS, 1), 0) // dec_seq
    rt = lax.broadcasted_iota(jnp.int32, (SROWS, 1), 0) % dec_seq

    def pick(parts):
        out = parts[nbat - 1]
        for b in range(nbat - 2, -1, -1):
            out = jnp.where(rb == b, parts[b], out)
        return out

    fgate, logf = _forget(proj_ref[:, F_B:F_B + 512], lb_ref[...])
    kb = 1.0 - fgate
    qb = _silu(proj_ref[:, Q_B:Q_B + 512])
    vb = proj_ref[:, I_B:I_B + 512]
    gcum = logf
    for d in range(1, dec_seq):
        gcum = gcum + jnp.where(rt >= d, pltpu.roll(logf, d, 0), 0.0)
    intra = [jnp.zeros((SROWS, 128), f32) for _ in range(H_B)]
    for s in range(SROWS):
        ok = (rb == s // dec_seq) & (rt >= s % dec_seq)
        y = qb * kb[s:s + 1, :] * jnp.exp2(jnp.where(ok, gcum - gcum[s:s + 1, :], NEG_INF))
        for h in range(H_B):
            sl = slice(128 * h, 128 * (h + 1))
            intra[h] = intra[h] + jnp.sum(y[:, sl], axis=-1, keepdims=True) * vb[s:s + 1, sl]

    qa = _rope_a(_head_norm(proj_ref[:, Q_A:Q_A + 1024], sel_ref, exp_ref, qn_ref[...]), ropea_ref)
    qs = (qa * Q_SCALE).astype(bf16)
    k_new = _rope_a(_head_norm(proj_ref[:, K_A:K_A + 256], sel_ref, exp_ref, kn_ref[...]), ropea_ref)
    v_new = proj_ref[:, V_A:V_A + 256]
    for b in range(nbat):
        nk_ref[b, 0:w - dec_seq, :] = ck_ref[b, dec_seq:w, :]
        nk_ref[b, w - dec_seq:w, :] = k_new[dec_seq * b:dec_seq * (b + 1), :]
        nv_ref[b, 0:w - dec_seq, :] = cv_ref[b, dec_seq:w, :]
        nv_ref[b, w - dec_seq:w, :] = v_new[dec_seq * b:dec_seq * (b + 1), :]
    knb = k_new.astype(bf16)
    vnb = v_new.astype(bf16)

    grows = GROUP_A * SROWS
    rt_g = jnp.concatenate([rt] * GROUP_A, axis=0)
    rb_g = jnp.concatenate([rb] * GROUP_A, axis=0)
    jc = lax.broadcasted_iota(jnp.int32, (grows, w), 1)
    valid_c = jc > rt_g + (w - WINDOW)
    cn = lax.broadcasted_iota(jnp.int32, (grows, SROWS), 1)
    valid_n = (cn // dec_seq == rb_g) & (cn % dec_seq <= rt_g)
    heads = [None] * N_HEADS_A
    for g in range(N_KV_A):
        ksl = slice(64 * g, 64 * (g + 1))
        qg = jnp.concatenate([qs[:, 64 * h:64 * (h + 1)] for h in range(GROUP_A * g, GROUP_A * (g + 1))], axis=0)
        sparts = [_dot(qg, ck_ref[b, :, ksl].astype(bf16), NT) for b in range(nbat)]
        s_c = sparts[nbat - 1]
        for b in range(nbat - 2, -1, -1):
            s_c = jnp.where(rb_g == b, sparts[b], s_c)
        s_c = jnp.where(valid_c, s_c, NEG_INF)
        s_n = jnp.where(valid_n, _dot(qg, knb[:, ksl], NT), NEG_INF)
        sk = jnp.concatenate([jnp.full((SROWS, 1), sink_ref[h] * LOG2E, f32)
                              for h in range(GROUP_A * g, GROUP_A * (g + 1))], axis=0)
        m = jnp.maximum(jnp.maximum(jnp.max(s_c, axis=-1, keepdims=True), jnp.max(s_n, axis=-1, keepdims=True)), sk)
        p_c = jnp.exp2(s_c - m)
        p_n = jnp.exp2(s_n - m)
        den = jnp.sum(p_c, axis=-1, keepdims=True) + jnp.sum(p_n, axis=-1, keepdims=True) + jnp.exp2(sk - m)
        pcb = p_c.astype(bf16)
        oparts = [_dot(pcb, cv_ref[b, :, ksl].astype(bf16)) for b in range(nbat)]
        o = oparts[nbat - 1]
        for b in range(nbat - 2, -1, -1):
            o = jnp.where(rb_g == b, oparts[b], o)
        o = (o + _dot(p_n.astype(bf16), vnb[:, ksl])) / den
        for hh in range(GROUP_A):
            heads[GROUP_A * g + hh] = o[SROWS * hh:SROWS * (hh + 1), :]
    mix_ref[:, 0:W_A] = jnp.concatenate(heads, axis=1).astype(bf16)

    qg = (qb * jnp.exp2(gcum)).astype(bf16)
    vbb = vb.astype(bf16)
    glast = [gcum[dec_seq * (b + 1) - 1:dec_seq * (b + 1), :] for b in range(nbat)]
    for h in range(H_B):
        sl = slice(128 * h, 128 * (h + 1))
        ob = pick([_dot(qg[:, sl], sh_ref[b, h].astype(bf16)) for b in range(nbat)]) + intra[h]
        for b in range(nbat):
            kt2 = jnp.where(rb == b, kb[:, sl] * jnp.exp2(glast[b][:, sl] - gcum[:, sl]), 0.0).astype(bf16)
            dcol = jnp.broadcast_to(jnp.exp2(glast[b][:, sl]), (128, 128)).T
            nsh_ref[b, h] = dcol * sh_ref[b, h] + _dot(kt2, vbb[:, sl], TN)
        yb = _rms(ob, hn_ref[...]) * _silu(proj_ref[:, G_B + 128 * h:G_B + 128 * (h + 1)])
        mix_ref[:, W_A + 128 * h:W_A + 128 * (h + 1)] = yb.astype(bf16)

    for h in range(H_C):
        qc = _rope_r(proj_ref[:, Q_C + 128 * h:Q_C + 128 * (h + 1)], roper_ref)
        kc = _rope_r(proj_ref[:, K_C + 128 * h:K_C + 128 * (h + 1)], roper_ref) * (DK_C ** -0.5)
        vc = proj_ref[:, V_C + 128 * h:V_C + 128 * (h + 1)]
        qcb = qc.astype(bf16)
        vcb = vc.astype(bf16)
        sc = _dot(qcb, kc.astype(bf16), NT) * dec_ref[h]
        oc = pick([_dot(qcb, sr_ref[b, h].astype(bf16)) for b in range(nbat)]) * rsc_ref[h] \
            + _dot(sc.astype(bf16), vcb)
        kct = kc * tail_ref[h]
        for b in range(nbat):
            nsr_ref[b, h] = gpow_ref[h] * sr_ref[b, h] + _dot(jnp.where(rb == b, kct, 0.0).astype(bf16), vcb, TN)
        yc = _rms(oc, rn_ref[...]) * _silu(proj_ref[:, G_C + 128 * h:G_C + 128 * (h + 1)])
        mix_ref[:, W_A + 512 + 128 * h:W_A + 512 + 128 * (h + 1)] = yc.astype(bf16)


def _mixer_sample(mix, proj, row0, layer, dec_batch, dec_seq, sinks, gpow, ropea, roper, qn, kn, lb, hn, rn, sel, expand,
                  dec, rsc, tail, cache_k, cache_v, state_h, state_r, prev_outs):
    smem = pl.BlockSpec(memory_space=pltpu.SMEM)
    nbat = SROWS // dec_seq
    steps = dec_batch // nbat
    blk0 = row0 // SROWS
    w = cache_k.shape[1]
    n_all = cache_k.shape[0]
    assert mix.shape[0] == row0 + steps * SROWS
    any_spec = pl.BlockSpec(memory_space=pl.ANY)
    n_in = 20
    return pl.pallas_call(
        functools.partial(_mixer_sample_kernel, dec_seq=dec_seq, n_prev=len(prev_outs)),
        grid=(steps,),
        in_specs=[smem, smem,
                  pl.BlockSpec(memory_space=pl.ANY),
                  pl.BlockSpec((SROWS, D_IN), lambda c: (blk0 + c, 0)),
                  _const_spec((2, SROWS, 128)), _const_spec((2, SROWS, 128)),
                  _const_spec((1, 1024)), _const_spec((1, 256)), _const_spec((1, 512)),
                  _const_spec((1, 128)), _const_spec((1, 128)),
                  _const_spec((1024, 128)), _const_spec((128, 1024)),
                  _const_spec((H_C, SROWS, SROWS)), _const_spec((H_C, SROWS, 128)), _const_spec((H_C, SROWS, 128)),
                  pl.BlockSpec((nbat, w, 256), lambda c: (layer * steps + c, 0, 0)),
                  pl.BlockSpec((nbat, w, 256), lambda c: (layer * steps + c, 0, 0)),
                  pl.BlockSpec((nbat, H_B, 128, 128), lambda c: (layer * steps + c, 0, 0, 0)),
                  pl.BlockSpec((nbat, H_C, 128, 128), lambda c: (layer * steps + c, 0, 0, 0))]
        + [any_spec] * len(prev_outs),
        out_specs=[pl.BlockSpec((SROWS, 2048), lambda c: (blk0 + c, 0)),
                   pl.BlockSpec((nbat, w, 256), lambda c: (layer * steps + c, 0, 0)),
                   pl.BlockSpec((nbat, w, 256), lambda c: (layer * steps + c, 0, 0)),
                   pl.BlockSpec((nbat, H_B, 128, 128), lambda c: (layer * steps + c, 0, 0, 0)),
                   pl.BlockSpec((nbat, H_C, 128, 128), lambda c: (layer * steps + c, 0, 0, 0))],
        out_shape=[jax.ShapeDtypeStruct(mix.shape, bf16),
                   jax.ShapeDtypeStruct((n_all, w, 256), f32),
                   jax.ShapeDtypeStruct((n_all, w, 256), f32),
                   jax.ShapeDtypeStruct((n_all, H_B, 128, 128), f32),
                   jax.ShapeDtypeStruct((n_all, H_C, 128, 128), f32)],
        input_output_aliases={2: 0, **{n_in + k: 1 + k for k in range(len(prev_outs))}},
        compiler_params=pltpu.CompilerParams(dimension_semantics=("arbitrary",), vmem_limit_bytes=VMEM_LIMIT),
        name="mixer_sample",
    )(sinks, gpow, mix, proj, ropea, roper, qn, kn, lb, hn, rn, sel, expand, dec, rsc, tail,
      cache_k, cache_v, state_h, state_r, *prev_outs)


def _rope_tables(pos):
    posf = jnp.asarray(pos).astype(f32)[:, None]
    t = pos.shape[0]
    half = N_ROT // 2
    inv = ROPE_THETA ** (-jnp.arange(half, dtype=f32) * (2.0 / N_ROT))
    ang = posf * inv[None, :]
    cos, sin = jnp.cos(ang), jnp.sin(ang)
    rest0 = jnp.zeros((t, HEAD_DIM_A - N_ROT), f32)
    c64 = jnp.concatenate([cos, cos, rest0 + 1.0], axis=1)
    s64 = jnp.concatenate([sin, sin, rest0], axis=1)
    ropea = jnp.stack([jnp.tile(c64, (1, 2)), jnp.tile(s64, (1, 2))])
    invr = RET_THETA ** (-jnp.arange(DK_C // 2, dtype=f32) * (2.0 / DK_C))
    angr = posf * invr[None, :]
    cr, sr = jnp.cos(angr), jnp.sin(angr)
    roper = jnp.stack([jnp.concatenate([cr, cr], axis=1), jnp.concatenate([-sr, sr], axis=1)])
    return ropea, roper


def _ret_tables(lg, seq_of_row, tok_of_row, length):
    nf = np.float32
    tq = tok_of_row.astype(nf)
    rel = tq[:, None] - tq[None, :]
    ok = (seq_of_row[:, None] == seq_of_row[None, :]) & (rel >= 0)
    dec = np.where(ok[None], np.exp(np.where(ok, rel, nf(0.0))[None] * lg[:, None, None]), nf(0.0))
    n = tq.shape[0]
    rsc = np.broadcast_to(np.exp((tq + nf(1.0))[None, :, None] * lg[:, None, None]), (H_C, n, 128))
    tail = np.broadcast_to(np.exp((nf(length) - nf(1.0) - tq)[None, :, None] * lg[:, None, None]), (H_C, n, 128))
    gpow = np.exp(nf(length) * lg)
    return dec.astype(nf), np.ascontiguousarray(rsc, nf), np.ascontiguousarray(tail, nf), gpow.astype(nf)


def kernel(x_prompt, x_sample, cache_k, cache_v, state_hgrn, state_ret, meta_tokens, norm_mix, norm_ffn, w_in, q_norm, k_norm, attn_sinks, hgrn_lb, hgrn_norm, ret_norm, w_out, w_gate, w_up, w_down):
    batch, seq, d = x_prompt.shape
    dec_batch, dec_seq, _ = x_sample.shape
    depth = w_in.shape[0]
    w = cache_k.shape[2]
    assert d == 2048 and w_in.shape[2] == D_IN and seq % BLK == 0
    assert SROWS % dec_seq == 0 and dec_batch % (SROWS // dec_seq) == 0 and w == WINDOW
    nb = seq // BLK + 1
    lp = nb * BLK
    rows_main = batch * seq
    rows_s = dec_batch * dec_seq
    row_s0 = rows_main + BLK
    tail_rows = BLK + rows_s
    rows = rows_main + tail_rows
    tm_ffn = _row_tile(rows, TM_FFN, 16)
    assert rows_main % tail_rows == 0 and rows_s % SROWS == 0 and tail_rows % 16 == 0 and tm_ffn >= tail_rows

    tail = jnp.concatenate([jnp.zeros((PAD, d), f32), meta_tokens.astype(f32), x_sample.reshape(rows_s, d)], axis=0)
    x, h = _embed(x_prompt.reshape(rows_main, d), tail, norm_mix[0][None])

    p = jax.nn.softmax(hgrn_lb.astype(f32), axis=0)
    lbs = jnp.cumsum(p, axis=0) - p[0]
    lg = np.log1p(-np.exp2(np.float32(-5.0) - np.arange(H_C, dtype=np.float32))).astype(np.float32)
    ropea_p, roper_p = _rope_tables(np.arange(lp) - PAD)
    srow = np.arange(SROWS)
    ropea_s, roper_s = _rope_tables(PAST_LEN + srow % dec_seq)
    dec_p, rsc_p, tail_p, gpow_p = _ret_tables(lg, np.zeros((BLK,), np.int32), np.arange(BLK), float(BLK))
    dec_s, rsc_s, tail_s, gpow_s = _ret_tables(lg, srow // dec_seq, srow % dec_seq, float(dec_seq))
    sel_np = np.arange(1024)[:, None] // HEAD_DIM_A == np.arange(128)[None, :]
    sel = jnp.asarray(sel_np, bf16)
    expand = jnp.asarray(sel_np.T, bf16)
    tri = jnp.asarray(np.arange(BLK)[:, None] >= np.arange(BLK)[None, :], bf16)
    dsel = jnp.asarray(np.arange(SUB * 128)[:, None] // 128 == np.arange(128)[None, :] % SUB, bf16)

    ck_flat = cache_k.reshape(depth * dec_batch, w, N_KV_A * HEAD_DIM_A)
    cv_flat = cache_v.reshape(depth * dec_batch, w, N_KV_A * HEAD_DIM_A)
    sh_flat = state_hgrn.reshape(depth * dec_batch, H_B, 128, 128)
    sr_flat = state_ret.reshape(depth * dec_batch, H_C, 128, 128)

    outs_p, outs_s = [], ()
    for l in range(depth):
        proj = _inproj(h, w_in, l)
        qn = jnp.tile(q_norm[l], N_HEADS_A)[None]
        kn = jnp.tile(k_norm[l], N_KV_A)[None]
        common = (qn, kn, lbs[l][None], hgrn_norm[l][None], ret_norm[l][None], sel, expand)
        mix, ck, cv, sh, sr, wg_b, wu_b, wd_b, wo_b = _mixer_prompt(
            proj, rows, batch, nb, l, attn_sinks[l], gpow_p, ropea_p, roper_p, *common, tri, dsel,
            dec_p, rsc_p, tail_p, w_gate, w_up, w_down, w_out)
        mix, *outs_s = _mixer_sample(mix, proj, row_s0, l, dec_batch, dec_seq, attn_sinks[l], gpow_s,
                                     ropea_s, roper_s, *common, dec_s, rsc_s, tail_s,
                                     ck_flat, cv_flat, sh_flat, sr_flat, tuple(outs_s))
        ffn_w = (wo_b, norm_ffn[l][None], wg_b, wu_b, wd_b)
        outs_p.append((ck, cv, sh, sr))
        if l + 1 < depth:
            x, h = _ffn(x, mix, *ffn_w, norm_mix[l + 1][None], tm_ffn, tail_rows)
        else:
            y_main, y_tail = _ffn(x, mix, *ffn_w, None, tm_ffn, tail_rows)

    y_prompt = y_main.reshape(batch, seq, d)
    y_sample = y_tail[BLK:].reshape(dec_batch, dec_seq, d)
    kv_shape = (depth, -1, w, N_KV_A, HEAD_DIM_A)
    st_shape = (depth, dec_batch, H_B, 128, 128)
    stack = lambda outs, k: jnp.stack([o[k] for o in outs])
    nk, nv, nsh, nsr = outs_s
    return (y_prompt, y_sample,
            stack(outs_p, 0).reshape(kv_shape), stack(outs_p, 1).reshape(kv_shape), stack(outs_p, 2), stack(outs_p, 3),
            nk.reshape(kv_shape), nv.reshape(kv_shape), nsh.reshape(st_shape), nsr.reshape(st_shape))
```

```python
import functools

import jax
import jax.numpy as jnp
import numpy as np
from jax import lax
from jax.experimental import pallas as pl
from jax.experimental.pallas import tpu as pltpu

f32 = jnp.float32
bf16 = jnp.bfloat16

N_META = 16
EPS = 1e-6
NEG_INF = -1e30
LB_FLOOR = 1e-30
WINDOW = 128
HEAD_DIM_A = 64
N_HEADS_A = 16
N_KV_A = 4
GROUP_A = N_HEADS_A // N_KV_A
N_ROT = 16
ROPE_THETA = 500000.0
H_B = 4
DK_B = 128
H_C = 4
DK_C = 128
RET_THETA = 10000.0
PAST_LEN = 16384

BLK = 128
PAD = BLK - N_META
SUB = 8
SROWS = 32
TM_FFN = 704
VMEM_LIMIT = 56 * 1024 * 1024
VMEM_LIMIT_FFN = 60 * 1024 * 1024

Q_A, K_A, V_A = 0, 1024, 1280
Q_B, F_B, I_B, G_B = 1536, 2048, 2560, 3072
Q_C, K_C, V_C, G_C = 3584, 4096, 4608, 5120
D_IN = 5632
W_A = 1024
LOG2E = 1.4426950408889634
Q_SCALE = HEAD_DIM_A ** -0.5 * LOG2E
NT = (((1,), (1,)), ((), ()))
TN = (((0,), (0,)), ((), ()))


def _dot(a, b, dims=None):
    if dims is None:
        return jnp.dot(a, b, preferred_element_type=f32)
    return lax.dot_general(a, b, dims, preferred_element_type=f32)


def _row_tile(rows, cap, align=8):
    best = align
    for t in range(align, cap + 1, align):
        if rows % t == 0:
            best = t
    return best


def _silu(x):
    return x * jax.nn.sigmoid(x)


def _rms(x, g):
    return x * lax.rsqrt(jnp.mean(x * x, axis=-1, keepdims=True) + EPS) * g


def _inproj_kernel(h_ref, w_hbm, o_hbm, w_buf, o_buf, sem_w, sem_o, *, layer, n_col, tn):
    i = pl.program_id(0)
    tm = h_ref.shape[0]
    n_chunks = pl.num_programs(0) * n_col

    def w_copy(c):
        col = pl.multiple_of((c % n_col) * tn, tn)
        return pltpu.make_async_copy(w_hbm.at[layer, :, pl.ds(col, tn)], w_buf.at[c % 2], sem_w.at[c % 2])

    def o_copy(c):
        row = pl.multiple_of((c // n_col) * tm, tm)
        col = pl.multiple_of((c % n_col) * tn, tn)
        return pltpu.make_async_copy(o_buf.at[c % 2], o_hbm.at[pl.ds(row, tm), pl.ds(col, tn)], sem_o.at[c % 2])

    @pl.when(i == 0)
    def _():
        w_copy(0).start()

    def col_step(j, carry):
        c = i * n_col + j
        w_copy(c).wait()

        @pl.when(c + 1 < n_chunks)
        def _():
            w_copy(c + 1).start()

        @pl.when(c >= 2)
        def _():
            o_copy(c - 2).wait()

        slot = c % 2
        o_buf[slot] = _dot(h_ref[...], w_buf[slot].astype(bf16))
        o_copy(c).start()
        return carry

    lax.fori_loop(0, n_col, col_step, 0)

    @pl.when(i == pl.num_programs(0) - 1)
    def _():
        o_copy(n_chunks - 2).wait()
        o_copy(n_chunks - 1).wait()


def _inproj(h, w_all, layer):
    rows, d = h.shape
    n = w_all.shape[2]
    tm = _row_tile(rows, 2816, 16)
    tn = 512
    assert (rows // tm) * (n // tn) >= 2
    return pl.pallas_call(
        functools.partial(_inproj_kernel, layer=layer, n_col=n // tn, tn=tn),
        grid=(rows // tm,),
        in_specs=[pl.BlockSpec((tm, d), lambda i: (i, 0)),
                  pl.BlockSpec(memory_space=pl.ANY)],
        out_specs=pl.BlockSpec(memory_space=pl.ANY),
        out_shape=jax.ShapeDtypeStruct((rows, n), f32),
        scratch_shapes=[pltpu.VMEM((2, d, tn), f32), pltpu.VMEM((2, tm, tn), f32),
                        pltpu.SemaphoreType.DMA((2,)), pltpu.SemaphoreType.DMA((2,))],
        compiler_params=pltpu.CompilerParams(dimension_semantics=("arbitrary",), vmem_limit_bytes=VMEM_LIMIT),
        name="inproj",
    )(h, w_all)


def _embed_kernel(xp_ref, tail_ref, g_ref, x_ref, h_ref):
    i = pl.program_id(0)
    last = pl.num_programs(0) - 1

    def emit(src_ref):
        x = src_ref[...]
        x_ref[...] = x
        h_ref[...] = _rms(x, g_ref[...]).astype(bf16)

    pl.when(i < last)(lambda: emit(xp_ref))
    pl.when(i == last)(lambda: emit(tail_ref))


def _embed(xp, tail, g):
    rows_main, d = xp.shape
    tm = tail.shape[0]
    n_main = rows_main // tm
    rows = rows_main + tm
    return pl.pallas_call(
        _embed_kernel,
        grid=(n_main + 1,),
        in_specs=[pl.BlockSpec((tm, d), lambda i: (jnp.minimum(i, n_main - 1), 0)),
                  pl.BlockSpec((tm, d), lambda i: (0, 0)),
                  pl.BlockSpec((1, d), lambda i: (0, 0))],
        out_specs=[pl.BlockSpec((tm, d), lambda i: (i, 0)),
                   pl.BlockSpec((tm, d), lambda i: (i, 0))],
        out_shape=[jax.ShapeDtypeStruct((rows, d), f32), jax.ShapeDtypeStruct((rows, d), bf16)],
        compiler_params=pltpu.CompilerParams(dimension_semantics=("arbitrary",), vmem_limit_bytes=VMEM_LIMIT),
        name="embed_norm",
    )(xp, tail, g)


def _ffn_kernel(*refs, with_next, tail_rows, n_ff, tf):
    if with_next:
        (x_ref, mix_ref, wo_ref, nf_ref, wg_hbm, wu_hbm, wd_hbm, gn_ref, o_ref, hn_ref,
         wg_buf, wu_buf, wd_buf, sem) = refs
        h_ref = hn_ref
    else:
        (x_ref, mix_ref, wo_ref, nf_ref, wg_hbm, wu_hbm, wd_hbm, o_ref, tail_ref,
         h_ref, wg_buf, wu_buf, wd_buf, sem) = refs
    i = pl.program_id(0)
    n_chunks = pl.num_programs(0) * n_ff

    def chunk_copies(c):
        jt = c % n_ff
        slot = c % 2
        col = pl.multiple_of(jt * tf, tf)
        return (pltpu.make_async_copy(wg_hbm.at[:, pl.ds(col, tf)], wg_buf.at[slot], sem.at[0, slot]),
                pltpu.make_async_copy(wu_hbm.at[:, pl.ds(col, tf)], wu_buf.at[slot], sem.at[1, slot]),
                pltpu.make_async_copy(wd_hbm.at[pl.ds(col, tf), :], wd_buf.at[slot], sem.at[2, slot]))

    @pl.when(i == 0)
    def _():
        for cp in chunk_copies(0):
            cp.start()

    x1 = x_ref[...] + _dot(mix_ref[...], wo_ref[...])
    h_ref[...] = _rms(x1, nf_ref[...]).astype(bf16)
    o_ref[...] = x1

    def ff_step(j, carry):
        c = i * n_ff + j
        for cp in chunk_copies(c):
            cp.wait()

        @pl.when(c + 1 < n_chunks)
        def _():
            for cp in chunk_copies(c + 1):
                cp.start()

        slot = c % 2
        h = h_ref[...]
        a = _silu(_dot(h, wg_buf[slot])) * _dot(h, wu_buf[slot])
        o_ref[...] += _dot(a.astype(bf16), wd_buf[slot])
        return carry

    lax.fori_loop(0, n_ff, ff_step, 0)

    if with_next:
        hn_ref[...] = _rms(o_ref[...], gn_ref[...]).astype(bf16)
    else:
        @pl.when(i == pl.num_programs(0) - 1)
        def _():
            tail_ref[...] = o_ref[o_ref.shape[0] - tail_rows:, :]


def _ffn(x, mix, wo, nf, wg, wu, wd, g_next, tm, tail_rows):
    rows, d = x.shape
    dff = wg.shape[1]
    tf = 512
    n_tiles = rows // tm
    with_next = g_next is not None
    row_spec = pl.BlockSpec((tm, d), lambda i: (i, 0))
    out_spec = row_spec
    hbm = pl.BlockSpec(memory_space=pl.ANY)
    in_specs = [row_spec, row_spec,
                pl.BlockSpec((d, d), lambda i: (0, 0), pipeline_mode=pl.Buffered(1)),
                pl.BlockSpec((1, d), lambda i: (0, 0)),
                hbm, hbm, hbm]
    args = [x, mix, wo, nf, wg, wu, wd]
    if with_next:
        in_specs.append(pl.BlockSpec((1, d), lambda i: (0, 0)))
        args.append(g_next)
        out_specs = [out_spec, out_spec]
        out_shape = [jax.ShapeDtypeStruct((rows, d), f32), jax.ShapeDtypeStruct((rows, d), bf16)]
    else:
        assert tail_rows % 8 == 0 and tail_rows <= tm
        out_specs = [out_spec, pl.BlockSpec((tail_rows, d), lambda i: (0, 0))]
        out_shape = [jax.ShapeDtypeStruct((rows - tail_rows, d), f32), jax.ShapeDtypeStruct((tail_rows, d), f32)]
    weight_ring = [pltpu.VMEM((2, d, tf), bf16), pltpu.VMEM((2, d, tf), bf16), pltpu.VMEM((2, tf, d), bf16),
                   pltpu.SemaphoreType.DMA((3, 2))]
    return pl.pallas_call(
        functools.partial(_ffn_kernel, with_next=with_next, tail_rows=tail_rows, n_ff=dff // tf, tf=tf),
        grid=(n_tiles,),
        in_specs=in_specs,
        out_specs=out_specs,
        out_shape=out_shape,
        scratch_shapes=([] if with_next else [pltpu.VMEM((tm, d), bf16)]) + weight_ring,
        compiler_params=pltpu.CompilerParams(dimension_semantics=("arbitrary",),
                                             vmem_limit_bytes=VMEM_LIMIT_FFN),
        name="outproj_ffn",
    )(*args)


def _head_norm(x, sel_ref, exp_ref, g):
    n, w = x.shape
    x2 = x * x
    hi = x2.astype(bf16)
    lo = (x2 - hi.astype(f32)).astype(bf16)
    s = _dot(jnp.concatenate([hi, lo], axis=0), sel_ref[0:w, :])
    return x * _dot_split3(lax.rsqrt((s[0:n] + s[n:2 * n]) * (1.0 / HEAD_DIM_A) + EPS), exp_ref[:, 0:w]) * g


def _dot_split3(x, m):
    n = x.shape[0]
    y = _dot(jnp.concatenate(_split3(x), axis=0), m)
    return y[0:n] + y[n:2 * n] + y[2 * n:3 * n]


def _rope_a(x, tab_ref):
    half = N_ROT // 2
    first = lax.broadcasted_iota(jnp.int32, (x.shape[0], 128), 1) % HEAD_DIM_A < half
    out = []
    for t in range(x.shape[1] // 128):
        xt = x[:, 128 * t:128 * (t + 1)]
        partner = jnp.where(first, -pltpu.roll(xt, 128 - half, 1), pltpu.roll(xt, half, 1))
        out.append(xt * tab_ref[0] + partner * tab_ref[1])
    return jnp.concatenate(out, axis=1)


def _rope_r(x, tab_ref):
    return x * tab_ref[0] + pltpu.roll(x, DK_C // 2, 1) * tab_ref[1]


def _forget(fb, lb):
    f = jnp.maximum(lb, LB_FLOOR) + (1.0 - lb) * jax.nn.sigmoid(fb)
    return f, jnp.log2(f)


def _split3(x):
    h1 = x.astype(bf16)
    r1 = x - h1.astype(f32)
    h2 = r1.astype(bf16)
    h3 = (r1 - h2.astype(f32)).astype(bf16)
    return h1, h2, h3


def _mixer_prompt_kernel(sink_ref, gpow_ref, proj_ref, ropea_ref, roper_ref, qn_ref, kn_ref, lb_ref, hn_ref, rn_ref,
                         sel_ref, exp_ref, tri_ref, dsel_ref, dec_ref, rsc_ref, tail_ref,
                         wg_ref, wu_ref, wd_ref, wo_ref,
                         mix_ref, ck_ref, cv_ref, sh_ref, sr_ref, wgb_ref, wub_ref, wdb_ref, wob_ref,
                         kk, vv, sht, srs, qs_s, qb_s, kb_s, g_s):
    i = pl.program_id(1)
    nb = pl.num_programs(1)

    @pl.when(i == 0)
    def _():
        kk[...] = jnp.zeros_like(kk)
        vv[...] = jnp.zeros_like(vv)
        for g in range(N_KV_A):
            vv[:, 256 * g + 128:256 * (g + 1)] = jnp.ones((2 * BLK, 128), bf16)
        sht[...] = jnp.zeros_like(sht)
        srs[...] = jnp.zeros_like(srs)

    wgb_ref[...] = wg_ref[...].astype(bf16)
    wub_ref[...] = wu_ref[...].astype(bf16)
    wdb_ref[...] = wd_ref[...].astype(bf16)
    wob_ref[...] = wo_ref[...].astype(bf16)

    rowabs = i * BLK + lax.broadcasted_iota(jnp.int32, (BLK, 1), 0)
    fgate, logf = _forget(proj_ref[:, F_B:F_B + 512], lb_ref[...])
    logf = jnp.where(rowabs >= PAD, logf, 0.0)
    kb_s[...] = jnp.where(rowabs >= PAD, 1.0 - fgate, 0.0)
    qb_s[...] = _silu(proj_ref[:, Q_B:Q_B + 512])
    l1, l2, l3 = _split3(logf)
    tri = tri_ref[...]
    g_s[...] = _dot(tri, l1) + _dot(tri, l2) + _dot(tri, l3)

    r2 = lax.broadcasted_iota(jnp.int32, (BLK, BLK), 0)
    c2 = lax.broadcasted_iota(jnp.int32, (BLK, BLK), 1)
    lane_lo = c2 < HEAD_DIM_A

    def attention_prep():
        qa = _rope_a(_head_norm(proj_ref[:, Q_A:Q_A + 1024], sel_ref, exp_ref, qn_ref[...]), ropea_ref)
        for t in range(N_HEADS_A // 2):
            xt = qa[:, 128 * t:128 * (t + 1)] * Q_SCALE
            qs_s[256 * t:256 * t + 128, :] = jnp.where(lane_lo, xt, 0.0).astype(bf16)
            qs_s[256 * t + 128:256 * (t + 1), :] = jnp.where(lane_lo, 0.0, xt).astype(bf16)
        k_cur = _rope_a(_head_norm(proj_ref[:, K_A:K_A + 256], sel_ref, exp_ref, kn_ref[...]), ropea_ref)
        v_cur = proj_ref[:, V_A:V_A + 256]
        ck_ref[0] = k_cur
        cv_ref[0] = v_cur
        for t in range(N_KV_A // 2):
            for src, dst, width in ((k_cur, kk, 128), (v_cur, vv, 256)):
                xt = src[:, 128 * t:128 * (t + 1)]
                xs = pltpu.roll(xt, HEAD_DIM_A, 1)
                dst[BLK:2 * BLK, width * 2 * t:width * 2 * t + 128] = jnp.where(lane_lo, xt, xs).astype(bf16)
                dst[BLK:2 * BLK, width * (2 * t + 1):width * (2 * t + 1) + 128] = jnp.where(lane_lo, xs, xt).astype(bf16)

    prow = 2 * BLK
    r4 = lax.broadcasted_iota(jnp.int32, (prow, BLK), 0) % BLK
    c4 = lax.broadcasted_iota(jnp.int32, (prow, BLK), 1)
    up = c4 > r4
    ok = c4 >= jnp.where(up, PAD - (i - 1) * BLK, PAD - i * BLK)

    def attention_pair(t):
        g = (2 * t) // GROUP_A
        s2 = _dot(qs_s[prow * t:prow * (t + 1), :], kk[:, 128 * g:128 * (g + 1)], NT)
        s = jnp.where(ok, jnp.where(up, s2[:, 0:BLK], s2[:, BLK:2 * BLK]), NEG_INF)
        sk = jnp.concatenate([jnp.full((BLK, BLK), sink_ref[2 * t + j] * LOG2E, f32) for j in range(2)], axis=0)
        m = jnp.maximum(jnp.max(s, axis=-1, keepdims=True), sk)
        p = jnp.exp2(s - m)
        p2 = jnp.concatenate([jnp.where(up, p, 0.0), jnp.where(up, 0.0, p)], axis=1).astype(bf16)
        res = _dot(p2, vv[:, 256 * g:256 * (g + 1)])
        y = res[:, 0:128] / (res[:, 128:256] + jnp.exp2(sk - m))
        mix_ref[:, 128 * t:128 * (t + 1)] = jnp.where(lane_lo, y[0:BLK], y[BLK:prow]).astype(bf16)

    levels = []
    bz = BLK
    while bz > SUB:
        levels.append((bz, (r2 % bz) >= bz // 2, None if bz == BLK else (r2 // bz) == (c2 // bz)))
        bz //= 2
    same_sub = (r2 // SUB) == (c2 // SUB)
    sub_ge = [(r2 % SUB) >= s for s in range(SUB)]

    def retention_head(h):
        qc = _rope_r(proj_ref[:, Q_C + 128 * h:Q_C + 128 * (h + 1)], roper_ref)
        kc = _rope_r(proj_ref[:, K_C + 128 * h:K_C + 128 * (h + 1)], roper_ref) * (DK_C ** -0.5)
        vc = proj_ref[:, V_C + 128 * h:V_C + 128 * (h + 1)].astype(bf16)
        qcb = qc.astype(bf16)
        sc = _dot(qcb, kc.astype(bf16), NT) * dec_ref[h]
        oc = _dot(qcb, srs[h].astype(bf16)) * rsc_ref[h] + _dot(sc.astype(bf16), vc)
        srs[h] = gpow_ref[h] * srs[h] + _dot((kc * tail_ref[h]).astype(bf16), vc, TN)
        yc = _rms(oc, rn_ref[...]) * _silu(proj_ref[:, G_C + 128 * h:G_C + 128 * (h + 1)])
        mix_ref[:, W_A + 512 + 128 * h:W_A + 512 + 128 * (h + 1)] = yc.astype(bf16)

    def hgrn_intra(h):
        sl = slice(128 * h, 128 * (h + 1))
        gc = g_s[:, sl]
        qb = qb_s[:, sl]
        kb = kb_s[:, sl]
        inter = _dot((qb * jnp.exp2(gc)).astype(bf16), sht[h].astype(bf16), NT)
        amat = None
        for bz, upper, same in levels:
            gref = jnp.concatenate([jnp.broadcast_to(g_s[pl.ds(st + bz // 2 - 1, 1), sl], (bz, 128))
                                    for st in range(0, BLK, bz)], axis=0)
            dg = gc - gref
            e = jnp.exp2(jnp.where(upper, dg, -dg))
            qt = jnp.where(upper, qb * e, 0.0).astype(bf16)
            kt = jnp.where(upper, 0.0, kb * e).astype(bf16)
            pm = _dot(qt, kt, NT)
            if same is not None:
                pm = jnp.where(same, pm, 0.0)
            amat = pm if amat is None else amat + pm
        ys = []
        for s in range(SUB):
            ks = jnp.concatenate([jnp.broadcast_to(kb_s[pl.ds(SUB * j + s, 1), sl], (SUB, 128))
                                  for j in range(BLK // SUB)], axis=0)
            gs = jnp.concatenate([jnp.broadcast_to(g_s[pl.ds(SUB * j + s, 1), sl], (SUB, 128))
                                  for j in range(BLK // SUB)], axis=0)
            ys.append((qb * ks * jnp.exp2(jnp.where(sub_ge[s], gc - gs, NEG_INF))).astype(bf16))
        amat = amat + jnp.where(same_sub, _dot(jnp.concatenate(ys, axis=1), dsel_ref[...]), 0.0)
        return inter, amat.astype(bf16)

    def hgrn_finish(h, inter, amat):
        sl = slice(128 * h, 128 * (h + 1))
        vb = proj_ref[:, I_B + 128 * h:I_B + 128 * (h + 1)].astype(bf16)
        ob = inter + _dot(amat, vb)
        glast = g_s[pl.ds(BLK - 1, 1), sl]
        kt2 = (kb_s[:, sl] * jnp.exp2(glast - g_s[:, sl])).astype(bf16)
        sht[h] = sht[h] * jnp.exp2(glast) + _dot(vb, kt2, TN)
        yb = _rms(ob, hn_ref[...]) * _silu(proj_ref[:, G_B + 128 * h:G_B + 128 * (h + 1)])
        mix_ref[:, W_A + 128 * h:W_A + 128 * (h + 1)] = yb.astype(bf16)

    attention_prep()
    pending = None
    for h in range(H_B):
        cur = hgrn_intra(h)
        if pending is not None:
            hgrn_finish(h - 1, *pending)
        retention_head(h)
        pending = cur
    hgrn_finish(H_B - 1, *pending)
    for t in range(N_HEADS_A // 2):
        attention_pair(t)
    kk[0:BLK, :] = kk[BLK:2 * BLK, :]
    vv[0:BLK, :] = vv[BLK:2 * BLK, :]

    @pl.when(i == nb - 1)
    def _():
        for h in range(H_B):
            sh_ref[0, h] = sht[h].T
            sr_ref[0, h] = srs[h]


def _const_spec(shape):
    nd = len(shape)
    return pl.BlockSpec(shape, lambda *_: (0,) * nd)


def _slab_rows(nrows, steps):
    for r in range(16, nrows + 1, 16):
        if nrows % r == 0 and nrows // r <= steps:
            return r
    return nrows


def _mixer_prompt(proj, rows_total, batch, nb, layer, sinks, gpow, ropea, roper, qn, kn, lb, hn, rn, sel, expand,
                  tri, dsel, dec, rsc, tail, w_gate, w_up, w_down, w_out):
    smem = pl.BlockSpec(memory_space=pltpu.SMEM)
    d, dff = w_gate.shape[1], w_gate.shape[2]

    def row_blk(b, i):
        return jnp.where(i == 0, batch * (nb - 1), b * (nb - 1) + i - 1)

    def mix_blk(b, i):
        return jnp.where((i == 0) & (b > 0), b * (nb - 1), row_blk(b, i))

    def slab_in(nrows, ncols):
        r = _slab_rows(nrows, batch * nb)
        return pl.BlockSpec((None, r, ncols), lambda b, i: (layer, jnp.minimum(b * nb + i, nrows // r - 1), 0))

    def slab_out(nrows, ncols):
        r = _slab_rows(nrows, batch * nb)
        return pl.BlockSpec((r, ncols), lambda b, i: (jnp.minimum(b * nb + i, nrows // r - 1), 0))

    return pl.pallas_call(
        _mixer_prompt_kernel,
        grid=(batch, nb),
        in_specs=[smem, smem,
                  pl.BlockSpec((BLK, D_IN), lambda b, i: (row_blk(b, i), 0)),
                  pl.BlockSpec((2, BLK, 128), lambda b, i: (0, i, 0)),
                  pl.BlockSpec((2, BLK, 128), lambda b, i: (0, i, 0)),
                  _const_spec((1, 1024)), _const_spec((1, 256)), _const_spec((1, 512)),
                  _const_spec((1, 128)), _const_spec((1, 128)),
                  _const_spec((1024, 128)), _const_spec((128, 1024)),
                  _const_spec((BLK, BLK)), _const_spec((SUB * 128, 128)),
                  _const_spec((H_C, BLK, BLK)), _const_spec((H_C, BLK, 128)), _const_spec((H_C, BLK, 128)),
                  slab_in(d, dff), slab_in(d, dff), slab_in(dff, d), slab_in(d, d)],
        out_specs=[pl.BlockSpec((BLK, 2048), lambda b, i: (mix_blk(b, i), 0)),
                   pl.BlockSpec((1, BLK, 256), lambda b, i: (b, 0, 0)),
                   pl.BlockSpec((1, BLK, 256), lambda b, i: (b, 0, 0)),
                   pl.BlockSpec((1, H_B, 128, 128), lambda b, i: (b, 0, 0, 0)),
                   pl.BlockSpec((1, H_C, 128, 128), lambda b, i: (b, 0, 0, 0)),
                   slab_out(d, dff), slab_out(d, dff), slab_out(dff, d), slab_out(d, d)],
        out_shape=[jax.ShapeDtypeStruct((rows_total, 2048), bf16),
                   jax.ShapeDtypeStruct((batch, BLK, 256), f32),
                   jax.ShapeDtypeStruct((batch, BLK, 256), f32),
                   jax.ShapeDtypeStruct((batch, H_B, 128, 128), f32),
                   jax.ShapeDtypeStruct((batch, H_C, 128, 128), f32),
                   jax.ShapeDtypeStruct((d, dff), bf16), jax.ShapeDtypeStruct((d, dff), bf16),
                   jax.ShapeDtypeStruct((dff, d), bf16), jax.ShapeDtypeStruct((d, d), bf16)],
        scratch_shapes=[pltpu.VMEM((2 * BLK, N_KV_A * 128), bf16), pltpu.VMEM((2 * BLK, N_KV_A * 256), bf16),
                        pltpu.VMEM((H_B, 128, 128), f32), pltpu.VMEM((H_C, 128, 128), f32),
                        pltpu.VMEM((N_HEADS_A * BLK, 128), bf16),
                        pltpu.VMEM((BLK, 512), f32), pltpu.VMEM((BLK, 512), f32), pltpu.VMEM((BLK, 512), f32)],
        compiler_params=pltpu.CompilerParams(dimension_semantics=("arbitrary", "arbitrary"),
                                             vmem_limit_bytes=VMEM_LIMIT),
        name="mixer_prompt",
    )(sinks, gpow, proj, ropea, roper, qn, kn, lb, hn, rn, sel, expand, tri, dsel, dec, rsc, tail,
      w_gate, w_up, w_down, w_out)


def _mixer_sample_kernel(sink_ref, gpow_ref, mixin_ref, *refs, dec_seq, n_prev):
    del mixin_ref
    _mixer_sample_body(sink_ref, gpow_ref, *refs[:17], *refs[17 + n_prev:], dec_seq=dec_seq)


def _mixer_sample_body(sink_ref, gpow_ref, proj_ref, ropea_ref, roper_ref, qn_ref, kn_ref, lb_ref,
                       hn_ref, rn_ref, sel_ref, exp_ref, dec_ref, rsc_ref, tail_ref,
                       ck_ref, cv_ref, sh_ref, sr_ref,
                       mix_ref, nk_ref, nv_ref, nsh_ref, nsr_ref, *, dec_seq):
    nbat = SROWS // dec_seq
    w = ck_ref.shape[1]
    rb = lax.broadcasted_iota(jnp.int32, (SROWS, 1), 0) // dec_seq
    rt = lax.broadcasted_iota(jnp.int32, (SROWS, 1), 0) % dec_seq

    def pick(parts):
        out = parts[nbat - 1]
        for b in range(nbat - 2, -1, -1):
            out = jnp.where(rb == b, parts[b], out)
        return out

    fgate, logf = _forget(proj_ref[:, F_B:F_B + 512], lb_ref[...])
    kb = 1.0 - fgate
    qb = _silu(proj_ref[:, Q_B:Q_B + 512])
    vb = proj_ref[:, I_B:I_B + 512]
    gcum = logf
    for d in range(1, dec_seq):
        gcum = gcum + jnp.where(rt >= d, pltpu.roll(logf, d, 0), 0.0)
    intra = [jnp.zeros((SROWS, 128), f32) for _ in range(H_B)]
    for s in range(SROWS):
        ok = (rb == s // dec_seq) & (rt >= s % dec_seq)
        y = qb * kb[s:s + 1, :] * jnp.exp2(jnp.where(ok, gcum - gcum[s:s + 1, :], NEG_INF))
        for h in range(H_B):
            sl = slice(128 * h, 128 * (h + 1))
            intra[h] = intra[h] + jnp.sum(y[:, sl], axis=-1, keepdims=True) * vb[s:s + 1, sl]

    qa = _rope_a(_head_norm(proj_ref[:, Q_A:Q_A + 1024], sel_ref, exp_ref, qn_ref[...]), ropea_ref)
    qs = (qa * Q_SCALE).astype(bf16)
    k_new = _rope_a(_head_norm(proj_ref[:, K_A:K_A + 256], sel_ref, exp_ref, kn_ref[...]), ropea_ref)
    v_new = proj_ref[:, V_A:V_A + 256]
    for b in range(nbat):
        nk_ref[b, 0:w - dec_seq, :] = ck_ref[b, dec_seq:w, :]
        nk_ref[b, w - dec_seq:w, :] = k_new[dec_seq * b:dec_seq * (b + 1), :]
        nv_ref[b, 0:w - dec_seq, :] = cv_ref[b, dec_seq:w, :]
        nv_ref[b, w - dec_seq:w, :] = v_new[dec_seq * b:dec_seq * (b + 1), :]
    knb = k_new.astype(bf16)
    vnb = v_new.astype(bf16)

    grows = GROUP_A * SROWS
    rt_g = jnp.concatenate([rt] * GROUP_A, axis=0)
    rb_g = jnp.concatenate([rb] * GROUP_A, axis=0)
    jc = lax.broadcasted_iota(jnp.int32, (grows, w), 1)
    valid_c = jc > rt_g + (w - WINDOW)
    cn = lax.broadcasted_iota(jnp.int32, (grows, SROWS), 1)
    valid_n = (cn // dec_seq == rb_g) & (cn % dec_seq <= rt_g)
    heads = [None] * N_HEADS_A
    for g in range(N_KV_A):
        ksl = slice(64 * g, 64 * (g + 1))
        qg = jnp.concatenate([qs[:, 64 * h:64 * (h + 1)] for h in range(GROUP_A * g, GROUP_A * (g + 1))], axis=0)
        sparts = [_dot(qg, ck_ref[b, :, ksl].astype(bf16), NT) for b in range(nbat)]
        s_c = sparts[nbat - 1]
        for b in range(nbat - 2, -1, -1):
            s_c = jnp.where(rb_g == b, sparts[b], s_c)
        s_c = jnp.where(valid_c, s_c, NEG_INF)
        s_n = jnp.where(valid_n, _dot(qg, knb[:, ksl], NT), NEG_INF)
        sk = jnp.concatenate([jnp.full((SROWS, 1), sink_ref[h] * LOG2E, f32)
                              for h in range(GROUP_A * g, GROUP_A * (g + 1))], axis=0)
        m = jnp.maximum(jnp.maximum(jnp.max(s_c, axis=-1, keepdims=True), jnp.max(s_n, axis=-1, keepdims=True)), sk)
        p_c = jnp.exp2(s_c - m)
        p_n = jnp.exp2(s_n - m)
        den = jnp.sum(p_c, axis=-1, keepdims=True) + jnp.sum(p_n, axis=-1, keepdims=True) + jnp.exp2(sk - m)
        pcb = p_c.astype(bf16)
        oparts = [_dot(pcb, cv_ref[b, :, ksl].astype(bf16)) for b in range(nbat)]
        o = oparts[nbat - 1]
        for b in range(nbat - 2, -1, -1):
            o = jnp.where(rb_g == b, oparts[b], o)
        o = (o + _dot(p_n.astype(bf16), vnb[:, ksl])) / den
        for hh in range(GROUP_A):
            heads[GROUP_A * g + hh] = o[SROWS * hh:SROWS * (hh + 1), :]
    mix_ref[:, 0:W_A] = jnp.concatenate(heads, axis=1).astype(bf16)

    qg = (qb * jnp.exp2(gcum)).astype(bf16)
    vbb = vb.astype(bf16)
    glast = [gcum[dec_seq * (b + 1) - 1:dec_seq * (b + 1), :] for b in range(nbat)]
    for h in range(H_B):
        sl = slice(128 * h, 128 * (h + 1))
        ob = pick([_dot(qg[:, sl], sh_ref[b, h].astype(bf16)) for b in range(nbat)]) + intra[h]
        for b in range(nbat):
            kt2 = jnp.where(rb == b, kb[:, sl] * jnp.exp2(glast[b][:, sl] - gcum[:, sl]), 0.0).astype(bf16)
            dcol = jnp.broadcast_to(jnp.exp2(glast[b][:, sl]), (128, 128)).T
            nsh_ref[b, h] = dcol * sh_ref[b, h] + _dot(kt2, vbb[:, sl], TN)
        yb = _rms(ob, hn_ref[...]) * _silu(proj_ref[:, G_B + 128 * h:G_B + 128 * (h + 1)])
        mix_ref[:, W_A + 128 * h:W_A + 128 * (h + 1)] = yb.astype(bf16)

    for h in range(H_C):
        qc = _rope_r(proj_ref[:, Q_C + 128 * h:Q_C + 128 * (h + 1)], roper_ref)
        kc = _rope_r(proj_ref[:, K_C + 128 * h:K_C + 128 * (h + 1)], roper_ref) * (DK_C ** -0.5)
        vc = proj_ref[:, V_C + 128 * h:V_C + 128 * (h + 1)]
        qcb = qc.astype(bf16)
        vcb = vc.astype(bf16)
        sc = _dot(qcb, kc.astype(bf16), NT) * dec_ref[h]
        oc = pick([_dot(qcb, sr_ref[b, h].astype(bf16)) for b in range(nbat)]) * rsc_ref[h] \
            + _dot(sc.astype(bf16), vcb)
        kct = kc * tail_ref[h]
        for b in range(nbat):
            nsr_ref[b, h] = gpow_ref[h] * sr_ref[b, h] + _dot(jnp.where(rb == b, kct, 0.0).astype(bf16), vcb, TN)
        yc = _rms(oc, rn_ref[...]) * _silu(proj_ref[:, G_C + 128 * h:G_C + 128 * (h + 1)])
        mix_ref[:, W_A + 512 + 128 * h:W_A + 512 + 128 * (h + 1)] = yc.astype(bf16)


def _mixer_sample(mix, proj, row0, layer, dec_batch, dec_seq, sinks, gpow, ropea, roper, qn, kn, lb, hn, rn, sel, expand,
                  dec, rsc, tail, cache_k, cache_v, state_h, state_r, prev_outs):
    smem = pl.BlockSpec(memory_space=pltpu.SMEM)
    nbat = SROWS // dec_seq
    steps = dec_batch // nbat
    blk0 = row0 // SROWS
    w = cache_k.shape[1]
    n_all = cache_k.shape[0]
    assert mix.shape[0] == row0 + steps * SROWS
    any_spec = pl.BlockSpec(memory_space=pl.ANY)
    n_in = 20
    return pl.pallas_call(
        functools.partial(_mixer_sample_kernel, dec_seq=dec_seq, n_prev=len(prev_outs)),
        grid=(steps,),
        in_specs=[smem, smem,
                  pl.BlockSpec(memory_space=pl.ANY),
                  pl.BlockSpec((SROWS, D_IN), lambda c: (blk0 + c, 0)),
                  _const_spec((2, SROWS, 128)), _const_spec((2, SROWS, 128)),
                  _const_spec((1, 1024)), _const_spec((1, 256)), _const_spec((1, 512)),
                  _const_spec((1, 128)), _const_spec((1, 128)),
                  _const_spec((1024, 128)), _const_spec((128, 1024)),
                  _const_spec((H_C, SROWS, SROWS)), _const_spec((H_C, SROWS, 128)), _const_spec((H_C, SROWS, 128)),
                  pl.BlockSpec((nbat, w, 256), lambda c: (layer * steps + c, 0, 0)),
                  pl.BlockSpec((nbat, w, 256), lambda c: (layer * steps + c, 0, 0)),
                  pl.BlockSpec((nbat, H_B, 128, 128), lambda c: (layer * steps + c, 0, 0, 0)),
                  pl.BlockSpec((nbat, H_C, 128, 128), lambda c: (layer * steps + c, 0, 0, 0))]
        + [any_spec] * len(prev_outs),
        out_specs=[pl.BlockSpec((SROWS, 2048), lambda c: (blk0 + c, 0)),
                   pl.BlockSpec((nbat, w, 256), lambda c: (layer * steps + c, 0, 0)),
                   pl.BlockSpec((nbat, w, 256), lambda c: (layer * steps + c, 0, 0)),
                   pl.BlockSpec((nbat, H_B, 128, 128), lambda c: (layer * steps + c, 0, 0, 0)),
                   pl.BlockSpec((nbat, H_C, 128, 128), lambda c: (layer * steps + c, 0, 0, 0))],
        out_shape=[jax.ShapeDtypeStruct(mix.shape, bf16),
                   jax.ShapeDtypeStruct((n_all, w, 256), f32),
                   jax.ShapeDtypeStruct((n_all, w, 256), f32),
                   jax.ShapeDtypeStruct((n_all, H_B, 128, 128), f32),
                   jax.ShapeDtypeStruct((n_all, H_C, 128, 128), f32)],
        input_output_aliases={2: 0, **{n_in + k: 1 + k for k in range(len(prev_outs))}},
        compiler_params=pltpu.CompilerParams(dimension_semantics=("arbitrary",), vmem_limit_bytes=VMEM_LIMIT),
        name="mixer_sample",
    )(sinks, gpow, mix, proj, ropea, roper, qn, kn, lb, hn, rn, sel, expand, dec, rsc, tail,
      cache_k, cache_v, state_h, state_r, *prev_outs)


def _rope_tables(pos):
    posf = jnp.asarray(pos).astype(f32)[:, None]
    t = pos.shape[0]
    half = N_ROT // 2
    inv = ROPE_THETA ** (-jnp.arange(half, dtype=f32) * (2.0 / N_ROT))
    ang = posf * inv[None, :]
    cos, sin = jnp.cos(ang), jnp.sin(ang)
    rest0 = jnp.zeros((t, HEAD_DIM_A - N_ROT), f32)
    c64 = jnp.concatenate([cos, cos, rest0 + 1.0], axis=1)
    s64 = jnp.concatenate([sin, sin, rest0], axis=1)
    ropea = jnp.stack([jnp.tile(c64, (1, 2)), jnp.tile(s64, (1, 2))])
    invr = RET_THETA ** (-jnp.arange(DK_C // 2, dtype=f32) * (2.0 / DK_C))
    angr = posf * invr[None, :]
    cr, sr = jnp.cos(angr), jnp.sin(angr)
    roper = jnp.stack([jnp.concatenate([cr, cr], axis=1), jnp.concatenate([-sr, sr], axis=1)])
    return ropea, roper


def _ret_tables(lg, seq_of_row, tok_of_row, length):
    nf = np.float32
    tq = tok_of_row.astype(nf)
    rel = tq[:, None] - tq[None, :]
    ok = (seq_of_row[:, None] == seq_of_row[None, :]) & (rel >= 0)
    dec = np.where(ok[None], np.exp(np.where(ok, rel, nf(0.0))[None] * lg[:, None, None]), nf(0.0))
    n = tq.shape[0]
    rsc = np.broadcast_to(np.exp((tq + nf(1.0))[None, :, None] * lg[:, None, None]), (H_C, n, 128))
    tail = np.broadcast_to(np.exp((nf(length) - nf(1.0) - tq)[None, :, None] * lg[:, None, None]), (H_C, n, 128))
    gpow = np.exp(nf(length) * lg)
    return dec.astype(nf), np.ascontiguousarray(rsc, nf), np.ascontiguousarray(tail, nf), gpow.astype(nf)


def kernel(x_prompt, x_sample, cache_k, cache_v, state_hgrn, state_ret, meta_tokens, norm_mix, norm_ffn, w_in, q_norm, k_norm, attn_sinks, hgrn_lb, hgrn_norm, ret_norm, w_out, w_gate, w_up, w_down):
    batch, seq, d = x_prompt.shape
    dec_batch, dec_seq, _ = x_sample.shape
    depth = w_in.shape[0]
    w = cache_k.shape[2]
    assert d == 2048 and w_in.shape[2] == D_IN and seq % BLK == 0
    assert SROWS % dec_seq == 0 and dec_batch % (SROWS // dec_seq) == 0 and w == WINDOW
    nb = seq // BLK + 1
    lp = nb * BLK
    rows_main = batch * seq
    rows_s = dec_batch * dec_seq
    row_s0 = rows_main + BLK
    tail_rows = BLK + rows_s
    rows = rows_main + tail_rows
    tm_ffn = _row_tile(rows, TM_FFN, 16)
    assert rows_main % tail_rows == 0 and rows_s % SROWS == 0 and tail_rows % 16 == 0 and tm_ffn >= tail_rows

    tail = jnp.concatenate([jnp.zeros((PAD, d), f32), meta_tokens.astype(f32), x_sample.reshape(rows_s, d)], axis=0)
    x, h = _embed(x_prompt.reshape(rows_main, d), tail, norm_mix[0][None])

    p = jax.nn.softmax(hgrn_lb.astype(f32), axis=0)
    lbs = jnp.cumsum(p, axis=0) - p[0]
    lg = np.log1p(-np.exp2(np.float32(-5.0) - np.arange(H_C, dtype=np.float32))).astype(np.float32)
    ropea_p, roper_p = _rope_tables(np.arange(lp) - PAD)
    srow = np.arange(SROWS)
    ropea_s, roper_s = _rope_tables(PAST_LEN + srow % dec_seq)
    dec_p, rsc_p, tail_p, gpow_p = _ret_tables(lg, np.zeros((BLK,), np.int32), np.arange(BLK), float(BLK))
    dec_s, rsc_s, tail_s, gpow_s = _ret_tables(lg, srow // dec_seq, srow % dec_seq, float(dec_seq))
    sel_np = np.arange(1024)[:, None] // HEAD_DIM_A == np.arange(128)[None, :]
    sel = jnp.asarray(sel_np, bf16)
    expand = jnp.asarray(sel_np.T, bf16)
    tri = jnp.asarray(np.arange(BLK)[:, None] >= np.arange(BLK)[None, :], bf16)
    dsel = jnp.asarray(np.arange(SUB * 128)[:, None] // 128 == np.arange(128)[None, :] % SUB, bf16)

    ck_flat = cache_k.reshape(depth * dec_batch, w, N_KV_A * HEAD_DIM_A)
    cv_flat = cache_v.reshape(depth * dec_batch, w, N_KV_A * HEAD_DIM_A)
    sh_flat = state_hgrn.reshape(depth * dec_batch, H_B, 128, 128)
    sr_flat = state_ret.reshape(depth * dec_batch, H_C, 128, 128)

    outs_p, outs_s = [], ()
    for l in range(depth):
        proj = _inproj(h, w_in, l)
        qn = jnp.tile(q_norm[l], N_HEADS_A)[None]
        kn = jnp.tile(k_norm[l], N_KV_A)[None]
        common = (qn, kn, lbs[l][None], hgrn_norm[l][None], ret_norm[l][None], sel, expand)
        mix, ck, cv, sh, sr, wg_b, wu_b, wd_b, wo_b = _mixer_prompt(
            proj, rows, batch, nb, l, attn_sinks[l], gpow_p, ropea_p, roper_p, *common, tri, dsel,
            dec_p, rsc_p, tail_p, w_gate, w_up, w_down, w_out)
        mix, *outs_s = _mixer_sample(mix, proj, row_s0, l, dec_batch, dec_seq, attn_sinks[l], gpow_s,
                                     ropea_s, roper_s, *common, dec_s, rsc_s, tail_s,
                                     ck_flat, cv_flat, sh_flat, sr_flat, tuple(outs_s))
        ffn_w = (wo_b, norm_ffn[l][None], wg_b, wu_b, wd_b)
        outs_p.append((ck, cv, sh, sr))
        if l + 1 < depth:
            x, h = _ffn(x, mix, *ffn_w, norm_mix[l + 1][None], tm_ffn, tail_rows)
        else:
            y_main, y_tail = _ffn(x, mix, *ffn_w, None, tm_ffn, tail_rows)

    y_prompt = y_main.reshape(batch, seq, d)
    y_sample = y_tail[BLK:].reshape(dec_batch, dec_seq, d)
    kv_shape = (depth, -1, w, N_KV_A, HEAD_DIM_A)
    st_shape = (depth, dec_batch, H_B, 128, 128)
    stack = lambda outs, k: jnp.stack([o[k] for o in outs])
    nk, nv, nsh, nsr = outs_s
    return (y_prompt, y_sample,
            stack(outs_p, 0).reshape(kv_shape), stack(outs_p, 1).reshape(kv_shape), stack(outs_p, 2), stack(outs_p, 3),
            nk.reshape(kv_shape), nv.reshape(kv_shape), nsh.reshape(st_shape), nsr.reshape(st_shape))
```

```python
import functools

import jax
import jax.numpy as jnp
import numpy as np
from jax import lax
from jax.experimental import pallas as pl
from jax.experimental.pallas import tpu as pltpu

f32 = jnp.float32
bf16 = jnp.bfloat16

N_META = 16
EPS = 1e-6
NEG_INF = -1e30
LB_FLOOR = 1e-30
WINDOW = 128
HEAD_DIM_A = 64
N_HEADS_A = 16
N_KV_A = 4
GROUP_A = N_HEADS_A // N_KV_A
N_ROT = 16
ROPE_THETA = 500000.0
H_B = 4
DK_B = 128
H_C = 4
DK_C = 128
RET_THETA = 10000.0
PAST_LEN = 16384

BLK = 128
PAD = BLK - N_META
SUB = 8
SROWS = 32
TM_FFN = 704
VMEM_LIMIT = 56 * 1024 * 1024
VMEM_LIMIT_FFN = 60 * 1024 * 1024

Q_A, K_A, V_A = 0, 1024, 1280
Q_B, F_B, I_B, G_B = 1536, 2048, 2560, 3072
Q_C, K_C, V_C, G_C = 3584, 4096, 4608, 5120
D_IN = 5632
W_A = 1024
LOG2E = 1.4426950408889634
Q_SCALE = HEAD_DIM_A ** -0.5 * LOG2E
NT = (((1,), (1,)), ((), ()))
TN = (((0,), (0,)), ((), ()))


def _dot(a, b, dims=None):
    if dims is None:
        return jnp.dot(a, b, preferred_element_type=f32)
    return lax.dot_general(a, b, dims, preferred_element_type=f32)


def _row_tile(rows, cap, align=8):
    best = align
    for t in range(align, cap + 1, align):
        if rows % t == 0:
            best = t
    return best


def _silu(x):
    return x * jax.nn.sigmoid(x)


def _rms(x, g):
    return x * lax.rsqrt(jnp.mean(x * x, axis=-1, keepdims=True) + EPS) * g


def _inproj_kernel(h_ref, w_ref, o_ref):
    o_ref[...] = _dot(h_ref[...], w_ref[...].astype(bf16))


def _inproj(h, w_all, layer):
    rows, d = h.shape
    n = w_all.shape[2]
    tm = _row_tile(rows, 2816, 16)
    tn = 512
    return pl.pallas_call(
        _inproj_kernel,
        grid=(rows // tm, n // tn),
        in_specs=[pl.BlockSpec((tm, d), lambda i, j: (i, 0)),
                  pl.BlockSpec((None, d, tn), lambda i, j: (layer, 0, j))],
        out_specs=pl.BlockSpec((tm, tn), lambda i, j: (i, j)),
        out_shape=jax.ShapeDtypeStruct((rows, n), f32),
        compiler_params=pltpu.CompilerParams(dimension_semantics=("arbitrary", "arbitrary"),
                                             vmem_limit_bytes=VMEM_LIMIT),
        name="inproj",
    )(h, w_all)


def _embed_kernel(xp_ref, tail_ref, g_ref, x_ref, h_ref):
    i = pl.program_id(0)
    last = pl.num_programs(0) - 1

    def emit(src_ref):
        x = src_ref[...]
        x_ref[...] = x
        h_ref[...] = _rms(x, g_ref[...]).astype(bf16)

    pl.when(i < last)(lambda: emit(xp_ref))
    pl.when(i == last)(lambda: emit(tail_ref))


def _embed(xp, tail, g):
    rows_main, d = xp.shape
    tm = tail.shape[0]
    n_main = rows_main // tm
    rows = rows_main + tm
    return pl.pallas_call(
        _embed_kernel,
        grid=(n_main + 1,),
        in_specs=[pl.BlockSpec((tm, d), lambda i: (jnp.minimum(i, n_main - 1), 0)),
                  pl.BlockSpec((tm, d), lambda i: (0, 0)),
                  pl.BlockSpec((1, d), lambda i: (0, 0))],
        out_specs=[pl.BlockSpec((tm, d), lambda i: (i, 0)),
                   pl.BlockSpec((tm, d), lambda i: (i, 0))],
        out_shape=[jax.ShapeDtypeStruct((rows, d), f32), jax.ShapeDtypeStruct((rows, d), bf16)],
        compiler_params=pltpu.CompilerParams(dimension_semantics=("arbitrary",), vmem_limit_bytes=VMEM_LIMIT),
        name="embed_norm",
    )(xp, tail, g)


def _ffn_kernel(*refs, with_next, tail_rows, n_ff, tf):
    if with_next:
        (x_ref, mix_ref, wo_ref, nf_ref, wg_hbm, wu_hbm, wd_hbm, gn_ref, o_ref, hn_ref,
         wg_buf, wu_buf, wd_buf, sem) = refs
        h_ref = hn_ref
    else:
        (x_ref, mix_ref, wo_ref, nf_ref, wg_hbm, wu_hbm, wd_hbm, o_ref, tail_ref,
         h_ref, wg_buf, wu_buf, wd_buf, sem) = refs
    i = pl.program_id(0)
    n_chunks = pl.num_programs(0) * n_ff

    def chunk_copies(c):
        jt = c % n_ff
        slot = c % 2
        col = pl.multiple_of(jt * tf, tf)
        return (pltpu.make_async_copy(wg_hbm.at[:, pl.ds(col, tf)], wg_buf.at[slot], sem.at[0, slot]),
                pltpu.make_async_copy(wu_hbm.at[:, pl.ds(col, tf)], wu_buf.at[slot], sem.at[1, slot]),
                pltpu.make_async_copy(wd_hbm.at[pl.ds(col, tf), :], wd_buf.at[slot], sem.at[2, slot]))

    @pl.when(i == 0)
    def _():
        for k, cp in enumerate(chunk_copies(0)):
            cp.start(priority=k % 2)

    x1 = x_ref[...] + _dot(mix_ref[...], wo_ref[...])
    h_ref[...] = _rms(x1, nf_ref[...]).astype(bf16)
    o_ref[...] = x1

    def ff_step(j, carry):
        c = i * n_ff + j
        for cp in chunk_copies(c):
            cp.wait()

        @pl.when(c + 1 < n_chunks)
        def _():
            for k, cp in enumerate(chunk_copies(c + 1)):
                cp.start(priority=k % 2)

        slot = c % 2
        h = h_ref[...]
        a = _silu(_dot(h, wg_buf[slot])) * _dot(h, wu_buf[slot])
        o_ref[...] += _dot(a.astype(bf16), wd_buf[slot])
        return carry

    lax.fori_loop(0, n_ff, ff_step, 0)

    if with_next:
        hn_ref[...] = _rms(o_ref[...], gn_ref[...]).astype(bf16)
    else:
        @pl.when(i == pl.num_programs(0) - 1)
        def _():
            tail_ref[...] = o_ref[o_ref.shape[0] - tail_rows:, :]


def _ffn(x, mix, wo, nf, wg, wu, wd, g_next, tm, tail_rows):
    rows, d = x.shape
    dff = wg.shape[1]
    tf = 512
    n_tiles = rows // tm
    with_next = g_next is not None
    row_spec = pl.BlockSpec((tm, d), lambda i: (i, 0))
    out_spec = row_spec
    hbm = pl.BlockSpec(memory_space=pl.ANY)
    in_specs = [row_spec, row_spec,
                pl.BlockSpec((d, d), lambda i: (0, 0), pipeline_mode=pl.Buffered(1)),
                pl.BlockSpec((1, d), lambda i: (0, 0)),
                hbm, hbm, hbm]
    args = [x, mix, wo, nf, wg, wu, wd]
    if with_next:
        in_specs.append(pl.BlockSpec((1, d), lambda i: (0, 0)))
        args.append(g_next)
        out_specs = [out_spec, out_spec]
        out_shape = [jax.ShapeDtypeStruct((rows, d), f32), jax.ShapeDtypeStruct((rows, d), bf16)]
    else:
        assert tail_rows % 8 == 0 and tail_rows <= tm
        out_specs = [out_spec, pl.BlockSpec((tail_rows, d), lambda i: (0, 0))]
        out_shape = [jax.ShapeDtypeStruct((rows - tail_rows, d), f32), jax.ShapeDtypeStruct((tail_rows, d), f32)]
    weight_ring = [pltpu.VMEM((2, d, tf), bf16), pltpu.VMEM((2, d, tf), bf16), pltpu.VMEM((2, tf, d), bf16),
                   pltpu.SemaphoreType.DMA((3, 2))]
    return pl.pallas_call(
        functools.partial(_ffn_kernel, with_next=with_next, tail_rows=tail_rows, n_ff=dff // tf, tf=tf),
        grid=(n_tiles,),
        in_specs=in_specs,
        out_specs=out_specs,
        out_shape=out_shape,
        scratch_shapes=([] if with_next else [pltpu.VMEM((tm, d), bf16)]) + weight_ring,
        compiler_params=pltpu.CompilerParams(dimension_semantics=("arbitrary",),
                                             vmem_limit_bytes=VMEM_LIMIT_FFN),
        name="outproj_ffn",
    )(*args)


def _head_norm(x, sel_ref, exp_ref, g):
    n, w = x.shape
    x2 = x * x
    hi = x2.astype(bf16)
    lo = (x2 - hi.astype(f32)).astype(bf16)
    s = _dot(jnp.concatenate([hi, lo], axis=0), sel_ref[0:w, :])
    return x * _dot_split3(lax.rsqrt((s[0:n] + s[n:2 * n]) * (1.0 / HEAD_DIM_A) + EPS), exp_ref[:, 0:w]) * g


def _dot_split3(x, m):
    n = x.shape[0]
    y = _dot(jnp.concatenate(_split3(x), axis=0), m)
    return y[0:n] + y[n:2 * n] + y[2 * n:3 * n]


def _rope_a(x, tab_ref):
    half = N_ROT // 2
    first = lax.broadcasted_iota(jnp.int32, (x.shape[0], 128), 1) % HEAD_DIM_A < half
    out = []
    for t in range(x.shape[1] // 128):
        xt = x[:, 128 * t:128 * (t + 1)]
        partner = jnp.where(first, -pltpu.roll(xt, 128 - half, 1), pltpu.roll(xt, half, 1))
        out.append(xt * tab_ref[0] + partner * tab_ref[1])
    return jnp.concatenate(out, axis=1)


def _rope_r(x, tab_ref):
    return x * tab_ref[0] + pltpu.roll(x, DK_C // 2, 1) * tab_ref[1]


def _forget(fb, lb):
    f = jnp.maximum(lb, LB_FLOOR) + (1.0 - lb) * jax.nn.sigmoid(fb)
    return f, jnp.log2(f)


def _split3(x):
    h1 = x.astype(bf16)
    r1 = x - h1.astype(f32)
    h2 = r1.astype(bf16)
    h3 = (r1 - h2.astype(f32)).astype(bf16)
    return h1, h2, h3


def _mixer_prompt_kernel(sink_ref, gpow_ref, proj_ref, ropea_ref, roper_ref, qn_ref, kn_ref, lb_ref, hn_ref, rn_ref,
                         sel_ref, exp_ref, tri_ref, dsel_ref, dec_ref, rsc_ref, tail_ref,
                         wg_ref, wu_ref, wd_ref, wo_ref,
                         mix_ref, ck_ref, cv_ref, sh_ref, sr_ref, wgb_ref, wub_ref, wdb_ref, wob_ref,
                         kk, vv, sht, srs, qs_s, qb_s, kb_s, g_s):
    i = pl.program_id(1)
    nb = pl.num_programs(1)

    @pl.when(i == 0)
    def _():
        kk[...] = jnp.zeros_like(kk)
        vv[...] = jnp.zeros_like(vv)
        for g in range(N_KV_A):
            vv[:, 256 * g + 128:256 * (g + 1)] = jnp.ones((2 * BLK, 128), bf16)
        sht[...] = jnp.zeros_like(sht)
        srs[...] = jnp.zeros_like(srs)

    wgb_ref[...] = wg_ref[...].astype(bf16)
    wub_ref[...] = wu_ref[...].astype(bf16)
    wdb_ref[...] = wd_ref[...].astype(bf16)
    wob_ref[...] = wo_ref[...].astype(bf16)

    rowabs = i * BLK + lax.broadcasted_iota(jnp.int32, (BLK, 1), 0)
    fgate, logf = _forget(proj_ref[:, F_B:F_B + 512], lb_ref[...])
    logf = jnp.where(rowabs >= PAD, logf, 0.0)
    kb_s[...] = jnp.where(rowabs >= PAD, 1.0 - fgate, 0.0)
    qb_s[...] = _silu(proj_ref[:, Q_B:Q_B + 512])
    l1, l2, l3 = _split3(logf)
    tri = tri_ref[...]
    g_s[...] = _dot(tri, l1) + _dot(tri, l2) + _dot(tri, l3)

    r2 = lax.broadcasted_iota(jnp.int32, (BLK, BLK), 0)
    c2 = lax.broadcasted_iota(jnp.int32, (BLK, BLK), 1)
    lane_lo = c2 < HEAD_DIM_A

    def attention_prep():
        qa = _rope_a(_head_norm(proj_ref[:, Q_A:Q_A + 1024], sel_ref, exp_ref, qn_ref[...]), ropea_ref)
        for t in range(N_HEADS_A // 2):
            xt = qa[:, 128 * t:128 * (t + 1)] * Q_SCALE
            qs_s[256 * t:256 * t + 128, :] = jnp.where(lane_lo, xt, 0.0).astype(bf16)
            qs_s[256 * t + 128:256 * (t + 1), :] = jnp.where(lane_lo, 0.0, xt).astype(bf16)
        k_cur = _rope_a(_head_norm(proj_ref[:, K_A:K_A + 256], sel_ref, exp_ref, kn_ref[...]), ropea_ref)
        v_cur = proj_ref[:, V_A:V_A + 256]
        ck_ref[0] = k_cur
        cv_ref[0] = v_cur
        for t in range(N_KV_A // 2):
            for src, dst, width in ((k_cur, kk, 128), (v_cur, vv, 256)):
                xt = src[:, 128 * t:128 * (t + 1)]
                xs = pltpu.roll(xt, HEAD_DIM_A, 1)
                dst[BLK:2 * BLK, width * 2 * t:width * 2 * t + 128] = jnp.where(lane_lo, xt, xs).astype(bf16)
                dst[BLK:2 * BLK, width * (2 * t + 1):width * (2 * t + 1) + 128] = jnp.where(lane_lo, xs, xt).astype(bf16)

    prow = 2 * BLK
    r4 = lax.broadcasted_iota(jnp.int32, (prow, BLK), 0) % BLK
    c4 = lax.broadcasted_iota(jnp.int32, (prow, BLK), 1)
    up = c4 > r4
    ok = c4 >= jnp.where(up, PAD - (i - 1) * BLK, PAD - i * BLK)

    def attention_pair(t):
        g = (2 * t) // GROUP_A
        s2 = _dot(qs_s[prow * t:prow * (t + 1), :], kk[:, 128 * g:128 * (g + 1)], NT)
        s = jnp.where(ok, jnp.where(up, s2[:, 0:BLK], s2[:, BLK:2 * BLK]), NEG_INF)
        sk = jnp.concatenate([jnp.full((BLK, BLK), sink_ref[2 * t + j] * LOG2E, f32) for j in range(2)], axis=0)
        m = jnp.maximum(jnp.max(s, axis=-1, keepdims=True), sk)
        p = jnp.exp2(s - m)
        p2 = jnp.concatenate([jnp.where(up, p, 0.0), jnp.where(up, 0.0, p)], axis=1).astype(bf16)
        res = _dot(p2, vv[:, 256 * g:256 * (g + 1)])
        y = res[:, 0:128] / (res[:, 128:256] + jnp.exp2(sk - m))
        mix_ref[:, 128 * t:128 * (t + 1)] = jnp.where(lane_lo, y[0:BLK], y[BLK:prow]).astype(bf16)

    levels = []
    bz = BLK
    while bz > SUB:
        levels.append((bz, (r2 % bz) >= bz // 2, None if bz == BLK else (r2 // bz) == (c2 // bz)))
        bz //= 2
    same_sub = (r2 // SUB) == (c2 // SUB)
    sub_ge = [(r2 % SUB) >= s for s in range(SUB)]

    def retention_head(h):
        qc = _rope_r(proj_ref[:, Q_C + 128 * h:Q_C + 128 * (h + 1)], roper_ref)
        kc = _rope_r(proj_ref[:, K_C + 128 * h:K_C + 128 * (h + 1)], roper_ref) * (DK_C ** -0.5)
        vc = proj_ref[:, V_C + 128 * h:V_C + 128 * (h + 1)].astype(bf16)
        qcb = qc.astype(bf16)
        sc = _dot(qcb, kc.astype(bf16), NT) * dec_ref[h]
        oc = _dot(qcb, srs[h].astype(bf16)) * rsc_ref[h] + _dot(sc.astype(bf16), vc)
        srs[h] = gpow_ref[h] * srs[h] + _dot((kc * tail_ref[h]).astype(bf16), vc, TN)
        yc = _rms(oc, rn_ref[...]) * _silu(proj_ref[:, G_C + 128 * h:G_C + 128 * (h + 1)])
        mix_ref[:, W_A + 512 + 128 * h:W_A + 512 + 128 * (h + 1)] = yc.astype(bf16)

    def hgrn_intra(h):
        sl = slice(128 * h, 128 * (h + 1))
        gc = g_s[:, sl]
        qb = qb_s[:, sl]
        kb = kb_s[:, sl]
        inter = _dot((qb * jnp.exp2(gc)).astype(bf16), sht[h].astype(bf16), NT)
        amat = None
        for bz, upper, same in levels:
            gref = jnp.concatenate([jnp.broadcast_to(g_s[pl.ds(st + bz // 2 - 1, 1), sl], (bz, 128))
                                    for st in range(0, BLK, bz)], axis=0)
            dg = gc - gref
            e = jnp.exp2(jnp.where(upper, dg, -dg))
            qt = jnp.where(upper, qb * e, 0.0).astype(bf16)
            kt = jnp.where(upper, 0.0, kb * e).astype(bf16)
            pm = _dot(qt, kt, NT)
            if same is not None:
                pm = jnp.where(same, pm, 0.0)
            amat = pm if amat is None else amat + pm
        ys = []
        for s in range(SUB):
            ks = jnp.concatenate([jnp.broadcast_to(kb_s[pl.ds(SUB * j + s, 1), sl], (SUB, 128))
                                  for j in range(BLK // SUB)], axis=0)
            gs = jnp.concatenate([jnp.broadcast_to(g_s[pl.ds(SUB * j + s, 1), sl], (SUB, 128))
                                  for j in range(BLK // SUB)], axis=0)
            ys.append((qb * ks * jnp.exp2(jnp.where(sub_ge[s], gc - gs, NEG_INF))).astype(bf16))
        amat = amat + jnp.where(same_sub, _dot(jnp.concatenate(ys, axis=1), dsel_ref[...]), 0.0)
        return inter, amat.astype(bf16)

    def hgrn_finish(h, inter, amat):
        sl = slice(128 * h, 128 * (h + 1))
        vb = proj_ref[:, I_B + 128 * h:I_B + 128 * (h + 1)].astype(bf16)
        ob = inter + _dot(amat, vb)
        glast = g_s[pl.ds(BLK - 1, 1), sl]
        kt2 = (kb_s[:, sl] * jnp.exp2(glast - g_s[:, sl])).astype(bf16)
        sht[h] = sht[h] * jnp.exp2(glast) + _dot(vb, kt2, TN)
        yb = _rms(ob, hn_ref[...]) * _silu(proj_ref[:, G_B + 128 * h:G_B + 128 * (h + 1)])
        mix_ref[:, W_A + 128 * h:W_A + 128 * (h + 1)] = yb.astype(bf16)

    attention_prep()
    pending = None
    for h in range(H_B):
        cur = hgrn_intra(h)
        if pending is not None:
            hgrn_finish(h - 1, *pending)
        retention_head(h)
        pending = cur
    hgrn_finish(H_B - 1, *pending)
    for t in range(N_HEADS_A // 2):
        attention_pair(t)
    kk[0:BLK, :] = kk[BLK:2 * BLK, :]
    vv[0:BLK, :] = vv[BLK:2 * BLK, :]

    @pl.when(i == nb - 1)
    def _():
        for h in range(H_B):
            sh_ref[0, h] = sht[h].T
            sr_ref[0, h] = srs[h]


def _const_spec(shape):
    nd = len(shape)
    return pl.BlockSpec(shape, lambda *_: (0,) * nd)


def _slab_rows(nrows, steps):
    for r in range(16, nrows + 1, 16):
        if nrows % r == 0 and nrows // r <= steps:
            return r
    return nrows


def _mixer_prompt(proj, rows_total, batch, nb, layer, sinks, gpow, ropea, roper, qn, kn, lb, hn, rn, sel, expand,
                  tri, dsel, dec, rsc, tail, w_gate, w_up, w_down, w_out):
    smem = pl.BlockSpec(memory_space=pltpu.SMEM)
    d, dff = w_gate.shape[1], w_gate.shape[2]

    def row_blk(b, i):
        return jnp.where(i == 0, batch * (nb - 1), b * (nb - 1) + i - 1)

    def mix_blk(b, i):
        return jnp.where((i == 0) & (b > 0), b * (nb - 1), row_blk(b, i))

    def slab_in(nrows, ncols):
        r = _slab_rows(nrows, batch * nb)
        return pl.BlockSpec((None, r, ncols), lambda b, i: (layer, jnp.minimum(b * nb + i, nrows // r - 1), 0))

    def slab_out(nrows, ncols):
        r = _slab_rows(nrows, batch * nb)
        return pl.BlockSpec((r, ncols), lambda b, i: (jnp.minimum(b * nb + i, nrows // r - 1), 0))

    return pl.pallas_call(
        _mixer_prompt_kernel,
        grid=(batch, nb),
        in_specs=[smem, smem,
                  pl.BlockSpec((BLK, D_IN), lambda b, i: (row_blk(b, i), 0)),
                  pl.BlockSpec((2, BLK, 128), lambda b, i: (0, i, 0)),
                  pl.BlockSpec((2, BLK, 128), lambda b, i: (0, i, 0)),
                  _const_spec((1, 1024)), _const_spec((1, 256)), _const_spec((1, 512)),
                  _const_spec((1, 128)), _const_spec((1, 128)),
                  _const_spec((1024, 128)), _const_spec((128, 1024)),
                  _const_spec((BLK, BLK)), _const_spec((SUB * 128, 128)),
                  _const_spec((H_C, BLK, BLK)), _const_spec((H_C, BLK, 128)), _const_spec((H_C, BLK, 128)),
                  slab_in(d, dff), slab_in(d, dff), slab_in(dff, d), slab_in(d, d)],
        out_specs=[pl.BlockSpec((BLK, 2048), lambda b, i: (mix_blk(b, i), 0)),
                   pl.BlockSpec((1, BLK, 256), lambda b, i: (b, 0, 0)),
                   pl.BlockSpec((1, BLK, 256), lambda b, i: (b, 0, 0)),
                   pl.BlockSpec((1, H_B, 128, 128), lambda b, i: (b, 0, 0, 0)),
                   pl.BlockSpec((1, H_C, 128, 128), lambda b, i: (b, 0, 0, 0)),
                   slab_out(d, dff), slab_out(d, dff), slab_out(dff, d), slab_out(d, d)],
        out_shape=[jax.ShapeDtypeStruct((rows_total, 2048), bf16),
                   jax.ShapeDtypeStruct((batch, BLK, 256), f32),
                   jax.ShapeDtypeStruct((batch, BLK, 256), f32),
                   jax.ShapeDtypeStruct((batch, H_B, 128, 128), f32),
                   jax.ShapeDtypeStruct((batch, H_C, 128, 128), f32),
                   jax.ShapeDtypeStruct((d, dff), bf16), jax.ShapeDtypeStruct((d, dff), bf16),
                   jax.ShapeDtypeStruct((dff, d), bf16), jax.ShapeDtypeStruct((d, d), bf16)],
        scratch_shapes=[pltpu.VMEM((2 * BLK, N_KV_A * 128), bf16), pltpu.VMEM((2 * BLK, N_KV_A * 256), bf16),
                        pltpu.VMEM((H_B, 128, 128), f32), pltpu.VMEM((H_C, 128, 128), f32),
                        pltpu.VMEM((N_HEADS_A * BLK, 128), bf16),
                        pltpu.VMEM((BLK, 512), f32), pltpu.VMEM((BLK, 512), f32), pltpu.VMEM((BLK, 512), f32)],
        compiler_params=pltpu.CompilerParams(dimension_semantics=("arbitrary", "arbitrary"),
                                             vmem_limit_bytes=VMEM_LIMIT),
        name="mixer_prompt",
    )(sinks, gpow, proj, ropea, roper, qn, kn, lb, hn, rn, sel, expand, tri, dsel, dec, rsc, tail,
      w_gate, w_up, w_down, w_out)


def _mixer_sample_kernel(sink_ref, gpow_ref, mixin_ref, *refs, dec_seq, n_prev):
    del mixin_ref
    _mixer_sample_body(sink_ref, gpow_ref, *refs[:17], *refs[17 + n_prev:], dec_seq=dec_seq)


def _mixer_sample_body(sink_ref, gpow_ref, proj_ref, ropea_ref, roper_ref, qn_ref, kn_ref, lb_ref,
                       hn_ref, rn_ref, sel_ref, exp_ref, dec_ref, rsc_ref, tail_ref,
                       ck_ref, cv_ref, sh_ref, sr_ref,
                       mix_ref, nk_ref, nv_ref, nsh_ref, nsr_ref, *, dec_seq):
    nbat = SROWS // dec_seq
    w = ck_ref.shape[1]
    rb = lax.broadcasted_iota(jnp.int32, (SROWS, 1), 0) // dec_seq
    rt = lax.broadcasted_iota(jnp.int32, (SROWS, 1), 0) % dec_seq

    def pick(parts):
        out = parts[nbat - 1]
        for b in range(nbat - 2, -1, -1):
            out = jnp.where(rb == b, parts[b], out)
        return out

    fgate, logf = _forget(proj_ref[:, F_B:F_B + 512], lb_ref[...])
    kb = 1.0 - fgate
    qb = _silu(proj_ref[:, Q_B:Q_B + 512])
    vb = proj_ref[:, I_B:I_B + 512]
    gcum = logf
    for d in range(1, dec_seq):
        gcum = gcum + jnp.where(rt >= d, pltpu.roll(logf, d, 0), 0.0)
    intra = [jnp.zeros((SROWS, 128), f32) for _ in range(H_B)]
    for s in range(SROWS):
        ok = (rb == s // dec_seq) & (rt >= s % dec_seq)
        y = qb * kb[s:s + 1, :] * jnp.exp2(jnp.where(ok, gcum - gcum[s:s + 1, :], NEG_INF))
        for h in range(H_B):
            sl = slice(128 * h, 128 * (h + 1))
            intra[h] = intra[h] + jnp.sum(y[:, sl], axis=-1, keepdims=True) * vb[s:s + 1, sl]

    qa = _rope_a(_head_norm(proj_ref[:, Q_A:Q_A + 1024], sel_ref, exp_ref, qn_ref[...]), ropea_ref)
    qs = (qa * Q_SCALE).astype(bf16)
    k_new = _rope_a(_head_norm(proj_ref[:, K_A:K_A + 256], sel_ref, exp_ref, kn_ref[...]), ropea_ref)
    v_new = proj_ref[:, V_A:V_A + 256]
    for b in range(nbat):
        nk_ref[b, 0:w - dec_seq, :] = ck_ref[b, dec_seq:w, :]
        nk_ref[b, w - dec_seq:w, :] = k_new[dec_seq * b:dec_seq * (b + 1), :]
        nv_ref[b, 0:w - dec_seq, :] = cv_ref[b, dec_seq:w, :]
        nv_ref[b, w - dec_seq:w, :] = v_new[dec_seq * b:dec_seq * (b + 1), :]
    knb = k_new.astype(bf16)
    vnb = v_new.astype(bf16)

    grows = GROUP_A * SROWS
    rt_g = jnp.concatenate([rt] * GROUP_A, axis=0)
    rb_g = jnp.concatenate([rb] * GROUP_A, axis=0)
    jc = lax.broadcasted_iota(jnp.int32, (grows, w), 1)
    valid_c = jc > rt_g + (w - WINDOW)
    cn = lax.broadcasted_iota(jnp.int32, (grows, SROWS), 1)
    valid_n = (cn // dec_seq == rb_g) & (cn % dec_seq <= rt_g)
    heads = [None] * N_HEADS_A
    for g in range(N_KV_A):
        ksl = slice(64 * g, 64 * (g + 1))
        qg = jnp.concatenate([qs[:, 64 * h:64 * (h + 1)] for h in range(GROUP_A * g, GROUP_A * (g + 1))], axis=0)
        sparts = [_dot(qg, ck_ref[b, :, ksl].astype(bf16), NT) for b in range(nbat)]
        s_c = sparts[nbat - 1]
        for b in range(nbat - 2, -1, -1):
            s_c = jnp.where(rb_g == b, sparts[b], s_c)
        s_c = jnp.where(valid_c, s_c, NEG_INF)
        s_n = jnp.where(valid_n, _dot(qg, knb[:, ksl], NT), NEG_INF)
        sk = jnp.concatenate([jnp.full((SROWS, 1), sink_ref[h] * LOG2E, f32)
                              for h in range(GROUP_A * g, GROUP_A * (g + 1))], axis=0)
        m = jnp.maximum(jnp.maximum(jnp.max(s_c, axis=-1, keepdims=True), jnp.max(s_n, axis=-1, keepdims=True)), sk)
        p_c = jnp.exp2(s_c - m)
        p_n = jnp.exp2(s_n - m)
        den = jnp.sum(p_c, axis=-1, keepdims=True) + jnp.sum(p_n, axis=-1, keepdims=True) + jnp.exp2(sk - m)
        pcb = p_c.astype(bf16)
        oparts = [_dot(pcb, cv_ref[b, :, ksl].astype(bf16)) for b in range(nbat)]
        o = oparts[nbat - 1]
        for b in range(nbat - 2, -1, -1):
            o = jnp.where(rb_g == b, oparts[b], o)
        o = (o + _dot(p_n.astype(bf16), vnb[:, ksl])) / den
        for hh in range(GROUP_A):
            heads[GROUP_A * g + hh] = o[SROWS * hh:SROWS * (hh + 1), :]
    mix_ref[:, 0:W_A] = jnp.concatenate(heads, axis=1).astype(bf16)

    qg = (qb * jnp.exp2(gcum)).astype(bf16)
    vbb = vb.astype(bf16)
    glast = [gcum[dec_seq * (b + 1) - 1:dec_seq * (b + 1), :] for b in range(nbat)]
    for h in range(H_B):
        sl = slice(128 * h, 128 * (h + 1))
        ob = pick([_dot(qg[:, sl], sh_ref[b, h].astype(bf16)) for b in range(nbat)]) + intra[h]
        for b in range(nbat):
            kt2 = jnp.where(rb == b, kb[:, sl] * jnp.exp2(glast[b][:, sl] - gcum[:, sl]), 0.0).astype(bf16)
            dcol = jnp.broadcast_to(jnp.exp2(glast[b][:, sl]), (128, 128)).T
            nsh_ref[b, h] = dcol * sh_ref[b, h] + _dot(kt2, vbb[:, sl], TN)
        yb = _rms(ob, hn_ref[...]) * _silu(proj_ref[:, G_B + 128 * h:G_B + 128 * (h + 1)])
        mix_ref[:, W_A + 128 * h:W_A + 128 * (h + 1)] = yb.astype(bf16)

    for h in range(H_C):
        qc = _rope_r(proj_ref[:, Q_C + 128 * h:Q_C + 128 * (h + 1)], roper_ref)
        kc = _rope_r(proj_ref[:, K_C + 128 * h:K_C + 128 * (h + 1)], roper_ref) * (DK_C ** -0.5)
        vc = proj_ref[:, V_C + 128 * h:V_C + 128 * (h + 1)]
        qcb = qc.astype(bf16)
        vcb = vc.astype(bf16)
        sc = _dot(qcb, kc.astype(bf16), NT) * dec_ref[h]
        oc = pick([_dot(qcb, sr_ref[b, h].astype(bf16)) for b in range(nbat)]) * rsc_ref[h] \
            + _dot(sc.astype(bf16), vcb)
        kct = kc * tail_ref[h]
        for b in range(nbat):
            nsr_ref[b, h] = gpow_ref[h] * sr_ref[b, h] + _dot(jnp.where(rb == b, kct, 0.0).astype(bf16), vcb, TN)
        yc = _rms(oc, rn_ref[...]) * _silu(proj_ref[:, G_C + 128 * h:G_C + 128 * (h + 1)])
        mix_ref[:, W_A + 512 + 128 * h:W_A + 512 + 128 * (h + 1)] = yc.astype(bf16)


def _mixer_sample(mix, proj, row0, layer, dec_batch, dec_seq, sinks, gpow, ropea, roper, qn, kn, lb, hn, rn, sel, expand,
                  dec, rsc, tail, cache_k, cache_v, state_h, state_r, prev_outs):
    smem = pl.BlockSpec(memory_space=pltpu.SMEM)
    nbat = SROWS // dec_seq
    steps = dec_batch // nbat
    blk0 = row0 // SROWS
    w = cache_k.shape[1]
    n_all = cache_k.shape[0]
    assert mix.shape[0] == row0 + steps * SROWS
    any_spec = pl.BlockSpec(memory_space=pl.ANY)
    n_in = 20
    return pl.pallas_call(
        functools.partial(_mixer_sample_kernel, dec_seq=dec_seq, n_prev=len(prev_outs)),
        grid=(steps,),
        in_specs=[smem, smem,
                  pl.BlockSpec(memory_space=pl.ANY),
                  pl.BlockSpec((SROWS, D_IN), lambda c: (blk0 + c, 0)),
                  _const_spec((2, SROWS, 128)), _const_spec((2, SROWS, 128)),
                  _const_spec((1, 1024)), _const_spec((1, 256)), _const_spec((1, 512)),
                  _const_spec((1, 128)), _const_spec((1, 128)),
                  _const_spec((1024, 128)), _const_spec((128, 1024)),
                  _const_spec((H_C, SROWS, SROWS)), _const_spec((H_C, SROWS, 128)), _const_spec((H_C, SROWS, 128)),
                  pl.BlockSpec((nbat, w, 256), lambda c: (layer * steps + c, 0, 0)),
                  pl.BlockSpec((nbat, w, 256), lambda c: (layer * steps + c, 0, 0)),
                  pl.BlockSpec((nbat, H_B, 128, 128), lambda c: (layer * steps + c, 0, 0, 0)),
                  pl.BlockSpec((nbat, H_C, 128, 128), lambda c: (layer * steps + c, 0, 0, 0))]
        + [any_spec] * len(prev_outs),
        out_specs=[pl.BlockSpec((SROWS, 2048), lambda c: (blk0 + c, 0)),
                   pl.BlockSpec((nbat, w, 256), lambda c: (layer * steps + c, 0, 0)),
                   pl.BlockSpec((nbat, w, 256), lambda c: (layer * steps + c, 0, 0)),
                   pl.BlockSpec((nbat, H_B, 128, 128), lambda c: (layer * steps + c, 0, 0, 0)),
                   pl.BlockSpec((nbat, H_C, 128, 128), lambda c: (layer * steps + c, 0, 0, 0))],
        out_shape=[jax.ShapeDtypeStruct(mix.shape, bf16),
                   jax.ShapeDtypeStruct((n_all, w, 256), f32),
                   jax.ShapeDtypeStruct((n_all, w, 256), f32),
                   jax.ShapeDtypeStruct((n_all, H_B, 128, 128), f32),
                   jax.ShapeDtypeStruct((n_all, H_C, 128, 128), f32)],
        input_output_aliases={2: 0, **{n_in + k: 1 + k for k in range(len(prev_outs))}},
        compiler_params=pltpu.CompilerParams(dimension_semantics=("arbitrary",), vmem_limit_bytes=VMEM_LIMIT),
        name="mixer_sample",
    )(sinks, gpow, mix, proj, ropea, roper, qn, kn, lb, hn, rn, sel, expand, dec, rsc, tail,
      cache_k, cache_v, state_h, state_r, *prev_outs)


def _rope_tables(pos):
    posf = jnp.asarray(pos).astype(f32)[:, None]
    t = pos.shape[0]
    half = N_ROT // 2
    inv = ROPE_THETA ** (-jnp.arange(half, dtype=f32) * (2.0 / N_ROT))
    ang = posf * inv[None, :]
    cos, sin = jnp.cos(ang), jnp.sin(ang)
    rest0 = jnp.zeros((t, HEAD_DIM_A - N_ROT), f32)
    c64 = jnp.concatenate([cos, cos, rest0 + 1.0], axis=1)
    s64 = jnp.concatenate([sin, sin, rest0], axis=1)
    ropea = jnp.stack([jnp.tile(c64, (1, 2)), jnp.tile(s64, (1, 2))])
    invr = RET_THETA ** (-jnp.arange(DK_C // 2, dtype=f32) * (2.0 / DK_C))
    angr = posf * invr[None, :]
    cr, sr = jnp.cos(angr), jnp.sin(angr)
    roper = jnp.stack([jnp.concatenate([cr, cr], axis=1), jnp.concatenate([-sr, sr], axis=1)])
    return ropea, roper


def _ret_tables(lg, seq_of_row, tok_of_row, length):
    nf = np.float32
    tq = tok_of_row.astype(nf)
    rel = tq[:, None] - tq[None, :]
    ok = (seq_of_row[:, None] == seq_of_row[None, :]) & (rel >= 0)
    dec = np.where(ok[None], np.exp(np.where(ok, rel, nf(0.0))[None] * lg[:, None, None]), nf(0.0))
    n = tq.shape[0]
    rsc = np.broadcast_to(np.exp((tq + nf(1.0))[None, :, None] * lg[:, None, None]), (H_C, n, 128))
    tail = np.broadcast_to(np.exp((nf(length) - nf(1.0) - tq)[None, :, None] * lg[:, None, None]), (H_C, n, 128))
    gpow = np.exp(nf(length) * lg)
    return dec.astype(nf), np.ascontiguousarray(rsc, nf), np.ascontiguousarray(tail, nf), gpow.astype(nf)


def kernel(x_prompt, x_sample, cache_k, cache_v, state_hgrn, state_ret, meta_tokens, norm_mix, norm_ffn, w_in, q_norm, k_norm, attn_sinks, hgrn_lb, hgrn_norm, ret_norm, w_out, w_gate, w_up, w_down):
    batch, seq, d = x_prompt.shape
    dec_batch, dec_seq, _ = x_sample.shape
    depth = w_in.shape[0]
    w = cache_k.shape[2]
    assert d == 2048 and w_in.shape[2] == D_IN and seq % BLK == 0
    assert SROWS % dec_seq == 0 and dec_batch % (SROWS // dec_seq) == 0 and w == WINDOW
    nb = seq // BLK + 1
    lp = nb * BLK
    rows_main = batch * seq
    rows_s = dec_batch * dec_seq
    row_s0 = rows_main + BLK
    tail_rows = BLK + rows_s
    rows = rows_main + tail_rows
    tm_ffn = _row_tile(rows, TM_FFN, 16)
    assert rows_main % tail_rows == 0 and rows_s % SROWS == 0 and tail_rows % 16 == 0 and tm_ffn >= tail_rows

    tail = jnp.concatenate([jnp.zeros((PAD, d), f32), meta_tokens.astype(f32), x_sample.reshape(rows_s, d)], axis=0)
    x, h = _embed(x_prompt.reshape(rows_main, d), tail, norm_mix[0][None])

    p = jax.nn.softmax(hgrn_lb.astype(f32), axis=0)
    lbs = jnp.cumsum(p, axis=0) - p[0]
    lg = np.log1p(-np.exp2(np.float32(-5.0) - np.arange(H_C, dtype=np.float32))).astype(np.float32)
    ropea_p, roper_p = _rope_tables(np.arange(lp) - PAD)
    srow = np.arange(SROWS)
    ropea_s, roper_s = _rope_tables(PAST_LEN + srow % dec_seq)
    dec_p, rsc_p, tail_p, gpow_p = _ret_tables(lg, np.zeros((BLK,), np.int32), np.arange(BLK), float(BLK))
    dec_s, rsc_s, tail_s, gpow_s = _ret_tables(lg, srow // dec_seq, srow % dec_seq, float(dec_seq))
    sel_np = np.arange(1024)[:, None] // HEAD_DIM_A == np.arange(128)[None, :]
    sel = jnp.asarray(sel_np, bf16)
    expand = jnp.asarray(sel_np.T, bf16)
    tri = jnp.asarray(np.arange(BLK)[:, None] >= np.arange(BLK)[None, :], bf16)
    dsel = jnp.asarray(np.arange(SUB * 128)[:, None] // 128 == np.arange(128)[None, :] % SUB, bf16)

    ck_flat = cache_k.reshape(depth * dec_batch, w, N_KV_A * HEAD_DIM_A)
    cv_flat = cache_v.reshape(depth * dec_batch, w, N_KV_A * HEAD_DIM_A)
    sh_flat = state_hgrn.reshape(depth * dec_batch, H_B, 128, 128)
    sr_flat = state_ret.reshape(depth * dec_batch, H_C, 128, 128)

    outs_p, outs_s = [], ()
    for l in range(depth):
        proj = _inproj(h, w_in, l)
        qn = jnp.tile(q_norm[l], N_HEADS_A)[None]
        kn = jnp.tile(k_norm[l], N_KV_A)[None]
        common = (qn, kn, lbs[l][None], hgrn_norm[l][None], ret_norm[l][None], sel, expand)
        mix, ck, cv, sh, sr, wg_b, wu_b, wd_b, wo_b = _mixer_prompt(
            proj, rows, batch, nb, l, attn_sinks[l], gpow_p, ropea_p, roper_p, *common, tri, dsel,
            dec_p, rsc_p, tail_p, w_gate, w_up, w_down, w_out)
        mix, *outs_s = _mixer_sample(mix, proj, row_s0, l, dec_batch, dec_seq, attn_sinks[l], gpow_s,
                                     ropea_s, roper_s, *common, dec_s, rsc_s, tail_s,
                                     ck_flat, cv_flat, sh_flat, sr_flat, tuple(outs_s))
        ffn_w = (wo_b, norm_ffn[l][None], wg_b, wu_b, wd_b)
        outs_p.append((ck, cv, sh, sr))
        if l + 1 < depth:
            x, h = _ffn(x, mix, *ffn_w, norm_mix[l + 1][None], tm_ffn, tail_rows)
        else:
            y_main, y_tail = _ffn(x, mix, *ffn_w, None, tm_ffn, tail_rows)

    y_prompt = y_main.reshape(batch, seq, d)
    y_sample = y_tail[BLK:].reshape(dec_batch, dec_seq, d)
    kv_shape = (depth, -1, w, N_KV_A, HEAD_DIM_A)
    st_shape = (depth, dec_batch, H_B, 128, 128)
    stack = lambda outs, k: jnp.stack([o[k] for o in outs])
    nk, nv, nsh, nsr = outs_s
    return (y_prompt, y_sample,
            stack(outs_p, 0).reshape(kv_shape), stack(outs_p, 1).reshape(kv_shape), stack(outs_p, 2), stack(outs_p, 3),
            nk.reshape(kv_shape), nv.reshape(kv_shape), nsh.reshape(st_shape), nsr.reshape(st_shape))
```
